```python
import math
import jax, jax.numpy as jnp
from jax import lax
import numpy as np


D_MODEL = 1024
BATCH = 2
SEQ = 8192
DEPTH = 4

N_MIXERS = 3
HEAD_DIM = 128
MIX_HEADS = D_MODEL // HEAD_DIM
MIX_WIDTH = MIX_HEADS * HEAD_DIM
MEM_HEADS = 4
MEM_WIDTH = MEM_HEADS * HEAD_DIM
MEM_TOKENS = 256
OUT_IN = MIX_WIDTH + MEM_WIDTH

HGRN_CHUNK = 64
HGRN_IN = 4 * MIX_WIDTH + MEM_WIDTH
FOX_QBLOCK = 128
FOX_IN = 3 * MIX_WIDTH + MIX_HEADS + MEM_WIDTH
MOBA_BLOCK = 256
MOBA_TOPK = 3
MOBA_QCHUNK = 32
MOBA_IN = 3 * MIX_WIDTH + MEM_WIDTH
REL_BUCKETS = 32
REL_MAX_DIST = 128

N_EXPERTS = 32
TOP_K = 4
D_FF_EXPERT = D_MODEL
SWIGLU_LIMIT = 7.0
SWIGLU_ALPHA = 1.702
MOE_BLOCK = 256

DEEPNORM_ALPHA = (2 * DEPTH) ** 0.25
DEEPNORM_BETA = (8 * DEPTH) ** -0.25
LN_EPS = 1e-5
RMS_EPS = 1e-6
N_HGRN = (DEPTH + 2) // 3
N_FOX = (DEPTH + 1) // 3
N_MOBA = DEPTH // 3

kernel_name = 'hybrid_hgrn2_fox_moba_memory_moe'


def layer_norm(x, g, b):
    xf = x.astype(jnp.float32)
    mu = xf.mean(-1, keepdims=True)
    var = jnp.square(xf - mu).mean(-1, keepdims=True)
    return ((xf - mu) * lax.rsqrt(var + LN_EPS) * g + b).astype(x.dtype)


def split_heads(t, n):
    B, S, _ = t.shape
    return t.reshape(B, S, n, -1).transpose(0, 2, 1, 3)


def merge_heads(t):
    B, n, S, dh = t.shape
    return t.transpose(0, 2, 1, 3).reshape(B, S, n * dh)


def hgrn_lower_bounds(lb_param):
    p = jax.nn.softmax(lb_param.astype(jnp.float32), axis=0)
    c = jnp.cumsum(p, axis=0)
    return c - c[0:1]


def hgrn2_chunk_scan(q, k, v, log_f):
    B, H, S, dk = q.shape
    dv = v.shape[-1]
    nc = S // HGRN_CHUNK

    def to_chunks(t):
        return t.reshape(B, H, nc, HGRN_CHUNK, t.shape[-1]).transpose(2, 0, 1, 3, 4)

    causal = jnp.tril(jnp.ones((HGRN_CHUNK, HGRN_CHUNK), bool))

    def step(state, inp):
        qc, kc, vc, gc = inp
        b = jnp.cumsum(gc, axis=2)
        o_inter = jnp.einsum('bhtd,bhde->bhte', qc * jnp.exp(b), state)
        rel = jnp.where(causal[:, :, None], b[:, :, :, None, :] - b[:, :, None, :, :], -jnp.inf)
        scores = jnp.einsum('bhtd,bhsd,bhtsd->bhts', qc, kc, jnp.exp(rel))
        o = o_inter + jnp.einsum('bhts,bhse->bhte', scores, vc)
        b_last = b[:, :, -1:, :]
        k_dec = kc * jnp.exp(b_last - b)
        state = jnp.exp(b_last[:, :, 0, :])[..., None] * state + jnp.einsum('bhsd,bhse->bhde', k_dec, vc)
        return state, o

    state0 = jnp.zeros((B, H, dk, dv), jnp.float32)
    _, o = lax.scan(step, state0, (to_chunks(q), to_chunks(k), to_chunks(v), to_chunks(log_f)))
    return o.transpose(1, 2, 0, 3, 4).reshape(B, H, S, dv)


def hgrn2_mixer(p, lb, norm_g):
    B, S, _ = p.shape
    q, z, i, g = jnp.split(p, 4, axis=-1)
    zf = z.astype(jnp.float32)
    log_f = jnp.logaddexp(jnp.log(lb), jnp.log1p(-lb) + jax.nn.log_sigmoid(zf))
    k = (1.0 - lb) * jax.nn.sigmoid(-zf)
    qf = jax.nn.silu(q.astype(jnp.float32))
    o = hgrn2_chunk_scan(split_heads(qf, MIX_HEADS), split_heads(k, MIX_HEADS),
                         split_heads(i.astype(jnp.float32), MIX_HEADS), split_heads(log_f, MIX_HEADS))
    o = o * lax.rsqrt(jnp.mean(jnp.square(o), axis=-1, keepdims=True) + RMS_EPS)
    o = merge_heads(o) * norm_g * jax.nn.silu(g.astype(jnp.float32))
    return o.astype(p.dtype)


def fox_attention(q, k, v, c):
    B, H, S, dh = q.shape
    nq = S // FOX_QBLOCK
    q_blocks = q.reshape(B, H, nq, FOX_QBLOCK, dh).transpose(2, 0, 1, 3, 4)
    c_blocks = c.reshape(B, H, nq, FOX_QBLOCK).transpose(2, 0, 1, 3)
    key_pos = jnp.arange(S)
    scale = dh ** -0.5

    def block(args):
        bi, qb, cb = args
        q_pos = bi * FOX_QBLOCK + jnp.arange(FOX_QBLOCK)
        s = jnp.einsum('bhqd,bhkd->bhqk', qb, k).astype(jnp.float32) * scale
        s = s + (cb[..., :, None] - c[:, :, None, :])
        s = jnp.where(key_pos[None, :] <= q_pos[:, None], s, -jnp.inf)
        pr = jax.nn.softmax(s, axis=-1).astype(v.dtype)
        return jnp.einsum('bhqk,bhkd->bhqd', pr, v)

    o = lax.map(block, (jnp.arange(nq), q_blocks, c_blocks))
    return o.transpose(1, 2, 0, 3, 4).reshape(B, H, S, dh)


def fox_mixer(p, b_f):
    q, k, v, zf = jnp.split(p, [MIX_WIDTH, 2 * MIX_WIDTH, 3 * MIX_WIDTH], axis=-1)
    log_f = jax.nn.log_sigmoid(zf.astype(jnp.float32) + b_f)
    c = jnp.cumsum(log_f, axis=1).transpose(0, 2, 1)
    o = fox_attention(split_heads(q, MIX_HEADS), split_heads(k, MIX_HEADS), split_heads(v, MIX_HEADS), c)
    return merge_heads(o)


def t5_bucket(dist):
    n = jnp.maximum(dist, 0)
    max_exact = REL_BUCKETS // 2
    log_ratio = jnp.log(jnp.maximum(n, 1).astype(jnp.float32) / max_exact) / math.log(REL_MAX_DIST / max_exact)
    large = jnp.minimum(max_exact + (log_ratio * (REL_BUCKETS - max_exact)).astype(jnp.int32), REL_BUCKETS - 1)
    return jnp.where(n < max_exact, n, large)


def moba_attention(q, k, v, rel_bias):
    B, H, S, dh = q.shape
    nb = -(-S // MOBA_BLOCK)
    Sp = nb * MOBA_BLOCK
    topk = min(MOBA_TOPK, nb)
    pad = ((0, 0), (0, 0), (0, Sp - S), (0, 0))
    q, k, v = jnp.pad(q, pad), jnp.pad(k, pad), jnp.pad(v, pad)
    k_blk = k.reshape(B, H, nb, MOBA_BLOCK, dh)
    v_blk = v.reshape(B, H, nb, MOBA_BLOCK, dh)
    k_mean = k_blk.astype(jnp.float32).mean(axis=3)
    nq = Sp // MOBA_QCHUNK
    q_chunks = q.reshape(B, H, nq, MOBA_QCHUNK, dh).transpose(2, 0, 1, 3, 4)
    bias_table = rel_bias.T
    b_idx = jnp.arange(B)[:, None, None, None]
    h_idx = jnp.arange(H)[None, :, None, None]
    offs = jnp.arange(MOBA_BLOCK)
    scale = dh ** -0.5
    n_past = topk * MOBA_BLOCK

    def chunk(args):
        ci, qc = args
        q_pos = ci * MOBA_QCHUNK + jnp.arange(MOBA_QCHUNK)
        own = (ci * MOBA_QCHUNK) // MOBA_BLOCK
        gate = jnp.einsum('bhqd,bhnd->bhqn', qc.astype(jnp.float32), k_mean)
        gate = jnp.where(jnp.arange(nb) < own, gate, -jnp.inf)
        _, sel = lax.top_k(gate, topk)
        sel_valid = sel < own
        k_sel = k_blk[b_idx, h_idx, sel]
        v_sel = v_blk[b_idx, h_idx, sel]
        s_past = jnp.einsum('bhqd,bhqnkd->bhqnk', qc, k_sel).astype(jnp.float32) * scale
        pos_past = sel[..., None] * MOBA_BLOCK + offs
        s_past = s_past + bias_table[h_idx[..., None], t5_bucket(q_pos[:, None, None] - pos_past)]
        s_past = jnp.where(sel_valid[..., None], s_past, -jnp.inf)
        k_own = lax.dynamic_index_in_dim(k_blk, own, axis=2, keepdims=False)
        v_own = lax.dynamic_index_in_dim(v_blk, own, axis=2, keepdims=False)
        s_own = jnp.einsum('bhqd,bhkd->bhqk', qc, k_own).astype(jnp.float32) * scale
        dist = q_pos[:, None] - (own * MOBA_BLOCK + offs)[None, :]
        s_own = s_own + bias_table[:, t5_bucket(dist)]
        s_own = jnp.where(dist >= 0, s_own, -jnp.inf)
        logits = jnp.concatenate([s_past.reshape(B, H, MOBA_QCHUNK, n_past), s_own], axis=-1)
        probs = jax.nn.softmax(logits, axis=-1).astype(v.dtype)
        p_past = probs[..., :n_past].reshape(B, H, MOBA_QCHUNK, topk, MOBA_BLOCK)
        p_own = probs[..., n_past:]
        return (jnp.einsum('bhqnk,bhqnkd->bhqd', p_past, v_sel)
                + jnp.einsum('bhqk,bhkd->bhqd', p_own, v_own))

    o = lax.map(chunk, (jnp.arange(nq), q_chunks))
    return o.transpose(1, 2, 0, 3, 4).reshape(B, H, Sp, dh)[:, :, :S]


def moba_mixer(p, rel_bias):
    q, k, v = jnp.split(p, 3, axis=-1)
    o = moba_attention(split_heads(q, MIX_HEADS), split_heads(k, MIX_HEADS), split_heads(v, MIX_HEADS), rel_bias)
    return merge_heads(o)


def memory_attention(q, mem, w_mem_kv):
    k, v = jnp.split(mem @ w_mem_kv, 2, axis=-1)
    qh, kh, vh = split_heads(q, MEM_HEADS), split_heads(k, MEM_HEADS), split_heads(v, MEM_HEADS)
    s = jnp.einsum('bhqd,bhkd->bhqk', qh, kh).astype(jnp.float32) * (HEAD_DIM ** -0.5)
    pr = jax.nn.softmax(s, axis=-1).astype(vh.dtype)
    return merge_heads(jnp.einsum('bhqk,bhkd->bhqd', pr, vh))


def moe_ffn(x, router_w, router_b, w_up, b_up, w_down, b_down):
    B, S, D = x.shape
    T = B * S
    xt = x.reshape(T, D)
    logits = (xt @ router_w + router_b).astype(jnp.float32)
    top_v, top_i = lax.top_k(logits, TOP_K)
    gates = jax.nn.softmax(top_v, axis=-1)
    A = T * TOP_K
    flat_e = top_i.reshape(A)
    flat_tok = jnp.repeat(jnp.arange(T, dtype=jnp.int32), TOP_K)
    flat_g = gates.reshape(A)
    order = jnp.argsort(flat_e)
    e_sorted = flat_e[order]
    counts = jnp.bincount(flat_e, length=N_EXPERTS)
    starts = jnp.cumsum(counts) - counts
    padded = ((counts + MOE_BLOCK - 1) // MOE_BLOCK) * MOE_BLOCK
    p_ends = jnp.cumsum(padded)
    p_starts = p_ends - padded
    dest = p_starts[e_sorted] + (jnp.arange(A) - starts[e_sorted])
    P = ((A + MOE_BLOCK - 1) // MOE_BLOCK + N_EXPERTS) * MOE_BLOCK
    tok_buf = jnp.full((P,), T, jnp.int32).at[dest].set(flat_tok[order])
    gate_buf = jnp.zeros((P,), jnp.float32).at[dest].set(flat_g[order])
    n_blk = P // MOE_BLOCK
    blk_expert = jnp.clip(jnp.searchsorted(p_ends, jnp.arange(n_blk) * MOE_BLOCK, side='right'), 0, N_EXPERTS - 1)
    x_pad = jnp.concatenate([xt, jnp.zeros((1, D), xt.dtype)], axis=0)
    x_blocks = x_pad[tok_buf].reshape(n_blk, MOE_BLOCK, D)

    def expert_block(args):
        xb, e = args
        h = xb @ w_up[e] + b_up[e]
        glu = jnp.minimum(h[..., ::2], SWIGLU_LIMIT)
        lin = jnp.clip(h[..., 1::2], -SWIGLU_LIMIT, SWIGLU_LIMIT)
        act = glu * jax.nn.sigmoid(SWIGLU_ALPHA * glu) * (lin + 1.0)
        return act @ w_down[e] + b_down[e]

    y = lax.map(expert_block, (x_blocks, blk_expert)).reshape(P, D)
    y = y * gate_buf[:, None].astype(y.dtype)
    out = jax.ops.segment_sum(y, tok_buf, num_segments=T + 1)[:T]
    return out.astype(x.dtype).reshape(B, S, D)


def setup_inputs(seed: int = 0) -> dict:
    key = jax.random.key(seed)
    ks = jax.random.split(key, 22)
    f32 = jnp.float32

    def nrm(k, shape, s):
        return jax.random.normal(k, shape, f32) * s

    return {
        'x': nrm(ks[0], (BATCH, SEQ, D_MODEL), 1.0),
        'mem': nrm(ks[1], (BATCH, MEM_TOKENS, D_MODEL), 1.0),
        'hgrn_w_in': nrm(ks[2], (N_HGRN, D_MODEL, HGRN_IN), D_MODEL ** -0.5),
        'hgrn_lb': nrm(ks[3], (N_HGRN, MIX_WIDTH), 0.5),
        'hgrn_norm_g': 1.0 + nrm(ks[4], (N_HGRN, MIX_WIDTH), 0.02),
        'fox_w_in': nrm(ks[5], (N_FOX, D_MODEL, FOX_IN), D_MODEL ** -0.5),
        'fox_b_f': jax.random.uniform(ks[6], (N_FOX, MIX_HEADS), f32, 1.0, 4.0),
        'moba_w_in': nrm(ks[7], (N_MOBA, D_MODEL, MOBA_IN), D_MODEL ** -0.5),
        'rel_bias': nrm(ks[8], (REL_BUCKETS, MIX_HEADS), 0.5),
        'w_mem_kv': nrm(ks[9], (DEPTH, D_MODEL, 2 * MEM_WIDTH), D_MODEL ** -0.5),
        'w_out': nrm(ks[10], (DEPTH, OUT_IN, D_MODEL), OUT_IN ** -0.5 * DEEPNORM_BETA),
        'ln1_g': 1.0 + nrm(ks[11], (DEPTH, D_MODEL), 0.02),
        'ln1_b': nrm(ks[12], (DEPTH, D_MODEL), 0.02),
        'router_w': nrm(ks[13], (DEPTH, D_MODEL, N_EXPERTS), D_MODEL ** -0.5),
        'router_b': nrm(ks[14], (DEPTH, N_EXPERTS), 0.01),
        'w_up': nrm(ks[15], (DEPTH, N_EXPERTS, D_MODEL, 2 * D_FF_EXPERT), D_MODEL ** -0.5),
        'b_up': nrm(ks[16], (DEPTH, N_EXPERTS, 2 * D_FF_EXPERT), 0.01),
        'w_down': nrm(ks[17], (DEPTH, N_EXPERTS, D_FF_EXPERT, D_MODEL), D_FF_EXPERT ** -0.5 * DEEPNORM_BETA),
        'b_down': nrm(ks[18], (DEPTH, N_EXPERTS, D_MODEL), 0.01),
        'ln2_g': 1.0 + nrm(ks[19], (DEPTH, D_MODEL), 0.02),
        'ln2_b': nrm(ks[20], (DEPTH, D_MODEL), 0.02),
    }


def reference(x, mem, hgrn_w_in, hgrn_lb, hgrn_norm_g, fox_w_in, fox_b_f, moba_w_in, rel_bias,
              w_mem_kv, w_out, ln1_g, ln1_b, router_w, router_b, w_up, b_up, w_down, b_down,
              ln2_g, ln2_b):
    lower_bounds = hgrn_lower_bounds(hgrn_lb)
    for i in range(DEPTH):
        kind, j = i % N_MIXERS, i // N_MIXERS
        if kind == 0:
            proj = x @ hgrn_w_in[j]
            mix = hgrn2_mixer(proj[..., :4 * MIX_WIDTH], lower_bounds[j], hgrn_norm_g[j])
        elif kind == 1:
            proj = x @ fox_w_in[j]
            mix = fox_mixer(proj[..., :3 * MIX_WIDTH + MIX_HEADS], fox_b_f[j])
        else:
            proj = x @ moba_w_in[j]
            mix = moba_mixer(proj[..., :3 * MIX_WIDTH], rel_bias)
        mem_out = memory_attention(proj[..., -MEM_WIDTH:], mem, w_mem_kv[i])
        h = jnp.concatenate([mix, mem_out], axis=-1) @ w_out[i]
        x = layer_norm(DEEPNORM_ALPHA * x + h, ln1_g[i], ln1_b[i])
        f = moe_ffn(x, router_w[i], router_b[i], w_up[i], b_up[i], w_down[i], b_down[i])
        x = layer_norm(DEEPNORM_ALPHA * x + f, ln2_g[i], ln2_b[i])
    return x
```

```python
import functools
import math

import numpy as np
import jax
import jax.numpy as jnp
from jax import lax
from jax.experimental import pallas as pl
from jax.experimental.pallas import tpu as pltpu

F32 = jnp.float32
BF16 = jnp.bfloat16
I32 = jnp.int32

HEAD_DIM = 128
MIX_HEADS = 8
MEM_HEADS = 4
HGRN_CHUNK = 64
HGRN_SUB = 16
HGRN_EXP_CLAMP = 80.0
MOBA_BLOCK = 256
MOBA_TOPK = 3
REL_BUCKETS = 32
REL_MAX_DIST = 128
N_EXPERTS = 32
TOP_K = 4
MOE_BLOCK = 256
SWIGLU_LIMIT = 7.0
SWIGLU_ALPHA = 1.702
LN_EPS = 1e-5
RMS_EPS = 1e-6
NEG = -1e30
VMEM_LIMIT = 56 * 1024 * 1024

NT_DIMS = (((1,), (1,)), ((), ()))
TN_DIMS = (((0,), (0,)), ((), ()))


def _cparams(*sem):
    return pltpu.CompilerParams(dimension_semantics=sem, vmem_limit_bytes=VMEM_LIMIT)


def _dot(a, b):
    return jnp.dot(a, b, preferred_element_type=F32)


def _dot_nt(a, b):
    return lax.dot_general(a, b, NT_DIMS, preferred_element_type=F32)


def _dot_tn(a, b):
    return lax.dot_general(a, b, TN_DIMS, preferred_element_type=F32)


def _split3(a):
    hi = a.astype(BF16)
    r = a - hi.astype(F32)
    mid = r.astype(BF16)
    lo = (r - mid.astype(F32)).astype(BF16)
    return hi, mid, lo


def _sigmoid(x):
    return 1.0 / (1.0 + jnp.exp(-x))


def _layer_norm(z, g, b):
    mu = jnp.mean(z, axis=-1, keepdims=True)
    zc = z - mu
    var = jnp.mean(zc * zc, axis=-1, keepdims=True)
    return zc * lax.rsqrt(var + LN_EPS) * g + b


def _mm_body(x_ref, w_ref, s_ref, o_ref, xb_ref):
    @pl.when(pl.program_id(1) == 0)
    def _():
        xb_ref[...] = x_ref[...].astype(BF16)

    acc = _dot(xb_ref[...], w_ref[...])
    o_ref[...] = (acc * s_ref[...]).astype(o_ref.dtype)


def _matmul(x, w, col_scale, out_dtype, tm, tn):
    m, k = x.shape
    n = w.shape[1]
    tm = min(tm, m)
    tn = min(tn, n)
    return pl.pallas_call(
        _mm_body,
        grid=(m // tm, n // tn),
        in_specs=[
            pl.BlockSpec((tm, k), lambda i, j: (i, 0)),
            pl.BlockSpec((k, tn), lambda i, j: (0, j)),
            pl.BlockSpec((1, tn), lambda i, j: (0, j)),
        ],
        out_specs=pl.BlockSpec((tm, tn), lambda i, j: (i, j)),
        out_shape=jax.ShapeDtypeStruct((m, n), out_dtype),
        scratch_shapes=[pltpu.VMEM((tm, k), BF16)],
        compiler_params=_cparams("parallel", "arbitrary"),
        name="proj_matmul",
    )(x, w, col_scale)


def _hgrn_body(q_ref, z_ref, i_ref, g_ref, lbp_ref, ng_ref, o_ref, st_ref, *, layer_j, tb):
    c_len, sb = HGRN_CHUNK, HGRN_SUB
    n_sub = c_len // sb

    @pl.when(pl.program_id(2) == 0)
    def _():
        st_ref[...] = jnp.zeros_like(st_ref)

    lbp = lbp_ref[...]
    ex = jnp.exp(lbp - jnp.max(lbp, axis=0, keepdims=True))
    p = ex / jnp.sum(ex, axis=0, keepdims=True)
    lb = jnp.zeros((1, HEAD_DIM), F32)
    for r in range(1, layer_j + 1):
        lb = lb + p[r:r + 1, :]
    log_lb = jnp.log(lb)
    log1m_lb = jnp.log1p(-lb)
    one_m_lb = 1.0 - lb
    ng = ng_ref[...]

    row = lax.broadcasted_iota(I32, (c_len, c_len), 0)
    col = lax.broadcasted_iota(I32, (c_len, c_len), 1)
    sub_start = (row // sb) * sb
    one = jnp.ones((c_len, c_len), F32)
    zero = jnp.zeros((c_len, c_len), F32)
    m_tri = jnp.where(col <= row, one, zero)
    m_in = jnp.where(col > sub_start, m_tri, zero)
    m_dec = jnp.where(col > row, one, zero)
    m_ke = jnp.where(col <= sub_start + (sb - 1), m_dec, zero)
    stack = jnp.concatenate([m_tri, m_in, m_ke, m_dec], axis=0).astype(BF16)
    same_sub = (row // sb) == (col // sb)
    diag_mask = jnp.logical_and(same_sub, col <= row)
    row_sub = row // sb
    krow_sub = lax.broadcasted_iota(I32, (c_len, HEAD_DIM), 0) // sb

    for c in range(tb // c_len):
        sl = pl.ds(c * c_len, c_len)
        z = z_ref[0, sl, :]
        e = jnp.exp(-jnp.abs(z))
        inv = 1.0 / (1.0 + e)
        sig_neg = jnp.where(z >= 0, e, 1.0) * inv
        log_sig = jnp.minimum(z, 0.0) - jnp.log1p(e)
        a2 = log1m_lb + log_sig
        log_f = jnp.maximum(log_lb, a2) + jnp.log1p(jnp.exp(-jnp.abs(log_lb - a2)))
        k = one_m_lb * sig_neg
        qv = q_ref[0, sl, :]
        qf = qv * _sigmoid(qv)
        v = i_ref[0, sl, :]
        vb = v.astype(BF16)

        pre = None
        for piece in _split3(log_f):
            t = _dot(stack, piece)
            pre = t if pre is None else pre + t
        b = pre[0:c_len]
        d_in = pre[c_len:2 * c_len]
        d_ke = pre[2 * c_len:3 * c_len]
        d_dec = pre[3 * c_len:4 * c_len]

        q_in = qf * jnp.exp(d_in)
        k_diag = (k * jnp.exp(jnp.minimum(-d_in, HGRN_EXP_CLAMP))).astype(BF16)
        k_end = k * jnp.exp(d_ke)
        k_dec = (k * jnp.exp(d_dec)).astype(BF16)
        q_all = (qf * jnp.exp(b)).astype(BF16)
        b_ref = b - d_in

        a = jnp.where(diag_mask, _dot_nt(q_in.astype(BF16), k_diag), 0.0)
        for j in range(n_sub - 1):
            b_end = b[(j + 1) * sb - 1:(j + 1) * sb, :]
            cross = jnp.exp(jnp.minimum(b_ref - b_end, 0.0))
            lhs = (q_in * cross).astype(BF16)
            k_j = jnp.where(krow_sub == j, k_end, 0.0).astype(BF16)
            a = a + jnp.where(row_sub > j, _dot_nt(lhs, k_j), 0.0)

        st = st_ref[...]
        o = _dot(a.astype(BF16), vb) + _dot_nt(q_all, st.astype(BF16))
        st_ref[...] = st * jnp.exp(b[c_len - 1:c_len, :]) + _dot_tn(vb, k_dec)

        o = o * lax.rsqrt(jnp.mean(o * o, axis=-1, keepdims=True) + RMS_EPS)
        gv = g_ref[0, sl, :]
        o_ref[0, sl, :] = (o * ng * (gv * _sigmoid(gv))).astype(o_ref.dtype)


def _hgrn_mixer(proj, lb_param, norm_g, layer_j, tb=256):
    bsz, seq, _ = proj.shape
    tb = min(tb, seq)
    n_layers = lb_param.shape[0]

    def col_block(off):
        return pl.BlockSpec((1, tb, HEAD_DIM), lambda b, h, t: (b, t, off + h))

    return pl.pallas_call(
        functools.partial(_hgrn_body, layer_j=layer_j, tb=tb),
        grid=(bsz, MIX_HEADS, seq // tb),
        in_specs=[
            col_block(0), col_block(MIX_HEADS), col_block(2 * MIX_HEADS), col_block(3 * MIX_HEADS),
            pl.BlockSpec((n_layers, HEAD_DIM), lambda b, h, t: (0, h)),
            pl.BlockSpec((1, HEAD_DIM), lambda b, h, t: (0, h)),
        ],
        out_specs=pl.BlockSpec((1, tb, HEAD_DIM), lambda b, h, t: (b, t, h)),
        out_shape=jax.ShapeDtypeStruct((bsz, seq, MIX_HEADS * HEAD_DIM), BF16),
        scratch_shapes=[pltpu.VMEM((HEAD_DIM, HEAD_DIM), F32)],
        compiler_params=_cparams("parallel", "parallel", "arbitrary"),
        name="hgrn2_mixer",
    )(proj, proj, proj, proj, lb_param, norm_g.reshape(1, -1))


def _log_sigmoid(z):
    return jnp.minimum(z, 0.0) - jnp.log1p(jnp.exp(-jnp.abs(z)))


def _fox_gate_body(x_ref, w_ref, wt_ref, bc_ref, br_ref, ccol_ref, crow_ref, carc_ref, carr_ref, *, tb):
    @pl.when(pl.program_id(1) == 0)
    def _():
        carc_ref[...] = jnp.zeros_like(carc_ref)
        carr_ref[...] = jnp.zeros_like(carr_ref)

    xb = x_ref[0].astype(BF16)
    ls_col = _log_sigmoid(_dot(xb, w_ref[...]) + bc_ref[...])
    ls_row = _log_sigmoid(_dot_nt(wt_ref[...], xb) + br_ref[...])
    row = lax.broadcasted_iota(I32, (tb, tb), 0)
    col = lax.broadcasted_iota(I32, (tb, tb), 1)
    lower = jnp.where(col <= row, 1.0, 0.0).astype(BF16)
    upper = jnp.where(row <= col, 1.0, 0.0).astype(BF16)
    c_col = carc_ref[...]
    for piece in _split3(ls_col):
        c_col = c_col + _dot(lower, piece)
    c_row = carr_ref[...]
    for piece in _split3(ls_row):
        c_row = c_row + _dot(piece, upper)
    ccol_ref[0] = c_col
    crow_ref[0] = c_row
    carc_ref[...] = c_col[tb - 1:tb, :]
    carr_ref[...] = c_row[:, tb - 1:tb]


def _fox_gate_cumsum(x, w_f, b_f, tb=512):
    bsz, seq, d = x.shape
    h = w_f.shape[1]
    tb = min(tb, seq)
    return pl.pallas_call(
        functools.partial(_fox_gate_body, tb=tb),
        grid=(bsz, seq // tb),
        in_specs=[
            pl.BlockSpec((1, tb, d), lambda b, t: (b, t, 0)),
            pl.BlockSpec((d, h), lambda b, t: (0, 0)),
            pl.BlockSpec((h, d), lambda b, t: (0, 0)),
            pl.BlockSpec((1, h), lambda b, t: (0, 0)),
            pl.BlockSpec((h, 1), lambda b, t: (0, 0)),
        ],
        out_specs=[
            pl.BlockSpec((1, tb, h), lambda b, t: (b, t, 0)),
            pl.BlockSpec((1, h, tb), lambda b, t: (b, 0, t)),
        ],
        out_shape=[
            jax.ShapeDtypeStruct((bsz, seq, h), F32),
            jax.ShapeDtypeStruct((bsz, h, seq), F32),
        ],
        scratch_shapes=[pltpu.VMEM((1, h), F32), pltpu.VMEM((h, 1), F32)],
        compiler_params=_cparams("parallel", "arbitrary"),
        name="fox_gate_cumsum",
    )(x, w_f.astype(BF16), w_f.T.astype(BF16), b_f.reshape(1, h), b_f.reshape(h, 1))


def _softmax_step(u, shift, v, m_ref, l_ref, acc_ref):
    m_old = m_ref[...]
    m_new = jnp.maximum(m_old, jnp.max(u, axis=-1, keepdims=True) + shift)
    p = jnp.exp(u - (m_new - shift))
    alpha = jnp.exp(m_old - m_new)
    l_ref[...] = alpha * l_ref[...] + jnp.sum(p, axis=-1, keepdims=True)
    acc_ref[...] = alpha * acc_ref[...] + _dot(p.astype(BF16), v)
    m_ref[...] = m_new


def _softmax_init(m_ref, l_ref, acc_ref):
    m_ref[...] = jnp.full(m_ref.shape, NEG, F32)
    l_ref[...] = jnp.zeros_like(l_ref)
    acc_ref[...] = jnp.zeros_like(acc_ref)


def _fox_attn_body(q_ref, k_ref, v_ref, ccol_ref, crow_ref, o_ref, m_ref, l_ref, acc_ref, *, tq):
    h = pl.program_id(1)
    qi = pl.program_id(2)
    q = q_ref[0]
    ccol = ccol_ref[0]
    lane = lax.broadcasted_iota(I32, ccol.shape, 1)
    c_t = jnp.sum(jnp.where(lane == h, ccol, 0.0), axis=1, keepdims=True)
    _softmax_init(m_ref, l_ref, acc_ref)

    def block(kb, masked):
        ks = pl.ds(pl.multiple_of(kb * tq, tq), tq)
        u = _dot_nt(q, k_ref[0, ks, :]) - crow_ref[0, pl.ds(h, 1), ks]
        if masked:
            row = lax.broadcasted_iota(I32, (tq, tq), 0)
            col = lax.broadcasted_iota(I32, (tq, tq), 1)
            u = jnp.where(col <= row, u, NEG)
        _softmax_step(u, c_t, v_ref[0, ks, :], m_ref, l_ref, acc_ref)

    def body(kb, carry):
        block(kb, False)
        return carry

    lax.fori_loop(0, qi, body, 0)
    block(qi, True)
    o_ref[0] = (acc_ref[...] / l_ref[...]).astype(o_ref.dtype)


def _fox_attention(qkv, c_col, c_row, tq=512):
    bsz, seq, _ = qkv.shape
    tq = min(tq, seq)
    nh = MIX_HEADS
    return pl.pallas_call(
        functools.partial(_fox_attn_body, tq=tq),
        grid=(bsz, nh, seq // tq),
        in_specs=[
            pl.BlockSpec((1, tq, HEAD_DIM), lambda b, h, i: (b, i, h)),
            pl.BlockSpec((1, seq, HEAD_DIM), lambda b, h, i: (b, 0, nh + h)),
            pl.BlockSpec((1, seq, HEAD_DIM), lambda b, h, i: (b, 0, 2 * nh + h)),
            pl.BlockSpec((1, tq, nh), lambda b, h, i: (b, i, 0)),
            pl.BlockSpec((1, nh, seq), lambda b, h, i: (b, 0, 0)),
        ],
        out_specs=pl.BlockSpec((1, tq, HEAD_DIM), lambda b, h, i: (b, i, h)),
        out_shape=jax.ShapeDtypeStruct((bsz, seq, nh * HEAD_DIM), BF16),
        scratch_shapes=[
            pltpu.VMEM((tq, 1), F32), pltpu.VMEM((tq, 1), F32), pltpu.VMEM((tq, HEAD_DIM), F32),
        ],
        compiler_params=_cparams("parallel", "parallel", "arbitrary"),
        name="fox_attention",
    )(qkv, qkv, qkv, c_col, c_row)


def _t5_bucket_thresholds():
    n = np.arange(0, 4 * REL_MAX_DIST, dtype=np.int64)
    max_exact = REL_BUCKETS // 2
    ratio = np.log(np.maximum(n, 1).astype(np.float32) / np.float32(max_exact)) / np.float32(
        math.log(REL_MAX_DIST / max_exact))
    large = np.minimum(max_exact + (ratio * np.float32(REL_BUCKETS - max_exact)).astype(np.int32),
                       REL_BUCKETS - 1)
    bucket = np.where(n < max_exact, n, large)
    assert np.all(np.diff(bucket) >= 0) and bucket[-1] == REL_BUCKETS - 1
    return [int(np.argmax(bucket >= b)) for b in range(REL_BUCKETS)]


def _moba_body(rb_ref, q_ref, k_ref, v_ref, o_ref, kmean_ref, bias_ref, sel_ref, m_ref, l_ref, acc_ref,
               *, n_blocks):
    blk = MOBA_BLOCK
    h = pl.program_id(1)
    qi = pl.program_id(2)
    far_bias = rb_ref[REL_BUCKETS - 1, h]

    @pl.when(qi == 0)
    def _():
        r = lax.broadcasted_iota(I32, (n_blocks, n_blocks * blk), 0)
        c = lax.broadcasted_iota(I32, (n_blocks, n_blocks * blk), 1)
        pool = jnp.where(c // blk == r, 1.0, 0.0).astype(BF16)
        kmean_ref[...] = _dot(pool, k_ref[0]) * (1.0 / blk)
        row = lax.broadcasted_iota(I32, (blk, blk), 0)
        col = lax.broadcasted_iota(I32, (blk, blk), 1)
        thresholds = _t5_bucket_thresholds()
        for slot, dist in ((0, row - col + blk), (1, row - col)):
            val = jnp.full((blk, blk), rb_ref[0, h], F32)
            for b in range(1, REL_BUCKETS):
                val = jnp.where(dist >= thresholds[b], rb_ref[b, h], val)
            if slot == 1:
                val = jnp.where(dist >= 0, val, NEG)
            bias_ref[slot] = val

    q = q_ref[0]
    km_hi = kmean_ref[...].astype(BF16)
    km_lo = (kmean_ref[...] - km_hi.astype(F32)).astype(BF16)
    gate = _dot_nt(q, km_hi) + _dot_nt(q, km_lo)
    lane = lax.broadcasted_iota(I32, gate.shape, 1)
    gate = jnp.where(lane < qi, gate, NEG)
    sel = jnp.zeros(gate.shape, F32)
    for _ in range(MOBA_TOPK):
        best = jnp.max(gate, axis=1, keepdims=True)
        first = jnp.min(jnp.where(gate == best, lane, n_blocks), axis=1, keepdims=True)
        pick = jnp.logical_and(lane == first, best > 0.5 * NEG)
        sel = jnp.where(pick, 1.0, sel)
        gate = jnp.where(lane == first, NEG, gate)
    sel_ref[...] = sel

    _softmax_init(m_ref, l_ref, acc_ref)
    zero_shift = jnp.zeros((blk, 1), F32)

    def sel_col(j):
        lanes = lax.broadcasted_iota(I32, (blk, n_blocks), 1)
        return jnp.sum(jnp.where(lanes == j, sel_ref[...], 0.0), axis=1, keepdims=True) > 0.5

    def far_block(j, carry):
        ks = pl.ds(pl.multiple_of(j * blk, blk), blk)
        u = _dot_nt(q, k_ref[0, ks, :]) + jnp.where(sel_col(j), far_bias, NEG)
        _softmax_step(u, zero_shift, v_ref[0, ks, :], m_ref, l_ref, acc_ref)
        return carry

    lax.fori_loop(0, jnp.maximum(qi - 1, 0), far_block, 0)

    @pl.when(qi >= 1)
    def _():
        j = qi - 1
        ks = pl.ds(pl.multiple_of(j * blk, blk), blk)
        u = _dot_nt(q, k_ref[0, ks, :]) + bias_ref[0] + jnp.where(sel_col(j), 0.0, NEG)
        _softmax_step(u, zero_shift, v_ref[0, ks, :], m_ref, l_ref, acc_ref)

    ks = pl.ds(pl.multiple_of(qi * blk, blk), blk)
    u = _dot_nt(q, k_ref[0, ks, :]) + bias_ref[1]
    _softmax_step(u, zero_shift, v_ref[0, ks, :], m_ref, l_ref, acc_ref)
    o_ref[0] = (acc_ref[...] / l_ref[...]).astype(o_ref.dtype)


def _moba_attention(qkv, rel_bias):
    bsz, seq, _ = qkv.shape
    nh = MIX_HEADS
    blk = MOBA_BLOCK
    assert seq % blk == 0
    n_blocks = seq // blk
    grid_spec = pltpu.PrefetchScalarGridSpec(
        num_scalar_prefetch=0,
        grid=(bsz, nh, n_blocks),
        in_specs=[
            pl.BlockSpec(memory_space=pltpu.SMEM),
            pl.BlockSpec((1, blk, HEAD_DIM), lambda b, h, i: (b, i, h)),
            pl.BlockSpec((1, seq, HEAD_DIM), lambda b, h, i: (b, 0, nh + h)),
            pl.BlockSpec((1, seq, HEAD_DIM), lambda b, h, i: (b, 0, 2 * nh + h)),
        ],
        out_specs=pl.BlockSpec((1, blk, HEAD_DIM), lambda b, h, i: (b, i, h)),
        scratch_shapes=[
            pltpu.VMEM((n_blocks, HEAD_DIM), F32),
            pltpu.VMEM((2, blk, blk), F32),
            pltpu.VMEM((blk, n_blocks), F32),
            pltpu.VMEM((blk, 1), F32), pltpu.VMEM((blk, 1), F32), pltpu.VMEM((blk, HEAD_DIM), F32),
        ],
    )
    return pl.pallas_call(
        functools.partial(_moba_body, n_blocks=n_blocks),
        grid_spec=grid_spec,
        out_shape=jax.ShapeDtypeStruct((bsz, seq, nh * HEAD_DIM), BF16),
        compiler_params=_cparams("parallel", "parallel", "arbitrary"),
        name="moba_attention",
    )(rel_bias, qkv, qkv, qkv)


def _mem_attn_body(q_ref, kv_ref, o_ref):
    width = MEM_HEADS * HEAD_DIM
    for hh in range(MEM_HEADS):
        cs = slice(hh * HEAD_DIM, (hh + 1) * HEAD_DIM)
        q = q_ref[0, :, cs].astype(BF16)
        k = kv_ref[0, :, cs]
        v = kv_ref[0, :, width + hh * HEAD_DIM:width + (hh + 1) * HEAD_DIM]
        s = _dot_nt(q, k)
        p = jnp.exp(s - jnp.max(s, axis=-1, keepdims=True))
        o = _dot(p.astype(BF16), v) / jnp.sum(p, axis=-1, keepdims=True)
        o_ref[0, :, cs] = o.astype(o_ref.dtype)


def _memory_attention(proj, q_col_block, kv, tq=512):
    bsz, seq, _ = proj.shape
    tq = min(tq, seq)
    width = MEM_HEADS * HEAD_DIM
    n_mem = kv.shape[1]
    return pl.pallas_call(
        _mem_attn_body,
        grid=(bsz, seq // tq),
        in_specs=[
            pl.BlockSpec((1, tq, width), lambda b, i: (b, i, q_col_block)),
            pl.BlockSpec((1, n_mem, 2 * width), lambda b, i: (b, 0, 0)),
        ],
        out_specs=pl.BlockSpec((1, tq, width), lambda b, i: (b, i, 0)),
        out_shape=jax.ShapeDtypeStruct((bsz, seq, width), BF16),
        compiler_params=_cparams("parallel", "parallel"),
        name="memory_attention",
    )(proj, kv)


def _out_ln_body(mix_ref, mem_ref, x_ref, w1_ref, w2_ref, g_ref, b_ref, o_ref, *, alpha):
    hproj = _dot(mix_ref[...], w1_ref[...]) + _dot(mem_ref[...], w2_ref[...])
    o_ref[...] = _layer_norm(alpha * x_ref[...] + hproj, g_ref[...], b_ref[...])


def _out_proj_ln(mix, mem_out, x, w_out, g, b, alpha, tm=512):
    t, d = x.shape
    tm = min(tm, t)
    wm = mix.shape[1]
    we = mem_out.shape[1]
    return pl.pallas_call(
        functools.partial(_out_ln_body, alpha=alpha),
        grid=(t // tm,),
        in_specs=[
            pl.BlockSpec((tm, wm), lambda i: (i, 0)),
            pl.BlockSpec((tm, we), lambda i: (i, 0)),
            pl.BlockSpec((tm, d), lambda i: (i, 0)),
            pl.BlockSpec((wm, d), lambda i: (0, 0)),
            pl.BlockSpec((we, d), lambda i: (0, 0)),
            pl.BlockSpec((1, d), lambda i: (0, 0)),
            pl.BlockSpec((1, d), lambda i: (0, 0)),
        ],
        out_specs=pl.BlockSpec((tm, d), lambda i: (i, 0)),
        out_shape=jax.ShapeDtypeStruct((t, d), F32),
        compiler_params=_cparams("parallel"),
        name="out_proj_ln",
    )(mix, mem_out, x, w_out[:wm].astype(BF16), w_out[wm:].astype(BF16), g.reshape(1, d), b.reshape(1, d))


def _router_body(x_ref, wt_ref, b_ref, idx_ref, gate_ref, cnt_ref, run_ref, *, tb):
    @pl.when(pl.program_id(0) == 0)
    def _():
        run_ref[...] = jnp.zeros_like(run_ref)

    x = x_ref[...]
    xh = x.astype(BF16)
    xl = (x - xh.astype(F32)).astype(BF16)
    w = wt_ref[...]
    wh = w.astype(BF16)
    wl = (w - wh.astype(F32)).astype(BF16)
    logits = _dot_nt(wh, xh) + _dot_nt(wh, xl) + _dot_nt(wl, xh) + b_ref[...]

    n_e = logits.shape[0]
    eid = lax.broadcasted_iota(I32, logits.shape, 0)
    vals, idxs, hots = [], [], []
    multi = jnp.zeros(logits.shape, F32)
    for _ in range(TOP_K):
        best = jnp.max(logits, axis=0, keepdims=True)
        first = jnp.min(jnp.where(logits == best, eid, n_e), axis=0, keepdims=True)
        hot = eid == first
        vals.append(best)
        idxs.append(first)
        hots.append(hot)
        multi = jnp.where(hot, 1.0, multi)
        logits = jnp.where(hot, -jnp.inf, logits)

    exps = [jnp.exp(v - vals[0]) for v in vals]
    den = exps[0]
    for e in exps[1:]:
        den = den + e

    r = lax.broadcasted_iota(I32, (tb, tb), 0)
    c = lax.broadcasted_iota(I32, (tb, tb), 1)
    before = jnp.where(r < c, 1.0, 0.0).astype(BF16)
    pos = _dot(multi.astype(BF16), before) + run_ref[:, 0:1]
    for kk in range(TOP_K):
        idx_ref[kk:kk + 1, :] = idxs[kk]
        rank = jnp.sum(jnp.where(hots[kk], pos, 0.0), axis=0, keepdims=True)
        idx_ref[TOP_K + kk:TOP_K + kk + 1, :] = rank.astype(I32)
        gate_ref[kk:kk + 1, :] = exps[kk] / den
    gate_ref[TOP_K:2 * TOP_K, :] = jnp.zeros((TOP_K, tb), F32)
    run_ref[...] = run_ref[...] + jnp.sum(multi, axis=1, keepdims=True)
    cnt_ref[...] = run_ref[...]


def _router(x, router_w, router_b, tb=512):
    t, d = x.shape
    tb = min(tb, t)
    n_e = router_w.shape[1]
    return pl.pallas_call(
        functools.partial(_router_body, tb=tb),
        grid=(t // tb,),
        in_specs=[
            pl.BlockSpec((tb, d), lambda i: (i, 0)),
            pl.BlockSpec((n_e, d), lambda i: (0, 0)),
            pl.BlockSpec((n_e, 1), lambda i: (0, 0)),
        ],
        out_specs=[
            pl.BlockSpec((2 * TOP_K, tb), lambda i: (0, i)),
            pl.BlockSpec((2 * TOP_K, tb), lambda i: (0, i)),
            pl.BlockSpec((n_e, HEAD_DIM), lambda i: (0, 0)),
        ],
        out_shape=[
            jax.ShapeDtypeStruct((2 * TOP_K, t), I32),
            jax.ShapeDtypeStruct((2 * TOP_K, t), F32),
            jax.ShapeDtypeStruct((n_e, HEAD_DIM), F32),
        ],
        scratch_shapes=[pltpu.VMEM((n_e, HEAD_DIM), F32)],
        compiler_params=_cparams("arbitrary"),
        name="moe_router",
    )(x, router_w.T, router_b.reshape(n_e, 1))


def _dispatch_body(dest_ref, x_hbm, xs_in, xs_hbm, sem, *, tb, n_tok):
    del xs_in
    base = pl.program_id(0) * tb

    def issue(t, carry):
        for kk in range(TOP_K):
            d = dest_ref[kk * n_tok + base + t]
            pltpu.make_async_copy(x_hbm.at[pl.ds(base + t, 1)], xs_hbm.at[pl.ds(d, 1)], sem).start()
        return carry

    lax.fori_loop(0, tb, issue, 0)
    for kk in range(TOP_K):
        pltpu.make_async_copy(x_hbm.at[pl.ds(0, tb)], xs_hbm.at[pl.ds(0, tb)], sem).wait()


def _dispatch(x, dest_flat, n_rows, tb=256):
    t, d = x.shape
    tb = min(tb, t)
    grid_spec = pltpu.PrefetchScalarGridSpec(
        num_scalar_prefetch=1,
        grid=(t // tb,),
        in_specs=[pl.BlockSpec(memory_space=pl.ANY), pl.BlockSpec(memory_space=pl.ANY)],
        out_specs=pl.BlockSpec(memory_space=pl.ANY),
        scratch_shapes=[pltpu.SemaphoreType.DMA(())],
    )
    return pl.pallas_call(
        functools.partial(_dispatch_body, tb=tb, n_tok=t),
        grid_spec=grid_spec,
        out_shape=jax.ShapeDtypeStruct((n_rows, d), x.dtype),
        input_output_aliases={2: 0},
        compiler_params=pltpu.CompilerParams(dimension_semantics=("arbitrary",), has_side_effects=True),
        name="moe_dispatch",
    )(dest_flat, x, jnp.zeros((n_rows, d), x.dtype))


def _expert_body(be_ref, nu_ref, x_ref, wg_ref, wl_ref, bg_ref, bl_ref, wd_ref, bd_ref, o_ref):
    del be_ref

    @pl.when(pl.program_id(0) < nu_ref[0])
    def _():
        xb = x_ref[...].astype(BF16)
        glu = jnp.minimum(_dot(xb, wg_ref[0]) + bg_ref[0], SWIGLU_LIMIT)
        lin = jnp.clip(_dot(xb, wl_ref[0]) + bl_ref[0], -SWIGLU_LIMIT, SWIGLU_LIMIT)
        act = glu * _sigmoid(SWIGLU_ALPHA * glu) * (lin + 1.0)
        o_ref[...] = _dot(act.astype(BF16), wd_ref[0]) + bd_ref[0]

    @pl.when(pl.program_id(0) >= nu_ref[0])
    def _():
        o_ref[...] = jnp.zeros_like(o_ref)


def _expert_mlp(xs, blk_expert, n_used, wg, wl, bg, bl, wd, bd):
    p_rows, d = xs.shape
    f = wg.shape[2]
    n_blk = p_rows // MOE_BLOCK

    def blk(i, be, nu):
        return jnp.minimum(i, nu[0] - 1)

    def w_spec(shape):
        return pl.BlockSpec((1,) + shape, lambda i, be, nu: (be[blk(i, be, nu)], 0, 0))

    grid_spec = pltpu.PrefetchScalarGridSpec(
        num_scalar_prefetch=2,
        grid=(n_blk,),
        in_specs=[
            pl.BlockSpec((MOE_BLOCK, d), lambda i, be, nu: (blk(i, be, nu), 0)),
            w_spec((d, f)), w_spec((d, f)), w_spec((1, f)), w_spec((1, f)),
            w_spec((f, d)), w_spec((1, d)),
        ],
        out_specs=pl.BlockSpec((MOE_BLOCK, d), lambda i, be, nu: (i, 0)),
    )
    return pl.pallas_call(
        _expert_body,
        grid_spec=grid_spec,
        out_shape=jax.ShapeDtypeStruct((p_rows, d), F32),
        compiler_params=_cparams("arbitrary"),
        name="moe_experts",
    )(blk_expert, n_used, xs, wg, wl, bg, bl, wd, bd)


def _combine_body(dest_ref, ys_hbm, gate_ref, x_ref, g_ref, b_ref, o_ref, buf_ref, sem_ref,
                  *, tb, n_tok, alpha):
    i = pl.program_id(0)
    n_steps = pl.num_programs(0)

    def issue(step, slot):
        base = step * tb

        def one(t, carry):
            for kk in range(TOP_K):
                d = dest_ref[kk * n_tok + base + t]
                pltpu.make_async_copy(ys_hbm.at[pl.ds(d, 1)], buf_ref.at[slot, kk, pl.ds(t, 1)],
                                      sem_ref.at[slot]).start()
            return carry

        lax.fori_loop(0, tb, one, 0)

    @pl.when(i == 0)
    def _():
        issue(0, 0)

    slot = lax.rem(i, 2)

    @pl.when(i + 1 < n_steps)
    def _():
        issue(i + 1, 1 - slot)

    for kk in range(TOP_K):
        pltpu.make_async_copy(ys_hbm.at[pl.ds(0, tb)], buf_ref.at[slot, kk], sem_ref.at[slot]).wait()

    g8 = gate_ref[...]
    gpad = jnp.concatenate([g8, jnp.zeros((HEAD_DIM - g8.shape[0], tb), F32)], axis=0)
    gt = gpad.T
    f = gt[:, 0:1] * buf_ref[slot, 0]
    for kk in range(1, TOP_K):
        f = f + gt[:, kk:kk + 1] * buf_ref[slot, kk]
    o_ref[...] = _layer_norm(alpha * x_ref[...] + f, g_ref[...], b_ref[...])


def _combine_ln(ys, dest_flat, gates, x, g, b, alpha, tb=256):
    t, d = x.shape
    tb = min(tb, t)
    grid_spec = pltpu.PrefetchScalarGridSpec(
        num_scalar_prefetch=1,
        grid=(t // tb,),
        in_specs=[
            pl.BlockSpec(memory_space=pl.ANY),
            pl.BlockSpec((2 * TOP_K, tb), lambda i, dest: (0, i)),
            pl.BlockSpec((tb, d), lambda i, dest: (i, 0)),
            pl.BlockSpec((1, d), lambda i, dest: (0, 0)),
            pl.BlockSpec((1, d), lambda i, dest: (0, 0)),
        ],
        out_specs=pl.BlockSpec((tb, d), lambda i, dest: (i, 0)),
        scratch_shapes=[pltpu.VMEM((2, TOP_K, tb, d), F32), pltpu.SemaphoreType.DMA((2,))],
    )
    return pl.pallas_call(
        functools.partial(_combine_body, tb=tb, n_tok=t, alpha=alpha),
        grid_spec=grid_spec,
        out_shape=jax.ShapeDtypeStruct((t, d), F32),
        compiler_params=_cparams("arbitrary"),
        name="moe_combine_ln",
    )(dest_flat, ys, gates, x, g.reshape(1, d), b.reshape(1, d))


def _moe_ln(x, router_w, router_b, w_up, b_up, w_down, b_down, g, b, alpha):
    t, d = x.shape
    n_e = router_w.shape[1]
    idx_rank, gates, cnt = _router(x, router_w, router_b)
    counts = cnt[:, 0].astype(I32)
    padded = ((counts + MOE_BLOCK - 1) // MOE_BLOCK) * MOE_BLOCK
    p_ends = jnp.cumsum(padded)
    p_starts = p_ends - padded
    n_blk = (t * TOP_K + MOE_BLOCK - 1) // MOE_BLOCK + n_e
    dest = p_starts[idx_rank[:TOP_K]] + idx_rank[TOP_K:]
    dest_flat = dest.reshape(-1)
    blk_expert = jnp.clip(
        jnp.searchsorted(p_ends, jnp.arange(n_blk, dtype=I32) * MOE_BLOCK, side='right'), 0, n_e - 1
    ).astype(I32)
    n_used = (p_ends[-1:] // MOE_BLOCK).astype(I32)

    xs = _dispatch(x, dest_flat, n_blk * MOE_BLOCK)
    wg = w_up[:, :, 0::2].astype(BF16)
    wl = w_up[:, :, 1::2].astype(BF16)
    bg = b_up[:, None, 0::2]
    bl = b_up[:, None, 1::2]
    ys = _expert_mlp(xs, blk_expert, n_used, wg, wl, bg, bl, w_down.astype(BF16), b_down[:, None, :])
    return _combine_ln(ys, dest_flat, gates, x, g, b, alpha)


def kernel(x, mem, hgrn_w_in, hgrn_lb, hgrn_norm_g, fox_w_in, fox_b_f, moba_w_in, rel_bias, w_mem_kv, w_out,
           ln1_g, ln1_b, router_w, router_b, w_up, b_up, w_down, b_down, ln2_g, ln2_b):
    bsz, seq, d = x.shape
    depth = w_out.shape[0]
    t = bsz * seq
    mix_w = MIX_HEADS * HEAD_DIM
    mem_w = MEM_HEADS * HEAD_DIM
    alpha = (2 * depth) ** 0.25
    scale = HEAD_DIM ** -0.5
    n_mem = mem.shape[1]
    mem2 = mem.reshape(bsz * n_mem, d)

    def scale_vec(n_scaled_front, n_plain, n_scaled_back):
        return jnp.concatenate([
            jnp.full((1, n_scaled_front), scale, F32), jnp.ones((1, n_plain), F32),
            jnp.full((1, n_scaled_back), scale, F32)], axis=1)

    x2 = x.reshape(t, d)
    for i in range(depth):
        kind, j = i % 3, i // 3
        if kind == 0:
            w_in = hgrn_w_in[j].astype(BF16)
            proj = _matmul(x2, w_in, scale_vec(0, 4 * mix_w, mem_w), F32, 1024, 512)
            proj = proj.reshape(bsz, seq, -1)
            mix = _hgrn_mixer(proj, hgrn_lb, hgrn_norm_g[j], j)
            memq_block = 4 * mix_w // mem_w
        elif kind == 1:
            w = fox_w_in[j]
            w_in = jnp.concatenate([w[:, :3 * mix_w], w[:, 3 * mix_w + MIX_HEADS:]], axis=1).astype(BF16)
            proj = _matmul(x2, w_in, scale_vec(mix_w, 2 * mix_w, mem_w), BF16, 1024, 512)
            proj = proj.reshape(bsz, seq, -1)
            c_col, c_row = _fox_gate_cumsum(x2.reshape(bsz, seq, d), w[:, 3 * mix_w:3 * mix_w + MIX_HEADS],
                                            fox_b_f[j])
            mix = _fox_attention(proj, c_col, c_row)
            memq_block = 3 * mix_w // mem_w
        else:
            w_in = moba_w_in[j].astype(BF16)
            proj = _matmul(x2, w_in, scale_vec(mix_w, 2 * mix_w, mem_w), BF16, 1024, 512)
            proj = proj.reshape(bsz, seq, -1)
            mix = _moba_attention(proj, rel_bias)
            memq_block = 3 * mix_w // mem_w
        kv = _matmul(mem2, w_mem_kv[i].astype(BF16), jnp.ones((1, 2 * mem_w), F32), BF16, 512, 512)
        mem_out = _memory_attention(proj, memq_block, kv.reshape(bsz, n_mem, 2 * mem_w))
        x2 = _out_proj_ln(mix.reshape(t, mix_w), mem_out.reshape(t, mem_w), x2, w_out[i],
                          ln1_g[i], ln1_b[i], alpha)
        x2 = _moe_ln(x2, router_w[i], router_b[i], w_up[i], b_up[i], w_down[i], b_down[i],
                     ln2_g[i], ln2_b[i], alpha)
    return x2.reshape(bsz, seq, d)
```

```python
import functools
import math

import numpy as np
import jax
import jax.numpy as jnp
from jax import lax
from jax.experimental import pallas as pl
from jax.experimental.pallas import tpu as pltpu

F32 = jnp.float32
BF16 = jnp.bfloat16
I32 = jnp.int32

HEAD_DIM = 128
MIX_HEADS = 8
MEM_HEADS = 4
HGRN_CHUNK = 64
HGRN_SUB = 16
HGRN_EXP_CLAMP = 80.0
MOBA_BLOCK = 256
MOBA_TOPK = 3
REL_BUCKETS = 32
REL_MAX_DIST = 128
N_EXPERTS = 32
TOP_K = 4
MOE_BLOCK = 256
SWIGLU_LIMIT = 7.0
SWIGLU_ALPHA = 1.702
LN_EPS = 1e-5
RMS_EPS = 1e-6
NEG = -1e30
VMEM_LIMIT = 56 * 1024 * 1024

NT_DIMS = (((1,), (1,)), ((), ()))
TN_DIMS = (((0,), (0,)), ((), ()))


def _cparams(*sem):
    return pltpu.CompilerParams(dimension_semantics=sem, vmem_limit_bytes=VMEM_LIMIT)


def _dot(a, b):
    return jnp.dot(a, b, preferred_element_type=F32)


def _dot_nt(a, b):
    return lax.dot_general(a, b, NT_DIMS, preferred_element_type=F32)


def _dot_tn(a, b):
    return lax.dot_general(a, b, TN_DIMS, preferred_element_type=F32)


def _split3(a):
    hi = a.astype(BF16)
    r = a - hi.astype(F32)
    mid = r.astype(BF16)
    lo = (r - mid.astype(F32)).astype(BF16)
    return hi, mid, lo


def _sigmoid(x):
    return 1.0 / (1.0 + jnp.exp(-x))


def _layer_norm(z, g, b):
    mu = jnp.mean(z, axis=-1, keepdims=True)
    zc = z - mu
    var = jnp.mean(zc * zc, axis=-1, keepdims=True)
    return zc * lax.rsqrt(var + LN_EPS) * g + b


def _mm_body(x_ref, w_ref, s_ref, o_ref, xb_ref):
    @pl.when(pl.program_id(1) == 0)
    def _():
        xb_ref[...] = x_ref[...].astype(BF16)

    acc = _dot(xb_ref[...], w_ref[...])
    o_ref[...] = (acc * s_ref[...]).astype(o_ref.dtype)


def _matmul(x, w, col_scale, out_dtype, tm, tn):
    m, k = x.shape
    n = w.shape[1]
    tm = min(tm, m)
    tn = min(tn, n)
    return pl.pallas_call(
        _mm_body,
        grid=(m // tm, n // tn),
        in_specs=[
            pl.BlockSpec((tm, k), lambda i, j: (i, 0)),
            pl.BlockSpec((k, tn), lambda i, j: (0, j)),
            pl.BlockSpec((1, tn), lambda i, j: (0, j)),
        ],
        out_specs=pl.BlockSpec((tm, tn), lambda i, j: (i, j)),
        out_shape=jax.ShapeDtypeStruct((m, n), out_dtype),
        scratch_shapes=[pltpu.VMEM((tm, k), BF16)],
        compiler_params=_cparams("parallel", "arbitrary"),
        name="proj_matmul",
    )(x, w, col_scale)


def _hgrn_body(q_ref, z_ref, i_ref, g_ref, lbp_ref, ng_ref, o_ref, st_ref, *, layer_j, tb):
    c_len, sb = HGRN_CHUNK, HGRN_SUB
    n_sub = c_len // sb

    @pl.when(pl.program_id(2) == 0)
    def _():
        st_ref[...] = jnp.zeros_like(st_ref)

    lbp = lbp_ref[...]
    ex = jnp.exp(lbp - jnp.max(lbp, axis=0, keepdims=True))
    p = ex / jnp.sum(ex, axis=0, keepdims=True)
    lb = jnp.zeros((1, HEAD_DIM), F32)
    for r in range(1, layer_j + 1):
        lb = lb + p[r:r + 1, :]
    log_lb = jnp.log(lb)
    log1m_lb = jnp.log1p(-lb)
    one_m_lb = 1.0 - lb
    ng = ng_ref[...]

    row = lax.broadcasted_iota(I32, (c_len, c_len), 0)
    col = lax.broadcasted_iota(I32, (c_len, c_len), 1)
    sub_start = (row // sb) * sb
    one = jnp.ones((c_len, c_len), F32)
    zero = jnp.zeros((c_len, c_len), F32)
    m_tri = jnp.where(col <= row, one, zero)
    m_in = jnp.where(col > sub_start, m_tri, zero)
    m_dec = jnp.where(col > row, one, zero)
    m_ke = jnp.where(col <= sub_start + (sb - 1), m_dec, zero)
    stack = jnp.concatenate([m_tri, m_in, m_ke, m_dec], axis=0).astype(BF16)
    same_sub = (row // sb) == (col // sb)
    diag_mask = jnp.logical_and(same_sub, col <= row)
    row_sub = row // sb
    krow_sub = lax.broadcasted_iota(I32, (c_len, HEAD_DIM), 0) // sb

    for c in range(tb // c_len):
        sl = pl.ds(c * c_len, c_len)
        z = z_ref[0, sl, :]
        e = jnp.exp(-jnp.abs(z))
        inv = 1.0 / (1.0 + e)
        sig_neg = jnp.where(z >= 0, e, 1.0) * inv
        log_sig = jnp.minimum(z, 0.0) - jnp.log1p(e)
        a2 = log1m_lb + log_sig
        log_f = jnp.maximum(log_lb, a2) + jnp.log1p(jnp.exp(-jnp.abs(log_lb - a2)))
        k = one_m_lb * sig_neg
        qv = q_ref[0, sl, :]
        qf = qv * _sigmoid(qv)
        v = i_ref[0, sl, :]
        vb = v.astype(BF16)

        pre = None
        for piece in _split3(log_f):
            t = _dot(stack, piece)
            pre = t if pre is None else pre + t
        b = pre[0:c_len]
        d_in = pre[c_len:2 * c_len]
        d_ke = pre[2 * c_len:3 * c_len]
        d_dec = pre[3 * c_len:4 * c_len]

        q_in = qf * jnp.exp(d_in)
        k_diag = (k * jnp.exp(jnp.minimum(-d_in, HGRN_EXP_CLAMP))).astype(BF16)
        k_end = k * jnp.exp(d_ke)
        k_dec = (k * jnp.exp(d_dec)).astype(BF16)
        q_all = (qf * jnp.exp(b)).astype(BF16)
        b_ref = b - d_in

        a = jnp.where(diag_mask, _dot_nt(q_in.astype(BF16), k_diag), 0.0)
        for j in range(n_sub - 1):
            b_end = b[(j + 1) * sb - 1:(j + 1) * sb, :]
            cross = jnp.exp(jnp.minimum(b_ref - b_end, 0.0))
            lhs = (q_in * cross).astype(BF16)
            k_j = jnp.where(krow_sub == j, k_end, 0.0).astype(BF16)
            a = a + jnp.where(row_sub > j, _dot_nt(lhs, k_j), 0.0)

        st = st_ref[...]
        o = _dot(a.astype(BF16), vb) + _dot_nt(q_all, st.astype(BF16))
        st_ref[...] = st * jnp.exp(b[c_len - 1:c_len, :]) + _dot_tn(vb, k_dec)

        o = o * lax.rsqrt(jnp.mean(o * o, axis=-1, keepdims=True) + RMS_EPS)
        gv = g_ref[0, sl, :]
        o_ref[0, sl, :] = (o * ng * (gv * _sigmoid(gv))).astype(o_ref.dtype)


def _hgrn_mixer(proj, lb_param, norm_g, layer_j, tb=256):
    bsz, seq, _ = proj.shape
    tb = min(tb, seq)
    n_layers = lb_param.shape[0]

    def col_block(off):
        return pl.BlockSpec((1, tb, HEAD_DIM), lambda b, h, t: (b, t, off + h))

    return pl.pallas_call(
        functools.partial(_hgrn_body, layer_j=layer_j, tb=tb),
        grid=(bsz, MIX_HEADS, seq // tb),
        in_specs=[
            col_block(0), col_block(MIX_HEADS), col_block(2 * MIX_HEADS), col_block(3 * MIX_HEADS),
            pl.BlockSpec((n_layers, HEAD_DIM), lambda b, h, t: (0, h)),
            pl.BlockSpec((1, HEAD_DIM), lambda b, h, t: (0, h)),
        ],
        out_specs=pl.BlockSpec((1, tb, HEAD_DIM), lambda b, h, t: (b, t, h)),
        out_shape=jax.ShapeDtypeStruct((bsz, seq, MIX_HEADS * HEAD_DIM), BF16),
        scratch_shapes=[pltpu.VMEM((HEAD_DIM, HEAD_DIM), F32)],
        compiler_params=_cparams("parallel", "parallel", "arbitrary"),
        name="hgrn2_mixer",
    )(proj, proj, proj, proj, lb_param, norm_g.reshape(1, -1))


def _log_sigmoid(z):
    return jnp.minimum(z, 0.0) - jnp.log1p(jnp.exp(-jnp.abs(z)))


def _fox_gate_body(x_ref, w_ref, wt_ref, bc_ref, br_ref, ccol_ref, crow_ref, carc_ref, carr_ref, *, tb):
    @pl.when(pl.program_id(1) == 0)
    def _():
        carc_ref[...] = jnp.zeros_like(carc_ref)
        carr_ref[...] = jnp.zeros_like(carr_ref)

    xb = x_ref[0].astype(BF16)
    ls_col = _log_sigmoid(_dot(xb, w_ref[...]) + bc_ref[...])
    ls_row = _log_sigmoid(_dot_nt(wt_ref[...], xb) + br_ref[...])
    row = lax.broadcasted_iota(I32, (tb, tb), 0)
    col = lax.broadcasted_iota(I32, (tb, tb), 1)
    lower = jnp.where(col <= row, 1.0, 0.0).astype(BF16)
    upper = jnp.where(row <= col, 1.0, 0.0).astype(BF16)
    c_col = carc_ref[...]
    for piece in _split3(ls_col):
        c_col = c_col + _dot(lower, piece)
    c_row = carr_ref[...]
    for piece in _split3(ls_row):
        c_row = c_row + _dot(piece, upper)
    ccol_ref[0] = c_col
    crow_ref[0] = c_row
    carc_ref[...] = c_col[tb - 1:tb, :]
    carr_ref[...] = c_row[:, tb - 1:tb]


def _fox_gate_cumsum(x, w_f, b_f, tb=512):
    bsz, seq, d = x.shape
    h = w_f.shape[1]
    tb = min(tb, seq)
    return pl.pallas_call(
        functools.partial(_fox_gate_body, tb=tb),
        grid=(bsz, seq // tb),
        in_specs=[
            pl.BlockSpec((1, tb, d), lambda b, t: (b, t, 0)),
            pl.BlockSpec((d, h), lambda b, t: (0, 0)),
            pl.BlockSpec((h, d), lambda b, t: (0, 0)),
            pl.BlockSpec((1, h), lambda b, t: (0, 0)),
            pl.BlockSpec((h, 1), lambda b, t: (0, 0)),
        ],
        out_specs=[
            pl.BlockSpec((1, tb, h), lambda b, t: (b, t, 0)),
            pl.BlockSpec((1, h, tb), lambda b, t: (b, 0, t)),
        ],
        out_shape=[
            jax.ShapeDtypeStruct((bsz, seq, h), F32),
            jax.ShapeDtypeStruct((bsz, h, seq), F32),
        ],
        scratch_shapes=[pltpu.VMEM((1, h), F32), pltpu.VMEM((h, 1), F32)],
        compiler_params=_cparams("parallel", "arbitrary"),
        name="fox_gate_cumsum",
    )(x, w_f.astype(BF16), w_f.T.astype(BF16), b_f.reshape(1, h), b_f.reshape(h, 1))


ATTN_ROWS = 128


def _softmax_probs(u, shift, rows, m_ref, l_ref):
    reps = u.shape[1] // HEAD_DIM
    m_old = m_ref[rows, :]
    if shift is None:
        m_new = jnp.maximum(m_old, jnp.max(u, axis=-1, keepdims=True))
        p = jnp.exp(u - jnp.tile(m_new, (1, reps)))
    else:
        m_new = jnp.maximum(m_old, jnp.max(u, axis=-1, keepdims=True) + shift)
        p = jnp.exp(u - jnp.tile(m_new - shift, (1, reps)))
    alpha = jnp.exp(m_old - m_new)
    l_ref[rows, :] = alpha * l_ref[rows, :] + jnp.sum(p, axis=-1, keepdims=True)
    m_ref[rows, :] = m_new
    return p.astype(BF16), alpha


def _accumulate(p, alpha, v, rows, acc_ref):
    acc_ref[rows, :] = alpha * acc_ref[rows, :] + _dot(p, v)


def _softmax_rows(u, shift, v, rows, m_ref, l_ref, acc_ref):
    p, alpha = _softmax_probs(u, shift, rows, m_ref, l_ref)
    _accumulate(p, alpha, v, rows, acc_ref)


def _softmax_init(m_ref, l_ref, acc_ref):
    m_ref[...] = jnp.full(m_ref.shape, NEG, F32)
    l_ref[...] = jnp.zeros_like(l_ref)
    acc_ref[...] = jnp.zeros_like(acc_ref)


def _fox_attn_body(q_ref, k_ref, v_ref, ccol_ref, crow_ref, o_ref, ct_ref, m_ref, l_ref, acc_ref, s_ref, *, tq):
    rs = min(ATTN_ROWS, tq)
    n_groups = tq // rs
    h = pl.program_id(1)
    qi = pl.program_id(2)
    ccol = ccol_ref[0]
    lane = lax.broadcasted_iota(I32, ccol.shape, 1)
    c_t = jnp.sum(jnp.where(lane == h, ccol, 0.0), axis=1, keepdims=True)
    ct_ref[...] = jnp.broadcast_to(c_t, ct_ref.shape)
    _softmax_init(m_ref, l_ref, acc_ref)

    def keys(kb):
        return pl.ds(pl.multiple_of(kb * tq, tq), tq)

    def logits(kb, r):
        rows = pl.ds(r * rs, rs)
        ks = keys(kb)
        s_ref[kb % 2, rows, :] = _dot_nt(q_ref[0, rows, :], k_ref[0, ks, :]) - crow_ref[0, pl.ds(h, 1), ks]

    def attend(kb, r, diagonal, prefetch):
        rows = pl.ds(r * rs, rs)
        u = s_ref[kb % 2, rows, :]
        if diagonal:
            row = lax.broadcasted_iota(I32, (rs, tq), 0) + r * rs
            col = lax.broadcasted_iota(I32, (rs, tq), 1)
            u = jnp.where(col <= row, u, NEG)
        if prefetch:
            logits(kb + 1, r)
        p, alpha = _softmax_probs(u, ct_ref[rows, :], rows, m_ref, l_ref)
        _accumulate(p, alpha, v_ref[0, keys(kb), :], rows, acc_ref)

    for r in range(n_groups):
        logits(0, r)

    def full_block(kb, carry):
        for r in range(n_groups):
            attend(kb, r, False, True)
        return carry

    lax.fori_loop(0, qi, full_block, 0)
    for r in range(n_groups):
        attend(qi, r, True, False)
    o_ref[0] = (acc_ref[...] / l_ref[...]).astype(o_ref.dtype)


def _fox_attention(qkv, c_col, c_row, tq=512):
    bsz, seq, _ = qkv.shape
    tq = min(tq, seq)
    nh = MIX_HEADS
    return pl.pallas_call(
        functools.partial(_fox_attn_body, tq=tq),
        grid=(bsz, nh, seq // tq),
        in_specs=[
            pl.BlockSpec((1, tq, HEAD_DIM), lambda b, h, i: (b, i, h)),
            pl.BlockSpec((1, seq, HEAD_DIM), lambda b, h, i: (b, 0, nh + h)),
            pl.BlockSpec((1, seq, HEAD_DIM), lambda b, h, i: (b, 0, 2 * nh + h)),
            pl.BlockSpec((1, tq, nh), lambda b, h, i: (b, i, 0)),
            pl.BlockSpec((1, nh, seq), lambda b, h, i: (b, 0, 0)),
        ],
        out_specs=pl.BlockSpec((1, tq, HEAD_DIM), lambda b, h, i: (b, i, h)),
        out_shape=jax.ShapeDtypeStruct((bsz, seq, nh * HEAD_DIM), BF16),
        scratch_shapes=[pltpu.VMEM((tq, HEAD_DIM), F32)] * 4 + [pltpu.VMEM((2, tq, tq), F32)],
        compiler_params=_cparams("parallel", "parallel", "arbitrary"),
        name="fox_attention",
    )(qkv, qkv, qkv, c_col, c_row)


def _t5_bucket_thresholds():
    n = np.arange(0, 4 * REL_MAX_DIST, dtype=np.int64)
    max_exact = REL_BUCKETS // 2
    ratio = np.log(np.maximum(n, 1).astype(np.float32) / np.float32(max_exact)) / np.float32(
        math.log(REL_MAX_DIST / max_exact))
    large = np.minimum(max_exact + (ratio * np.float32(REL_BUCKETS - max_exact)).astype(np.int32),
                       REL_BUCKETS - 1)
    bucket = np.where(n < max_exact, n, large)
    assert np.all(np.diff(bucket) >= 0) and bucket[-1] == REL_BUCKETS - 1
    return [int(np.argmax(bucket >= b)) for b in range(REL_BUCKETS)]


def _moba_body(rb_ref, q_ref, k_ref, v_ref, o_ref, kmean_ref, bias_ref, sel_ref, m_ref, l_ref, acc_ref,
               s_ref, *, n_blocks):
    blk = MOBA_BLOCK
    h = pl.program_id(1)
    qi = pl.program_id(2)
    far_bias = rb_ref[REL_BUCKETS - 1, h]

    @pl.when(qi == 0)
    def _():
        r = lax.broadcasted_iota(I32, (n_blocks, n_blocks * blk), 0)
        c = lax.broadcasted_iota(I32, (n_blocks, n_blocks * blk), 1)
        pool = jnp.where(c // blk == r, 1.0, 0.0).astype(BF16)
        kmean_ref[...] = _dot(pool, k_ref[0]) * (1.0 / blk)
        row = lax.broadcasted_iota(I32, (blk, blk), 0)
        col = lax.broadcasted_iota(I32, (blk, blk), 1)
        thresholds = _t5_bucket_thresholds()
        for slot, dist in ((0, row - col + blk), (1, row - col)):
            val = jnp.full((blk, blk), rb_ref[0, h], F32)
            for b in range(1, REL_BUCKETS):
                val = jnp.where(dist >= thresholds[b], rb_ref[b, h], val)
            if slot == 1:
                val = jnp.where(dist >= 0, val, NEG)
            bias_ref[slot] = val

    q = q_ref[0]
    km_hi = kmean_ref[...].astype(BF16)
    km_lo = (kmean_ref[...] - km_hi.astype(F32)).astype(BF16)
    gate = _dot_nt(q, km_hi) + _dot_nt(q, km_lo)
    lane = lax.broadcasted_iota(I32, gate.shape, 1)
    gate = jnp.where(lane < qi, gate, NEG)
    sel = jnp.zeros(gate.shape, F32)
    for _ in range(MOBA_TOPK):
        best = jnp.max(gate, axis=1, keepdims=True)
        first = jnp.min(jnp.where(gate == best, lane, n_blocks), axis=1, keepdims=True)
        pick = jnp.logical_and(lane == first, best > 0.5 * NEG)
        sel = jnp.where(pick, 1.0, sel)
        gate = jnp.where(lane == first, NEG, gate)
    sel_ref[...] = sel

    _softmax_init(m_ref, l_ref, acc_ref)
    rs = min(ATTN_ROWS, blk)
    n_groups = blk // rs

    def sel_col(j):
        lanes = lax.broadcasted_iota(I32, (blk, n_blocks), 1)
        return jnp.sum(jnp.where(lanes == j, sel_ref[...], 0.0), axis=1, keepdims=True) > 0.5

    def keys(j):
        return pl.ds(pl.multiple_of(j * blk, blk), blk)

    def logits(j, r):
        rows = pl.ds(r * rs, rs)
        s_ref[j % 2, rows, :] = _dot_nt(q_ref[0, rows, :], k_ref[0, keys(j), :])

    def attend(j, r, extra, prefetch):
        rows = pl.ds(r * rs, rs)
        u = s_ref[j % 2, rows, :] + extra
        if prefetch:
            logits(j + 1, r)
        p, alpha = _softmax_probs(u, None, rows, m_ref, l_ref)
        _accumulate(p, alpha, v_ref[0, keys(j), :], rows, acc_ref)

    for r in range(n_groups):
        logits(0, r)

    def far_block(j, carry):
        col_bias = jnp.where(sel_col(j), far_bias, NEG)
        for r in range(n_groups):
            attend(j, r, col_bias[r * rs:(r + 1) * rs], True)
        return carry

    lax.fori_loop(0, jnp.maximum(qi - 1, 0), far_block, 0)

    @pl.when(qi >= 1)
    def _():
        col_mask = jnp.where(sel_col(qi - 1), 0.0, NEG)
        for r in range(n_groups):
            attend(qi - 1, r, bias_ref[0, r * rs:(r + 1) * rs, :] + col_mask[r * rs:(r + 1) * rs], True)

    for r in range(n_groups):
        attend(qi, r, bias_ref[1, r * rs:(r + 1) * rs, :], False)
    o_ref[0] = (acc_ref[...] / l_ref[...]).astype(o_ref.dtype)


def _moba_attention(qkv, rel_bias):
    bsz, seq, _ = qkv.shape
    nh = MIX_HEADS
    blk = MOBA_BLOCK
    assert seq % blk == 0
    n_blocks = seq // blk
    grid_spec = pltpu.PrefetchScalarGridSpec(
        num_scalar_prefetch=0,
        grid=(bsz, nh, n_blocks),
        in_specs=[
            pl.BlockSpec(memory_space=pltpu.SMEM),
            pl.BlockSpec((1, blk, HEAD_DIM), lambda b, h, i: (b, i, h)),
            pl.BlockSpec((1, seq, HEAD_DIM), lambda b, h, i: (b, 0, nh + h)),
            pl.BlockSpec((1, seq, HEAD_DIM), lambda b, h, i: (b, 0, 2 * nh + h)),
        ],
        out_specs=pl.BlockSpec((1, blk, HEAD_DIM), lambda b, h, i: (b, i, h)),
        scratch_shapes=[
            pltpu.VMEM((n_blocks, HEAD_DIM), F32),
            pltpu.VMEM((2, blk, blk), F32),
            pltpu.VMEM((blk, n_blocks), F32),
            pltpu.VMEM((blk, HEAD_DIM), F32), pltpu.VMEM((blk, HEAD_DIM), F32), pltpu.VMEM((blk, HEAD_DIM), F32),
            pltpu.VMEM((2, blk, blk), F32),
        ],
    )
    return pl.pallas_call(
        functools.partial(_moba_body, n_blocks=n_blocks),
        grid_spec=grid_spec,
        out_shape=jax.ShapeDtypeStruct((bsz, seq, nh * HEAD_DIM), BF16),
        compiler_params=_cparams("parallel", "parallel", "arbitrary"),
        name="moba_attention",
    )(rel_bias, qkv, qkv, qkv)


def _mem_attn_body(q_ref, kv_ref, o_ref):
    width = MEM_HEADS * HEAD_DIM
    for hh in range(MEM_HEADS):
        cs = slice(hh * HEAD_DIM, (hh + 1) * HEAD_DIM)
        q = q_ref[0, :, cs].astype(BF16)
        k = kv_ref[0, :, cs]
        v = kv_ref[0, :, width + hh * HEAD_DIM:width + (hh + 1) * HEAD_DIM]
        s = _dot_nt(q, k)
        p = jnp.exp(s - jnp.max(s, axis=-1, keepdims=True))
        o = _dot(p.astype(BF16), v) / jnp.sum(p, axis=-1, keepdims=True)
        o_ref[0, :, cs] = o.astype(o_ref.dtype)


def _memory_attention(proj, q_col_block, kv, tq=512):
    bsz, seq, _ = proj.shape
    tq = min(tq, seq)
    width = MEM_HEADS * HEAD_DIM
    n_mem = kv.shape[1]
    return pl.pallas_call(
        _mem_attn_body,
        grid=(bsz, seq // tq),
        in_specs=[
            pl.BlockSpec((1, tq, width), lambda b, i: (b, i, q_col_block)),
            pl.BlockSpec((1, n_mem, 2 * width), lambda b, i: (b, 0, 0)),
        ],
        out_specs=pl.BlockSpec((1, tq, width), lambda b, i: (b, i, 0)),
        out_shape=jax.ShapeDtypeStruct((bsz, seq, width), BF16),
        compiler_params=_cparams("parallel", "parallel"),
        name="memory_attention",
    )(proj, kv)


def _out_ln_body(mix_ref, mem_ref, x_ref, w1_ref, w2_ref, g_ref, b_ref, o_ref, *, alpha):
    hproj = _dot(mix_ref[...], w1_ref[...]) + _dot(mem_ref[...], w2_ref[...])
    o_ref[...] = _layer_norm(alpha * x_ref[...] + hproj, g_ref[...], b_ref[...])


def _out_proj_ln(mix, mem_out, x, w_out, g, b, alpha, tm=512):
    t, d = x.shape
    tm = min(tm, t)
    wm = mix.shape[1]
    we = mem_out.shape[1]
    return pl.pallas_call(
        functools.partial(_out_ln_body, alpha=alpha),
        grid=(t // tm,),
        in_specs=[
            pl.BlockSpec((tm, wm), lambda i: (i, 0)),
            pl.BlockSpec((tm, we), lambda i: (i, 0)),
            pl.BlockSpec((tm, d), lambda i: (i, 0)),
            pl.BlockSpec((wm, d), lambda i: (0, 0)),
            pl.BlockSpec((we, d), lambda i: (0, 0)),
            pl.BlockSpec((1, d), lambda i: (0, 0)),
            pl.BlockSpec((1, d), lambda i: (0, 0)),
        ],
        out_specs=pl.BlockSpec((tm, d), lambda i: (i, 0)),
        out_shape=jax.ShapeDtypeStruct((t, d), F32),
        compiler_params=_cparams("parallel"),
        name="out_proj_ln",
    )(mix, mem_out, x, w_out[:wm].astype(BF16), w_out[wm:].astype(BF16), g.reshape(1, d), b.reshape(1, d))


def _router_body(x_ref, wt_ref, b_ref, idx_ref, gate_ref, cnt_ref, run_ref, *, tb):
    @pl.when(pl.program_id(0) == 0)
    def _():
        run_ref[...] = jnp.zeros_like(run_ref)

    x = x_ref[...]
    xh = x.astype(BF16)
    xl = (x - xh.astype(F32)).astype(BF16)
    w = wt_ref[...]
    wh = w.astype(BF16)
    wl = (w - wh.astype(F32)).astype(BF16)
    logits = _dot_nt(wh, xh) + _dot_nt(wh, xl) + _dot_nt(wl, xh) + b_ref[...]

    n_e = logits.shape[0]
    eid = lax.broadcasted_iota(I32, logits.shape, 0)
    vals, idxs, hots = [], [], []
    multi = jnp.zeros(logits.shape, F32)
    for _ in range(TOP_K):
        best = jnp.max(logits, axis=0, keepdims=True)
        first = jnp.min(jnp.where(logits == best, eid, n_e), axis=0, keepdims=True)
        hot = eid == first
        vals.append(best)
        idxs.append(first)
        hots.append(hot)
        multi = jnp.where(hot, 1.0, multi)
        logits = jnp.where(hot, -jnp.inf, logits)

    exps = [jnp.exp(v - vals[0]) for v in vals]
    den = exps[0]
    for e in exps[1:]:
        den = den + e

    r = lax.broadcasted_iota(I32, (tb, tb), 0)
    c = lax.broadcasted_iota(I32, (tb, tb), 1)
    before = jnp.where(r < c, 1.0, 0.0).astype(BF16)
    pos = _dot(multi.astype(BF16), before) + run_ref[:, 0:1]
    for kk in range(TOP_K):
        idx_ref[kk:kk + 1, :] = idxs[kk]
        rank = jnp.sum(jnp.where(hots[kk], pos, 0.0), axis=0, keepdims=True)
        idx_ref[TOP_K + kk:TOP_K + kk + 1, :] = rank.astype(I32)
        gate_ref[kk:kk + 1, :] = exps[kk] / den
    gate_ref[TOP_K:2 * TOP_K, :] = jnp.zeros((TOP_K, tb), F32)
    run_ref[...] = run_ref[...] + jnp.sum(multi, axis=1, keepdims=True)
    cnt_ref[...] = run_ref[...]


def _router(x, router_w, router_b, tb=512):
    t, d = x.shape
    tb = min(tb, t)
    n_e = router_w.shape[1]
    return pl.pallas_call(
        functools.partial(_router_body, tb=tb),
        grid=(t // tb,),
        in_specs=[
            pl.BlockSpec((tb, d), lambda i: (i, 0)),
            pl.BlockSpec((n_e, d), lambda i: (0, 0)),
            pl.BlockSpec((n_e, 1), lambda i: (0, 0)),
        ],
        out_specs=[
            pl.BlockSpec((2 * TOP_K, tb), lambda i: (0, i)),
            pl.BlockSpec((2 * TOP_K, tb), lambda i: (0, i)),
            pl.BlockSpec((n_e, HEAD_DIM), lambda i: (0, 0)),
        ],
        out_shape=[
            jax.ShapeDtypeStruct((2 * TOP_K, t), I32),
            jax.ShapeDtypeStruct((2 * TOP_K, t), F32),
            jax.ShapeDtypeStruct((n_e, HEAD_DIM), F32),
        ],
        scratch_shapes=[pltpu.VMEM((n_e, HEAD_DIM), F32)],
        compiler_params=_cparams("arbitrary"),
        name="moe_router",
    )(x, router_w.T, router_b.reshape(n_e, 1))


def _dispatch_body(dest_ref, x_ref, xs_in, xs_hbm, sem, *, tb, n_tok):
    del xs_in
    base = pl.program_id(0) * tb

    def issue(t, carry):
        for kk in range(TOP_K):
            d = dest_ref[kk * n_tok + base + t]
            pltpu.make_async_copy(x_ref.at[pl.ds(t, 1)], xs_hbm.at[pl.ds(d, 1)], sem).start()
        return carry

    lax.fori_loop(0, tb, issue, 0)
    for kk in range(TOP_K):
        pltpu.make_async_copy(x_ref, xs_hbm.at[pl.ds(0, tb)], sem).wait()


def _dispatch(x, dest_flat, n_rows, tb=512):
    t, d = x.shape
    tb = min(tb, t)
    grid_spec = pltpu.PrefetchScalarGridSpec(
        num_scalar_prefetch=1,
        grid=(t // tb,),
        in_specs=[pl.BlockSpec((tb, d), lambda i, dest: (i, 0)), pl.BlockSpec(memory_space=pl.ANY)],
        out_specs=pl.BlockSpec(memory_space=pl.ANY),
        scratch_shapes=[pltpu.SemaphoreType.DMA(())],
    )
    return pl.pallas_call(
        functools.partial(_dispatch_body, tb=tb, n_tok=t),
        grid_spec=grid_spec,
        out_shape=jax.ShapeDtypeStruct((n_rows, d), x.dtype),
        input_output_aliases={2: 0},
        compiler_params=_cparams("arbitrary"),
        name="moe_dispatch",
    )(dest_flat, x, jnp.zeros((n_rows, d), x.dtype))


UNZIP = 256


def _expert_body(be_ref, nu_ref, x_ref, wu_ref, bg_ref, bl_ref, wd_ref, bd_ref, o_ref, wg_s, wl_s, wd_s):
    i = pl.program_id(0)
    n_used = nu_ref[0]
    cur = jnp.minimum(i, n_used - 1)
    new_expert = jnp.logical_or(i == 0, be_ref[cur] != be_ref[jnp.maximum(cur - 1, 0)])

    @pl.when(jnp.logical_and(new_expert, i < n_used))
    def _():
        half = UNZIP // 2
        r = lax.broadcasted_iota(I32, (UNZIP, UNZIP), 0)
        c = lax.broadcasted_iota(I32, (UNZIP, UNZIP), 1)
        src = jnp.where(c < half, 2 * c, 2 * (c - half) + 1)
        perm = jnp.where(r == src, 1.0, 0.0).astype(BF16)
        for g in range(wu_ref.shape[2] // UNZIP):
            w = wu_ref[0, :, g * UNZIP:(g + 1) * UNZIP].astype(BF16)
            sep = _dot(w, perm).astype(BF16)
            wg_s[:, g * half:(g + 1) * half] = sep[:, :half]
            wl_s[:, g * half:(g + 1) * half] = sep[:, half:]
        wd_s[...] = wd_ref[0].astype(BF16)

    @pl.when(i < n_used)
    def _():
        xb = x_ref[...].astype(BF16)
        glu = jnp.minimum(_dot(xb, wg_s[...]) + bg_ref[0], SWIGLU_LIMIT)
        lin = jnp.clip(_dot(xb, wl_s[...]) + bl_ref[0], -SWIGLU_LIMIT, SWIGLU_LIMIT)
        act = glu * _sigmoid(SWIGLU_ALPHA * glu) * (lin + 1.0)
        o_ref[...] = _dot(act.astype(BF16), wd_s[...]) + bd_ref[0]

    @pl.when(i >= n_used)
    def _():
        o_ref[...] = jnp.zeros_like(o_ref)


def _expert_mlp(xs, blk_expert, n_used, w_up, bg, bl, w_down, bd):
    p_rows, d = xs.shape
    f = w_down.shape[1]
    n_blk = p_rows // MOE_BLOCK

    def blk(i, be, nu):
        return jnp.minimum(i, nu[0] - 1)

    def w_spec(shape):
        return pl.BlockSpec((1,) + shape, lambda i, be, nu: (be[blk(i, be, nu)], 0, 0))

    grid_spec = pltpu.PrefetchScalarGridSpec(
        num_scalar_prefetch=2,
        grid=(n_blk,),
        in_specs=[
            pl.BlockSpec((MOE_BLOCK, d), lambda i, be, nu: (blk(i, be, nu), 0)),
            w_spec((d, 2 * f)), w_spec((1, f)), w_spec((1, f)),
            w_spec((f, d)), w_spec((1, d)),
        ],
        out_specs=pl.BlockSpec((MOE_BLOCK, d), lambda i, be, nu: (i, 0)),
        scratch_shapes=[pltpu.VMEM((d, f), BF16), pltpu.VMEM((d, f), BF16), pltpu.VMEM((f, d), BF16)],
    )
    return pl.pallas_call(
        _expert_body,
        grid_spec=grid_spec,
        out_shape=jax.ShapeDtypeStruct((p_rows, d), F32),
        compiler_params=_cparams("arbitrary"),
        name="moe_experts",
    )(blk_expert, n_used, xs, w_up, bg, bl, w_down, bd)


def _combine_body(dest_ref, ys_hbm, gate_ref, x_ref, g_ref, b_ref, o_ref, buf_ref, sem_ref,
                  *, tb, n_tok, alpha):
    i = pl.program_id(0)
    n_steps = pl.num_programs(0)

    def issue(step, slot):
        base = step * tb

        def one(t, carry):
            for kk in range(TOP_K):
                d = dest_ref[kk * n_tok + base + t]
                pltpu.make_async_copy(ys_hbm.at[pl.ds(d, 1)], buf_ref.at[slot, kk, pl.ds(t, 1)],
                                      sem_ref.at[slot]).start()
            return carry

        lax.fori_loop(0, tb, one, 0)

    @pl.when(i == 0)
    def _():
        issue(0, 0)

    slot = lax.rem(i, 2)

    @pl.when(i + 1 < n_steps)
    def _():
        issue(i + 1, 1 - slot)

    for kk in range(TOP_K):
        pltpu.make_async_copy(ys_hbm.at[pl.ds(0, tb)], buf_ref.at[slot, kk], sem_ref.at[slot]).wait()

    g8 = gate_ref[...]
    gpad = jnp.concatenate([g8, jnp.zeros((HEAD_DIM - g8.shape[0], tb), F32)], axis=0)
    gt = gpad.T
    f = gt[:, 0:1] * buf_ref[slot, 0]
    for kk in range(1, TOP_K):
        f = f + gt[:, kk:kk + 1] * buf_ref[slot, kk]
    o_ref[...] = _layer_norm(alpha * x_ref[...] + f, g_ref[...], b_ref[...])


def _combine_ln(ys, dest_flat, gates, x, g, b, alpha, tb=256):
    t, d = x.shape
    tb = min(tb, t)
    grid_spec = pltpu.PrefetchScalarGridSpec(
        num_scalar_prefetch=1,
        grid=(t // tb,),
        in_specs=[
            pl.BlockSpec(memory_space=pl.ANY),
            pl.BlockSpec((2 * TOP_K, tb), lambda i, dest: (0, i)),
            pl.BlockSpec((tb, d), lambda i, dest: (i, 0)),
            pl.BlockSpec((1, d), lambda i, dest: (0, 0)),
            pl.BlockSpec((1, d), lambda i, dest: (0, 0)),
        ],
        out_specs=pl.BlockSpec((tb, d), lambda i, dest: (i, 0)),
        scratch_shapes=[pltpu.VMEM((2, TOP_K, tb, d), F32), pltpu.SemaphoreType.DMA((2,))],
    )
    return pl.pallas_call(
        functools.partial(_combine_body, tb=tb, n_tok=t, alpha=alpha),
        grid_spec=grid_spec,
        out_shape=jax.ShapeDtypeStruct((t, d), F32),
        compiler_params=_cparams("arbitrary"),
        name="moe_combine_ln",
    )(dest_flat, ys, gates, x, g.reshape(1, d), b.reshape(1, d))


def _moe_ln(x, router_w, router_b, w_up, b_up, w_down, b_down, g, b, alpha):
    t, d = x.shape
    n_e = router_w.shape[1]
    idx_rank, gates, cnt = _router(x, router_w, router_b)
    counts = cnt[:, 0].astype(I32)
    padded = ((counts + MOE_BLOCK - 1) // MOE_BLOCK) * MOE_BLOCK
    p_ends = jnp.cumsum(padded)
    p_starts = p_ends - padded
    n_blk = (t * TOP_K + MOE_BLOCK - 1) // MOE_BLOCK + n_e
    dest = p_starts[idx_rank[:TOP_K]] + idx_rank[TOP_K:]
    dest_flat = dest.reshape(-1)
    blk_expert = jnp.clip(
        jnp.searchsorted(p_ends, jnp.arange(n_blk, dtype=I32) * MOE_BLOCK, side='right'), 0, n_e - 1
    ).astype(I32)
    n_used = (p_ends[-1:] // MOE_BLOCK).astype(I32)

    xs = _dispatch(x, dest_flat, n_blk * MOE_BLOCK)
    bg = b_up[:, None, 0::2]
    bl = b_up[:, None, 1::2]
    ys = _expert_mlp(xs, blk_expert, n_used, w_up, bg, bl, w_down, b_down[:, None, :])
    return _combine_ln(ys, dest_flat, gates, x, g, b, alpha)


def kernel(x, mem, hgrn_w_in, hgrn_lb, hgrn_norm_g, fox_w_in, fox_b_f, moba_w_in, rel_bias, w_mem_kv, w_out,
           ln1_g, ln1_b, router_w, router_b, w_up, b_up, w_down, b_down, ln2_g, ln2_b):
    bsz, seq, d = x.shape
    depth = w_out.shape[0]
    t = bsz * seq
    mix_w = MIX_HEADS * HEAD_DIM
    mem_w = MEM_HEADS * HEAD_DIM
    alpha = (2 * depth) ** 0.25
    scale = HEAD_DIM ** -0.5
    n_mem = mem.shape[1]
    mem2 = mem.reshape(bsz * n_mem, d)

    def scale_vec(n_scaled_front, n_plain, n_scaled_back):
        return jnp.concatenate([
            jnp.full((1, n_scaled_front), scale, F32), jnp.ones((1, n_plain), F32),
            jnp.full((1, n_scaled_back), scale, F32)], axis=1)

    x2 = x.reshape(t, d)
    for i in range(depth):
        kind, j = i % 3, i // 3
        if kind == 0:
            w_in = hgrn_w_in[j].astype(BF16)
            proj = _matmul(x2, w_in, scale_vec(0, 4 * mix_w, mem_w), F32, 1024, 512)
            proj = proj.reshape(bsz, seq, -1)
            mix = _hgrn_mixer(proj, hgrn_lb, hgrn_norm_g[j], j)
            memq_block = 4 * mix_w // mem_w
        elif kind == 1:
            w = fox_w_in[j]
            w_in = jnp.concatenate([w[:, :3 * mix_w], w[:, 3 * mix_w + MIX_HEADS:]], axis=1).astype(BF16)
            proj = _matmul(x2, w_in, scale_vec(mix_w, 2 * mix_w, mem_w), BF16, 1024, 512)
            proj = proj.reshape(bsz, seq, -1)
            c_col, c_row = _fox_gate_cumsum(x2.reshape(bsz, seq, d), w[:, 3 * mix_w:3 * mix_w + MIX_HEADS],
                                            fox_b_f[j])
            mix = _fox_attention(proj, c_col, c_row)
            memq_block = 3 * mix_w // mem_w
        else:
            w_in = moba_w_in[j].astype(BF16)
            proj = _matmul(x2, w_in, scale_vec(mix_w, 2 * mix_w, mem_w), BF16, 1024, 512)
            proj = proj.reshape(bsz, seq, -1)
            mix = _moba_attention(proj, rel_bias)
            memq_block = 3 * mix_w // mem_w
        kv = _matmul(mem2, w_mem_kv[i].astype(BF16), jnp.ones((1, 2 * mem_w), F32), BF16, 512, 512)
        mem_out = _memory_attention(proj, memq_block, kv.reshape(bsz, n_mem, 2 * mem_w))
        x2 = _out_proj_ln(mix.reshape(t, mix_w), mem_out.reshape(t, mem_w), x2, w_out[i],
                          ln1_g[i], ln1_b[i], alpha)
        x2 = _moe_ln(x2, router_w[i], router_b[i], w_up[i], b_up[i], w_down[i], b_down[i],
                     ln2_g[i], ln2_b[i], alpha)
    return x2.reshape(bsz, seq, d)
```

```python
import functools
import math

import numpy as np
import jax
import jax.numpy as jnp
from jax import lax
from jax.experimental import pallas as pl
from jax.experimental.pallas import tpu as pltpu

F32 = jnp.float32
BF16 = jnp.bfloat16
I32 = jnp.int32

HEAD_DIM = 128
MIX_HEADS = 8
MEM_HEADS = 4
HGRN_CHUNK = 64
HGRN_SUB = 16
HGRN_EXP_CLAMP = 80.0
MOBA_BLOCK = 256
MOBA_TOPK = 3
REL_BUCKETS = 32
REL_MAX_DIST = 128
N_EXPERTS = 32
TOP_K = 4
MOE_BLOCK = 256
SWIGLU_LIMIT = 7.0
SWIGLU_ALPHA = 1.702
LN_EPS = 1e-5
RMS_EPS = 1e-6
NEG = -1e30
VMEM_LIMIT = 56 * 1024 * 1024

NT_DIMS = (((1,), (1,)), ((), ()))
TN_DIMS = (((0,), (0,)), ((), ()))


def _cparams(*sem):
    return pltpu.CompilerParams(dimension_semantics=sem, vmem_limit_bytes=VMEM_LIMIT)


def _dot(a, b):
    return jnp.dot(a, b, preferred_element_type=F32)


def _dot_nt(a, b):
    return lax.dot_general(a, b, NT_DIMS, preferred_element_type=F32)


def _dot_tn(a, b):
    return lax.dot_general(a, b, TN_DIMS, preferred_element_type=F32)


def _split3(a):
    hi = a.astype(BF16)
    r = a - hi.astype(F32)
    mid = r.astype(BF16)
    lo = (r - mid.astype(F32)).astype(BF16)
    return hi, mid, lo


def _sigmoid(x):
    return 1.0 / (1.0 + jnp.exp(-x))


def _layer_norm(z, g, b):
    mu = jnp.mean(z, axis=-1, keepdims=True)
    zc = z - mu
    var = jnp.mean(zc * zc, axis=-1, keepdims=True)
    return zc * lax.rsqrt(var + LN_EPS) * g + b


def _mm_body(x_ref, w_ref, s_ref, o_ref, xb_ref):
    @pl.when(pl.program_id(1) == 0)
    def _():
        xb_ref[...] = x_ref[...].astype(BF16)

    acc = _dot(xb_ref[...], w_ref[...])
    o_ref[...] = (acc * s_ref[...]).astype(o_ref.dtype)


def _matmul(x, w, col_scale, out_dtype, tm, tn):
    m, k = x.shape
    n = w.shape[1]
    tm = min(tm, m)
    tn = min(tn, n)
    return pl.pallas_call(
        _mm_body,
        grid=(m // tm, n // tn),
        in_specs=[
            pl.BlockSpec((tm, k), lambda i, j: (i, 0)),
            pl.BlockSpec((k, tn), lambda i, j: (0, j)),
            pl.BlockSpec((1, tn), lambda i, j: (0, j)),
        ],
        out_specs=pl.BlockSpec((tm, tn), lambda i, j: (i, j)),
        out_shape=jax.ShapeDtypeStruct((m, n), out_dtype),
        scratch_shapes=[pltpu.VMEM((tm, k), BF16)],
        compiler_params=_cparams("parallel", "arbitrary"),
        name="proj_matmul",
    )(x, w, col_scale)


def _hgrn_body(q_ref, z_ref, i_ref, g_ref, lbp_ref, ng_ref, o_ref, st_ref, *, layer_j, tb):
    c_len, sb = HGRN_CHUNK, HGRN_SUB
    n_sub = c_len // sb

    @pl.when(pl.program_id(2) == 0)
    def _():
        st_ref[...] = jnp.zeros_like(st_ref)

    lbp = lbp_ref[...]
    ex = jnp.exp(lbp - jnp.max(lbp, axis=0, keepdims=True))
    p = ex / jnp.sum(ex, axis=0, keepdims=True)
    lb = jnp.zeros((1, HEAD_DIM), F32)
    for r in range(1, layer_j + 1):
        lb = lb + p[r:r + 1, :]
    log_lb = jnp.log(lb)
    log1m_lb = jnp.log1p(-lb)
    one_m_lb = 1.0 - lb
    ng = ng_ref[...]

    row = lax.broadcasted_iota(I32, (c_len, c_len), 0)
    col = lax.broadcasted_iota(I32, (c_len, c_len), 1)
    sub_start = (row // sb) * sb
    one = jnp.ones((c_len, c_len), F32)
    zero = jnp.zeros((c_len, c_len), F32)
    m_tri = jnp.where(col <= row, one, zero)
    m_in = jnp.where(col > sub_start, m_tri, zero)
    m_dec = jnp.where(col > row, one, zero)
    m_ke = jnp.where(col <= sub_start + (sb - 1), m_dec, zero)
    stack = jnp.concatenate([m_tri, m_in, m_ke, m_dec], axis=0).astype(BF16)
    same_sub = (row // sb) == (col // sb)
    diag_mask = jnp.logical_and(same_sub, col <= row)
    row_sub = row // sb
    krow_sub = lax.broadcasted_iota(I32, (c_len, HEAD_DIM), 0) // sb

    for c in range(tb // c_len):
        sl = pl.ds(c * c_len, c_len)
        z = z_ref[0, sl, :]
        e = jnp.exp(-jnp.abs(z))
        inv = 1.0 / (1.0 + e)
        sig_neg = jnp.where(z >= 0, e, 1.0) * inv
        log_sig = jnp.minimum(z, 0.0) - jnp.log1p(e)
        a2 = log1m_lb + log_sig
        log_f = jnp.maximum(log_lb, a2) + jnp.log1p(jnp.exp(-jnp.abs(log_lb - a2)))
        k = one_m_lb * sig_neg
        qv = q_ref[0, sl, :]
        qf = qv * _sigmoid(qv)
        v = i_ref[0, sl, :]
        vb = v.astype(BF16)

        pre = None
        for piece in _split3(log_f):
            t = _dot(stack, piece)
            pre = t if pre is None else pre + t
        b = pre[0:c_len]
        d_in = pre[c_len:2 * c_len]
        d_ke = pre[2 * c_len:3 * c_len]
        d_dec = pre[3 * c_len:4 * c_len]

        q_in = qf * jnp.exp(d_in)
        k_diag = (k * jnp.exp(jnp.minimum(-d_in, HGRN_EXP_CLAMP))).astype(BF16)
        k_end = k * jnp.exp(d_ke)
        k_dec = (k * jnp.exp(d_dec)).astype(BF16)
        q_all = (qf * jnp.exp(b)).astype(BF16)
        b_ref = b - d_in

        a = jnp.where(diag_mask, _dot_nt(q_in.astype(BF16), k_diag), 0.0)
        for j in range(n_sub - 1):
            b_end = b[(j + 1) * sb - 1:(j + 1) * sb, :]
            cross = jnp.exp(jnp.minimum(b_ref - b_end, 0.0))
            lhs = (q_in * cross).astype(BF16)
            k_j = jnp.where(krow_sub == j, k_end, 0.0).astype(BF16)
            a = a + jnp.where(row_sub > j, _dot_nt(lhs, k_j), 0.0)

        st = st_ref[...]
        o = _dot(a.astype(BF16), vb) + _dot_nt(q_all, st.astype(BF16))
        st_ref[...] = st * jnp.exp(b[c_len - 1:c_len, :]) + _dot_tn(vb, k_dec)

        o = o * lax.rsqrt(jnp.mean(o * o, axis=-1, keepdims=True) + RMS_EPS)
        gv = g_ref[0, sl, :]
        o_ref[0, sl, :] = (o * ng * (gv * _sigmoid(gv))).astype(o_ref.dtype)


def _hgrn_mixer(proj, lb_param, norm_g, layer_j, tb=256):
    bsz, seq, _ = proj.shape
    tb = min(tb, seq)
    n_layers = lb_param.shape[0]

    def col_block(off):
        return pl.BlockSpec((1, tb, HEAD_DIM), lambda b, h, t: (b, t, off + h))

    return pl.pallas_call(
        functools.partial(_hgrn_body, layer_j=layer_j, tb=tb),
        grid=(bsz, MIX_HEADS, seq // tb),
        in_specs=[
            col_block(0), col_block(MIX_HEADS), col_block(2 * MIX_HEADS), col_block(3 * MIX_HEADS),
            pl.BlockSpec((n_layers, HEAD_DIM), lambda b, h, t: (0, h)),
            pl.BlockSpec((1, HEAD_DIM), lambda b, h, t: (0, h)),
        ],
        out_specs=pl.BlockSpec((1, tb, HEAD_DIM), lambda b, h, t: (b, t, h)),
        out_shape=jax.ShapeDtypeStruct((bsz, seq, MIX_HEADS * HEAD_DIM), BF16),
        scratch_shapes=[pltpu.VMEM((HEAD_DIM, HEAD_DIM), F32)],
        compiler_params=_cparams("parallel", "parallel", "arbitrary"),
        name="hgrn2_mixer",
    )(proj, proj, proj, proj, lb_param, norm_g.reshape(1, -1))


def _log_sigmoid(z):
    return jnp.minimum(z, 0.0) - jnp.log1p(jnp.exp(-jnp.abs(z)))


def _fox_gate_body(x_ref, w_ref, wt_ref, bc_ref, br_ref, ccol_ref, crow_ref, carc_ref, carr_ref, *, tb):
    @pl.when(pl.program_id(1) == 0)
    def _():
        carc_ref[...] = jnp.zeros_like(carc_ref)
        carr_ref[...] = jnp.zeros_like(carr_ref)

    xb = x_ref[0].astype(BF16)
    ls_col = _log_sigmoid(_dot(xb, w_ref[...]) + bc_ref[...])
    ls_row = _log_sigmoid(_dot_nt(wt_ref[...], xb) + br_ref[...])
    row = lax.broadcasted_iota(I32, (tb, tb), 0)
    col = lax.broadcasted_iota(I32, (tb, tb), 1)
    lower = jnp.where(col <= row, 1.0, 0.0).astype(BF16)
    upper = jnp.where(row <= col, 1.0, 0.0).astype(BF16)
    c_col = carc_ref[...]
    for piece in _split3(ls_col):
        c_col = c_col + _dot(lower, piece)
    c_row = carr_ref[...]
    for piece in _split3(ls_row):
        c_row = c_row + _dot(piece, upper)
    ccol_ref[0] = c_col
    crow_ref[0] = c_row
    carc_ref[...] = c_col[tb - 1:tb, :]
    carr_ref[...] = c_row[:, tb - 1:tb]


def _fox_gate_cumsum(x, w_f, b_f, tb=512):
    bsz, seq, d = x.shape
    h = w_f.shape[1]
    tb = min(tb, seq)
    return pl.pallas_call(
        functools.partial(_fox_gate_body, tb=tb),
        grid=(bsz, seq // tb),
        in_specs=[
            pl.BlockSpec((1, tb, d), lambda b, t: (b, t, 0)),
            pl.BlockSpec((d, h), lambda b, t: (0, 0)),
            pl.BlockSpec((h, d), lambda b, t: (0, 0)),
            pl.BlockSpec((1, h), lambda b, t: (0, 0)),
            pl.BlockSpec((h, 1), lambda b, t: (0, 0)),
        ],
        out_specs=[
            pl.BlockSpec((1, tb, h), lambda b, t: (b, t, 0)),
            pl.BlockSpec((1, h, tb), lambda b, t: (b, 0, t)),
        ],
        out_shape=[
            jax.ShapeDtypeStruct((bsz, seq, h), F32),
            jax.ShapeDtypeStruct((bsz, h, seq), F32),
        ],
        scratch_shapes=[pltpu.VMEM((1, h), F32), pltpu.VMEM((h, 1), F32)],
        compiler_params=_cparams("parallel", "arbitrary"),
        name="fox_gate_cumsum",
    )(x, w_f.astype(BF16), w_f.T.astype(BF16), b_f.reshape(1, h), b_f.reshape(h, 1))


ATTN_ROWS = 128


def _softmax_probs(u, shift, rows, m_ref, l_ref):
    reps = u.shape[1] // HEAD_DIM
    m_old = m_ref[rows, :]
    if shift is None:
        m_new = jnp.maximum(m_old, jnp.max(u, axis=-1, keepdims=True))
        p = jnp.exp(u - jnp.tile(m_new, (1, reps)))
    else:
        m_new = jnp.maximum(m_old, jnp.max(u, axis=-1, keepdims=True) + shift)
        p = jnp.exp(u - jnp.tile(m_new - shift, (1, reps)))
    alpha = jnp.exp(m_old - m_new)
    l_ref[rows, :] = alpha * l_ref[rows, :] + jnp.sum(p, axis=-1, keepdims=True)
    m_ref[rows, :] = m_new
    return p.astype(BF16), alpha


def _accumulate(p, alpha, v, rows, acc_ref):
    acc_ref[rows, :] = alpha * acc_ref[rows, :] + _dot(p, v)


def _softmax_rows(u, shift, v, rows, m_ref, l_ref, acc_ref):
    p, alpha = _softmax_probs(u, shift, rows, m_ref, l_ref)
    _accumulate(p, alpha, v, rows, acc_ref)


def _softmax_init(m_ref, l_ref, acc_ref):
    m_ref[...] = jnp.full(m_ref.shape, NEG, F32)
    l_ref[...] = jnp.zeros_like(l_ref)
    acc_ref[...] = jnp.zeros_like(acc_ref)


def _fox_attn_body(q_ref, k_ref, v_ref, ccol_ref, crow_ref, o_ref, ct_ref, m_ref, l_ref, acc_ref, s_ref, *, tq):
    rs = min(ATTN_ROWS, tq)
    n_groups = tq // rs
    h = pl.program_id(1)
    qi = pl.program_id(2)
    ccol = ccol_ref[0]
    lane = lax.broadcasted_iota(I32, ccol.shape, 1)
    c_t = jnp.sum(jnp.where(lane == h, ccol, 0.0), axis=1, keepdims=True)
    ct_ref[...] = jnp.broadcast_to(c_t, ct_ref.shape)
    _softmax_init(m_ref, l_ref, acc_ref)

    def keys(kb):
        return pl.ds(pl.multiple_of(kb * tq, tq), tq)

    def logits(kb, r):
        rows = pl.ds(r * rs, rs)
        ks = keys(kb)
        s_ref[kb % 2, rows, :] = _dot_nt(q_ref[0, rows, :], k_ref[0, ks, :]) - crow_ref[0, pl.ds(h, 1), ks]

    def attend(kb, r, diagonal, prefetch):
        rows = pl.ds(r * rs, rs)
        u = s_ref[kb % 2, rows, :]
        if diagonal:
            row = lax.broadcasted_iota(I32, (rs, tq), 0) + r * rs
            col = lax.broadcasted_iota(I32, (rs, tq), 1)
            u = jnp.where(col <= row, u, NEG)
        if prefetch:
            logits(kb + 1, r)
        p, alpha = _softmax_probs(u, ct_ref[rows, :], rows, m_ref, l_ref)
        _accumulate(p, alpha, v_ref[0, keys(kb), :], rows, acc_ref)

    for r in range(n_groups):
        logits(0, r)

    def full_block(kb, carry):
        for r in range(n_groups):
            attend(kb, r, False, True)
        return carry

    lax.fori_loop(0, qi, full_block, 0)
    for r in range(n_groups):
        attend(qi, r, True, False)
    o_ref[0] = (acc_ref[...] / l_ref[...]).astype(o_ref.dtype)


def _fox_attention(qkv, c_col, c_row, tq=512):
    bsz, seq, _ = qkv.shape
    tq = min(tq, seq)
    nh = MIX_HEADS
    return pl.pallas_call(
        functools.partial(_fox_attn_body, tq=tq),
        grid=(bsz, nh, seq // tq),
        in_specs=[
            pl.BlockSpec((1, tq, HEAD_DIM), lambda b, h, i: (b, i, h)),
            pl.BlockSpec((1, seq, HEAD_DIM), lambda b, h, i: (b, 0, nh + h)),
            pl.BlockSpec((1, seq, HEAD_DIM), lambda b, h, i: (b, 0, 2 * nh + h)),
            pl.BlockSpec((1, tq, nh), lambda b, h, i: (b, i, 0)),
            pl.BlockSpec((1, nh, seq), lambda b, h, i: (b, 0, 0)),
        ],
        out_specs=pl.BlockSpec((1, tq, HEAD_DIM), lambda b, h, i: (b, i, h)),
        out_shape=jax.ShapeDtypeStruct((bsz, seq, nh * HEAD_DIM), BF16),
        scratch_shapes=[pltpu.VMEM((tq, HEAD_DIM), F32)] * 4 + [pltpu.VMEM((2, tq, tq), F32)],
        compiler_params=_cparams("parallel", "parallel", "arbitrary"),
        name="fox_attention",
    )(qkv, qkv, qkv, c_col, c_row)


def _t5_bucket_thresholds():
    n = np.arange(0, 4 * REL_MAX_DIST, dtype=np.int64)
    max_exact = REL_BUCKETS // 2
    ratio = np.log(np.maximum(n, 1).astype(np.float32) / np.float32(max_exact)) / np.float32(
        math.log(REL_MAX_DIST / max_exact))
    large = np.minimum(max_exact + (ratio * np.float32(REL_BUCKETS - max_exact)).astype(np.int32),
                       REL_BUCKETS - 1)
    bucket = np.where(n < max_exact, n, large)
    assert np.all(np.diff(bucket) >= 0) and bucket[-1] == REL_BUCKETS - 1
    return [int(np.argmax(bucket >= b)) for b in range(REL_BUCKETS)]


def _moba_body(rb_ref, q_ref, k_ref, v_ref, o_ref, kmean_ref, bias_ref, sel_ref, m_ref, l_ref, acc_ref,
               s_ref, *, n_blocks):
    blk = MOBA_BLOCK
    h = pl.program_id(1)
    qi = pl.program_id(2)
    far_bias = rb_ref[REL_BUCKETS - 1, h]

    @pl.when(qi == 0)
    def _():
        r = lax.broadcasted_iota(I32, (n_blocks, n_blocks * blk), 0)
        c = lax.broadcasted_iota(I32, (n_blocks, n_blocks * blk), 1)
        pool = jnp.where(c // blk == r, 1.0, 0.0).astype(BF16)
        kmean_ref[...] = _dot(pool, k_ref[0]) * (1.0 / blk)
        row = lax.broadcasted_iota(I32, (blk, blk), 0)
        col = lax.broadcasted_iota(I32, (blk, blk), 1)
        thresholds = _t5_bucket_thresholds()
        for slot, dist in ((0, row - col + blk), (1, row - col)):
            val = jnp.full((blk, blk), rb_ref[0, h], F32)
            for b in range(1, REL_BUCKETS):
                val = jnp.where(dist >= thresholds[b], rb_ref[b, h], val)
            if slot == 1:
                val = jnp.where(dist >= 0, val, NEG)
            bias_ref[slot] = val

    q = q_ref[0]
    km_hi = kmean_ref[...].astype(BF16)
    km_lo = (kmean_ref[...] - km_hi.astype(F32)).astype(BF16)
    gate = _dot_nt(q, km_hi) + _dot_nt(q, km_lo)
    lane = lax.broadcasted_iota(I32, gate.shape, 1)
    gate = jnp.where(lane < qi, gate, NEG)
    sel = jnp.zeros(gate.shape, F32)
    for _ in range(MOBA_TOPK):
        best = jnp.max(gate, axis=1, keepdims=True)
        first = jnp.min(jnp.where(gate == best, lane, n_blocks), axis=1, keepdims=True)
        pick = jnp.logical_and(lane == first, best > 0.5 * NEG)
        sel = jnp.where(pick, 1.0, sel)
        gate = jnp.where(lane == first, NEG, gate)
    sel_ref[...] = sel

    _softmax_init(m_ref, l_ref, acc_ref)
    rs = min(ATTN_ROWS, blk)
    n_groups = blk // rs

    def sel_col(j):
        lanes = lax.broadcasted_iota(I32, (blk, n_blocks), 1)
        return jnp.sum(jnp.where(lanes == j, sel_ref[...], 0.0), axis=1, keepdims=True) > 0.5

    def finish():
        o_ref[0] = (acc_ref[...] / l_ref[...]).astype(o_ref.dtype)

    @pl.when(qi == 0)
    def _():
        for r in range(n_groups):
            rows = pl.ds(r * rs, rs)
            u = _dot_nt(q_ref[0, rows, :], k_ref[0, 0:blk, :]) + bias_ref[1, r * rs:(r + 1) * rs, :]
            _softmax_rows(u, None, v_ref[0, 0:blk, :], rows, m_ref, l_ref, acc_ref)
        finish()

    @pl.when(qi >= 1)
    def _():
        n_far = qi - 1
        odd = n_far % 2
        n_far_units = (n_far + odd) // 2

        def first_block(unit):
            return jnp.maximum(2 * unit - odd, 0)

        def keys(unit):
            return pl.ds(pl.multiple_of(first_block(unit) * blk, blk), 2 * blk)

        def logits(unit, r):
            rows = pl.ds(r * rs, rs)
            s_ref[unit % 2, rows, :] = _dot_nt(q_ref[0, rows, :], k_ref[0, keys(unit), :])

        def attend(unit, r, extra_a, extra_b, prefetch):
            rows = pl.ds(r * rs, rs)
            u = s_ref[unit % 2, rows, :]
            u = jnp.concatenate([u[:, :blk] + extra_a, u[:, blk:] + extra_b], axis=1)
            if prefetch:
                logits(unit + 1, r)
            p, alpha = _softmax_probs(u, None, rows, m_ref, l_ref)
            _accumulate(p, alpha, v_ref[0, keys(unit), :], rows, acc_ref)

        for r in range(n_groups):
            logits(0, r)

        def far_unit(unit, carry):
            j = first_block(unit)
            masked_b = jnp.logical_and(unit == 0, odd == 1)
            bias_a = jnp.where(sel_col(j), far_bias, NEG)
            bias_b = jnp.where(jnp.logical_and(sel_col(j + 1), jnp.logical_not(masked_b)), far_bias, NEG)
            for r in range(n_groups):
                attend(unit, r, bias_a[r * rs:(r + 1) * rs], bias_b[r * rs:(r + 1) * rs], True)
            return carry

        lax.fori_loop(0, n_far_units, far_unit, 0)

        prev_mask = jnp.where(sel_col(qi - 1), 0.0, NEG)
        for r in range(n_groups):
            rr = slice(r * rs, (r + 1) * rs)
            attend(n_far_units, r, bias_ref[0, rr, :] + prev_mask[rr], bias_ref[1, rr, :], False)
        finish()


def _moba_attention(qkv, rel_bias):
    bsz, seq, _ = qkv.shape
    nh = MIX_HEADS
    blk = MOBA_BLOCK
    assert seq % blk == 0
    n_blocks = seq // blk
    grid_spec = pltpu.PrefetchScalarGridSpec(
        num_scalar_prefetch=0,
        grid=(bsz, nh, n_blocks),
        in_specs=[
            pl.BlockSpec(memory_space=pltpu.SMEM),
            pl.BlockSpec((1, blk, HEAD_DIM), lambda b, h, i: (b, i, h)),
            pl.BlockSpec((1, seq, HEAD_DIM), lambda b, h, i: (b, 0, nh + h)),
            pl.BlockSpec((1, seq, HEAD_DIM), lambda b, h, i: (b, 0, 2 * nh + h)),
        ],
        out_specs=pl.BlockSpec((1, blk, HEAD_DIM), lambda b, h, i: (b, i, h)),
        scratch_shapes=[
            pltpu.VMEM((n_blocks, HEAD_DIM), F32),
            pltpu.VMEM((2, blk, blk), F32),
            pltpu.VMEM((blk, n_blocks), F32),
            pltpu.VMEM((blk, HEAD_DIM), F32), pltpu.VMEM((blk, HEAD_DIM), F32), pltpu.VMEM((blk, HEAD_DIM), F32),
            pltpu.VMEM((2, blk, 2 * blk), F32),
        ],
    )
    return pl.pallas_call(
        functools.partial(_moba_body, n_blocks=n_blocks),
        grid_spec=grid_spec,
        out_shape=jax.ShapeDtypeStruct((bsz, seq, nh * HEAD_DIM), BF16),
        compiler_params=_cparams("parallel", "parallel", "arbitrary"),
        name="moba_attention",
    )(rel_bias, qkv, qkv, qkv)


def _mem_attn_body(q_ref, kv_ref, o_ref):
    width = MEM_HEADS * HEAD_DIM
    for hh in range(MEM_HEADS):
        cs = slice(hh * HEAD_DIM, (hh + 1) * HEAD_DIM)
        q = q_ref[0, :, cs].astype(BF16)
        k = kv_ref[0, :, cs]
        v = kv_ref[0, :, width + hh * HEAD_DIM:width + (hh + 1) * HEAD_DIM]
        s = _dot_nt(q, k)
        p = jnp.exp(s - jnp.max(s, axis=-1, keepdims=True))
        o = _dot(p.astype(BF16), v) / jnp.sum(p, axis=-1, keepdims=True)
        o_ref[0, :, cs] = o.astype(o_ref.dtype)


def _memory_attention(proj, q_col_block, kv, tq=512):
    bsz, seq, _ = proj.shape
    tq = min(tq, seq)
    width = MEM_HEADS * HEAD_DIM
    n_mem = kv.shape[1]
    return pl.pallas_call(
        _mem_attn_body,
        grid=(bsz, seq // tq),
        in_specs=[
            pl.BlockSpec((1, tq, width), lambda b, i: (b, i, q_col_block)),
            pl.BlockSpec((1, n_mem, 2 * width), lambda b, i: (b, 0, 0)),
        ],
        out_specs=pl.BlockSpec((1, tq, width), lambda b, i: (b, i, 0)),
        out_shape=jax.ShapeDtypeStruct((bsz, seq, width), BF16),
        compiler_params=_cparams("parallel", "parallel"),
        name="memory_attention",
    )(proj, kv)


def _out_ln_body(mix_ref, mem_ref, x_ref, w1_ref, w2_ref, g_ref, b_ref, o_ref, *, alpha):
    hproj = _dot(mix_ref[...], w1_ref[...]) + _dot(mem_ref[...], w2_ref[...])
    o_ref[...] = _layer_norm(alpha * x_ref[...] + hproj, g_ref[...], b_ref[...])


def _out_proj_ln(mix, mem_out, x, w_out, g, b, alpha, tm=512):
    t, d = x.shape
    tm = min(tm, t)
    wm = mix.shape[1]
    we = mem_out.shape[1]
    return pl.pallas_call(
        functools.partial(_out_ln_body, alpha=alpha),
        grid=(t // tm,),
        in_specs=[
            pl.BlockSpec((tm, wm), lambda i: (i, 0)),
            pl.BlockSpec((tm, we), lambda i: (i, 0)),
            pl.BlockSpec((tm, d), lambda i: (i, 0)),
            pl.BlockSpec((wm, d), lambda i: (0, 0)),
            pl.BlockSpec((we, d), lambda i: (0, 0)),
            pl.BlockSpec((1, d), lambda i: (0, 0)),
            pl.BlockSpec((1, d), lambda i: (0, 0)),
        ],
        out_specs=pl.BlockSpec((tm, d), lambda i: (i, 0)),
        out_shape=jax.ShapeDtypeStruct((t, d), F32),
        compiler_params=_cparams("parallel"),
        name="out_proj_ln",
    )(mix, mem_out, x, w_out[:wm].astype(BF16), w_out[wm:].astype(BF16), g.reshape(1, d), b.reshape(1, d))


def _router_body(x_ref, wt_ref, b_ref, idx_ref, gate_ref, cnt_ref, run_ref, *, tb):
    @pl.when(pl.program_id(0) == 0)
    def _():
        run_ref[...] = jnp.zeros_like(run_ref)

    x = x_ref[...]
    xh = x.astype(BF16)
    xl = (x - xh.astype(F32)).astype(BF16)
    w = wt_ref[...]
    wh = w.astype(BF16)
    wl = (w - wh.astype(F32)).astype(BF16)
    logits = _dot_nt(wh, xh) + _dot_nt(wh, xl) + _dot_nt(wl, xh) + b_ref[...]

    n_e = logits.shape[0]
    eid = lax.broadcasted_iota(I32, logits.shape, 0)
    vals, idxs, hots = [], [], []
    multi = jnp.zeros(logits.shape, F32)
    for _ in range(TOP_K):
        best = jnp.max(logits, axis=0, keepdims=True)
        first = jnp.min(jnp.where(logits == best, eid, n_e), axis=0, keepdims=True)
        hot = eid == first
        vals.append(best)
        idxs.append(first)
        hots.append(hot)
        multi = jnp.where(hot, 1.0, multi)
        logits = jnp.where(hot, -jnp.inf, logits)

    exps = [jnp.exp(v - vals[0]) for v in vals]
    den = exps[0]
    for e in exps[1:]:
        den = den + e

    r = lax.broadcasted_iota(I32, (tb, tb), 0)
    c = lax.broadcasted_iota(I32, (tb, tb), 1)
    before = jnp.where(r < c, 1.0, 0.0).astype(BF16)
    pos = _dot(multi.astype(BF16), before) + run_ref[:, 0:1]
    for kk in range(TOP_K):
        idx_ref[kk:kk + 1, :] = idxs[kk]
        rank = jnp.sum(jnp.where(hots[kk], pos, 0.0), axis=0, keepdims=True)
        idx_ref[TOP_K + kk:TOP_K + kk + 1, :] = rank.astype(I32)
        gate_ref[kk:kk + 1, :] = exps[kk] / den
    gate_ref[TOP_K:2 * TOP_K, :] = jnp.zeros((TOP_K, tb), F32)
    run_ref[...] = run_ref[...] + jnp.sum(multi, axis=1, keepdims=True)
    cnt_ref[...] = run_ref[...]


def _router(x, router_w, router_b, tb=512):
    t, d = x.shape
    tb = min(tb, t)
    n_e = router_w.shape[1]
    return pl.pallas_call(
        functools.partial(_router_body, tb=tb),
        grid=(t // tb,),
        in_specs=[
            pl.BlockSpec((tb, d), lambda i: (i, 0)),
            pl.BlockSpec((n_e, d), lambda i: (0, 0)),
            pl.BlockSpec((n_e, 1), lambda i: (0, 0)),
        ],
        out_specs=[
            pl.BlockSpec((2 * TOP_K, tb), lambda i: (0, i)),
            pl.BlockSpec((2 * TOP_K, tb), lambda i: (0, i)),
            pl.BlockSpec((n_e, HEAD_DIM), lambda i: (0, 0)),
        ],
        out_shape=[
            jax.ShapeDtypeStruct((2 * TOP_K, t), I32),
            jax.ShapeDtypeStruct((2 * TOP_K, t), F32),
            jax.ShapeDtypeStruct((n_e, HEAD_DIM), F32),
        ],
        scratch_shapes=[pltpu.VMEM((n_e, HEAD_DIM), F32)],
        compiler_params=_cparams("arbitrary"),
        name="moe_router",
    )(x, router_w.T, router_b.reshape(n_e, 1))


def _dispatch_body(dest_ref, x_ref, xs_in, xs_hbm, sem, *, tb, n_tok):
    del xs_in
    base = pl.program_id(0) * tb

    def issue(t, carry):
        for kk in range(TOP_K):
            d = dest_ref[kk * n_tok + base + t]
            pltpu.make_async_copy(x_ref.at[pl.ds(t, 1)], xs_hbm.at[pl.ds(d, 1)], sem).start()
        return carry

    lax.fori_loop(0, tb, issue, 0)
    for kk in range(TOP_K):
        pltpu.make_async_copy(x_ref, xs_hbm.at[pl.ds(0, tb)], sem).wait()


def _dispatch(x, dest_flat, n_rows, tb=512):
    t, d = x.shape
    tb = min(tb, t)
    grid_spec = pltpu.PrefetchScalarGridSpec(
        num_scalar_prefetch=1,
        grid=(t // tb,),
        in_specs=[pl.BlockSpec((tb, d), lambda i, dest: (i, 0)), pl.BlockSpec(memory_space=pl.ANY)],
        out_specs=pl.BlockSpec(memory_space=pl.ANY),
        scratch_shapes=[pltpu.SemaphoreType.DMA(())],
    )
    return pl.pallas_call(
        functools.partial(_dispatch_body, tb=tb, n_tok=t),
        grid_spec=grid_spec,
        out_shape=jax.ShapeDtypeStruct((n_rows, d), x.dtype),
        input_output_aliases={2: 0},
        compiler_params=_cparams("arbitrary"),
        name="moe_dispatch",
    )(dest_flat, x, jnp.zeros((n_rows, d), x.dtype))


UNZIP = 256


def _expert_body(be_ref, nu_ref, x_ref, wu_ref, bg_ref, bl_ref, wd_ref, bd_ref, o_ref, wg_s, wl_s, wd_s):
    i = pl.program_id(0)
    n_used = nu_ref[0]
    cur = jnp.minimum(i, n_used - 1)
    new_expert = jnp.logical_or(i == 0, be_ref[cur] != be_ref[jnp.maximum(cur - 1, 0)])

    @pl.when(jnp.logical_and(new_expert, i < n_used))
    def _():
        half = UNZIP // 2
        r = lax.broadcasted_iota(I32, (UNZIP, UNZIP), 0)
        c = lax.broadcasted_iota(I32, (UNZIP, UNZIP), 1)
        src = jnp.where(c < half, 2 * c, 2 * (c - half) + 1)
        perm = jnp.where(r == src, 1.0, 0.0).astype(BF16)
        for g in range(wu_ref.shape[3] // UNZIP):
            w = wu_ref[0, 0, :, g * UNZIP:(g + 1) * UNZIP].astype(BF16)
            sep = _dot(w, perm).astype(BF16)
            wg_s[:, g * half:(g + 1) * half] = sep[:, :half]
            wl_s[:, g * half:(g + 1) * half] = sep[:, half:]
        wd_s[...] = wd_ref[0, 0].astype(BF16)

    @pl.when(i < n_used)
    def _():
        xb = x_ref[...].astype(BF16)
        glu = jnp.minimum(_dot(xb, wg_s[...]) + bg_ref[0], SWIGLU_LIMIT)
        lin = jnp.clip(_dot(xb, wl_s[...]) + bl_ref[0], -SWIGLU_LIMIT, SWIGLU_LIMIT)
        act = glu * _sigmoid(SWIGLU_ALPHA * glu) * (lin + 1.0)
        o_ref[...] = _dot(act.astype(BF16), wd_s[...]) + bd_ref[0]

    @pl.when(i >= n_used)
    def _():
        o_ref[...] = jnp.zeros_like(o_ref)


def _expert_mlp(xs, blk_expert, n_used, layer, w_up, bg, bl, w_down, bd):
    p_rows, d = xs.shape
    f = w_down.shape[2]
    n_blk = p_rows // MOE_BLOCK

    def blk(i, be, nu):
        return jnp.minimum(i, nu[0] - 1)

    def w_spec(shape):
        return pl.BlockSpec((1,) + shape, lambda i, be, nu: (be[blk(i, be, nu)], 0, 0))

    def stacked_w_spec(shape):
        return pl.BlockSpec((1, 1) + shape, lambda i, be, nu: (layer, be[blk(i, be, nu)], 0, 0))

    grid_spec = pltpu.PrefetchScalarGridSpec(
        num_scalar_prefetch=2,
        grid=(n_blk,),
        in_specs=[
            pl.BlockSpec((MOE_BLOCK, d), lambda i, be, nu: (blk(i, be, nu), 0)),
            stacked_w_spec((d, 2 * f)), w_spec((1, f)), w_spec((1, f)),
            stacked_w_spec((f, d)), w_spec((1, d)),
        ],
        out_specs=pl.BlockSpec((MOE_BLOCK, d), lambda i, be, nu: (i, 0)),
        scratch_shapes=[pltpu.VMEM((d, f), BF16), pltpu.VMEM((d, f), BF16), pltpu.VMEM((f, d), BF16)],
    )
    return pl.pallas_call(
        _expert_body,
        grid_spec=grid_spec,
        out_shape=jax.ShapeDtypeStruct((p_rows, d), F32),
        compiler_params=_cparams("arbitrary"),
        name="moe_experts",
    )(blk_expert, n_used, xs, w_up, bg, bl, w_down, bd)


def _combine_body(dest_ref, ys_hbm, gate_ref, x_ref, g_ref, b_ref, o_ref, buf_ref, sem_ref,
                  *, tb, n_tok, alpha):
    i = pl.program_id(0)
    n_steps = pl.num_programs(0)

    def issue(step, slot):
        base = step * tb

        def one(t, carry):
            for kk in range(TOP_K):
                d = dest_ref[kk * n_tok + base + t]
                pltpu.make_async_copy(ys_hbm.at[pl.ds(d, 1)], buf_ref.at[slot, kk, pl.ds(t, 1)],
                                      sem_ref.at[slot]).start()
            return carry

        lax.fori_loop(0, tb, one, 0)

    @pl.when(i == 0)
    def _():
        issue(0, 0)

    slot = lax.rem(i, 2)

    @pl.when(i + 1 < n_steps)
    def _():
        issue(i + 1, 1 - slot)

    for kk in range(TOP_K):
        pltpu.make_async_copy(ys_hbm.at[pl.ds(0, tb)], buf_ref.at[slot, kk], sem_ref.at[slot]).wait()

    g8 = gate_ref[...]
    gpad = jnp.concatenate([g8, jnp.zeros((HEAD_DIM - g8.shape[0], tb), F32)], axis=0)
    gt = gpad.T
    f = gt[:, 0:1] * buf_ref[slot, 0]
    for kk in range(1, TOP_K):
        f = f + gt[:, kk:kk + 1] * buf_ref[slot, kk]
    o_ref[...] = _layer_norm(alpha * x_ref[...] + f, g_ref[...], b_ref[...])


def _combine_ln(ys, dest_flat, gates, x, g, b, alpha, tb=256):
    t, d = x.shape
    tb = min(tb, t)
    grid_spec = pltpu.PrefetchScalarGridSpec(
        num_scalar_prefetch=1,
        grid=(t // tb,),
        in_specs=[
            pl.BlockSpec(memory_space=pl.ANY),
            pl.BlockSpec((2 * TOP_K, tb), lambda i, dest: (0, i)),
            pl.BlockSpec((tb, d), lambda i, dest: (i, 0)),
            pl.BlockSpec((1, d), lambda i, dest: (0, 0)),
            pl.BlockSpec((1, d), lambda i, dest: (0, 0)),
        ],
        out_specs=pl.BlockSpec((tb, d), lambda i, dest: (i, 0)),
        scratch_shapes=[pltpu.VMEM((2, TOP_K, tb, d), F32), pltpu.SemaphoreType.DMA((2,))],
    )
    return pl.pallas_call(
        functools.partial(_combine_body, tb=tb, n_tok=t, alpha=alpha),
        grid_spec=grid_spec,
        out_shape=jax.ShapeDtypeStruct((t, d), F32),
        compiler_params=_cparams("arbitrary"),
        name="moe_combine_ln",
    )(dest_flat, ys, gates, x, g.reshape(1, d), b.reshape(1, d))


def _moe_ln(x, router_w, router_b, layer, w_up, b_up, w_down, b_down, g, b, alpha):
    t, d = x.shape
    n_e = router_w.shape[1]
    idx_rank, gates, cnt = _router(x, router_w, router_b)
    counts = cnt[:, 0].astype(I32)
    padded = ((counts + MOE_BLOCK - 1) // MOE_BLOCK) * MOE_BLOCK
    p_ends = jnp.cumsum(padded)
    p_starts = p_ends - padded
    n_blk = (t * TOP_K + MOE_BLOCK - 1) // MOE_BLOCK + n_e
    experts = jnp.arange(n_e, dtype=I32)
    start_of = jnp.sum(jnp.where(idx_rank[:TOP_K, :, None] == experts, p_starts, 0), axis=-1)
    dest_flat = (start_of + idx_rank[TOP_K:]).reshape(-1)
    blk_first_row = jnp.arange(n_blk, dtype=I32) * MOE_BLOCK
    blk_expert = jnp.clip(jnp.sum((p_ends[None, :] <= blk_first_row[:, None]).astype(I32), axis=1), 0, n_e - 1)
    n_used = (p_ends[-1:] // MOE_BLOCK).astype(I32)

    xs = _dispatch(x, dest_flat, n_blk * MOE_BLOCK)
    bg = b_up[:, None, 0::2]
    bl = b_up[:, None, 1::2]
    ys = _expert_mlp(xs, blk_expert, n_used, layer, w_up, bg, bl, w_down, b_down[:, None, :])
    return _combine_ln(ys, dest_flat, gates, x, g, b, alpha)


def kernel(x, mem, hgrn_w_in, hgrn_lb, hgrn_norm_g, fox_w_in, fox_b_f, moba_w_in, rel_bias, w_mem_kv, w_out,
           ln1_g, ln1_b, router_w, router_b, w_up, b_up, w_down, b_down, ln2_g, ln2_b):
    bsz, seq, d = x.shape
    depth = w_out.shape[0]
    t = bsz * seq
    mix_w = MIX_HEADS * HEAD_DIM
    mem_w = MEM_HEADS * HEAD_DIM
    alpha = (2 * depth) ** 0.25
    scale = HEAD_DIM ** -0.5
    n_mem = mem.shape[1]
    mem2 = mem.reshape(bsz * n_mem, d)

    def scale_vec(n_scaled_front, n_plain, n_scaled_back):
        return jnp.concatenate([
            jnp.full((1, n_scaled_front), scale, F32), jnp.ones((1, n_plain), F32),
            jnp.full((1, n_scaled_back), scale, F32)], axis=1)

    x2 = x.reshape(t, d)
    for i in range(depth):
        kind, j = i % 3, i // 3
        if kind == 0:
            w_in = hgrn_w_in[j].astype(BF16)
            proj = _matmul(x2, w_in, scale_vec(0, 4 * mix_w, mem_w), F32, 1024, 512)
            proj = proj.reshape(bsz, seq, -1)
            mix = _hgrn_mixer(proj, hgrn_lb, hgrn_norm_g[j], j)
            memq_block = 4 * mix_w // mem_w
        elif kind == 1:
            w = fox_w_in[j]
            w_in = jnp.concatenate([w[:, :3 * mix_w], w[:, 3 * mix_w + MIX_HEADS:]], axis=1).astype(BF16)
            proj = _matmul(x2, w_in, scale_vec(mix_w, 2 * mix_w, mem_w), BF16, 1024, 512)
            proj = proj.reshape(bsz, seq, -1)
            c_col, c_row = _fox_gate_cumsum(x2.reshape(bsz, seq, d), w[:, 3 * mix_w:3 * mix_w + MIX_HEADS],
                                            fox_b_f[j])
            mix = _fox_attention(proj, c_col, c_row)
            memq_block = 3 * mix_w // mem_w
        else:
            w_in = moba_w_in[j].astype(BF16)
            proj = _matmul(x2, w_in, scale_vec(mix_w, 2 * mix_w, mem_w), BF16, 1024, 512)
            proj = proj.reshape(bsz, seq, -1)
            mix = _moba_attention(proj, rel_bias)
            memq_block = 3 * mix_w // mem_w
        kv = _matmul(mem2, w_mem_kv[i].astype(BF16), jnp.ones((1, 2 * mem_w), F32), BF16, 512, 512)
        mem_out = _memory_attention(proj, memq_block, kv.reshape(bsz, n_mem, 2 * mem_w))
        x2 = _out_proj_ln(mix.reshape(t, mix_w), mem_out.reshape(t, mem_w), x2, w_out[i],
                          ln1_g[i], ln1_b[i], alpha)
        x2 = _moe_ln(x2, router_w[i], router_b[i], i, w_up, b_up[i], w_down, b_down[i],
                     ln2_g[i], ln2_b[i], alpha)
    return x2.reshape(bsz, seq, d)
```

```python
import functools
import math

import numpy as np
import jax
import jax.numpy as jnp
from jax import lax
from jax.experimental import pallas as pl
from jax.experimental.pallas import tpu as pltpu

F32 = jnp.float32
BF16 = jnp.bfloat16
I32 = jnp.int32

HEAD_DIM = 128
MIX_HEADS = 8
MEM_HEADS = 4
HGRN_CHUNK = 64
HGRN_SUB = 16
HGRN_EXP_CLAMP = 80.0
MOBA_BLOCK = 256
MOBA_TOPK = 3
REL_BUCKETS = 32
REL_MAX_DIST = 128
N_EXPERTS = 32
TOP_K = 4
MOE_BLOCK = 256
SWIGLU_LIMIT = 7.0
SWIGLU_ALPHA = 1.702
LN_EPS = 1e-5
RMS_EPS = 1e-6
NEG = -1e30
VMEM_LIMIT = 56 * 1024 * 1024

NT_DIMS = (((1,), (1,)), ((), ()))
TN_DIMS = (((0,), (0,)), ((), ()))


def _cparams(*sem):
    return pltpu.CompilerParams(dimension_semantics=sem, vmem_limit_bytes=VMEM_LIMIT)


def _dot(a, b):
    return jnp.dot(a, b, preferred_element_type=F32)


def _dot_nt(a, b):
    return lax.dot_general(a, b, NT_DIMS, preferred_element_type=F32)


def _dot_tn(a, b):
    return lax.dot_general(a, b, TN_DIMS, preferred_element_type=F32)


def _split3(a):
    hi = a.astype(BF16)
    r = a - hi.astype(F32)
    mid = r.astype(BF16)
    lo = (r - mid.astype(F32)).astype(BF16)
    return hi, mid, lo


def _sigmoid(x):
    return 1.0 / (1.0 + jnp.exp(-x))


def _layer_norm(z, g, b):
    mu = jnp.mean(z, axis=-1, keepdims=True)
    zc = z - mu
    var = jnp.mean(zc * zc, axis=-1, keepdims=True)
    return zc * lax.rsqrt(var + LN_EPS) * g + b


def _mm_body(x_ref, w_ref, s_ref, o_ref, xb_ref):
    @pl.when(pl.program_id(1) == 0)
    def _():
        xb_ref[...] = x_ref[...].astype(BF16)

    acc = _dot(xb_ref[...], w_ref[...])
    o_ref[...] = (acc * s_ref[...]).astype(o_ref.dtype)


def _matmul(x, w, col_scale, out_dtype, tm, tn):
    m, k = x.shape
    n = w.shape[1]
    tm = min(tm, m)
    tn = min(tn, n)
    return pl.pallas_call(
        _mm_body,
        grid=(m // tm, n // tn),
        in_specs=[
            pl.BlockSpec((tm, k), lambda i, j: (i, 0)),
            pl.BlockSpec((k, tn), lambda i, j: (0, j)),
            pl.BlockSpec((1, tn), lambda i, j: (0, j)),
        ],
        out_specs=pl.BlockSpec((tm, tn), lambda i, j: (i, j)),
        out_shape=jax.ShapeDtypeStruct((m, n), out_dtype),
        scratch_shapes=[pltpu.VMEM((tm, k), BF16)],
        compiler_params=_cparams("parallel", "arbitrary"),
        name="proj_matmul",
    )(x, w, col_scale)


def _hgrn_body(q_ref, z_ref, i_ref, g_ref, lbp_ref, ng_ref, o_ref, st_ref, *, layer_j, tb):
    c_len, sb = HGRN_CHUNK, HGRN_SUB
    n_sub = c_len // sb

    @pl.when(pl.program_id(2) == 0)
    def _():
        st_ref[...] = jnp.zeros_like(st_ref)

    lbp = lbp_ref[...]
    ex = jnp.exp(lbp - jnp.max(lbp, axis=0, keepdims=True))
    p = ex / jnp.sum(ex, axis=0, keepdims=True)
    lb = jnp.zeros((1, HEAD_DIM), F32)
    for r in range(1, layer_j + 1):
        lb = lb + p[r:r + 1, :]
    log_lb = jnp.log(lb)
    log1m_lb = jnp.log1p(-lb)
    one_m_lb = 1.0 - lb
    ng = ng_ref[...]

    row = lax.broadcasted_iota(I32, (c_len, c_len), 0)
    col = lax.broadcasted_iota(I32, (c_len, c_len), 1)
    sub_start = (row // sb) * sb
    one = jnp.ones((c_len, c_len), F32)
    zero = jnp.zeros((c_len, c_len), F32)
    m_tri = jnp.where(col <= row, one, zero)
    m_in = jnp.where(col > sub_start, m_tri, zero)
    m_dec = jnp.where(col > row, one, zero)
    m_ke = jnp.where(col <= sub_start + (sb - 1), m_dec, zero)
    stack = jnp.concatenate([m_tri, m_in, m_ke, m_dec], axis=0).astype(BF16)
    same_sub = (row // sb) == (col // sb)
    diag_mask = jnp.logical_and(same_sub, col <= row)
    row_sub = row // sb
    krow_sub = lax.broadcasted_iota(I32, (c_len, HEAD_DIM), 0) // sb

    chunks = [pl.ds(c * c_len, c_len) for c in range(tb // c_len)]

    gates = []
    for sl in chunks:
        z = z_ref[0, sl, :]
        e = jnp.exp(-jnp.abs(z))
        inv = 1.0 / (1.0 + e)
        sig_neg = jnp.where(z >= 0, e, 1.0) * inv
        log_sig = jnp.minimum(z, 0.0) - jnp.log1p(e)
        a2 = log1m_lb + log_sig
        log_f = jnp.maximum(log_lb, a2) + jnp.log1p(jnp.exp(-jnp.abs(log_lb - a2)))
        qv = q_ref[0, sl, :]
        gates.append((log_f, one_m_lb * sig_neg, qv * _sigmoid(qv), i_ref[0, sl, :].astype(BF16)))

    prefix = []
    for log_f, _, _, _ in gates:
        pre = None
        for piece in _split3(log_f):
            t = _dot(stack, piece)
            pre = t if pre is None else pre + t
        prefix.append(pre)

    factors = []
    for (_, k, qf, _), pre in zip(gates, prefix):
        b = pre[0:c_len]
        d_in = pre[c_len:2 * c_len]
        d_ke = pre[2 * c_len:3 * c_len]
        d_dec = pre[3 * c_len:4 * c_len]
        q_in = qf * jnp.exp(d_in)
        k_diag = (k * jnp.exp(jnp.minimum(-d_in, HGRN_EXP_CLAMP))).astype(BF16)
        k_end = k * jnp.exp(d_ke)
        k_dec = (k * jnp.exp(d_dec)).astype(BF16)
        q_all = (qf * jnp.exp(b)).astype(BF16)
        b_ref = b - d_in
        lhs = [q_in.astype(BF16)]
        rhs = [k_diag]
        for j in range(n_sub - 1):
            b_end = b[(j + 1) * sb - 1:(j + 1) * sb, :]
            cross = jnp.exp(jnp.minimum(b_ref - b_end, 0.0))
            lhs.append((q_in * cross).astype(BF16))
            rhs.append(jnp.where(krow_sub == j, k_end, 0.0).astype(BF16))
        factors.append((lhs, rhs, k_dec, q_all, jnp.exp(b[c_len - 1:c_len, :])))

    scores = []
    for lhs, rhs, _, _, _ in factors:
        a = jnp.where(diag_mask, _dot_nt(lhs[0], rhs[0]), 0.0)
        for j in range(n_sub - 1):
            a = a + jnp.where(row_sub > j, _dot_nt(lhs[j + 1], rhs[j + 1]), 0.0)
        scores.append(a.astype(BF16))

    updates = [_dot_tn(vb, k_dec) for (_, _, _, vb), (_, _, k_dec, _, _) in zip(gates, factors)]
    states = [st_ref[...]]
    for (_, _, _, _, decay), kv in zip(factors, updates):
        states.append(states[-1] * decay + kv)
    st_ref[...] = states[-1]

    for sl, (_, _, _, vb), (_, _, _, q_all, _), a, st in zip(chunks, gates, factors, scores, states):
        o = _dot(a, vb) + _dot_nt(q_all, st.astype(BF16))
        o = o * lax.rsqrt(jnp.mean(o * o, axis=-1, keepdims=True) + RMS_EPS)
        gv = g_ref[0, sl, :]
        o_ref[0, sl, :] = (o * ng * (gv * _sigmoid(gv))).astype(o_ref.dtype)


def _hgrn_mixer(proj, lb_param, norm_g, layer_j, tb=512):
    bsz, seq, _ = proj.shape
    tb = min(tb, seq)
    n_layers = lb_param.shape[0]

    def col_block(off):
        return pl.BlockSpec((1, tb, HEAD_DIM), lambda b, h, t: (b, t, off + h))

    return pl.pallas_call(
        functools.partial(_hgrn_body, layer_j=layer_j, tb=tb),
        grid=(bsz, MIX_HEADS, seq // tb),
        in_specs=[
            col_block(0), col_block(MIX_HEADS), col_block(2 * MIX_HEADS), col_block(3 * MIX_HEADS),
            pl.BlockSpec((n_layers, HEAD_DIM), lambda b, h, t: (0, h)),
            pl.BlockSpec((1, HEAD_DIM), lambda b, h, t: (0, h)),
        ],
        out_specs=pl.BlockSpec((1, tb, HEAD_DIM), lambda b, h, t: (b, t, h)),
        out_shape=jax.ShapeDtypeStruct((bsz, seq, MIX_HEADS * HEAD_DIM), BF16),
        scratch_shapes=[pltpu.VMEM((HEAD_DIM, HEAD_DIM), F32)],
        compiler_params=_cparams("parallel", "parallel", "arbitrary"),
        name="hgrn2_mixer",
    )(proj, proj, proj, proj, lb_param, norm_g.reshape(1, -1))


def _log_sigmoid(z):
    return jnp.minimum(z, 0.0) - jnp.log1p(jnp.exp(-jnp.abs(z)))


def _fox_gate_body(x_ref, w_ref, wt_ref, bc_ref, br_ref, ccol_ref, crow_ref, carc_ref, carr_ref, *, tb):
    @pl.when(pl.program_id(1) == 0)
    def _():
        carc_ref[...] = jnp.zeros_like(carc_ref)
        carr_ref[...] = jnp.zeros_like(carr_ref)

    xb = x_ref[0].astype(BF16)
    ls_col = _log_sigmoid(_dot(xb, w_ref[...]) + bc_ref[...])
    ls_row = _log_sigmoid(_dot_nt(wt_ref[...], xb) + br_ref[...])
    row = lax.broadcasted_iota(I32, (tb, tb), 0)
    col = lax.broadcasted_iota(I32, (tb, tb), 1)
    lower = jnp.where(col <= row, 1.0, 0.0).astype(BF16)
    upper = jnp.where(row <= col, 1.0, 0.0).astype(BF16)
    c_col = carc_ref[...]
    for piece in _split3(ls_col):
        c_col = c_col + _dot(lower, piece)
    c_row = carr_ref[...]
    for piece in _split3(ls_row):
        c_row = c_row + _dot(piece, upper)
    ccol_ref[0] = c_col
    crow_ref[0] = c_row
    carc_ref[...] = c_col[tb - 1:tb, :]
    carr_ref[...] = c_row[:, tb - 1:tb]


def _fox_gate_cumsum(x, w_f, b_f, tb=512):
    bsz, seq, d = x.shape
    h = w_f.shape[1]
    tb = min(tb, seq)
    return pl.pallas_call(
        functools.partial(_fox_gate_body, tb=tb),
        grid=(bsz, seq // tb),
        in_specs=[
            pl.BlockSpec((1, tb, d), lambda b, t: (b, t, 0)),
            pl.BlockSpec((d, h), lambda b, t: (0, 0)),
            pl.BlockSpec((h, d), lambda b, t: (0, 0)),
            pl.BlockSpec((1, h), lambda b, t: (0, 0)),
            pl.BlockSpec((h, 1), lambda b, t: (0, 0)),
        ],
        out_specs=[
            pl.BlockSpec((1, tb, h), lambda b, t: (b, t, 0)),
            pl.BlockSpec((1, h, tb), lambda b, t: (b, 0, t)),
        ],
        out_shape=[
            jax.ShapeDtypeStruct((bsz, seq, h), F32),
            jax.ShapeDtypeStruct((bsz, h, seq), F32),
        ],
        scratch_shapes=[pltpu.VMEM((1, h), F32), pltpu.VMEM((h, 1), F32)],
        compiler_params=_cparams("parallel", "arbitrary"),
        name="fox_gate_cumsum",
    )(x, w_f.astype(BF16), w_f.T.astype(BF16), b_f.reshape(1, h), b_f.reshape(h, 1))


ATTN_ROWS = 128


def _softmax_probs(u, shift, rows, m_ref, l_ref):
    reps = u.shape[1] // HEAD_DIM
    m_old = m_ref[rows, :]
    if shift is None:
        m_new = jnp.maximum(m_old, jnp.max(u, axis=-1, keepdims=True))
        p = jnp.exp(u - jnp.tile(m_new, (1, reps)))
    else:
        m_new = jnp.maximum(m_old, jnp.max(u, axis=-1, keepdims=True) + shift)
        p = jnp.exp(u - jnp.tile(m_new - shift, (1, reps)))
    alpha = jnp.exp(m_old - m_new)
    l_ref[rows, :] = alpha * l_ref[rows, :] + jnp.sum(p, axis=-1, keepdims=True)
    m_ref[rows, :] = m_new
    return p.astype(BF16), alpha


def _accumulate(p, alpha, v, rows, acc_ref):
    acc_ref[rows, :] = alpha * acc_ref[rows, :] + _dot(p, v)


def _softmax_rows(u, shift, v, rows, m_ref, l_ref, acc_ref):
    p, alpha = _softmax_probs(u, shift, rows, m_ref, l_ref)
    _accumulate(p, alpha, v, rows, acc_ref)


def _softmax_init(m_ref, l_ref, acc_ref):
    m_ref[...] = jnp.full(m_ref.shape, NEG, F32)
    l_ref[...] = jnp.zeros_like(l_ref)
    acc_ref[...] = jnp.zeros_like(acc_ref)


def _fox_attn_body(q_ref, k_ref, v_ref, ccol_ref, crow_ref, o_ref, ct_ref, m_ref, l_ref, acc_ref, s_ref, *, tq):
    rs = min(ATTN_ROWS, tq)
    n_groups = tq // rs
    h = pl.program_id(1)
    qi = pl.program_id(2)
    ccol = ccol_ref[0]
    lane = lax.broadcasted_iota(I32, ccol.shape, 1)
    c_t = jnp.sum(jnp.where(lane == h, ccol, 0.0), axis=1, keepdims=True)
    ct_ref[...] = jnp.broadcast_to(c_t, ct_ref.shape)
    _softmax_init(m_ref, l_ref, acc_ref)

    def keys(kb):
        return pl.ds(pl.multiple_of(kb * tq, tq), tq)

    def logits(kb, r):
        rows = pl.ds(r * rs, rs)
        ks = keys(kb)
        s_ref[kb % 2, rows, :] = _dot_nt(q_ref[0, rows, :], k_ref[0, ks, :]) - crow_ref[0, pl.ds(h, 1), ks]

    def attend(kb, r, diagonal, prefetch):
        rows = pl.ds(r * rs, rs)
        u = s_ref[kb % 2, rows, :]
        if diagonal:
            row = lax.broadcasted_iota(I32, (rs, tq), 0) + r * rs
            col = lax.broadcasted_iota(I32, (rs, tq), 1)
            u = jnp.where(col <= row, u, NEG)
        if prefetch:
            logits(kb + 1, r)
        p, alpha = _softmax_probs(u, ct_ref[rows, :], rows, m_ref, l_ref)
        _accumulate(p, alpha, v_ref[0, keys(kb), :], rows, acc_ref)

    for r in range(n_groups):
        logits(0, r)

    def full_block(kb, carry):
        for r in range(n_groups):
            attend(kb, r, False, True)
        return carry

    lax.fori_loop(0, qi, full_block, 0)
    for r in range(n_groups):
        attend(qi, r, True, False)
    o_ref[0] = (acc_ref[...] / l_ref[...]).astype(o_ref.dtype)


def _fox_attention(qkv, c_col, c_row, tq=512):
    bsz, seq, _ = qkv.shape
    tq = min(tq, seq)
    nh = MIX_HEADS
    return pl.pallas_call(
        functools.partial(_fox_attn_body, tq=tq),
        grid=(bsz, nh, seq // tq),
        in_specs=[
            pl.BlockSpec((1, tq, HEAD_DIM), lambda b, h, i: (b, i, h)),
            pl.BlockSpec((1, seq, HEAD_DIM), lambda b, h, i: (b, 0, nh + h)),
            pl.BlockSpec((1, seq, HEAD_DIM), lambda b, h, i: (b, 0, 2 * nh + h)),
            pl.BlockSpec((1, tq, nh), lambda b, h, i: (b, i, 0)),
            pl.BlockSpec((1, nh, seq), lambda b, h, i: (b, 0, 0)),
        ],
        out_specs=pl.BlockSpec((1, tq, HEAD_DIM), lambda b, h, i: (b, i, h)),
        out_shape=jax.ShapeDtypeStruct((bsz, seq, nh * HEAD_DIM), BF16),
        scratch_shapes=[pltpu.VMEM((tq, HEAD_DIM), F32)] * 4 + [pltpu.VMEM((2, tq, tq), F32)],
        compiler_params=_cparams("parallel", "parallel", "arbitrary"),
        name="fox_attention",
    )(qkv, qkv, qkv, c_col, c_row)


def _t5_bucket_thresholds():
    n = np.arange(0, 4 * REL_MAX_DIST, dtype=np.int64)
    max_exact = REL_BUCKETS // 2
    ratio = np.log(np.maximum(n, 1).astype(np.float32) / np.float32(max_exact)) / np.float32(
        math.log(REL_MAX_DIST / max_exact))
    large = np.minimum(max_exact + (ratio * np.float32(REL_BUCKETS - max_exact)).astype(np.int32),
                       REL_BUCKETS - 1)
    bucket = np.where(n < max_exact, n, large)
    assert np.all(np.diff(bucket) >= 0) and bucket[-1] == REL_BUCKETS - 1
    return [int(np.argmax(bucket >= b)) for b in range(REL_BUCKETS)]


def _moba_body(rb_ref, q_ref, k_ref, v_ref, o_ref, kmean_ref, bias_ref, sel_ref, m_ref, l_ref, acc_ref,
               s_ref, *, n_blocks):
    blk = MOBA_BLOCK
    h = pl.program_id(1)
    qi = pl.program_id(2)
    far_bias = rb_ref[REL_BUCKETS - 1, h]

    @pl.when(qi == 0)
    def _():
        r = lax.broadcasted_iota(I32, (n_blocks, n_blocks * blk), 0)
        c = lax.broadcasted_iota(I32, (n_blocks, n_blocks * blk), 1)
        pool = jnp.where(c // blk == r, 1.0, 0.0).astype(BF16)
        kmean_ref[...] = _dot(pool, k_ref[0]) * (1.0 / blk)
        row = lax.broadcasted_iota(I32, (blk, blk), 0)
        col = lax.broadcasted_iota(I32, (blk, blk), 1)
        thresholds = _t5_bucket_thresholds()
        for slot, dist in ((0, row - col + blk), (1, row - col)):
            val = jnp.full((blk, blk), rb_ref[0, h], F32)
            for b in range(1, REL_BUCKETS):
                val = jnp.where(dist >= thresholds[b], rb_ref[b, h], val)
            if slot == 1:
                val = jnp.where(dist >= 0, val, NEG)
            bias_ref[slot] = val

    q = q_ref[0]
    km_hi = kmean_ref[...].astype(BF16)
    km_lo = (kmean_ref[...] - km_hi.astype(F32)).astype(BF16)
    gate = _dot_nt(q, km_hi) + _dot_nt(q, km_lo)
    lane = lax.broadcasted_iota(I32, gate.shape, 1)
    gate = jnp.where(lane < qi, gate, NEG)
    sel = jnp.zeros(gate.shape, F32)
    for _ in range(MOBA_TOPK):
        best = jnp.max(gate, axis=1, keepdims=True)
        first = jnp.min(jnp.where(gate == best, lane, n_blocks), axis=1, keepdims=True)
        pick = jnp.logical_and(lane == first, best > 0.5 * NEG)
        sel = jnp.where(pick, 1.0, sel)
        gate = jnp.where(lane == first, NEG, gate)
    sel_ref[...] = sel

    _softmax_init(m_ref, l_ref, acc_ref)
    rs = min(ATTN_ROWS, blk)
    n_groups = blk // rs

    def sel_col(j):
        lanes = lax.broadcasted_iota(I32, (blk, n_blocks), 1)
        return jnp.sum(jnp.where(lanes == j, sel_ref[...], 0.0), axis=1, keepdims=True) > 0.5

    def finish():
        o_ref[0] = (acc_ref[...] / l_ref[...]).astype(o_ref.dtype)

    @pl.when(qi == 0)
    def _():
        for r in range(n_groups):
            rows = pl.ds(r * rs, rs)
            u = _dot_nt(q_ref[0, rows, :], k_ref[0, 0:blk, :]) + bias_ref[1, r * rs:(r + 1) * rs, :]
            _softmax_rows(u, None, v_ref[0, 0:blk, :], rows, m_ref, l_ref, acc_ref)
        finish()

    @pl.when(qi >= 1)
    def _():
        n_far = qi - 1
        odd = n_far % 2
        n_far_units = (n_far + odd) // 2

        def first_block(unit):
            return jnp.maximum(2 * unit - odd, 0)

        def keys(unit):
            return pl.ds(pl.multiple_of(first_block(unit) * blk, blk), 2 * blk)

        def logits(unit, r):
            rows = pl.ds(r * rs, rs)
            s_ref[unit % 2, rows, :] = _dot_nt(q_ref[0, rows, :], k_ref[0, keys(unit), :])

        def attend(unit, r, extra_a, extra_b, prefetch):
            rows = pl.ds(r * rs, rs)
            u = s_ref[unit % 2, rows, :]
            u = jnp.concatenate([u[:, :blk] + extra_a, u[:, blk:] + extra_b], axis=1)
            if prefetch:
                logits(unit + 1, r)
            p, alpha = _softmax_probs(u, None, rows, m_ref, l_ref)
            _accumulate(p, alpha, v_ref[0, keys(unit), :], rows, acc_ref)

        for r in range(n_groups):
            logits(0, r)

        def far_unit(unit, carry):
            j = first_block(unit)
            masked_b = jnp.logical_and(unit == 0, odd == 1)
            bias_a = jnp.where(sel_col(j), far_bias, NEG)
            bias_b = jnp.where(jnp.logical_and(sel_col(j + 1), jnp.logical_not(masked_b)), far_bias, NEG)
            for r in range(n_groups):
                attend(unit, r, bias_a[r * rs:(r + 1) * rs], bias_b[r * rs:(r + 1) * rs], True)
            return carry

        lax.fori_loop(0, n_far_units, far_unit, 0)

        prev_mask = jnp.where(sel_col(qi - 1), 0.0, NEG)
        for r in range(n_groups):
            rr = slice(r * rs, (r + 1) * rs)
            attend(n_far_units, r, bias_ref[0, rr, :] + prev_mask[rr], bias_ref[1, rr, :], False)
        finish()


def _moba_attention(qkv, rel_bias):
    bsz, seq, _ = qkv.shape
    nh = MIX_HEADS
    blk = MOBA_BLOCK
    assert seq % blk == 0
    n_blocks = seq // blk
    grid_spec = pltpu.PrefetchScalarGridSpec(
        num_scalar_prefetch=0,
        grid=(bsz, nh, n_blocks),
        in_specs=[
            pl.BlockSpec(memory_space=pltpu.SMEM),
            pl.BlockSpec((1, blk, HEAD_DIM), lambda b, h, i: (b, i, h)),
            pl.BlockSpec((1, seq, HEAD_DIM), lambda b, h, i: (b, 0, nh + h)),
            pl.BlockSpec((1, seq, HEAD_DIM), lambda b, h, i: (b, 0, 2 * nh + h)),
        ],
        out_specs=pl.BlockSpec((1, blk, HEAD_DIM), lambda b, h, i: (b, i, h)),
        scratch_shapes=[
            pltpu.VMEM((n_blocks, HEAD_DIM), F32),
            pltpu.VMEM((2, blk, blk), F32),
            pltpu.VMEM((blk, n_blocks), F32),
            pltpu.VMEM((blk, HEAD_DIM), F32), pltpu.VMEM((blk, HEAD_DIM), F32), pltpu.VMEM((blk, HEAD_DIM), F32),
            pltpu.VMEM((2, blk, 2 * blk), F32),
        ],
    )
    return pl.pallas_call(
        functools.partial(_moba_body, n_blocks=n_blocks),
        grid_spec=grid_spec,
        out_shape=jax.ShapeDtypeStruct((bsz, seq, nh * HEAD_DIM), BF16),
        compiler_params=_cparams("parallel", "parallel", "arbitrary"),
        name="moba_attention",
    )(rel_bias, qkv, qkv, qkv)


def _mem_attn_body(q_ref, kv_ref, o_ref):
    width = MEM_HEADS * HEAD_DIM
    for hh in range(MEM_HEADS):
        cs = slice(hh * HEAD_DIM, (hh + 1) * HEAD_DIM)
        q = q_ref[0, :, cs].astype(BF16)
        k = kv_ref[0, :, cs]
        v = kv_ref[0, :, width + hh * HEAD_DIM:width + (hh + 1) * HEAD_DIM]
        s = _dot_nt(q, k)
        p = jnp.exp(s - jnp.max(s, axis=-1, keepdims=True))
        o = _dot(p.astype(BF16), v) / jnp.sum(p, axis=-1, keepdims=True)
        o_ref[0, :, cs] = o.astype(o_ref.dtype)


def _memory_attention(proj, q_col_block, kv, tq=512):
    bsz, seq, _ = proj.shape
    tq = min(tq, seq)
    width = MEM_HEADS * HEAD_DIM
    n_mem = kv.shape[1]
    return pl.pallas_call(
        _mem_attn_body,
        grid=(bsz, seq // tq),
        in_specs=[
            pl.BlockSpec((1, tq, width), lambda b, i: (b, i, q_col_block)),
            pl.BlockSpec((1, n_mem, 2 * width), lambda b, i: (b, 0, 0)),
        ],
        out_specs=pl.BlockSpec((1, tq, width), lambda b, i: (b, i, 0)),
        out_shape=jax.ShapeDtypeStruct((bsz, seq, width), BF16),
        compiler_params=_cparams("parallel", "parallel"),
        name="memory_attention",
    )(proj, kv)


def _out_ln_body(mix_ref, mem_ref, x_ref, w1_ref, w2_ref, g_ref, b_ref, o_ref, *, alpha):
    hproj = _dot(mix_ref[...], w1_ref[...]) + _dot(mem_ref[...], w2_ref[...])
    o_ref[...] = _layer_norm(alpha * x_ref[...] + hproj, g_ref[...], b_ref[...])


def _out_proj_ln(mix, mem_out, x, w_out, g, b, alpha, tm=512):
    t, d = x.shape
    tm = min(tm, t)
    wm = mix.shape[1]
    we = mem_out.shape[1]
    return pl.pallas_call(
        functools.partial(_out_ln_body, alpha=alpha),
        grid=(t // tm,),
        in_specs=[
            pl.BlockSpec((tm, wm), lambda i: (i, 0)),
            pl.BlockSpec((tm, we), lambda i: (i, 0)),
            pl.BlockSpec((tm, d), lambda i: (i, 0)),
            pl.BlockSpec((wm, d), lambda i: (0, 0)),
            pl.BlockSpec((we, d), lambda i: (0, 0)),
            pl.BlockSpec((1, d), lambda i: (0, 0)),
            pl.BlockSpec((1, d), lambda i: (0, 0)),
        ],
        out_specs=pl.BlockSpec((tm, d), lambda i: (i, 0)),
        out_shape=jax.ShapeDtypeStruct((t, d), F32),
        compiler_params=_cparams("parallel"),
        name="out_proj_ln",
    )(mix, mem_out, x, w_out[:wm].astype(BF16), w_out[wm:].astype(BF16), g.reshape(1, d), b.reshape(1, d))


def _router_body(x_ref, wt_ref, b_ref, idx_ref, gate_ref, cnt_ref, run_ref, *, tb):
    @pl.when(pl.program_id(0) == 0)
    def _():
        run_ref[...] = jnp.zeros_like(run_ref)

    x = x_ref[...]
    xh = x.astype(BF16)
    xl = (x - xh.astype(F32)).astype(BF16)
    w = wt_ref[...]
    wh = w.astype(BF16)
    wl = (w - wh.astype(F32)).astype(BF16)
    logits = _dot_nt(wh, xh) + _dot_nt(wh, xl) + _dot_nt(wl, xh) + b_ref[...]

    n_e = logits.shape[0]
    eid = lax.broadcasted_iota(I32, logits.shape, 0)
    vals, idxs, hots = [], [], []
    multi = jnp.zeros(logits.shape, F32)
    for _ in range(TOP_K):
        best = jnp.max(logits, axis=0, keepdims=True)
        first = jnp.min(jnp.where(logits == best, eid, n_e), axis=0, keepdims=True)
        hot = eid == first
        vals.append(best)
        idxs.append(first)
        hots.append(hot)
        multi = jnp.where(hot, 1.0, multi)
        logits = jnp.where(hot, -jnp.inf, logits)

    exps = [jnp.exp(v - vals[0]) for v in vals]
    den = exps[0]
    for e in exps[1:]:
        den = den + e

    r = lax.broadcasted_iota(I32, (tb, tb), 0)
    c = lax.broadcasted_iota(I32, (tb, tb), 1)
    before = jnp.where(r < c, 1.0, 0.0).astype(BF16)
    pos = _dot(multi.astype(BF16), before) + run_ref[:, 0:1]
    for kk in range(TOP_K):
        idx_ref[kk:kk + 1, :] = idxs[kk]
        rank = jnp.sum(jnp.where(hots[kk], pos, 0.0), axis=0, keepdims=True)
        idx_ref[TOP_K + kk:TOP_K + kk + 1, :] = rank.astype(I32)
        gate_ref[kk:kk + 1, :] = exps[kk] / den
    gate_ref[TOP_K:2 * TOP_K, :] = jnp.zeros((TOP_K, tb), F32)
    run_ref[...] = run_ref[...] + jnp.sum(multi, axis=1, keepdims=True)
    cnt_ref[...] = run_ref[...]


def _router(x, router_w, router_b, tb=512):
    t, d = x.shape
    tb = min(tb, t)
    n_e = router_w.shape[1]
    return pl.pallas_call(
        functools.partial(_router_body, tb=tb),
        grid=(t // tb,),
        in_specs=[
            pl.BlockSpec((tb, d), lambda i: (i, 0)),
            pl.BlockSpec((n_e, d), lambda i: (0, 0)),
            pl.BlockSpec((n_e, 1), lambda i: (0, 0)),
        ],
        out_specs=[
            pl.BlockSpec((2 * TOP_K, tb), lambda i: (0, i)),
            pl.BlockSpec((2 * TOP_K, tb), lambda i: (0, i)),
            pl.BlockSpec((n_e, HEAD_DIM), lambda i: (0, 0)),
        ],
        out_shape=[
            jax.ShapeDtypeStruct((2 * TOP_K, t), I32),
            jax.ShapeDtypeStruct((2 * TOP_K, t), F32),
            jax.ShapeDtypeStruct((n_e, HEAD_DIM), F32),
        ],
        scratch_shapes=[pltpu.VMEM((n_e, HEAD_DIM), F32)],
        compiler_params=_cparams("arbitrary"),
        name="moe_router",
    )(x, router_w.T, router_b.reshape(n_e, 1))


def _dispatch_body(dest_ref, x_ref, xs_in, xs_hbm, sem, *, tb, n_tok):
    del xs_in
    base = pl.program_id(0) * tb

    def issue(t, carry):
        for kk in range(TOP_K):
            d = dest_ref[kk * n_tok + base + t]
            pltpu.make_async_copy(x_ref.at[pl.ds(t, 1)], xs_hbm.at[pl.ds(d, 1)], sem).start()
        return carry

    lax.fori_loop(0, tb, issue, 0)
    for kk in range(TOP_K):
        pltpu.make_async_copy(x_ref, xs_hbm.at[pl.ds(0, tb)], sem).wait()


def _dispatch(x, dest_flat, n_rows, tb=512):
    t, d = x.shape
    tb = min(tb, t)
    grid_spec = pltpu.PrefetchScalarGridSpec(
        num_scalar_prefetch=1,
        grid=(t // tb,),
        in_specs=[pl.BlockSpec((tb, d), lambda i, dest: (i, 0)), pl.BlockSpec(memory_space=pl.ANY)],
        out_specs=pl.BlockSpec(memory_space=pl.ANY),
        scratch_shapes=[pltpu.SemaphoreType.DMA(())],
    )
    return pl.pallas_call(
        functools.partial(_dispatch_body, tb=tb, n_tok=t),
        grid_spec=grid_spec,
        out_shape=jax.ShapeDtypeStruct((n_rows, d), x.dtype),
        input_output_aliases={2: 0},
        compiler_params=_cparams("arbitrary"),
        name="moe_dispatch",
    )(dest_flat, x, jnp.zeros((n_rows, d), x.dtype))


UNZIP = 256


def _expert_body(be_ref, nu_ref, first_ref, slot_ref, next_ref, x_ref, wu_hbm, bg_ref, bl_ref, wd_hbm, bd_ref,
                 o_ref, wu_buf, wd_buf, wg_s, wl_s, wd_s, sem, *, layer):
    i = pl.program_id(0)
    n_used = nu_ref[0]

    def weight_copies(expert, slot):
        return (pltpu.make_async_copy(wu_hbm.at[layer, expert], wu_buf.at[slot], sem.at[0, slot]),
                pltpu.make_async_copy(wd_hbm.at[layer, expert], wd_buf.at[slot], sem.at[1, slot]))

    @pl.when(i == 0)
    def _():
        for cp in weight_copies(be_ref[0], 0):
            cp.start()

    @pl.when(jnp.logical_and(first_ref[i] == 1, i < n_used))
    def _():
        slot = slot_ref[i]
        for cp in weight_copies(be_ref[i], slot):
            cp.wait()

        @pl.when(next_ref[i] >= 0)
        def _():
            for cp in weight_copies(next_ref[i], 1 - slot):
                cp.start()

        half = UNZIP // 2
        r = lax.broadcasted_iota(I32, (UNZIP, UNZIP), 0)
        c = lax.broadcasted_iota(I32, (UNZIP, UNZIP), 1)
        src = jnp.where(c < half, 2 * c, 2 * (c - half) + 1)
        perm = jnp.where(r == src, 1.0, 0.0).astype(BF16)
        for g in range(wu_buf.shape[2] // UNZIP):
            w = wu_buf[slot, :, g * UNZIP:(g + 1) * UNZIP].astype(BF16)
            sep = _dot(w, perm).astype(BF16)
            wg_s[:, g * half:(g + 1) * half] = sep[:, :half]
            wl_s[:, g * half:(g + 1) * half] = sep[:, half:]
        wd_s[...] = wd_buf[slot].astype(BF16)

    @pl.when(i < n_used)
    def _():
        xb = x_ref[...].astype(BF16)
        glu = jnp.minimum(_dot(xb, wg_s[...]) + bg_ref[0], SWIGLU_LIMIT)
        lin = jnp.clip(_dot(xb, wl_s[...]) + bl_ref[0], -SWIGLU_LIMIT, SWIGLU_LIMIT)
        act = glu * _sigmoid(SWIGLU_ALPHA * glu) * (lin + 1.0)
        o_ref[...] = _dot(act.astype(BF16), wd_s[...]) + bd_ref[0]

    @pl.when(i >= n_used)
    def _():
        o_ref[...] = jnp.zeros_like(o_ref)


def _expert_mlp(xs, blk_expert, n_used, counts, layer, w_up, bg, bl, w_down, bd):
    p_rows, d = xs.shape
    f = w_down.shape[2]
    n_e = w_down.shape[1]
    n_blk = p_rows // MOE_BLOCK

    blocks = jnp.arange(n_blk, dtype=I32)
    valid = blocks < n_used[0]
    prev_e = jnp.concatenate([jnp.full((1,), -1, I32), blk_expert[:-1]])
    first = jnp.logical_and(valid, blk_expert != prev_e)
    slot = (jnp.cumsum(first.astype(I32)) - 1) % 2
    experts = jnp.arange(n_e, dtype=I32)
    later = jnp.logical_and(experts[None, :] > blk_expert[:, None], counts[None, :] > 0)
    nxt = jnp.min(jnp.where(later, experts[None, :], n_e), axis=1)
    nxt = jnp.where(nxt == n_e, -1, nxt)

    def blk(i, nu):
        return jnp.minimum(i, nu[0] - 1)

    def w_spec(shape):
        return pl.BlockSpec((1,) + shape, lambda i, be, nu, fi, sl, nx: (be[blk(i, nu)], 0, 0))

    grid_spec = pltpu.PrefetchScalarGridSpec(
        num_scalar_prefetch=5,
        grid=(n_blk,),
        in_specs=[
            pl.BlockSpec((MOE_BLOCK, d), lambda i, be, nu, fi, sl, nx: (blk(i, nu), 0)),
            pl.BlockSpec(memory_space=pl.ANY), w_spec((1, f)), w_spec((1, f)),
            pl.BlockSpec(memory_space=pl.ANY), w_spec((1, d)),
        ],
        out_specs=pl.BlockSpec((MOE_BLOCK, d), lambda i, be, nu, fi, sl, nx: (i, 0)),
        scratch_shapes=[
            pltpu.VMEM((2, d, 2 * f), F32), pltpu.VMEM((2, f, d), F32),
            pltpu.VMEM((d, f), BF16), pltpu.VMEM((d, f), BF16), pltpu.VMEM((f, d), BF16),
            pltpu.SemaphoreType.DMA((2, 2)),
        ],
    )
    return pl.pallas_call(
        functools.partial(_expert_body, layer=layer),
        grid_spec=grid_spec,
        out_shape=jax.ShapeDtypeStruct((p_rows, d), F32),
        compiler_params=_cparams("arbitrary"),
        name="moe_experts",
    )(blk_expert, n_used, first.astype(I32), slot.astype(I32), nxt.astype(I32), xs, w_up, bg, bl, w_down, bd)


def _combine_body(dest_ref, ys_hbm, gate_ref, x_ref, g_ref, b_ref, o_ref, buf_ref, sem_ref,
                  *, tb, n_tok, alpha):
    i = pl.program_id(0)
    n_steps = pl.num_programs(0)

    def issue(step, slot):
        base = step * tb

        def one(t, carry):
            for kk in range(TOP_K):
                d = dest_ref[kk * n_tok + base + t]
                pltpu.make_async_copy(ys_hbm.at[pl.ds(d, 1)], buf_ref.at[slot, kk, pl.ds(t, 1)],
                                      sem_ref.at[slot]).start()
            return carry

        lax.fori_loop(0, tb, one, 0)

    @pl.when(i == 0)
    def _():
        issue(0, 0)

    slot = lax.rem(i, 2)

    @pl.when(i + 1 < n_steps)
    def _():
        issue(i + 1, 1 - slot)

    for kk in range(TOP_K):
        pltpu.make_async_copy(ys_hbm.at[pl.ds(0, tb)], buf_ref.at[slot, kk], sem_ref.at[slot]).wait()

    g8 = gate_ref[...]
    gpad = jnp.concatenate([g8, jnp.zeros((HEAD_DIM - g8.shape[0], tb), F32)], axis=0)
    gt = gpad.T
    f = gt[:, 0:1] * buf_ref[slot, 0]
    for kk in range(1, TOP_K):
        f = f + gt[:, kk:kk + 1] * buf_ref[slot, kk]
    o_ref[...] = _layer_norm(alpha * x_ref[...] + f, g_ref[...], b_ref[...])


def _combine_ln(ys, dest_flat, gates, x, g, b, alpha, tb=256):
    t, d = x.shape
    tb = min(tb, t)
    grid_spec = pltpu.PrefetchScalarGridSpec(
        num_scalar_prefetch=1,
        grid=(t // tb,),
        in_specs=[
            pl.BlockSpec(memory_space=pl.ANY),
            pl.BlockSpec((2 * TOP_K, tb), lambda i, dest: (0, i)),
            pl.BlockSpec((tb, d), lambda i, dest: (i, 0)),
            pl.BlockSpec((1, d), lambda i, dest: (0, 0)),
            pl.BlockSpec((1, d), lambda i, dest: (0, 0)),
        ],
        out_specs=pl.BlockSpec((tb, d), lambda i, dest: (i, 0)),
        scratch_shapes=[pltpu.VMEM((2, TOP_K, tb, d), F32), pltpu.SemaphoreType.DMA((2,))],
    )
    return pl.pallas_call(
        functools.partial(_combine_body, tb=tb, n_tok=t, alpha=alpha),
        grid_spec=grid_spec,
        out_shape=jax.ShapeDtypeStruct((t, d), F32),
        compiler_params=_cparams("arbitrary"),
        name="moe_combine_ln",
    )(dest_flat, ys, gates, x, g.reshape(1, d), b.reshape(1, d))


def _moe_ln(x, router_w, router_b, layer, w_up, b_up, w_down, b_down, g, b, alpha):
    t, d = x.shape
    n_e = router_w.shape[1]
    idx_rank, gates, cnt = _router(x, router_w, router_b)
    counts = cnt[:, 0].astype(I32)
    padded = ((counts + MOE_BLOCK - 1) // MOE_BLOCK) * MOE_BLOCK
    p_ends = jnp.cumsum(padded)
    p_starts = p_ends - padded
    n_blk = (t * TOP_K + MOE_BLOCK - 1) // MOE_BLOCK + n_e
    experts = jnp.arange(n_e, dtype=I32)
    start_of = jnp.sum(jnp.where(idx_rank[:TOP_K, :, None] == experts, p_starts, 0), axis=-1)
    dest_flat = (start_of + idx_rank[TOP_K:]).reshape(-1)
    blk_first_row = jnp.arange(n_blk, dtype=I32) * MOE_BLOCK
    blk_expert = jnp.clip(jnp.sum((p_ends[None, :] <= blk_first_row[:, None]).astype(I32), axis=1), 0, n_e - 1)
    n_used = (p_ends[-1:] // MOE_BLOCK).astype(I32)

    xs = _dispatch(x, dest_flat, n_blk * MOE_BLOCK)
    bg = b_up[:, None, 0::2]
    bl = b_up[:, None, 1::2]
    ys = _expert_mlp(xs, blk_expert, n_used, counts, layer, w_up, bg, bl, w_down, b_down[:, None, :])
    return _combine_ln(ys, dest_flat, gates, x, g, b, alpha)


def kernel(x, mem, hgrn_w_in, hgrn_lb, hgrn_norm_g, fox_w_in, fox_b_f, moba_w_in, rel_bias, w_mem_kv, w_out,
           ln1_g, ln1_b, router_w, router_b, w_up, b_up, w_down, b_down, ln2_g, ln2_b):
    bsz, seq, d = x.shape
    depth = w_out.shape[0]
    t = bsz * seq
    mix_w = MIX_HEADS * HEAD_DIM
    mem_w = MEM_HEADS * HEAD_DIM
    alpha = (2 * depth) ** 0.25
    scale = HEAD_DIM ** -0.5
    n_mem = mem.shape[1]
    mem2 = mem.reshape(bsz * n_mem, d)

    def scale_vec(n_scaled_front, n_plain, n_scaled_back):
        return jnp.concatenate([
            jnp.full((1, n_scaled_front), scale, F32), jnp.ones((1, n_plain), F32),
            jnp.full((1, n_scaled_back), scale, F32)], axis=1)

    x2 = x.reshape(t, d)
    for i in range(depth):
        kind, j = i % 3, i // 3
        if kind == 0:
            w_in = hgrn_w_in[j].astype(BF16)
            proj = _matmul(x2, w_in, scale_vec(0, 4 * mix_w, mem_w), F32, 1024, 512)
            proj = proj.reshape(bsz, seq, -1)
            mix = _hgrn_mixer(proj, hgrn_lb, hgrn_norm_g[j], j)
            memq_block = 4 * mix_w // mem_w
        elif kind == 1:
            w = fox_w_in[j]
            w_in = jnp.concatenate([w[:, :3 * mix_w], w[:, 3 * mix_w + MIX_HEADS:]], axis=1).astype(BF16)
            proj = _matmul(x2, w_in, scale_vec(mix_w, 2 * mix_w, mem_w), BF16, 1024, 512)
            proj = proj.reshape(bsz, seq, -1)
            c_col, c_row = _fox_gate_cumsum(x2.reshape(bsz, seq, d), w[:, 3 * mix_w:3 * mix_w + MIX_HEADS],
                                            fox_b_f[j])
            mix = _fox_attention(proj, c_col, c_row)
            memq_block = 3 * mix_w // mem_w
        else:
            w_in = moba_w_in[j].astype(BF16)
            proj = _matmul(x2, w_in, scale_vec(mix_w, 2 * mix_w, mem_w), BF16, 1024, 512)
            proj = proj.reshape(bsz, seq, -1)
            mix = _moba_attention(proj, rel_bias)
            memq_block = 3 * mix_w // mem_w
        kv = _matmul(mem2, w_mem_kv[i].astype(BF16), jnp.ones((1, 2 * mem_w), F32), BF16, 512, 512)
        mem_out = _memory_attention(proj, memq_block, kv.reshape(bsz, n_mem, 2 * mem_w))
        x2 = _out_proj_ln(mix.reshape(t, mix_w), mem_out.reshape(t, mem_w), x2, w_out[i],
                          ln1_g[i], ln1_b[i], alpha)
        x2 = _moe_ln(x2, router_w[i], router_b[i], i, w_up, b_up[i], w_down, b_down[i],
                     ln2_g[i], ln2_b[i], alpha)
    return x2.reshape(bsz, seq, d)
```

```python
import functools
import math

import numpy as np
import jax
import jax.numpy as jnp
from jax import lax
from jax.experimental import pallas as pl
from jax.experimental.pallas import tpu as pltpu

F32 = jnp.float32
BF16 = jnp.bfloat16
I32 = jnp.int32

HEAD_DIM = 128
MIX_HEADS = 8
MEM_HEADS = 4
HGRN_CHUNK = 64
HGRN_SUB = 16
HGRN_EXP_CLAMP = 80.0
MOBA_BLOCK = 256
MOBA_TOPK = 3
REL_BUCKETS = 32
REL_MAX_DIST = 128
N_EXPERTS = 32
TOP_K = 4
MOE_BLOCK = 256
SWIGLU_LIMIT = 7.0
SWIGLU_ALPHA = 1.702
LN_EPS = 1e-5
RMS_EPS = 1e-6
NEG = -1e30
VMEM_LIMIT = 56 * 1024 * 1024
DMA_UNROLL = 8

NT_DIMS = (((1,), (1,)), ((), ()))
TN_DIMS = (((0,), (0,)), ((), ()))


def _cparams(*sem):
    return pltpu.CompilerParams(dimension_semantics=sem, vmem_limit_bytes=VMEM_LIMIT)


def _dot(a, b):
    return jnp.dot(a, b, preferred_element_type=F32)


def _dot_nt(a, b):
    return lax.dot_general(a, b, NT_DIMS, preferred_element_type=F32)


def _dot_tn(a, b):
    return lax.dot_general(a, b, TN_DIMS, preferred_element_type=F32)


def _split3(a):
    hi = a.astype(BF16)
    r = a - hi.astype(F32)
    mid = r.astype(BF16)
    lo = (r - mid.astype(F32)).astype(BF16)
    return hi, mid, lo


def _sigmoid(x):
    return 1.0 / (1.0 + jnp.exp(-x))


def _layer_norm(z, g, b):
    mu = jnp.mean(z, axis=-1, keepdims=True)
    zc = z - mu
    var = jnp.mean(zc * zc, axis=-1, keepdims=True)
    return zc * lax.rsqrt(var + LN_EPS) * g + b


def _mm_body(x_ref, w_ref, s_ref, o_ref, xb_ref):
    @pl.when(pl.program_id(1) == 0)
    def _():
        xb_ref[...] = x_ref[...].astype(BF16)

    acc = _dot(xb_ref[...], w_ref[...])
    o_ref[...] = (acc * s_ref[...]).astype(o_ref.dtype)


def _matmul(x, w, col_scale, out_dtype, tm, tn):
    m, k = x.shape
    n = w.shape[1]
    tm = min(tm, m)
    tn = min(tn, n)
    return pl.pallas_call(
        _mm_body,
        grid=(m // tm, n // tn),
        in_specs=[
            pl.BlockSpec((tm, k), lambda i, j: (i, 0)),
            pl.BlockSpec((k, tn), lambda i, j: (0, j)),
            pl.BlockSpec((1, tn), lambda i, j: (0, j)),
        ],
        out_specs=pl.BlockSpec((tm, tn), lambda i, j: (i, j)),
        out_shape=jax.ShapeDtypeStruct((m, n), out_dtype),
        scratch_shapes=[pltpu.VMEM((tm, k), BF16)],
        compiler_params=_cparams("parallel", "arbitrary"),
        name="proj_matmul",
    )(x, w, col_scale)


def _hgrn_body(q_ref, z_ref, i_ref, g_ref, lbp_ref, ng_ref, o_ref, st_ref, *, layer_j, tb):
    c_len, sb = HGRN_CHUNK, HGRN_SUB
    n_sub = c_len // sb

    @pl.when(pl.program_id(2) == 0)
    def _():
        st_ref[...] = jnp.zeros_like(st_ref)

    lbp = lbp_ref[...]
    ex = jnp.exp(lbp - jnp.max(lbp, axis=0, keepdims=True))
    p = ex / jnp.sum(ex, axis=0, keepdims=True)
    lb = jnp.zeros((1, HEAD_DIM), F32)
    for r in range(1, layer_j + 1):
        lb = lb + p[r:r + 1, :]
    log_lb = jnp.log(lb)
    log1m_lb = jnp.log1p(-lb)
    one_m_lb = 1.0 - lb
    ng = ng_ref[...]

    row = lax.broadcasted_iota(I32, (c_len, c_len), 0)
    col = lax.broadcasted_iota(I32, (c_len, c_len), 1)
    sub_start = (row // sb) * sb
    one = jnp.ones((c_len, c_len), F32)
    zero = jnp.zeros((c_len, c_len), F32)
    m_tri = jnp.where(col <= row, one, zero)
    m_in = jnp.where(col > sub_start, m_tri, zero)
    m_dec = jnp.where(col > row, one, zero)
    m_ke = jnp.where(col <= sub_start + (sb - 1), m_dec, zero)
    stack = jnp.concatenate([m_tri, m_in, m_ke, m_dec], axis=0).astype(BF16)
    same_sub = (row // sb) == (col // sb)
    diag_mask = jnp.logical_and(same_sub, col <= row)
    row_sub = row // sb
    krow_sub = lax.broadcasted_iota(I32, (c_len, HEAD_DIM), 0) // sb

    chunks = [pl.ds(c * c_len, c_len) for c in range(tb // c_len)]

    gates = []
    for sl in chunks:
        z = z_ref[0, sl, :]
        e = jnp.exp(-jnp.abs(z))
        inv = 1.0 / (1.0 + e)
        sig_neg = jnp.where(z >= 0, e, 1.0) * inv
        log_sig = jnp.minimum(z, 0.0) - jnp.log1p(e)
        a2 = log1m_lb + log_sig
        log_f = jnp.maximum(log_lb, a2) + jnp.log1p(jnp.exp(-jnp.abs(log_lb - a2)))
        qv = q_ref[0, sl, :]
        gates.append((log_f, one_m_lb * sig_neg, qv * _sigmoid(qv), i_ref[0, sl, :].astype(BF16)))

    prefix = []
    for log_f, _, _, _ in gates:
        pre = None
        for piece in _split3(log_f):
            t = _dot(stack, piece)
            pre = t if pre is None else pre + t
        prefix.append(pre)

    factors = []
    for (_, k, qf, _), pre in zip(gates, prefix):
        b = pre[0:c_len]
        d_in = pre[c_len:2 * c_len]
        d_ke = pre[2 * c_len:3 * c_len]
        d_dec = pre[3 * c_len:4 * c_len]
        q_in = qf * jnp.exp(d_in)
        k_diag = (k * jnp.exp(jnp.minimum(-d_in, HGRN_EXP_CLAMP))).astype(BF16)
        k_end = k * jnp.exp(d_ke)
        k_dec = (k * jnp.exp(d_dec)).astype(BF16)
        q_all = (qf * jnp.exp(b)).astype(BF16)
        b_ref = b - d_in
        lhs = [q_in.astype(BF16)]
        rhs = [k_diag]
        for j in range(n_sub - 1):
            b_end = b[(j + 1) * sb - 1:(j + 1) * sb, :]
            cross = jnp.exp(jnp.minimum(b_ref - b_end, 0.0))
            lhs.append((q_in * cross).astype(BF16))
            rhs.append(jnp.where(krow_sub == j, k_end, 0.0).astype(BF16))
        factors.append((lhs, rhs, k_dec, q_all, jnp.exp(b[c_len - 1:c_len, :])))

    scores = []
    for lhs, rhs, _, _, _ in factors:
        a = jnp.where(diag_mask, _dot_nt(lhs[0], rhs[0]), 0.0)
        for j in range(n_sub - 1):
            a = a + jnp.where(row_sub > j, _dot_nt(lhs[j + 1], rhs[j + 1]), 0.0)
        scores.append(a.astype(BF16))

    updates = [_dot_tn(vb, k_dec) for (_, _, _, vb), (_, _, k_dec, _, _) in zip(gates, factors)]
    states = [st_ref[...]]
    for (_, _, _, _, decay), kv in zip(factors, updates):
        states.append(states[-1] * decay + kv)
    st_ref[...] = states[-1]

    for sl, (_, _, _, vb), (_, _, _, q_all, _), a, st in zip(chunks, gates, factors, scores, states):
        o = _dot(a, vb) + _dot_nt(q_all, st.astype(BF16))
        o = o * lax.rsqrt(jnp.mean(o * o, axis=-1, keepdims=True) + RMS_EPS)
        gv = g_ref[0, sl, :]
        o_ref[0, sl, :] = (o * ng * (gv * _sigmoid(gv))).astype(o_ref.dtype)


def _hgrn_mixer(proj, lb_param, norm_g, layer_j, tb=512):
    bsz, seq, _ = proj.shape
    tb = min(tb, seq)
    n_layers = lb_param.shape[0]

    def col_block(off):
        return pl.BlockSpec((1, tb, HEAD_DIM), lambda b, h, t: (b, t, off + h))

    return pl.pallas_call(
        functools.partial(_hgrn_body, layer_j=layer_j, tb=tb),
        grid=(bsz, MIX_HEADS, seq // tb),
        in_specs=[
            col_block(0), col_block(MIX_HEADS), col_block(2 * MIX_HEADS), col_block(3 * MIX_HEADS),
            pl.BlockSpec((n_layers, HEAD_DIM), lambda b, h, t: (0, h)),
            pl.BlockSpec((1, HEAD_DIM), lambda b, h, t: (0, h)),
        ],
        out_specs=pl.BlockSpec((1, tb, HEAD_DIM), lambda b, h, t: (b, t, h)),
        out_shape=jax.ShapeDtypeStruct((bsz, seq, MIX_HEADS * HEAD_DIM), BF16),
        scratch_shapes=[pltpu.VMEM((HEAD_DIM, HEAD_DIM), F32)],
        compiler_params=_cparams("parallel", "parallel", "arbitrary"),
        name="hgrn2_mixer",
    )(proj, proj, proj, proj, lb_param, norm_g.reshape(1, -1))


def _log_sigmoid(z):
    return jnp.minimum(z, 0.0) - jnp.log1p(jnp.exp(-jnp.abs(z)))


def _fox_gate_body(x_ref, w_ref, wt_ref, bc_ref, br_ref, ccol_ref, crow_ref, carc_ref, carr_ref, *, tb):
    @pl.when(pl.program_id(1) == 0)
    def _():
        carc_ref[...] = jnp.zeros_like(carc_ref)
        carr_ref[...] = jnp.zeros_like(carr_ref)

    xb = x_ref[0].astype(BF16)
    ls_col = _log_sigmoid(_dot(xb, w_ref[...]) + bc_ref[...])
    ls_row = _log_sigmoid(_dot_nt(wt_ref[...], xb) + br_ref[...])
    row = lax.broadcasted_iota(I32, (tb, tb), 0)
    col = lax.broadcasted_iota(I32, (tb, tb), 1)
    lower = jnp.where(col <= row, 1.0, 0.0).astype(BF16)
    upper = jnp.where(row <= col, 1.0, 0.0).astype(BF16)
    c_col = carc_ref[...]
    for piece in _split3(ls_col):
        c_col = c_col + _dot(lower, piece)
    c_row = carr_ref[...]
    for piece in _split3(ls_row):
        c_row = c_row + _dot(piece, upper)
    ccol_ref[0] = c_col
    crow_ref[0] = c_row
    carc_ref[...] = c_col[tb - 1:tb, :]
    carr_ref[...] = c_row[:, tb - 1:tb]


def _fox_gate_cumsum(x, w_f, b_f, tb=512):
    bsz, seq, d = x.shape
    h = w_f.shape[1]
    tb = min(tb, seq)
    return pl.pallas_call(
        functools.partial(_fox_gate_body, tb=tb),
        grid=(bsz, seq // tb),
        in_specs=[
            pl.BlockSpec((1, tb, d), lambda b, t: (b, t, 0)),
            pl.BlockSpec((d, h), lambda b, t: (0, 0)),
            pl.BlockSpec((h, d), lambda b, t: (0, 0)),
            pl.BlockSpec((1, h), lambda b, t: (0, 0)),
            pl.BlockSpec((h, 1), lambda b, t: (0, 0)),
        ],
        out_specs=[
            pl.BlockSpec((1, tb, h), lambda b, t: (b, t, 0)),
            pl.BlockSpec((1, h, tb), lambda b, t: (b, 0, t)),
        ],
        out_shape=[
            jax.ShapeDtypeStruct((bsz, seq, h), F32),
            jax.ShapeDtypeStruct((bsz, h, seq), F32),
        ],
        scratch_shapes=[pltpu.VMEM((1, h), F32), pltpu.VMEM((h, 1), F32)],
        compiler_params=_cparams("parallel", "arbitrary"),
        name="fox_gate_cumsum",
    )(x, w_f.astype(BF16), w_f.T.astype(BF16), b_f.reshape(1, h), b_f.reshape(h, 1))


ATTN_ROWS = 128


def _softmax_probs(u, shift, rows, m_ref, l_ref):
    reps = u.shape[1] // HEAD_DIM
    m_old = m_ref[rows, :]
    if shift is None:
        m_new = jnp.maximum(m_old, jnp.max(u, axis=-1, keepdims=True))
        p = jnp.exp(u - jnp.tile(m_new, (1, reps)))
    else:
        m_new = jnp.maximum(m_old, jnp.max(u, axis=-1, keepdims=True) + shift)
        p = jnp.exp(u - jnp.tile(m_new - shift, (1, reps)))
    alpha = jnp.exp(m_old - m_new)
    l_ref[rows, :] = alpha * l_ref[rows, :] + jnp.sum(p, axis=-1, keepdims=True)
    m_ref[rows, :] = m_new
    return p.astype(BF16), alpha


def _accumulate(p, alpha, v, rows, acc_ref):
    acc_ref[rows, :] = alpha * acc_ref[rows, :] + _dot(p, v)


def _softmax_rows(u, shift, v, rows, m_ref, l_ref, acc_ref):
    p, alpha = _softmax_probs(u, shift, rows, m_ref, l_ref)
    _accumulate(p, alpha, v, rows, acc_ref)


def _softmax_init(m_ref, l_ref, acc_ref):
    m_ref[...] = jnp.full(m_ref.shape, NEG, F32)
    l_ref[...] = jnp.zeros_like(l_ref)
    acc_ref[...] = jnp.zeros_like(acc_ref)


def _fox_attn_body(q_ref, k_ref, v_ref, ccol_ref, crow_ref, o_ref, ct_ref, m_ref, l_ref, acc_ref, s_ref, *, tq):
    rs = min(ATTN_ROWS, tq)
    n_groups = tq // rs
    h = pl.program_id(1)
    qi = pl.program_id(2)
    ccol = ccol_ref[0]
    lane = lax.broadcasted_iota(I32, ccol.shape, 1)
    c_t = jnp.sum(jnp.where(lane == h, ccol, 0.0), axis=1, keepdims=True)
    ct_ref[...] = jnp.broadcast_to(c_t, ct_ref.shape)
    _softmax_init(m_ref, l_ref, acc_ref)

    def keys(kb):
        return pl.ds(pl.multiple_of(kb * tq, tq), tq)

    def logits(kb, r):
        rows = pl.ds(r * rs, rs)
        ks = keys(kb)
        s_ref[kb % 2, rows, :] = _dot_nt(q_ref[0, rows, :], k_ref[0, ks, :]) - crow_ref[0, pl.ds(h, 1), ks]

    def attend(kb, r, diagonal, prefetch):
        rows = pl.ds(r * rs, rs)
        u = s_ref[kb % 2, rows, :]
        if diagonal:
            row = lax.broadcasted_iota(I32, (rs, tq), 0) + r * rs
            col = lax.broadcasted_iota(I32, (rs, tq), 1)
            u = jnp.where(col <= row, u, NEG)
        if prefetch:
            logits(kb + 1, r)
        p, alpha = _softmax_probs(u, ct_ref[rows, :], rows, m_ref, l_ref)
        _accumulate(p, alpha, v_ref[0, keys(kb), :], rows, acc_ref)

    for r in range(n_groups):
        logits(0, r)

    def full_block(kb, carry):
        for r in range(n_groups):
            attend(kb, r, False, True)
        return carry

    lax.fori_loop(0, qi, full_block, 0)
    for r in range(n_groups):
        attend(qi, r, True, False)
    o_ref[0] = (acc_ref[...] / l_ref[...]).astype(o_ref.dtype)


def _fox_attention(qkv, c_col, c_row, tq=512):
    bsz, seq, _ = qkv.shape
    tq = min(tq, seq)
    nh = MIX_HEADS
    return pl.pallas_call(
        functools.partial(_fox_attn_body, tq=tq),
        grid=(bsz, nh, seq // tq),
        in_specs=[
            pl.BlockSpec((1, tq, HEAD_DIM), lambda b, h, i: (b, i, h)),
            pl.BlockSpec((1, seq, HEAD_DIM), lambda b, h, i: (b, 0, nh + h)),
            pl.BlockSpec((1, seq, HEAD_DIM), lambda b, h, i: (b, 0, 2 * nh + h)),
            pl.BlockSpec((1, tq, nh), lambda b, h, i: (b, i, 0)),
            pl.BlockSpec((1, nh, seq), lambda b, h, i: (b, 0, 0)),
        ],
        out_specs=pl.BlockSpec((1, tq, HEAD_DIM), lambda b, h, i: (b, i, h)),
        out_shape=jax.ShapeDtypeStruct((bsz, seq, nh * HEAD_DIM), BF16),
        scratch_shapes=[pltpu.VMEM((tq, HEAD_DIM), F32)] * 4 + [pltpu.VMEM((2, tq, tq), F32)],
        compiler_params=_cparams("parallel", "parallel", "arbitrary"),
        name="fox_attention",
    )(qkv, qkv, qkv, c_col, c_row)


def _t5_bucket_thresholds():
    n = np.arange(0, 4 * REL_MAX_DIST, dtype=np.int64)
    max_exact = REL_BUCKETS // 2
    ratio = np.log(np.maximum(n, 1).astype(np.float32) / np.float32(max_exact)) / np.float32(
        math.log(REL_MAX_DIST / max_exact))
    large = np.minimum(max_exact + (ratio * np.float32(REL_BUCKETS - max_exact)).astype(np.int32),
                       REL_BUCKETS - 1)
    bucket = np.where(n < max_exact, n, large)
    assert np.all(np.diff(bucket) >= 0) and bucket[-1] == REL_BUCKETS - 1
    return [int(np.argmax(bucket >= b)) for b in range(REL_BUCKETS)]


def _moba_body(rb_ref, q_ref, k_ref, v_ref, o_ref, kmean_ref, bias_ref, sel_ref, m_ref, l_ref, acc_ref,
               s_ref, *, n_blocks):
    blk = MOBA_BLOCK
    h = pl.program_id(1)
    qi = pl.program_id(2)
    far_bias = rb_ref[REL_BUCKETS - 1, h]

    @pl.when(qi == 0)
    def _():
        r = lax.broadcasted_iota(I32, (n_blocks, n_blocks * blk), 0)
        c = lax.broadcasted_iota(I32, (n_blocks, n_blocks * blk), 1)
        pool = jnp.where(c // blk == r, 1.0, 0.0).astype(BF16)
        kmean_ref[...] = _dot(pool, k_ref[0]) * (1.0 / blk)
        row = lax.broadcasted_iota(I32, (blk, blk), 0)
        col = lax.broadcasted_iota(I32, (blk, blk), 1)
        thresholds = _t5_bucket_thresholds()
        for slot, dist in ((0, row - col + blk), (1, row - col)):
            val = jnp.full((blk, blk), rb_ref[0, h], F32)
            for b in range(1, REL_BUCKETS):
                val = jnp.where(dist >= thresholds[b], rb_ref[b, h], val)
            if slot == 1:
                val = jnp.where(dist >= 0, val, NEG)
            bias_ref[slot] = val

    q = q_ref[0]
    km_hi = kmean_ref[...].astype(BF16)
    km_lo = (kmean_ref[...] - km_hi.astype(F32)).astype(BF16)
    gate = _dot_nt(km_hi, q) + _dot_nt(km_lo, q)
    blk_id = lax.broadcasted_iota(I32, gate.shape, 0)
    gate = jnp.where(blk_id < qi, gate, NEG)
    sel = jnp.zeros(gate.shape, F32)
    for _ in range(MOBA_TOPK):
        best = jnp.max(gate, axis=0, keepdims=True)
        first = jnp.min(jnp.where(gate == best, blk_id, n_blocks), axis=0, keepdims=True)
        pick = jnp.logical_and(blk_id == first, best > 0.5 * NEG)
        sel = jnp.where(pick, 1.0, sel)
        gate = jnp.where(blk_id == first, NEG, gate)
    sel_ref[...] = jnp.concatenate([sel, jnp.zeros((HEAD_DIM - n_blocks, blk), F32)], axis=0).T

    _softmax_init(m_ref, l_ref, acc_ref)
    rs = blk
    n_groups = blk // rs

    def sel_col(j):
        lanes = lax.broadcasted_iota(I32, (blk, HEAD_DIM), 1)
        return jnp.sum(jnp.where(lanes == j, sel_ref[...], 0.0), axis=1, keepdims=True) > 0.5

    def finish():
        o_ref[0] = (acc_ref[...] / l_ref[...]).astype(o_ref.dtype)

    @pl.when(qi == 0)
    def _():
        for r in range(n_groups):
            rows = pl.ds(r * rs, rs)
            u = _dot_nt(q_ref[0, rows, :], k_ref[0, 0:blk, :]) + bias_ref[1, r * rs:(r + 1) * rs, :]
            _softmax_rows(u, None, v_ref[0, 0:blk, :], rows, m_ref, l_ref, acc_ref)
        finish()

    @pl.when(qi >= 1)
    def _():
        n_far = qi - 1
        odd = n_far % 2
        n_far_units = (n_far + odd) // 2

        def first_block(unit):
            return jnp.maximum(2 * unit - odd, 0)

        def keys(unit):
            return pl.ds(pl.multiple_of(first_block(unit) * blk, blk), 2 * blk)

        def logits(unit, r):
            rows = pl.ds(r * rs, rs)
            s_ref[unit % 2, rows, :] = _dot_nt(q_ref[0, rows, :], k_ref[0, keys(unit), :])

        def attend(unit, r, extra_a, extra_b, prefetch):
            rows = pl.ds(r * rs, rs)
            u = s_ref[unit % 2, rows, :]
            u = jnp.concatenate([u[:, :blk] + extra_a, u[:, blk:] + extra_b], axis=1)
            if prefetch:
                logits(unit + 1, r)
            p, alpha = _softmax_probs(u, None, rows, m_ref, l_ref)
            _accumulate(p, alpha, v_ref[0, keys(unit), :], rows, acc_ref)

        for r in range(n_groups):
            logits(0, r)

        def far_unit(unit, carry):
            j = first_block(unit)
            masked_b = jnp.logical_and(unit == 0, odd == 1)
            bias_a = jnp.where(sel_col(j), far_bias, NEG)
            bias_b = jnp.where(jnp.logical_and(sel_col(j + 1), jnp.logical_not(masked_b)), far_bias, NEG)
            for r in range(n_groups):
                attend(unit, r, bias_a[r * rs:(r + 1) * rs], bias_b[r * rs:(r + 1) * rs], True)
            return carry

        lax.fori_loop(0, n_far_units, far_unit, 0)

        prev_mask = jnp.where(sel_col(qi - 1), 0.0, NEG)
        for r in range(n_groups):
            rr = slice(r * rs, (r + 1) * rs)
            attend(n_far_units, r, bias_ref[0, rr, :] + prev_mask[rr], bias_ref[1, rr, :], False)
        finish()


def _moba_attention(qkv, rel_bias):
    bsz, seq, _ = qkv.shape
    nh = MIX_HEADS
    blk = MOBA_BLOCK
    assert seq % blk == 0 and seq // blk <= HEAD_DIM
    n_blocks = seq // blk
    grid_spec = pltpu.PrefetchScalarGridSpec(
        num_scalar_prefetch=0,
        grid=(bsz, nh, n_blocks),
        in_specs=[
            pl.BlockSpec(memory_space=pltpu.SMEM),
            pl.BlockSpec((1, blk, HEAD_DIM), lambda b, h, i: (b, i, h)),
            pl.BlockSpec((1, seq, HEAD_DIM), lambda b, h, i: (b, 0, nh + h)),
            pl.BlockSpec((1, seq, HEAD_DIM), lambda b, h, i: (b, 0, 2 * nh + h)),
        ],
        out_specs=pl.BlockSpec((1, blk, HEAD_DIM), lambda b, h, i: (b, i, h)),
        scratch_shapes=[
            pltpu.VMEM((n_blocks, HEAD_DIM), F32),
            pltpu.VMEM((2, blk, blk), F32),
            pltpu.VMEM((blk, HEAD_DIM), F32),
            pltpu.VMEM((blk, HEAD_DIM), F32), pltpu.VMEM((blk, HEAD_DIM), F32), pltpu.VMEM((blk, HEAD_DIM), F32),
            pltpu.VMEM((2, blk, 2 * blk), F32),
        ],
    )
    return pl.pallas_call(
        functools.partial(_moba_body, n_blocks=n_blocks),
        grid_spec=grid_spec,
        out_shape=jax.ShapeDtypeStruct((bsz, seq, nh * HEAD_DIM), BF16),
        compiler_params=_cparams("parallel", "parallel", "arbitrary"),
        name="moba_attention",
    )(rel_bias, qkv, qkv, qkv)


def _mem_attn_body(q_ref, kv_ref, o_ref):
    width = MEM_HEADS * HEAD_DIM
    for hh in range(MEM_HEADS):
        cs = slice(hh * HEAD_DIM, (hh + 1) * HEAD_DIM)
        q = q_ref[0, :, cs].astype(BF16)
        k = kv_ref[0, :, cs]
        v = kv_ref[0, :, width + hh * HEAD_DIM:width + (hh + 1) * HEAD_DIM]
        s = _dot_nt(q, k)
        p = jnp.exp(s - jnp.max(s, axis=-1, keepdims=True))
        o = _dot(p.astype(BF16), v) / jnp.sum(p, axis=-1, keepdims=True)
        o_ref[0, :, cs] = o.astype(o_ref.dtype)


def _memory_attention(proj, q_col_block, kv, tq=512):
    bsz, seq, _ = proj.shape
    tq = min(tq, seq)
    width = MEM_HEADS * HEAD_DIM
    n_mem = kv.shape[1]
    return pl.pallas_call(
        _mem_attn_body,
        grid=(bsz, seq // tq),
        in_specs=[
            pl.BlockSpec((1, tq, width), lambda b, i: (b, i, q_col_block)),
            pl.BlockSpec((1, n_mem, 2 * width), lambda b, i: (b, 0, 0)),
        ],
        out_specs=pl.BlockSpec((1, tq, width), lambda b, i: (b, i, 0)),
        out_shape=jax.ShapeDtypeStruct((bsz, seq, width), BF16),
        compiler_params=_cparams("parallel", "parallel"),
        name="memory_attention",
    )(proj, kv)


def _to_row_tiles(ref, y):
    rows, d = y.shape
    n_tiles = d // HEAD_DIM
    for c in range(n_tiles):
        ref[pl.ds(c, rows, stride=n_tiles), :] = y[:, c * HEAD_DIM:(c + 1) * HEAD_DIM]


def _from_row_tiles(ref, rows, n_tiles, *lead):
    return jnp.concatenate(
        [ref[(*lead, pl.ds(c, rows, stride=n_tiles), slice(None))] for c in range(n_tiles)], axis=1)


def _row_tile(ref, r, n_tiles, *lead):
    return ref.at[(*lead, pl.ds(pl.multiple_of(r * n_tiles, n_tiles), n_tiles))]


def _out_ln_body(mix_ref, mem_ref, x_ref, w1_ref, w2_ref, g_ref, b_ref, o_ref, ort_ref, *, alpha):
    hproj = _dot(mix_ref[...], w1_ref[...]) + _dot(mem_ref[...], w2_ref[...])
    y = _layer_norm(alpha * x_ref[...] + hproj, g_ref[...], b_ref[...])
    o_ref[...] = y
    _to_row_tiles(ort_ref, y)


def _out_proj_ln(mix, mem_out, x, w_out, g, b, alpha, tm=512):
    t, d = x.shape
    tm = min(tm, t)
    wm = mix.shape[1]
    we = mem_out.shape[1]
    return pl.pallas_call(
        functools.partial(_out_ln_body, alpha=alpha),
        grid=(t // tm,),
        in_specs=[
            pl.BlockSpec((tm, wm), lambda i: (i, 0)),
            pl.BlockSpec((tm, we), lambda i: (i, 0)),
            pl.BlockSpec((tm, d), lambda i: (i, 0)),
            pl.BlockSpec((wm, d), lambda i: (0, 0)),
            pl.BlockSpec((we, d), lambda i: (0, 0)),
            pl.BlockSpec((1, d), lambda i: (0, 0)),
            pl.BlockSpec((1, d), lambda i: (0, 0)),
        ],
        out_specs=[pl.BlockSpec((tm, d), lambda i: (i, 0)),
                   pl.BlockSpec((tm * (d // HEAD_DIM), HEAD_DIM), lambda i: (i, 0))],
        out_shape=[jax.ShapeDtypeStruct((t, d), F32),
                   jax.ShapeDtypeStruct((t * (d // HEAD_DIM), HEAD_DIM), F32)],
        compiler_params=_cparams("parallel"),
        name="out_proj_ln",
    )(mix, mem_out, x, w_out[:wm].astype(BF16), w_out[wm:].astype(BF16), g.reshape(1, d), b.reshape(1, d))


def _router_body(x_ref, wt_ref, b_ref, idx_ref, gate_ref, cnt_ref, run_ref, *, tb):
    @pl.when(pl.program_id(0) == 0)
    def _():
        run_ref[...] = jnp.zeros_like(run_ref)

    x = x_ref[...]
    xh = x.astype(BF16)
    xl = (x - xh.astype(F32)).astype(BF16)
    w = wt_ref[...]
    wh = w.astype(BF16)
    wl = (w - wh.astype(F32)).astype(BF16)
    logits = _dot_nt(wh, xh) + _dot_nt(wh, xl) + _dot_nt(wl, xh) + b_ref[...]

    n_e = logits.shape[0]
    eid = lax.broadcasted_iota(I32, logits.shape, 0)
    vals, idxs, hots = [], [], []
    multi = jnp.zeros(logits.shape, F32)
    for _ in range(TOP_K):
        best = jnp.max(logits, axis=0, keepdims=True)
        first = jnp.min(jnp.where(logits == best, eid, n_e), axis=0, keepdims=True)
        hot = eid == first
        vals.append(best)
        idxs.append(first)
        hots.append(hot)
        multi = jnp.where(hot, 1.0, multi)
        logits = jnp.where(hot, -jnp.inf, logits)

    exps = [jnp.exp(v - vals[0]) for v in vals]
    den = exps[0]
    for e in exps[1:]:
        den = den + e

    r = lax.broadcasted_iota(I32, (tb, tb), 0)
    c = lax.broadcasted_iota(I32, (tb, tb), 1)
    before = jnp.where(r < c, 1.0, 0.0).astype(BF16)
    pos = _dot(multi.astype(BF16), before) + run_ref[:, 0:1]
    for kk in range(TOP_K):
        idx_ref[kk:kk + 1, :] = idxs[kk]
        rank = jnp.sum(jnp.where(hots[kk], pos, 0.0), axis=0, keepdims=True)
        idx_ref[TOP_K + kk:TOP_K + kk + 1, :] = rank.astype(I32)
        gate_ref[kk:kk + 1, :] = exps[kk] / den
    gate_ref[TOP_K:2 * TOP_K, :] = jnp.zeros((TOP_K, tb), F32)
    run_ref[...] = run_ref[...] + jnp.sum(multi, axis=1, keepdims=True)
    cnt_ref[...] = run_ref[...]


def _router(x, router_w, router_b, tb=512):
    t, d = x.shape
    tb = min(tb, t)
    n_e = router_w.shape[1]
    return pl.pallas_call(
        functools.partial(_router_body, tb=tb),
        grid=(t // tb,),
        in_specs=[
            pl.BlockSpec((tb, d), lambda i: (i, 0)),
            pl.BlockSpec((n_e, d), lambda i: (0, 0)),
            pl.BlockSpec((n_e, 1), lambda i: (0, 0)),
        ],
        out_specs=[
            pl.BlockSpec((2 * TOP_K, tb), lambda i: (0, i)),
            pl.BlockSpec((2 * TOP_K, tb), lambda i: (0, i)),
            pl.BlockSpec((n_e, HEAD_DIM), lambda i: (0, 0)),
        ],
        out_shape=[
            jax.ShapeDtypeStruct((2 * TOP_K, t), I32),
            jax.ShapeDtypeStruct((2 * TOP_K, t), F32),
            jax.ShapeDtypeStruct((n_e, HEAD_DIM), F32),
        ],
        scratch_shapes=[pltpu.VMEM((n_e, HEAD_DIM), F32)],
        compiler_params=_cparams("arbitrary"),
        name="moe_router",
    )(x, router_w.T, router_b.reshape(n_e, 1))


def _dispatch_body(dest_ref, x_ref, xs_in, xs_hbm, sem, *, tb, n_tok, n_tiles):
    del xs_in
    base = pl.program_id(0) * tb

    def issue(t, carry):
        for kk in range(TOP_K):
            d = dest_ref[kk * n_tok + base + t]
            pltpu.make_async_copy(_row_tile(x_ref, t, n_tiles), _row_tile(xs_hbm, d, n_tiles), sem).start()
        return carry

    lax.fori_loop(0, tb, issue, 0, unroll=DMA_UNROLL)
    for kk in range(TOP_K):
        pltpu.make_async_copy(x_ref, xs_hbm.at[pl.ds(0, tb * n_tiles)], sem).wait()


def _dispatch(x, n_tiles, dest_flat, n_rows, tb=512):
    t = x.shape[0] // n_tiles
    tb = min(tb, t)
    grid_spec = pltpu.PrefetchScalarGridSpec(
        num_scalar_prefetch=1,
        grid=(t // tb,),
        in_specs=[pl.BlockSpec((tb * n_tiles, HEAD_DIM), lambda i, dest: (i, 0)), pl.BlockSpec(memory_space=pl.ANY)],
        out_specs=pl.BlockSpec(memory_space=pl.ANY),
        scratch_shapes=[pltpu.SemaphoreType.DMA(())],
    )
    return pl.pallas_call(
        functools.partial(_dispatch_body, tb=tb, n_tok=t, n_tiles=n_tiles),
        grid_spec=grid_spec,
        out_shape=jax.ShapeDtypeStruct((n_rows * n_tiles, HEAD_DIM), x.dtype),
        input_output_aliases={2: 0},
        compiler_params=_cparams("arbitrary"),
        name="moe_dispatch",
    )(dest_flat, x, jnp.zeros((n_rows * n_tiles, HEAD_DIM), x.dtype))


UNZIP = 256


def _expert_body(be_ref, nu_ref, first_ref, slot_ref, next_ref, x_ref, wu_hbm, bg_ref, bl_ref, wd_hbm, bd_ref,
                 o_ref, wu_buf, wd_buf, wg_s, wl_s, wd_s, sem, *, layer, n_tiles):
    i = pl.program_id(0)
    n_used = nu_ref[0]

    def weight_copies(expert, slot):
        return (pltpu.make_async_copy(wu_hbm.at[layer, expert], wu_buf.at[slot], sem.at[0, slot]),
                pltpu.make_async_copy(wd_hbm.at[layer, expert], wd_buf.at[slot], sem.at[1, slot]))

    @pl.when(i == 0)
    def _():
        for cp in weight_copies(be_ref[0], 0):
            cp.start()

    @pl.when(jnp.logical_and(first_ref[i] == 1, i < n_used))
    def _():
        slot = slot_ref[i]
        for cp in weight_copies(be_ref[i], slot):
            cp.wait()

        @pl.when(next_ref[i] >= 0)
        def _():
            for cp in weight_copies(next_ref[i], 1 - slot):
                cp.start()

        half = UNZIP // 2
        r = lax.broadcasted_iota(I32, (UNZIP, UNZIP), 0)
        c = lax.broadcasted_iota(I32, (UNZIP, UNZIP), 1)
        src = jnp.where(c < half, 2 * c, 2 * (c - half) + 1)
        perm = jnp.where(r == src, 1.0, 0.0).astype(BF16)
        for g in range(wu_buf.shape[2] // UNZIP):
            w = wu_buf[slot, :, g * UNZIP:(g + 1) * UNZIP].astype(BF16)
            sep = _dot(w, perm).astype(BF16)
            wg_s[:, g * half:(g + 1) * half] = sep[:, :half]
            wl_s[:, g * half:(g + 1) * half] = sep[:, half:]
        wd_s[...] = wd_buf[slot].astype(BF16)

    @pl.when(i < n_used)
    def _():
        xb = _from_row_tiles(x_ref, MOE_BLOCK, n_tiles).astype(BF16)
        glu = jnp.minimum(_dot(xb, wg_s[...]) + bg_ref[0], SWIGLU_LIMIT)
        lin = jnp.clip(_dot(xb, wl_s[...]) + bl_ref[0], -SWIGLU_LIMIT, SWIGLU_LIMIT)
        act = glu * _sigmoid(SWIGLU_ALPHA * glu) * (lin + 1.0)
        _to_row_tiles(o_ref, _dot(act.astype(BF16), wd_s[...]) + bd_ref[0])

    @pl.when(i >= n_used)
    def _():
        o_ref[...] = jnp.zeros_like(o_ref)


def _expert_mlp(xs, n_tiles, blk_expert, n_used, counts, layer, w_up, bg, bl, w_down, bd):
    p_rows = xs.shape[0] // n_tiles
    d = n_tiles * HEAD_DIM
    f = w_down.shape[2]
    n_e = w_down.shape[1]
    n_blk = p_rows // MOE_BLOCK

    blocks = jnp.arange(n_blk, dtype=I32)
    valid = blocks < n_used[0]
    prev_e = jnp.concatenate([jnp.full((1,), -1, I32), blk_expert[:-1]])
    first = jnp.logical_and(valid, blk_expert != prev_e)
    slot = (jnp.cumsum(first.astype(I32)) - 1) % 2
    experts = jnp.arange(n_e, dtype=I32)
    later = jnp.logical_and(experts[None, :] > blk_expert[:, None], counts[None, :] > 0)
    nxt = jnp.min(jnp.where(later, experts[None, :], n_e), axis=1)
    nxt = jnp.where(nxt == n_e, -1, nxt)

    def blk(i, nu):
        return jnp.minimum(i, nu[0] - 1)

    def w_spec(shape):
        return pl.BlockSpec((1,) + shape, lambda i, be, nu, fi, sl, nx: (be[blk(i, nu)], 0, 0))

    grid_spec = pltpu.PrefetchScalarGridSpec(
        num_scalar_prefetch=5,
        grid=(n_blk,),
        in_specs=[
            pl.BlockSpec((MOE_BLOCK * n_tiles, HEAD_DIM), lambda i, be, nu, fi, sl, nx: (blk(i, nu), 0)),
            pl.BlockSpec(memory_space=pl.ANY), w_spec((1, f)), w_spec((1, f)),
            pl.BlockSpec(memory_space=pl.ANY), w_spec((1, d)),
        ],
        out_specs=pl.BlockSpec((MOE_BLOCK * n_tiles, HEAD_DIM), lambda i, be, nu, fi, sl, nx: (i, 0)),
        scratch_shapes=[
            pltpu.VMEM((2, d, 2 * f), F32), pltpu.VMEM((2, f, d), F32),
            pltpu.VMEM((d, f), BF16), pltpu.VMEM((d, f), BF16), pltpu.VMEM((f, d), BF16),
            pltpu.SemaphoreType.DMA((2, 2)),
        ],
    )
    return pl.pallas_call(
        functools.partial(_expert_body, layer=layer, n_tiles=n_tiles),
        grid_spec=grid_spec,
        out_shape=jax.ShapeDtypeStruct(xs.shape, F32),
        compiler_params=_cparams("arbitrary"),
        name="moe_experts",
    )(blk_expert, n_used, first.astype(I32), slot.astype(I32), nxt.astype(I32), xs, w_up, bg, bl, w_down, bd)


def _combine_body(dest_ref, ys_hbm, gate_ref, x_ref, g_ref, b_ref, o_ref, buf_ref, sem_ref,
                  *, tb, n_tok, n_tiles, alpha):
    i = pl.program_id(0)
    n_steps = pl.num_programs(0)

    def issue(step, slot):
        base = step * tb

        def one(t, carry):
            for kk in range(TOP_K):
                d = dest_ref[kk * n_tok + base + t]
                pltpu.make_async_copy(_row_tile(ys_hbm, d, n_tiles), _row_tile(buf_ref, t, n_tiles, slot, kk),
                                      sem_ref.at[slot]).start()
            return carry

        lax.fori_loop(0, tb, one, 0, unroll=DMA_UNROLL)

    @pl.when(i == 0)
    def _():
        issue(0, 0)

    slot = lax.rem(i, 2)

    @pl.when(i + 1 < n_steps)
    def _():
        issue(i + 1, 1 - slot)

    for kk in range(TOP_K):
        pltpu.make_async_copy(ys_hbm.at[pl.ds(0, tb * n_tiles)], buf_ref.at[slot, kk], sem_ref.at[slot]).wait()

    g8 = gate_ref[...]
    gpad = jnp.concatenate([g8, jnp.zeros((HEAD_DIM - g8.shape[0], tb), F32)], axis=0)
    gt = gpad.T
    f = gt[:, 0:1] * _from_row_tiles(buf_ref, tb, n_tiles, slot, 0)
    for kk in range(1, TOP_K):
        f = f + gt[:, kk:kk + 1] * _from_row_tiles(buf_ref, tb, n_tiles, slot, kk)
    o_ref[...] = _layer_norm(alpha * x_ref[...] + f, g_ref[...], b_ref[...])


def _combine_ln(ys, dest_flat, gates, x, g, b, alpha, tb=256):
    t, d = x.shape
    tb = min(tb, t)
    n_tiles = d // HEAD_DIM
    grid_spec = pltpu.PrefetchScalarGridSpec(
        num_scalar_prefetch=1,
        grid=(t // tb,),
        in_specs=[
            pl.BlockSpec(memory_space=pl.ANY),
            pl.BlockSpec((2 * TOP_K, tb), lambda i, dest: (0, i)),
            pl.BlockSpec((tb, d), lambda i, dest: (i, 0)),
            pl.BlockSpec((1, d), lambda i, dest: (0, 0)),
            pl.BlockSpec((1, d), lambda i, dest: (0, 0)),
        ],
        out_specs=pl.BlockSpec((tb, d), lambda i, dest: (i, 0)),
        scratch_shapes=[pltpu.VMEM((2, TOP_K, tb * n_tiles, HEAD_DIM), F32), pltpu.SemaphoreType.DMA((2,))],
    )
    return pl.pallas_call(
        functools.partial(_combine_body, tb=tb, n_tok=t, n_tiles=n_tiles, alpha=alpha),
        grid_spec=grid_spec,
        out_shape=jax.ShapeDtypeStruct((t, d), F32),
        compiler_params=_cparams("arbitrary"),
        name="moe_combine_ln",
    )(dest_flat, ys, gates, x, g.reshape(1, d), b.reshape(1, d))


def _moe_ln(x, x_tiles, router_w, router_b, layer, w_up, b_up, w_down, b_down, g, b, alpha):
    t, d = x.shape
    n_tiles = d // HEAD_DIM
    n_e = router_w.shape[1]
    idx_rank, gates, cnt = _router(x, router_w, router_b)
    counts = cnt[:, 0].astype(I32)
    padded = ((counts + MOE_BLOCK - 1) // MOE_BLOCK) * MOE_BLOCK
    p_ends = jnp.cumsum(padded)
    p_starts = p_ends - padded
    n_blk = (t * TOP_K + MOE_BLOCK - 1) // MOE_BLOCK + n_e
    experts = jnp.arange(n_e, dtype=I32)
    start_of = jnp.sum(jnp.where(idx_rank[:TOP_K, :, None] == experts, p_starts, 0), axis=-1)
    dest_flat = (start_of + idx_rank[TOP_K:]).reshape(-1)
    blk_first_row = jnp.arange(n_blk, dtype=I32) * MOE_BLOCK
    blk_expert = jnp.clip(jnp.sum((p_ends[None, :] <= blk_first_row[:, None]).astype(I32), axis=1), 0, n_e - 1)
    n_used = (p_ends[-1:] // MOE_BLOCK).astype(I32)

    xs = _dispatch(x_tiles, n_tiles, dest_flat, n_blk * MOE_BLOCK)
    bg = b_up[:, None, 0::2]
    bl = b_up[:, None, 1::2]
    ys = _expert_mlp(xs, n_tiles, blk_expert, n_used, counts, layer, w_up, bg, bl, w_down, b_down[:, None, :])
    return _combine_ln(ys, dest_flat, gates, x, g, b, alpha)


def kernel(x, mem, hgrn_w_in, hgrn_lb, hgrn_norm_g, fox_w_in, fox_b_f, moba_w_in, rel_bias, w_mem_kv, w_out,
           ln1_g, ln1_b, router_w, router_b, w_up, b_up, w_down, b_down, ln2_g, ln2_b):
    bsz, seq, d = x.shape
    depth = w_out.shape[0]
    t = bsz * seq
    mix_w = MIX_HEADS * HEAD_DIM
    mem_w = MEM_HEADS * HEAD_DIM
    alpha = (2 * depth) ** 0.25
    scale = HEAD_DIM ** -0.5
    n_mem = mem.shape[1]
    mem2 = mem.reshape(bsz * n_mem, d)

    def scale_vec(n_scaled_front, n_plain, n_scaled_back):
        return jnp.concatenate([
            jnp.full((1, n_scaled_front), scale, F32), jnp.ones((1, n_plain), F32),
            jnp.full((1, n_scaled_back), scale, F32)], axis=1)

    x2 = x.reshape(t, d)
    for i in range(depth):
        kind, j = i % 3, i // 3
        if kind == 0:
            w_in = hgrn_w_in[j].astype(BF16)
            proj = _matmul(x2, w_in, scale_vec(0, 4 * mix_w, mem_w), F32, 1024, 512)
            proj = proj.reshape(bsz, seq, -1)
            mix = _hgrn_mixer(proj, hgrn_lb, hgrn_norm_g[j], j)
            memq_block = 4 * mix_w // mem_w
        elif kind == 1:
            w = fox_w_in[j]
            w_in = jnp.concatenate([w[:, :3 * mix_w], w[:, 3 * mix_w + MIX_HEADS:]], axis=1).astype(BF16)
            proj = _matmul(x2, w_in, scale_vec(mix_w, 2 * mix_w, mem_w), BF16, 1024, 512)
            proj = proj.reshape(bsz, seq, -1)
            c_col, c_row = _fox_gate_cumsum(x2.reshape(bsz, seq, d), w[:, 3 * mix_w:3 * mix_w + MIX_HEADS],
                                            fox_b_f[j])
            mix = _fox_attention(proj, c_col, c_row)
            memq_block = 3 * mix_w // mem_w
        else:
            w_in = moba_w_in[j].astype(BF16)
            proj = _matmul(x2, w_in, scale_vec(mix_w, 2 * mix_w, mem_w), BF16, 1024, 512)
            proj = proj.reshape(bsz, seq, -1)
            mix = _moba_attention(proj, rel_bias)
            memq_block = 3 * mix_w // mem_w
        kv = _matmul(mem2, w_mem_kv[i].astype(BF16), jnp.ones((1, 2 * mem_w), F32), BF16, 512, 512)
        mem_out = _memory_attention(proj, memq_block, kv.reshape(bsz, n_mem, 2 * mem_w))
        x2, x2_tiles = _out_proj_ln(mix.reshape(t, mix_w), mem_out.reshape(t, mem_w), x2, w_out[i],
                                    ln1_g[i], ln1_b[i], alpha)
        x2 = _moe_ln(x2, x2_tiles, router_w[i], router_b[i], i, w_up, b_up[i], w_down, b_down[i],
                     ln2_g[i], ln2_b[i], alpha)
    return x2.reshape(bsz, seq, d)
```

```python
import functools
import math

import numpy as np
import jax
import jax.numpy as jnp
from jax import lax
from jax.experimental import pallas as pl
from jax.experimental.pallas import tpu as pltpu

F32 = jnp.float32
BF16 = jnp.bfloat16
I32 = jnp.int32

HEAD_DIM = 128
MIX_HEADS = 8
MEM_HEADS = 4
HGRN_CHUNK = 64
HGRN_SUB = 16
HGRN_EXP_CLAMP = 80.0
MOBA_BLOCK = 256
MOBA_TOPK = 3
REL_BUCKETS = 32
REL_MAX_DIST = 128
N_EXPERTS = 32
TOP_K = 4
MOE_BLOCK = 256
SWIGLU_LIMIT = 7.0
SWIGLU_ALPHA = 1.702
LN_EPS = 1e-5
RMS_EPS = 1e-6
NEG = -1e30
VMEM_LIMIT = 56 * 1024 * 1024
DMA_UNROLL = 8

NT_DIMS = (((1,), (1,)), ((), ()))
TN_DIMS = (((0,), (0,)), ((), ()))


def _cparams(*sem):
    return pltpu.CompilerParams(dimension_semantics=sem, vmem_limit_bytes=VMEM_LIMIT)


def _dot(a, b):
    return jnp.dot(a, b, preferred_element_type=F32)


def _dot_nt(a, b):
    return lax.dot_general(a, b, NT_DIMS, preferred_element_type=F32)


def _dot_tn(a, b):
    return lax.dot_general(a, b, TN_DIMS, preferred_element_type=F32)


def _split3(a):
    hi = a.astype(BF16)
    r = a - hi.astype(F32)
    mid = r.astype(BF16)
    lo = (r - mid.astype(F32)).astype(BF16)
    return hi, mid, lo


def _sigmoid(x):
    return 1.0 / (1.0 + jnp.exp(-x))


def _layer_norm(z, g, b):
    mu = jnp.mean(z, axis=-1, keepdims=True)
    zc = z - mu
    var = jnp.mean(zc * zc, axis=-1, keepdims=True)
    return zc * lax.rsqrt(var + LN_EPS) * g + b


def _mm_body(x_ref, w_ref, s_ref, o_ref, xb_ref):
    @pl.when(pl.program_id(1) == 0)
    def _():
        xb_ref[...] = x_ref[...].astype(BF16)

    acc = _dot(xb_ref[...], w_ref[...])
    o_ref[...] = (acc * s_ref[...]).astype(o_ref.dtype)


def _matmul(x, w, col_scale, out_dtype, tm, tn):
    m, k = x.shape
    n = w.shape[1]
    tm = min(tm, m)
    tn = min(tn, n)
    return pl.pallas_call(
        _mm_body,
        grid=(m // tm, n // tn),
        in_specs=[
            pl.BlockSpec((tm, k), lambda i, j: (i, 0)),
            pl.BlockSpec((k, tn), lambda i, j: (0, j)),
            pl.BlockSpec((1, tn), lambda i, j: (0, j)),
        ],
        out_specs=pl.BlockSpec((tm, tn), lambda i, j: (i, j)),
        out_shape=jax.ShapeDtypeStruct((m, n), out_dtype),
        scratch_shapes=[pltpu.VMEM((tm, k), BF16)],
        compiler_params=_cparams("parallel", "arbitrary"),
        name="proj_matmul",
    )(x, w, col_scale)


def _hgrn_body(q_ref, z_ref, i_ref, g_ref, lbp_ref, ng_ref, o_ref, st_ref, *, layer_j, tb):
    c_len, sb = HGRN_CHUNK, HGRN_SUB
    n_sub = c_len // sb

    @pl.when(pl.program_id(2) == 0)
    def _():
        st_ref[...] = jnp.zeros_like(st_ref)

    lbp = lbp_ref[...]
    ex = jnp.exp(lbp - jnp.max(lbp, axis=0, keepdims=True))
    p = ex / jnp.sum(ex, axis=0, keepdims=True)
    lb = jnp.zeros((1, HEAD_DIM), F32)
    for r in range(1, layer_j + 1):
        lb = lb + p[r:r + 1, :]
    log_lb = jnp.log(lb)
    log1m_lb = jnp.log1p(-lb)
    one_m_lb = 1.0 - lb
    ng = ng_ref[...]

    row = lax.broadcasted_iota(I32, (c_len, c_len), 0)
    col = lax.broadcasted_iota(I32, (c_len, c_len), 1)
    sub_start = (row // sb) * sb
    one = jnp.ones((c_len, c_len), F32)
    zero = jnp.zeros((c_len, c_len), F32)
    m_tri = jnp.where(col <= row, one, zero)
    m_in = jnp.where(col > sub_start, m_tri, zero)
    m_dec = jnp.where(col > row, one, zero)
    m_ke = jnp.where(col <= sub_start + (sb - 1), m_dec, zero)
    stack = jnp.concatenate([m_tri, m_in, m_ke, m_dec], axis=0).astype(BF16)
    same_sub = (row // sb) == (col // sb)
    diag_mask = jnp.logical_and(same_sub, col <= row)
    row_sub = row // sb
    krow_sub = lax.broadcasted_iota(I32, (c_len, HEAD_DIM), 0) // sb

    chunks = [pl.ds(c * c_len, c_len) for c in range(tb // c_len)]

    gates = []
    for sl in chunks:
        z = z_ref[0, sl, :]
        e = jnp.exp(-jnp.abs(z))
        inv = 1.0 / (1.0 + e)
        sig_neg = jnp.where(z >= 0, e, 1.0) * inv
        log_sig = jnp.minimum(z, 0.0) - jnp.log1p(e)
        a2 = log1m_lb + log_sig
        log_f = jnp.maximum(log_lb, a2) + jnp.log1p(jnp.exp(-jnp.abs(log_lb - a2)))
        qv = q_ref[0, sl, :]
        gates.append((log_f, one_m_lb * sig_neg, qv * _sigmoid(qv), i_ref[0, sl, :].astype(BF16)))

    prefix = []
    for log_f, _, _, _ in gates:
        pre = None
        for piece in _split3(log_f):
            t = _dot(stack, piece)
            pre = t if pre is None else pre + t
        prefix.append(pre)

    factors = []
    for (_, k, qf, _), pre in zip(gates, prefix):
        b = pre[0:c_len]
        d_in = pre[c_len:2 * c_len]
        d_ke = pre[2 * c_len:3 * c_len]
        d_dec = pre[3 * c_len:4 * c_len]
        q_in = qf * jnp.exp(d_in)
        k_diag = (k * jnp.exp(jnp.minimum(-d_in, HGRN_EXP_CLAMP))).astype(BF16)
        k_end = k * jnp.exp(d_ke)
        k_dec = (k * jnp.exp(d_dec)).astype(BF16)
        q_all = (qf * jnp.exp(b)).astype(BF16)
        b_ref = b - d_in
        lhs = [q_in.astype(BF16)]
        rhs = [k_diag]
        for j in range(n_sub - 1):
            b_end = b[(j + 1) * sb - 1:(j + 1) * sb, :]
            cross = jnp.exp(jnp.minimum(b_ref - b_end, 0.0))
            lhs.append((q_in * cross).astype(BF16))
            rhs.append(jnp.where(krow_sub == j, k_end, 0.0).astype(BF16))
        factors.append((lhs, rhs, k_dec, q_all, jnp.exp(b[c_len - 1:c_len, :])))

    scores = []
    for lhs, rhs, _, _, _ in factors:
        a = jnp.where(diag_mask, _dot_nt(lhs[0], rhs[0]), 0.0)
        for j in range(n_sub - 1):
            a = a + jnp.where(row_sub > j, _dot_nt(lhs[j + 1], rhs[j + 1]), 0.0)
        scores.append(a.astype(BF16))

    updates = [_dot_tn(vb, k_dec) for (_, _, _, vb), (_, _, k_dec, _, _) in zip(gates, factors)]
    states = [st_ref[...]]
    for (_, _, _, _, decay), kv in zip(factors, updates):
        states.append(states[-1] * decay + kv)
    st_ref[...] = states[-1]

    for sl, (_, _, _, vb), (_, _, _, q_all, _), a, st in zip(chunks, gates, factors, scores, states):
        o = _dot(a, vb) + _dot_nt(q_all, st.astype(BF16))
        o = o * lax.rsqrt(jnp.mean(o * o, axis=-1, keepdims=True) + RMS_EPS)
        gv = g_ref[0, sl, :]
        o_ref[0, sl, :] = (o * ng * (gv * _sigmoid(gv))).astype(o_ref.dtype)


def _hgrn_mixer(proj, lb_param, norm_g, layer_j, tb=512):
    bsz, seq, _ = proj.shape
    tb = min(tb, seq)
    n_layers = lb_param.shape[0]

    def col_block(off):
        return pl.BlockSpec((1, tb, HEAD_DIM), lambda b, h, t: (b, t, off + h))

    return pl.pallas_call(
        functools.partial(_hgrn_body, layer_j=layer_j, tb=tb),
        grid=(bsz, MIX_HEADS, seq // tb),
        in_specs=[
            col_block(0), col_block(MIX_HEADS), col_block(2 * MIX_HEADS), col_block(3 * MIX_HEADS),
            pl.BlockSpec((n_layers, HEAD_DIM), lambda b, h, t: (0, h)),
            pl.BlockSpec((1, HEAD_DIM), lambda b, h, t: (0, h)),
        ],
        out_specs=pl.BlockSpec((1, tb, HEAD_DIM), lambda b, h, t: (b, t, h)),
        out_shape=jax.ShapeDtypeStruct((bsz, seq, MIX_HEADS * HEAD_DIM), BF16),
        scratch_shapes=[pltpu.VMEM((HEAD_DIM, HEAD_DIM), F32)],
        compiler_params=_cparams("parallel", "parallel", "arbitrary"),
        name="hgrn2_mixer",
    )(proj, proj, proj, proj, lb_param, norm_g.reshape(1, -1))


def _log_sigmoid(z):
    return jnp.minimum(z, 0.0) - jnp.log1p(jnp.exp(-jnp.abs(z)))


def _fox_gate_body(x_ref, w_ref, wt_ref, bc_ref, br_ref, ccol_ref, crow_ref, carc_ref, carr_ref, *, tb):
    @pl.when(pl.program_id(1) == 0)
    def _():
        carc_ref[...] = jnp.zeros_like(carc_ref)
        carr_ref[...] = jnp.zeros_like(carr_ref)

    xb = x_ref[0].astype(BF16)
    ls_col = _log_sigmoid(_dot(xb, w_ref[...]) + bc_ref[...])
    ls_row = _log_sigmoid(_dot_nt(wt_ref[...], xb) + br_ref[...])
    row = lax.broadcasted_iota(I32, (tb, tb), 0)
    col = lax.broadcasted_iota(I32, (tb, tb), 1)
    lower = jnp.where(col <= row, 1.0, 0.0).astype(BF16)
    upper = jnp.where(row <= col, 1.0, 0.0).astype(BF16)
    c_col = carc_ref[...]
    for piece in _split3(ls_col):
        c_col = c_col + _dot(lower, piece)
    c_row = carr_ref[...]
    for piece in _split3(ls_row):
        c_row = c_row + _dot(piece, upper)
    ccol_ref[0] = c_col
    crow_ref[0] = c_row
    carc_ref[...] = c_col[tb - 1:tb, :]
    carr_ref[...] = c_row[:, tb - 1:tb]


def _fox_gate_cumsum(x, w_f, b_f, tb=512):
    bsz, seq, d = x.shape
    h = w_f.shape[1]
    tb = min(tb, seq)
    return pl.pallas_call(
        functools.partial(_fox_gate_body, tb=tb),
        grid=(bsz, seq // tb),
        in_specs=[
            pl.BlockSpec((1, tb, d), lambda b, t: (b, t, 0)),
            pl.BlockSpec((d, h), lambda b, t: (0, 0)),
            pl.BlockSpec((h, d), lambda b, t: (0, 0)),
            pl.BlockSpec((1, h), lambda b, t: (0, 0)),
            pl.BlockSpec((h, 1), lambda b, t: (0, 0)),
        ],
        out_specs=[
            pl.BlockSpec((1, tb, h), lambda b, t: (b, t, 0)),
            pl.BlockSpec((1, h, tb), lambda b, t: (b, 0, t)),
        ],
        out_shape=[
            jax.ShapeDtypeStruct((bsz, seq, h), F32),
            jax.ShapeDtypeStruct((bsz, h, seq), F32),
        ],
        scratch_shapes=[pltpu.VMEM((1, h), F32), pltpu.VMEM((h, 1), F32)],
        compiler_params=_cparams("parallel", "arbitrary"),
        name="fox_gate_cumsum",
    )(x, w_f.astype(BF16), w_f.T.astype(BF16), b_f.reshape(1, h), b_f.reshape(h, 1))


ATTN_ROWS = 128


def _softmax_probs(u, shift, rows, m_ref, l_ref):
    reps = u.shape[1] // HEAD_DIM
    m_old = m_ref[rows, :]
    if shift is None:
        m_new = jnp.maximum(m_old, jnp.max(u, axis=-1, keepdims=True))
        p = jnp.exp(u - jnp.tile(m_new, (1, reps)))
    else:
        m_new = jnp.maximum(m_old, jnp.max(u, axis=-1, keepdims=True) + shift)
        p = jnp.exp(u - jnp.tile(m_new - shift, (1, reps)))
    alpha = jnp.exp(m_old - m_new)
    l_ref[rows, :] = alpha * l_ref[rows, :] + jnp.sum(p, axis=-1, keepdims=True)
    m_ref[rows, :] = m_new
    return p.astype(BF16), alpha


def _accumulate(p, alpha, v, rows, acc_ref):
    acc_ref[rows, :] = alpha * acc_ref[rows, :] + _dot(p, v)


def _softmax_rows(u, shift, v, rows, m_ref, l_ref, acc_ref):
    p, alpha = _softmax_probs(u, shift, rows, m_ref, l_ref)
    _accumulate(p, alpha, v, rows, acc_ref)


def _softmax_init(m_ref, l_ref, acc_ref):
    m_ref[...] = jnp.full(m_ref.shape, NEG, F32)
    l_ref[...] = jnp.zeros_like(l_ref)
    acc_ref[...] = jnp.zeros_like(acc_ref)


def _fox_attn_body(q_ref, k_ref, v_ref, ccol_ref, crow_ref, o_ref, ct_ref, m_ref, l_ref, acc_ref, s_ref, *, tq):
    rs = min(ATTN_ROWS, tq)
    n_groups = tq // rs
    h = pl.program_id(1)
    qi = pl.program_id(2)
    ccol = ccol_ref[0]
    lane = lax.broadcasted_iota(I32, ccol.shape, 1)
    c_t = jnp.sum(jnp.where(lane == h, ccol, 0.0), axis=1, keepdims=True)
    ct_ref[...] = jnp.broadcast_to(c_t, ct_ref.shape)
    _softmax_init(m_ref, l_ref, acc_ref)

    def keys(kb):
        return pl.ds(pl.multiple_of(kb * tq, tq), tq)

    def logits(kb, r):
        rows = pl.ds(r * rs, rs)
        ks = keys(kb)
        s_ref[kb % 2, rows, :] = _dot_nt(q_ref[0, rows, :], k_ref[0, ks, :]) - crow_ref[0, pl.ds(h, 1), ks]

    def attend(kb, r, diagonal, prefetch):
        rows = pl.ds(r * rs, rs)
        u = s_ref[kb % 2, rows, :]
        if diagonal:
            row = lax.broadcasted_iota(I32, (rs, tq), 0) + r * rs
            col = lax.broadcasted_iota(I32, (rs, tq), 1)
            u = jnp.where(col <= row, u, NEG)
        if prefetch:
            logits(kb + 1, r)
        p, alpha = _softmax_probs(u, ct_ref[rows, :], rows, m_ref, l_ref)
        _accumulate(p, alpha, v_ref[0, keys(kb), :], rows, acc_ref)

    for r in range(n_groups):
        logits(0, r)

    def full_block(kb, carry):
        for r in range(n_groups):
            attend(kb, r, False, True)
        return carry

    lax.fori_loop(0, qi, full_block, 0)
    for r in range(n_groups):
        attend(qi, r, True, False)
    o_ref[0] = (acc_ref[...] / l_ref[...]).astype(o_ref.dtype)


def _fox_attention(qkv, c_col, c_row, tq=512):
    bsz, seq, _ = qkv.shape
    tq = min(tq, seq)
    nh = MIX_HEADS
    return pl.pallas_call(
        functools.partial(_fox_attn_body, tq=tq),
        grid=(bsz, nh, seq // tq),
        in_specs=[
            pl.BlockSpec((1, tq, HEAD_DIM), lambda b, h, i: (b, i, h)),
            pl.BlockSpec((1, seq, HEAD_DIM), lambda b, h, i: (b, 0, nh + h)),
            pl.BlockSpec((1, seq, HEAD_DIM), lambda b, h, i: (b, 0, 2 * nh + h)),
            pl.BlockSpec((1, tq, nh), lambda b, h, i: (b, i, 0)),
            pl.BlockSpec((1, nh, seq), lambda b, h, i: (b, 0, 0)),
        ],
        out_specs=pl.BlockSpec((1, tq, HEAD_DIM), lambda b, h, i: (b, i, h)),
        out_shape=jax.ShapeDtypeStruct((bsz, seq, nh * HEAD_DIM), BF16),
        scratch_shapes=[pltpu.VMEM((tq, HEAD_DIM), F32)] * 4 + [pltpu.VMEM((2, tq, tq), F32)],
        compiler_params=_cparams("parallel", "parallel", "arbitrary"),
        name="fox_attention",
    )(qkv, qkv, qkv, c_col, c_row)


def _t5_bucket_thresholds():
    n = np.arange(0, 4 * REL_MAX_DIST, dtype=np.int64)
    max_exact = REL_BUCKETS // 2
    ratio = np.log(np.maximum(n, 1).astype(np.float32) / np.float32(max_exact)) / np.float32(
        math.log(REL_MAX_DIST / max_exact))
    large = np.minimum(max_exact + (ratio * np.float32(REL_BUCKETS - max_exact)).astype(np.int32),
                       REL_BUCKETS - 1)
    bucket = np.where(n < max_exact, n, large)
    assert np.all(np.diff(bucket) >= 0) and bucket[-1] == REL_BUCKETS - 1
    return [int(np.argmax(bucket >= b)) for b in range(REL_BUCKETS)]


def _moba_body(rb_ref, q_ref, k_ref, v_ref, o_ref, kmean_ref, bias_ref, sel_ref, m_ref, l_ref, acc_ref,
               s_ref, *, n_blocks):
    blk = MOBA_BLOCK
    h = pl.program_id(1)
    qi = pl.program_id(2)
    far_bias = rb_ref[REL_BUCKETS - 1, h]

    @pl.when(qi == 0)
    def _():
        r = lax.broadcasted_iota(I32, (n_blocks, n_blocks * blk), 0)
        c = lax.broadcasted_iota(I32, (n_blocks, n_blocks * blk), 1)
        pool = jnp.where(c // blk == r, 1.0, 0.0).astype(BF16)
        kmean_ref[...] = _dot(pool, k_ref[0]) * (1.0 / blk)
        row = lax.broadcasted_iota(I32, (blk, blk), 0)
        col = lax.broadcasted_iota(I32, (blk, blk), 1)
        thresholds = _t5_bucket_thresholds()
        for slot, dist in ((0, row - col + blk), (1, row - col)):
            val = jnp.full((blk, blk), rb_ref[0, h], F32)
            for b in range(1, REL_BUCKETS):
                val = jnp.where(dist >= thresholds[b], rb_ref[b, h], val)
            if slot == 1:
                val = jnp.where(dist >= 0, val, NEG)
            bias_ref[slot] = val

    q = q_ref[0]
    km_hi = kmean_ref[...].astype(BF16)
    km_lo = (kmean_ref[...] - km_hi.astype(F32)).astype(BF16)
    gate = _dot_nt(km_hi, q) + _dot_nt(km_lo, q)
    blk_id = lax.broadcasted_iota(I32, gate.shape, 0)
    gate = jnp.where(blk_id < qi, gate, NEG)
    sel = jnp.zeros(gate.shape, F32)
    for _ in range(MOBA_TOPK):
        best = jnp.max(gate, axis=0, keepdims=True)
        first = jnp.min(jnp.where(gate == best, blk_id, n_blocks), axis=0, keepdims=True)
        pick = jnp.logical_and(blk_id == first, best > 0.5 * NEG)
        sel = jnp.where(pick, 1.0, sel)
        gate = jnp.where(blk_id == first, NEG, gate)
    sel_ref[...] = jnp.concatenate([sel, jnp.zeros((HEAD_DIM - n_blocks, blk), F32)], axis=0).T

    _softmax_init(m_ref, l_ref, acc_ref)
    rs = blk
    n_groups = blk // rs

    def sel_col(j):
        lanes = lax.broadcasted_iota(I32, (blk, HEAD_DIM), 1)
        return jnp.sum(jnp.where(lanes == j, sel_ref[...], 0.0), axis=1, keepdims=True) > 0.5

    def finish():
        o_ref[0] = (acc_ref[...] / l_ref[...]).astype(o_ref.dtype)

    @pl.when(qi == 0)
    def _():
        for r in range(n_groups):
            rows = pl.ds(r * rs, rs)
            u = _dot_nt(q_ref[0, rows, :], k_ref[0, 0:blk, :]) + bias_ref[1, r * rs:(r + 1) * rs, :]
            _softmax_rows(u, None, v_ref[0, 0:blk, :], rows, m_ref, l_ref, acc_ref)
        finish()

    @pl.when(qi >= 1)
    def _():
        n_far = qi - 1
        odd = n_far % 2
        n_far_units = (n_far + odd) // 2

        def first_block(unit):
            return jnp.maximum(2 * unit - odd, 0)

        def keys(unit):
            return pl.ds(pl.multiple_of(first_block(unit) * blk, blk), 2 * blk)

        def logits(unit, r):
            rows = pl.ds(r * rs, rs)
            s_ref[unit % 2, rows, :] = _dot_nt(q_ref[0, rows, :], k_ref[0, keys(unit), :])

        def attend(unit, r, extra_a, extra_b, prefetch):
            rows = pl.ds(r * rs, rs)
            u = s_ref[unit % 2, rows, :]
            u = jnp.concatenate([u[:, :blk] + extra_a, u[:, blk:] + extra_b], axis=1)
            if prefetch:
                logits(unit + 1, r)
            p, alpha = _softmax_probs(u, None, rows, m_ref, l_ref)
            _accumulate(p, alpha, v_ref[0, keys(unit), :], rows, acc_ref)

        for r in range(n_groups):
            logits(0, r)

        def far_unit(unit, carry):
            j = first_block(unit)
            masked_b = jnp.logical_and(unit == 0, odd == 1)
            bias_a = jnp.where(sel_col(j), far_bias, NEG)
            bias_b = jnp.where(jnp.logical_and(sel_col(j + 1), jnp.logical_not(masked_b)), far_bias, NEG)
            for r in range(n_groups):
                attend(unit, r, bias_a[r * rs:(r + 1) * rs], bias_b[r * rs:(r + 1) * rs], True)
            return carry

        lax.fori_loop(0, n_far_units, far_unit, 0)

        prev_mask = jnp.where(sel_col(qi - 1), 0.0, NEG)
        for r in range(n_groups):
            rr = slice(r * rs, (r + 1) * rs)
            attend(n_far_units, r, bias_ref[0, rr, :] + prev_mask[rr], bias_ref[1, rr, :], False)
        finish()


def _moba_attention(qkv, rel_bias):
    bsz, seq, _ = qkv.shape
    nh = MIX_HEADS
    blk = MOBA_BLOCK
    assert seq % blk == 0 and seq // blk <= HEAD_DIM
    n_blocks = seq // blk
    grid_spec = pltpu.PrefetchScalarGridSpec(
        num_scalar_prefetch=0,
        grid=(bsz, nh, n_blocks),
        in_specs=[
            pl.BlockSpec(memory_space=pltpu.SMEM),
            pl.BlockSpec((1, blk, HEAD_DIM), lambda b, h, i: (b, i, h)),
            pl.BlockSpec((1, seq, HEAD_DIM), lambda b, h, i: (b, 0, nh + h)),
            pl.BlockSpec((1, seq, HEAD_DIM), lambda b, h, i: (b, 0, 2 * nh + h)),
        ],
        out_specs=pl.BlockSpec((1, blk, HEAD_DIM), lambda b, h, i: (b, i, h)),
        scratch_shapes=[
            pltpu.VMEM((n_blocks, HEAD_DIM), F32),
            pltpu.VMEM((2, blk, blk), F32),
            pltpu.VMEM((blk, HEAD_DIM), F32),
            pltpu.VMEM((blk, HEAD_DIM), F32), pltpu.VMEM((blk, HEAD_DIM), F32), pltpu.VMEM((blk, HEAD_DIM), F32),
            pltpu.VMEM((2, blk, 2 * blk), F32),
        ],
    )
    return pl.pallas_call(
        functools.partial(_moba_body, n_blocks=n_blocks),
        grid_spec=grid_spec,
        out_shape=jax.ShapeDtypeStruct((bsz, seq, nh * HEAD_DIM), BF16),
        compiler_params=_cparams("parallel", "parallel", "arbitrary"),
        name="moba_attention",
    )(rel_bias, qkv, qkv, qkv)


def _mem_attn_body(q_ref, kv_ref, o_ref):
    width = MEM_HEADS * HEAD_DIM
    for hh in range(MEM_HEADS):
        cs = slice(hh * HEAD_DIM, (hh + 1) * HEAD_DIM)
        q = q_ref[0, :, cs].astype(BF16)
        k = kv_ref[0, :, cs]
        v = kv_ref[0, :, width + hh * HEAD_DIM:width + (hh + 1) * HEAD_DIM]
        s = _dot_nt(q, k)
        p = jnp.exp(s - jnp.max(s, axis=-1, keepdims=True))
        o = _dot(p.astype(BF16), v) / jnp.sum(p, axis=-1, keepdims=True)
        o_ref[0, :, cs] = o.astype(o_ref.dtype)


def _memory_attention(proj, q_col_block, kv, tq=512):
    bsz, seq, _ = proj.shape
    tq = min(tq, seq)
    width = MEM_HEADS * HEAD_DIM
    n_mem = kv.shape[1]
    return pl.pallas_call(
        _mem_attn_body,
        grid=(bsz, seq // tq),
        in_specs=[
            pl.BlockSpec((1, tq, width), lambda b, i: (b, i, q_col_block)),
            pl.BlockSpec((1, n_mem, 2 * width), lambda b, i: (b, 0, 0)),
        ],
        out_specs=pl.BlockSpec((1, tq, width), lambda b, i: (b, i, 0)),
        out_shape=jax.ShapeDtypeStruct((bsz, seq, width), BF16),
        compiler_params=_cparams("parallel", "parallel"),
        name="memory_attention",
    )(proj, kv)


def _to_row_tiles(ref, y):
    rows, d = y.shape
    n_tiles = d // HEAD_DIM
    for c in range(n_tiles):
        ref[pl.ds(c, rows, stride=n_tiles), :] = y[:, c * HEAD_DIM:(c + 1) * HEAD_DIM]


def _from_row_tiles(ref, rows, n_tiles, *lead):
    return jnp.concatenate(
        [ref[(*lead, pl.ds(c, rows, stride=n_tiles), slice(None))] for c in range(n_tiles)], axis=1)


def _row_tile(ref, r, n_tiles, *lead):
    return ref.at[(*lead, pl.ds(pl.multiple_of(r * n_tiles, n_tiles), n_tiles))]


def _out_ln_body(mix_ref, mem_ref, x_ref, w1_ref, w2_ref, g_ref, b_ref, o_ref, ort_ref, *, alpha):
    hproj = _dot(mix_ref[...], w1_ref[...]) + _dot(mem_ref[...], w2_ref[...])
    y = _layer_norm(alpha * x_ref[...] + hproj, g_ref[...], b_ref[...])
    o_ref[...] = y
    _to_row_tiles(ort_ref, y)


def _out_proj_ln(mix, mem_out, x, w_out, g, b, alpha, tm=512):
    t, d = x.shape
    tm = min(tm, t)
    wm = mix.shape[1]
    we = mem_out.shape[1]
    return pl.pallas_call(
        functools.partial(_out_ln_body, alpha=alpha),
        grid=(t // tm,),
        in_specs=[
            pl.BlockSpec((tm, wm), lambda i: (i, 0)),
            pl.BlockSpec((tm, we), lambda i: (i, 0)),
            pl.BlockSpec((tm, d), lambda i: (i, 0)),
            pl.BlockSpec((wm, d), lambda i: (0, 0)),
            pl.BlockSpec((we, d), lambda i: (0, 0)),
            pl.BlockSpec((1, d), lambda i: (0, 0)),
            pl.BlockSpec((1, d), lambda i: (0, 0)),
        ],
        out_specs=[pl.BlockSpec((tm, d), lambda i: (i, 0)),
                   pl.BlockSpec((tm * (d // HEAD_DIM), HEAD_DIM), lambda i: (i, 0))],
        out_shape=[jax.ShapeDtypeStruct((t, d), F32),
                   jax.ShapeDtypeStruct((t * (d // HEAD_DIM), HEAD_DIM), F32)],
        compiler_params=_cparams("parallel"),
        name="out_proj_ln",
    )(mix, mem_out, x, w_out[:wm].astype(BF16), w_out[wm:].astype(BF16), g.reshape(1, d), b.reshape(1, d))


def _router_body(x_ref, wt_ref, b_ref, idx_ref, gate_ref, cnt_ref, run_ref, *, tb):
    @pl.when(pl.program_id(0) == 0)
    def _():
        run_ref[...] = jnp.zeros_like(run_ref)

    x = x_ref[...]
    xh = x.astype(BF16)
    xl = (x - xh.astype(F32)).astype(BF16)
    w = wt_ref[...]
    wh = w.astype(BF16)
    wl = (w - wh.astype(F32)).astype(BF16)
    logits = _dot_nt(wh, xh) + _dot_nt(wh, xl) + _dot_nt(wl, xh) + b_ref[...]

    n_e = logits.shape[0]
    eid = lax.broadcasted_iota(I32, logits.shape, 0)
    vals, idxs, hots = [], [], []
    multi = jnp.zeros(logits.shape, F32)
    for _ in range(TOP_K):
        best = jnp.max(logits, axis=0, keepdims=True)
        first = jnp.min(jnp.where(logits == best, eid, n_e), axis=0, keepdims=True)
        hot = eid == first
        vals.append(best)
        idxs.append(first)
        hots.append(hot)
        multi = jnp.where(hot, 1.0, multi)
        logits = jnp.where(hot, -jnp.inf, logits)

    exps = [jnp.exp(v - vals[0]) for v in vals]
    den = exps[0]
    for e in exps[1:]:
        den = den + e

    r = lax.broadcasted_iota(I32, (tb, tb), 0)
    c = lax.broadcasted_iota(I32, (tb, tb), 1)
    before = jnp.where(r < c, 1.0, 0.0).astype(BF16)
    pos = _dot(multi.astype(BF16), before) + run_ref[:, 0:1]
    for kk in range(TOP_K):
        idx_ref[kk:kk + 1, :] = idxs[kk]
        rank = jnp.sum(jnp.where(hots[kk], pos, 0.0), axis=0, keepdims=True)
        idx_ref[TOP_K + kk:TOP_K + kk + 1, :] = rank.astype(I32)
        gate_ref[kk:kk + 1, :] = exps[kk] / den
    gate_ref[TOP_K:2 * TOP_K, :] = jnp.zeros((TOP_K, tb), F32)
    run_ref[...] = run_ref[...] + jnp.sum(multi, axis=1, keepdims=True)
    cnt_ref[...] = run_ref[...]


def _router(x, router_w, router_b, tb=512):
    t, d = x.shape
    tb = min(tb, t)
    n_e = router_w.shape[1]
    return pl.pallas_call(
        functools.partial(_router_body, tb=tb),
        grid=(t // tb,),
        in_specs=[
            pl.BlockSpec((tb, d), lambda i: (i, 0)),
            pl.BlockSpec((n_e, d), lambda i: (0, 0)),
            pl.BlockSpec((n_e, 1), lambda i: (0, 0)),
        ],
        out_specs=[
            pl.BlockSpec((2 * TOP_K, tb), lambda i: (0, i)),
            pl.BlockSpec((2 * TOP_K, tb), lambda i: (0, i)),
            pl.BlockSpec((n_e, HEAD_DIM), lambda i: (0, 0)),
        ],
        out_shape=[
            jax.ShapeDtypeStruct((2 * TOP_K, t), I32),
            jax.ShapeDtypeStruct((2 * TOP_K, t), F32),
            jax.ShapeDtypeStruct((n_e, HEAD_DIM), F32),
        ],
        scratch_shapes=[pltpu.VMEM((n_e, HEAD_DIM), F32)],
        compiler_params=_cparams("arbitrary"),
        name="moe_router",
    )(x, router_w.T, router_b.reshape(n_e, 1))


def _dispatch_body(dest_ref, pad_lo_ref, pad_hi_ref, nu_ref, x_ref, xs_hbm, zero_ref, sem, zsem,
                   *, tb, n_tok, n_tiles, n_experts, n_blocks):
    base = pl.program_id(0) * tb

    def issue(t, carry):
        for kk in range(TOP_K):
            d = dest_ref[kk * n_tok + base + t]
            pltpu.make_async_copy(_row_tile(x_ref, t, n_tiles), _row_tile(xs_hbm, d, n_tiles), sem).start(
                priority=kk % 2)
        return carry

    lax.fori_loop(0, tb, issue, 0, unroll=DMA_UNROLL)

    @pl.when(pl.program_id(0) == 0)
    def _():
        zero_ref[...] = jnp.zeros_like(zero_ref)

        def pad_row(r):
            return pltpu.make_async_copy(zero_ref.at[pl.ds(0, n_tiles)], _row_tile(xs_hbm, r, n_tiles), zsem)

        def spare_block(blk):
            rows = pl.ds(pl.multiple_of(blk * MOE_BLOCK * n_tiles, MOE_BLOCK * n_tiles), MOE_BLOCK * n_tiles)
            return pltpu.make_async_copy(zero_ref, xs_hbm.at[rows], zsem)

        def for_all(action):
            for e in range(n_experts):
                lax.fori_loop(pad_lo_ref[e], pad_hi_ref[e], lambda r, c: (action(pad_row(r)), c)[1], 0)
            lax.fori_loop(nu_ref[0], n_blocks, lambda blk, c: (action(spare_block(blk)), c)[1], 0)

        for_all(lambda cp: cp.start())
        for_all(lambda cp: cp.wait())

    for kk in range(TOP_K):
        pltpu.make_async_copy(x_ref, xs_hbm.at[pl.ds(0, tb * n_tiles)], sem).wait()


def _dispatch(x, n_tiles, dest_flat, pad_lo, pad_hi, n_used, n_rows, tb=512):
    t = x.shape[0] // n_tiles
    tb = min(tb, t)
    grid_spec = pltpu.PrefetchScalarGridSpec(
        num_scalar_prefetch=4,
        grid=(t // tb,),
        in_specs=[pl.BlockSpec((tb * n_tiles, HEAD_DIM), lambda i, *_: (i, 0))],
        out_specs=pl.BlockSpec(memory_space=pl.ANY),
        scratch_shapes=[pltpu.VMEM((MOE_BLOCK * n_tiles, HEAD_DIM), x.dtype),
                        pltpu.SemaphoreType.DMA(()), pltpu.SemaphoreType.DMA(())],
    )
    return pl.pallas_call(
        functools.partial(_dispatch_body, tb=tb, n_tok=t, n_tiles=n_tiles, n_experts=pad_lo.shape[0],
                          n_blocks=n_rows // MOE_BLOCK),
        grid_spec=grid_spec,
        out_shape=jax.ShapeDtypeStruct((n_rows * n_tiles, HEAD_DIM), x.dtype),
        compiler_params=_cparams("arbitrary"),
        name="moe_dispatch",
    )(dest_flat, pad_lo, pad_hi, n_used, x)


UNZIP = 256


def _expert_body(be_ref, nu_ref, first_ref, slot_ref, next_ref, x_ref, wu_hbm, bg_ref, bl_ref, wd_hbm, bd_ref,
                 o_ref, wu_buf, wd_buf, wg_s, wl_s, wd_s, sem, *, layer, n_tiles):
    i = pl.program_id(0)
    n_used = nu_ref[0]

    def weight_copies(expert, slot):
        return (pltpu.make_async_copy(wu_hbm.at[layer, expert], wu_buf.at[slot], sem.at[0, slot]),
                pltpu.make_async_copy(wd_hbm.at[layer, expert], wd_buf.at[slot], sem.at[1, slot]))

    @pl.when(i == 0)
    def _():
        for cp in weight_copies(be_ref[0], 0):
            cp.start()

    @pl.when(jnp.logical_and(first_ref[i] == 1, i < n_used))
    def _():
        slot = slot_ref[i]
        for cp in weight_copies(be_ref[i], slot):
            cp.wait()

        @pl.when(next_ref[i] >= 0)
        def _():
            for cp in weight_copies(next_ref[i], 1 - slot):
                cp.start()

        half = UNZIP // 2
        r = lax.broadcasted_iota(I32, (UNZIP, UNZIP), 0)
        c = lax.broadcasted_iota(I32, (UNZIP, UNZIP), 1)
        src = jnp.where(c < half, 2 * c, 2 * (c - half) + 1)
        perm = jnp.where(r == src, 1.0, 0.0).astype(BF16)
        for g in range(wu_buf.shape[2] // UNZIP):
            w = wu_buf[slot, :, g * UNZIP:(g + 1) * UNZIP].astype(BF16)
            sep = _dot(w, perm).astype(BF16)
            wg_s[:, g * half:(g + 1) * half] = sep[:, :half]
            wl_s[:, g * half:(g + 1) * half] = sep[:, half:]
        wd_s[...] = wd_buf[slot].astype(BF16)

    @pl.when(i < n_used)
    def _():
        xb = _from_row_tiles(x_ref, MOE_BLOCK, n_tiles).astype(BF16)
        glu = jnp.minimum(_dot(xb, wg_s[...]) + bg_ref[0], SWIGLU_LIMIT)
        lin = jnp.clip(_dot(xb, wl_s[...]) + bl_ref[0], -SWIGLU_LIMIT, SWIGLU_LIMIT)
        act = glu * _sigmoid(SWIGLU_ALPHA * glu) * (lin + 1.0)
        _to_row_tiles(o_ref, _dot(act.astype(BF16), wd_s[...]) + bd_ref[0])

    @pl.when(i >= n_used)
    def _():
        o_ref[...] = jnp.zeros_like(o_ref)


def _expert_mlp(xs, n_tiles, blk_expert, n_used, counts, layer, w_up, bg, bl, w_down, bd):
    p_rows = xs.shape[0] // n_tiles
    d = n_tiles * HEAD_DIM
    f = w_down.shape[2]
    n_e = w_down.shape[1]
    n_blk = p_rows // MOE_BLOCK

    blocks = jnp.arange(n_blk, dtype=I32)
    valid = blocks < n_used[0]
    prev_e = jnp.concatenate([jnp.full((1,), -1, I32), blk_expert[:-1]])
    first = jnp.logical_and(valid, blk_expert != prev_e)
    slot = (jnp.cumsum(first.astype(I32)) - 1) % 2
    experts = jnp.arange(n_e, dtype=I32)
    later = jnp.logical_and(experts[None, :] > blk_expert[:, None], counts[None, :] > 0)
    nxt = jnp.min(jnp.where(later, experts[None, :], n_e), axis=1)
    nxt = jnp.where(nxt == n_e, -1, nxt)

    def blk(i, nu):
        return jnp.minimum(i, nu[0] - 1)

    def w_spec(shape):
        return pl.BlockSpec((1,) + shape, lambda i, be, nu, fi, sl, nx: (be[blk(i, nu)], 0, 0))

    grid_spec = pltpu.PrefetchScalarGridSpec(
        num_scalar_prefetch=5,
        grid=(n_blk,),
        in_specs=[
            pl.BlockSpec((MOE_BLOCK * n_tiles, HEAD_DIM), lambda i, be, nu, fi, sl, nx: (blk(i, nu), 0)),
            pl.BlockSpec(memory_space=pl.ANY), w_spec((1, f)), w_spec((1, f)),
            pl.BlockSpec(memory_space=pl.ANY), w_spec((1, d)),
        ],
        out_specs=pl.BlockSpec((MOE_BLOCK * n_tiles, HEAD_DIM), lambda i, be, nu, fi, sl, nx: (i, 0)),
        scratch_shapes=[
            pltpu.VMEM((2, d, 2 * f), F32), pltpu.VMEM((2, f, d), F32),
            pltpu.VMEM((d, f), BF16), pltpu.VMEM((d, f), BF16), pltpu.VMEM((f, d), BF16),
            pltpu.SemaphoreType.DMA((2, 2)),
        ],
    )
    return pl.pallas_call(
        functools.partial(_expert_body, layer=layer, n_tiles=n_tiles),
        grid_spec=grid_spec,
        out_shape=jax.ShapeDtypeStruct(xs.shape, F32),
        compiler_params=_cparams("arbitrary"),
        name="moe_experts",
    )(blk_expert, n_used, first.astype(I32), slot.astype(I32), nxt.astype(I32), xs, w_up, bg, bl, w_down, bd)


def _combine_body(dest_ref, ys_hbm, gate_ref, x_ref, g_ref, b_ref, o_ref, buf_ref, sem_ref,
                  *, tb, n_tok, n_tiles, alpha):
    i = pl.program_id(0)
    n_steps = pl.num_programs(0)

    def issue(step, slot):
        base = step * tb

        def one(t, carry):
            for kk in range(TOP_K):
                d = dest_ref[kk * n_tok + base + t]
                pltpu.make_async_copy(_row_tile(ys_hbm, d, n_tiles), _row_tile(buf_ref, t, n_tiles, slot, kk),
                                      sem_ref.at[slot]).start(priority=kk % 2)
            return carry

        lax.fori_loop(0, tb, one, 0, unroll=DMA_UNROLL)

    @pl.when(i == 0)
    def _():
        issue(0, 0)

    slot = lax.rem(i, 2)

    @pl.when(i + 1 < n_steps)
    def _():
        issue(i + 1, 1 - slot)

    for kk in range(TOP_K):
        pltpu.make_async_copy(ys_hbm.at[pl.ds(0, tb * n_tiles)], buf_ref.at[slot, kk], sem_ref.at[slot]).wait()

    g8 = gate_ref[...]
    gpad = jnp.concatenate([g8, jnp.zeros((HEAD_DIM - g8.shape[0], tb), F32)], axis=0)
    gt = gpad.T
    f = gt[:, 0:1] * _from_row_tiles(buf_ref, tb, n_tiles, slot, 0)
    for kk in range(1, TOP_K):
        f = f + gt[:, kk:kk + 1] * _from_row_tiles(buf_ref, tb, n_tiles, slot, kk)
    o_ref[...] = _layer_norm(alpha * x_ref[...] + f, g_ref[...], b_ref[...])


def _combine_ln(ys, dest_flat, gates, x, g, b, alpha, tb=256):
    t, d = x.shape
    tb = min(tb, t)
    n_tiles = d // HEAD_DIM
    grid_spec = pltpu.PrefetchScalarGridSpec(
        num_scalar_prefetch=1,
        grid=(t // tb,),
        in_specs=[
            pl.BlockSpec(memory_space=pl.ANY),
            pl.BlockSpec((2 * TOP_K, tb), lambda i, dest: (0, i)),
            pl.BlockSpec((tb, d), lambda i, dest: (i, 0)),
            pl.BlockSpec((1, d), lambda i, dest: (0, 0)),
            pl.BlockSpec((1, d), lambda i, dest: (0, 0)),
        ],
        out_specs=pl.BlockSpec((tb, d), lambda i, dest: (i, 0)),
        scratch_shapes=[pltpu.VMEM((2, TOP_K, tb * n_tiles, HEAD_DIM), F32), pltpu.SemaphoreType.DMA((2,))],
    )
    return pl.pallas_call(
        functools.partial(_combine_body, tb=tb, n_tok=t, n_tiles=n_tiles, alpha=alpha),
        grid_spec=grid_spec,
        out_shape=jax.ShapeDtypeStruct((t, d), F32),
        compiler_params=_cparams("arbitrary"),
        name="moe_combine_ln",
    )(dest_flat, ys, gates, x, g.reshape(1, d), b.reshape(1, d))


def _moe_ln(x, x_tiles, router_w, router_b, layer, w_up, b_up, w_down, b_down, g, b, alpha):
    t, d = x.shape
    n_tiles = d // HEAD_DIM
    n_e = router_w.shape[1]
    idx_rank, gates, cnt = _router(x, router_w, router_b)
    counts = cnt[:, 0].astype(I32)
    padded = ((counts + MOE_BLOCK - 1) // MOE_BLOCK) * MOE_BLOCK
    p_ends = jnp.cumsum(padded)
    p_starts = p_ends - padded
    n_blk = (t * TOP_K + MOE_BLOCK - 1) // MOE_BLOCK + n_e
    experts = jnp.arange(n_e, dtype=I32)
    start_of = jnp.sum(jnp.where(idx_rank[:TOP_K, :, None] == experts, p_starts, 0), axis=-1)
    dest_flat = (start_of + idx_rank[TOP_K:]).reshape(-1)
    blk_first_row = jnp.arange(n_blk, dtype=I32) * MOE_BLOCK
    blk_expert = jnp.clip(jnp.sum((p_ends[None, :] <= blk_first_row[:, None]).astype(I32), axis=1), 0, n_e - 1)
    n_used = (p_ends[-1:] // MOE_BLOCK).astype(I32)

    xs = _dispatch(x_tiles, n_tiles, dest_flat, p_starts + counts, p_ends, n_used, n_blk * MOE_BLOCK)
    bg = b_up[:, None, 0::2]
    bl = b_up[:, None, 1::2]
    ys = _expert_mlp(xs, n_tiles, blk_expert, n_used, counts, layer, w_up, bg, bl, w_down, b_down[:, None, :])
    return _combine_ln(ys, dest_flat, gates, x, g, b, alpha)


def kernel(x, mem, hgrn_w_in, hgrn_lb, hgrn_norm_g, fox_w_in, fox_b_f, moba_w_in, rel_bias, w_mem_kv, w_out,
           ln1_g, ln1_b, router_w, router_b, w_up, b_up, w_down, b_down, ln2_g, ln2_b):
    bsz, seq, d = x.shape
    depth = w_out.shape[0]
    t = bsz * seq
    mix_w = MIX_HEADS * HEAD_DIM
    mem_w = MEM_HEADS * HEAD_DIM
    alpha = (2 * depth) ** 0.25
    scale = HEAD_DIM ** -0.5
    n_mem = mem.shape[1]
    mem2 = mem.reshape(bsz * n_mem, d)

    def scale_vec(n_scaled_front, n_plain, n_scaled_back):
        return jnp.concatenate([
            jnp.full((1, n_scaled_front), scale, F32), jnp.ones((1, n_plain), F32),
            jnp.full((1, n_scaled_back), scale, F32)], axis=1)

    x2 = x.reshape(t, d)
    for i in range(depth):
        kind, j = i % 3, i // 3
        if kind == 0:
            w_in = hgrn_w_in[j].astype(BF16)
            proj = _matmul(x2, w_in, scale_vec(0, 4 * mix_w, mem_w), F32, 1024, w_in.shape[1] // 3)
            proj = proj.reshape(bsz, seq, -1)
            mix = _hgrn_mixer(proj, hgrn_lb, hgrn_norm_g[j], j)
            memq_block = 4 * mix_w // mem_w
        elif kind == 1:
            w = fox_w_in[j]
            w_in = jnp.concatenate([w[:, :3 * mix_w], w[:, 3 * mix_w + MIX_HEADS:]], axis=1).astype(BF16)
            proj = _matmul(x2, w_in, scale_vec(mix_w, 2 * mix_w, mem_w), BF16, 1024, w_in.shape[1] // 2)
            proj = proj.reshape(bsz, seq, -1)
            c_col, c_row = _fox_gate_cumsum(x2.reshape(bsz, seq, d), w[:, 3 * mix_w:3 * mix_w + MIX_HEADS],
                                            fox_b_f[j])
            mix = _fox_attention(proj, c_col, c_row)
            memq_block = 3 * mix_w // mem_w
        else:
            w_in = moba_w_in[j].astype(BF16)
            proj = _matmul(x2, w_in, scale_vec(mix_w, 2 * mix_w, mem_w), BF16, 1024, w_in.shape[1] // 2)
            proj = proj.reshape(bsz, seq, -1)
            mix = _moba_attention(proj, rel_bias)
            memq_block = 3 * mix_w // mem_w
        kv = _matmul(mem2, w_mem_kv[i].astype(BF16), jnp.ones((1, 2 * mem_w), F32), BF16, 512, 512)
        mem_out = _memory_attention(proj, memq_block, kv.reshape(bsz, n_mem, 2 * mem_w))
        x2, x2_tiles = _out_proj_ln(mix.reshape(t, mix_w), mem_out.reshape(t, mem_w), x2, w_out[i],
                                    ln1_g[i], ln1_b[i], alpha)
        x2 = _moe_ln(x2, x2_tiles, router_w[i], router_b[i], i, w_up, b_up[i], w_down, b_down[i],
                     ln2_g[i], ln2_b[i], alpha)
    return x2.reshape(bsz, seq, d)
```

```python
import functools
import math

import numpy as np
import jax
import jax.numpy as jnp
from jax import lax
from jax.experimental import pallas as pl
from jax.experimental.pallas import tpu as pltpu

F32 = jnp.float32
BF16 = jnp.bfloat16
I32 = jnp.int32

HEAD_DIM = 128
MIX_HEADS = 8
MEM_HEADS = 4
HGRN_CHUNK = 64
HGRN_SUB = 16
HGRN_EXP_CLAMP = 80.0
MOBA_BLOCK = 256
MOBA_TOPK = 3
REL_BUCKETS = 32
REL_MAX_DIST = 128
N_EXPERTS = 32
TOP_K = 4
MOE_BLOCK = 256
SWIGLU_LIMIT = 7.0
SWIGLU_ALPHA = 1.702
LN_EPS = 1e-5
RMS_EPS = 1e-6
NEG = -1e30
LOG2E = 1.4426950408889634
VMEM_LIMIT = 56 * 1024 * 1024
DMA_UNROLL = 8

NT_DIMS = (((1,), (1,)), ((), ()))
TN_DIMS = (((0,), (0,)), ((), ()))


def _cparams(*sem):
    return pltpu.CompilerParams(dimension_semantics=sem, vmem_limit_bytes=VMEM_LIMIT)


def _dot(a, b):
    return jnp.dot(a, b, preferred_element_type=F32)


def _dot_nt(a, b):
    return lax.dot_general(a, b, NT_DIMS, preferred_element_type=F32)


def _dot_tn(a, b):
    return lax.dot_general(a, b, TN_DIMS, preferred_element_type=F32)


def _split3(a):
    hi = a.astype(BF16)
    r = a - hi.astype(F32)
    mid = r.astype(BF16)
    lo = (r - mid.astype(F32)).astype(BF16)
    return hi, mid, lo


def _sigmoid(x):
    return 1.0 / (1.0 + jnp.exp(-x))


def _layer_norm(z, g, b):
    mu = jnp.mean(z, axis=-1, keepdims=True)
    zc = z - mu
    var = jnp.mean(zc * zc, axis=-1, keepdims=True)
    return zc * lax.rsqrt(var + LN_EPS) * g + b


def _mm_body(x_ref, w_ref, s_ref, o_ref, xb_ref):
    @pl.when(pl.program_id(1) == 0)
    def _():
        xb_ref[...] = x_ref[...].astype(BF16)

    acc = _dot(xb_ref[...], w_ref[...])
    o_ref[...] = (acc * s_ref[...]).astype(o_ref.dtype)


def _matmul(x, w, col_scale, out_dtype, tm, tn):
    m, k = x.shape
    n = w.shape[1]
    tm = min(tm, m)
    tn = min(tn, n)
    return pl.pallas_call(
        _mm_body,
        grid=(m // tm, n // tn),
        in_specs=[
            pl.BlockSpec((tm, k), lambda i, j: (i, 0)),
            pl.BlockSpec((k, tn), lambda i, j: (0, j)),
            pl.BlockSpec((1, tn), lambda i, j: (0, j)),
        ],
        out_specs=pl.BlockSpec((tm, tn), lambda i, j: (i, j)),
        out_shape=jax.ShapeDtypeStruct((m, n), out_dtype),
        scratch_shapes=[pltpu.VMEM((tm, k), BF16)],
        compiler_params=_cparams("parallel", "arbitrary"),
        name="proj_matmul",
    )(x, w, col_scale)


def _hgrn_body(q_ref, z_ref, i_ref, g_ref, lbp_ref, ng_ref, o_ref, st_ref, *, layer_j, tb):
    c_len, sb = HGRN_CHUNK, HGRN_SUB
    n_sub = c_len // sb

    @pl.when(pl.program_id(2) == 0)
    def _():
        st_ref[...] = jnp.zeros_like(st_ref)

    lbp = lbp_ref[...]
    ex = jnp.exp(lbp - jnp.max(lbp, axis=0, keepdims=True))
    p = ex / jnp.sum(ex, axis=0, keepdims=True)
    lb = jnp.zeros((1, HEAD_DIM), F32)
    for r in range(1, layer_j + 1):
        lb = lb + p[r:r + 1, :]
    log_lb = jnp.log(lb)
    log1m_lb = jnp.log1p(-lb)
    one_m_lb = 1.0 - lb
    ng = ng_ref[...]

    row = lax.broadcasted_iota(I32, (c_len, c_len), 0)
    col = lax.broadcasted_iota(I32, (c_len, c_len), 1)
    sub_start = (row // sb) * sb
    one = jnp.ones((c_len, c_len), F32)
    zero = jnp.zeros((c_len, c_len), F32)
    m_tri = jnp.where(col <= row, one, zero)
    m_in = jnp.where(col > sub_start, m_tri, zero)
    m_dec = jnp.where(col > row, one, zero)
    m_ke = jnp.where(col <= sub_start + (sb - 1), m_dec, zero)
    stack = jnp.concatenate([m_tri, m_in, m_ke, m_dec], axis=0).astype(BF16)
    same_sub = (row // sb) == (col // sb)
    diag_mask = jnp.logical_and(same_sub, col <= row)
    row_sub = row // sb
    krow_sub = lax.broadcasted_iota(I32, (c_len, HEAD_DIM), 0) // sb

    chunks = [pl.ds(c * c_len, c_len) for c in range(tb // c_len)]

    gates = []
    for sl in chunks:
        z = z_ref[0, sl, :]
        e = jnp.exp(-jnp.abs(z))
        inv = 1.0 / (1.0 + e)
        sig_neg = jnp.where(z >= 0, e, 1.0) * inv
        log_sig = jnp.minimum(z, 0.0) - jnp.log1p(e)
        a2 = log1m_lb + log_sig
        log_f = jnp.maximum(log_lb, a2) + jnp.log1p(jnp.exp(-jnp.abs(log_lb - a2)))
        qv = q_ref[0, sl, :]
        gates.append((log_f, one_m_lb * sig_neg, qv * _sigmoid(qv), i_ref[0, sl, :].astype(BF16)))

    prefix = []
    for log_f, _, _, _ in gates:
        pre = None
        for piece in _split3(log_f):
            t = _dot(stack, piece)
            pre = t if pre is None else pre + t
        prefix.append(pre)

    factors = []
    for (_, k, qf, _), pre in zip(gates, prefix):
        b = pre[0:c_len]
        d_in = pre[c_len:2 * c_len]
        d_ke = pre[2 * c_len:3 * c_len]
        d_dec = pre[3 * c_len:4 * c_len]
        q_in = qf * jnp.exp(d_in)
        k_diag = (k * jnp.exp(jnp.minimum(-d_in, HGRN_EXP_CLAMP))).astype(BF16)
        k_end = k * jnp.exp(d_ke)
        k_dec = (k * jnp.exp(d_dec)).astype(BF16)
        q_all = (qf * jnp.exp(b)).astype(BF16)
        b_ref = b - d_in
        lhs = [q_in.astype(BF16)]
        rhs = [k_diag]
        for j in range(n_sub - 1):
            b_end = b[(j + 1) * sb - 1:(j + 1) * sb, :]
            cross = jnp.exp(jnp.minimum(b_ref - b_end, 0.0))
            lhs.append((q_in * cross).astype(BF16))
            rhs.append(jnp.where(krow_sub == j, k_end, 0.0).astype(BF16))
        factors.append((lhs, rhs, k_dec, q_all, jnp.exp(b[c_len - 1:c_len, :])))

    scores = []
    for lhs, rhs, _, _, _ in factors:
        a = jnp.where(diag_mask, _dot_nt(lhs[0], rhs[0]), 0.0)
        for j in range(n_sub - 1):
            a = a + jnp.where(row_sub > j, _dot_nt(lhs[j + 1], rhs[j + 1]), 0.0)
        scores.append(a.astype(BF16))

    updates = [_dot_tn(vb, k_dec) for (_, _, _, vb), (_, _, k_dec, _, _) in zip(gates, factors)]
    states = [st_ref[...]]
    for (_, _, _, _, decay), kv in zip(factors, updates):
        states.append(states[-1] * decay + kv)
    st_ref[...] = states[-1]

    for sl, (_, _, _, vb), (_, _, _, q_all, _), a, st in zip(chunks, gates, factors, scores, states):
        o = _dot(a, vb) + _dot_nt(q_all, st.astype(BF16))
        o = o * lax.rsqrt(jnp.mean(o * o, axis=-1, keepdims=True) + RMS_EPS)
        gv = g_ref[0, sl, :]
        o_ref[0, sl, :] = (o * ng * (gv * _sigmoid(gv))).astype(o_ref.dtype)


def _hgrn_mixer(proj, lb_param, norm_g, layer_j, tb=512):
    bsz, seq, _ = proj.shape
    tb = min(tb, seq)
    n_layers = lb_param.shape[0]

    def col_block(off):
        return pl.BlockSpec((1, tb, HEAD_DIM), lambda b, h, t: (b, t, off + h))

    return pl.pallas_call(
        functools.partial(_hgrn_body, layer_j=layer_j, tb=tb),
        grid=(bsz, MIX_HEADS, seq // tb),
        in_specs=[
            col_block(0), col_block(MIX_HEADS), col_block(2 * MIX_HEADS), col_block(3 * MIX_HEADS),
            pl.BlockSpec((n_layers, HEAD_DIM), lambda b, h, t: (0, h)),
            pl.BlockSpec((1, HEAD_DIM), lambda b, h, t: (0, h)),
        ],
        out_specs=pl.BlockSpec((1, tb, HEAD_DIM), lambda b, h, t: (b, t, h)),
        out_shape=jax.ShapeDtypeStruct((bsz, seq, MIX_HEADS * HEAD_DIM), BF16),
        scratch_shapes=[pltpu.VMEM((HEAD_DIM, HEAD_DIM), F32)],
        compiler_params=_cparams("parallel", "parallel", "arbitrary"),
        name="hgrn2_mixer",
    )(proj, proj, proj, proj, lb_param, norm_g.reshape(1, -1))


def _log_sigmoid(z):
    return jnp.minimum(z, 0.0) - jnp.log1p(jnp.exp(-jnp.abs(z)))


def _fox_gate_body(x_ref, w_ref, wt_ref, bc_ref, br_ref, ccol_ref, crow_ref, carc_ref, carr_ref, *, tb):
    @pl.when(pl.program_id(1) == 0)
    def _():
        carc_ref[...] = jnp.zeros_like(carc_ref)
        carr_ref[...] = jnp.zeros_like(carr_ref)

    xb = x_ref[0].astype(BF16)
    ls_col = LOG2E * _log_sigmoid(_dot(xb, w_ref[...]) + bc_ref[...])
    ls_row = LOG2E * _log_sigmoid(_dot_nt(wt_ref[...], xb) + br_ref[...])
    row = lax.broadcasted_iota(I32, (tb, tb), 0)
    col = lax.broadcasted_iota(I32, (tb, tb), 1)
    lower = jnp.where(col <= row, 1.0, 0.0).astype(BF16)
    upper = jnp.where(row <= col, 1.0, 0.0).astype(BF16)
    c_col = carc_ref[...]
    for piece in _split3(ls_col):
        c_col = c_col + _dot(lower, piece)
    c_row = carr_ref[...]
    for piece in _split3(ls_row):
        c_row = c_row + _dot(piece, upper)
    ccol_ref[0] = c_col
    crow_ref[0] = c_row
    carc_ref[...] = c_col[tb - 1:tb, :]
    carr_ref[...] = c_row[:, tb - 1:tb]


def _fox_gate_cumsum(x, w_f, b_f, tb=512):
    bsz, seq, d = x.shape
    h = w_f.shape[1]
    tb = min(tb, seq)
    return pl.pallas_call(
        functools.partial(_fox_gate_body, tb=tb),
        grid=(bsz, seq // tb),
        in_specs=[
            pl.BlockSpec((1, tb, d), lambda b, t: (b, t, 0)),
            pl.BlockSpec((d, h), lambda b, t: (0, 0)),
            pl.BlockSpec((h, d), lambda b, t: (0, 0)),
            pl.BlockSpec((1, h), lambda b, t: (0, 0)),
            pl.BlockSpec((h, 1), lambda b, t: (0, 0)),
        ],
        out_specs=[
            pl.BlockSpec((1, tb, h), lambda b, t: (b, t, 0)),
            pl.BlockSpec((1, h, tb), lambda b, t: (b, 0, t)),
        ],
        out_shape=[
            jax.ShapeDtypeStruct((bsz, seq, h), F32),
            jax.ShapeDtypeStruct((bsz, h, seq), F32),
        ],
        scratch_shapes=[pltpu.VMEM((1, h), F32), pltpu.VMEM((h, 1), F32)],
        compiler_params=_cparams("parallel", "arbitrary"),
        name="fox_gate_cumsum",
    )(x, w_f.astype(BF16), w_f.T.astype(BF16), b_f.reshape(1, h), b_f.reshape(h, 1))


ATTN_ROWS = 128


def _softmax_probs(u, shift, rows, m_ref, l_ref):
    reps = u.shape[1] // HEAD_DIM
    m_old = m_ref[rows, :]
    if shift is None:
        m_new = jnp.maximum(m_old, jnp.max(u, axis=-1, keepdims=True))
        p = jnp.exp2(u - jnp.tile(m_new, (1, reps)))
    else:
        m_new = jnp.maximum(m_old, jnp.max(u, axis=-1, keepdims=True) + shift)
        p = jnp.exp2(u - jnp.tile(m_new - shift, (1, reps)))
    alpha = jnp.exp2(m_old - m_new)
    l_ref[rows, :] = alpha * l_ref[rows, :] + jnp.sum(p, axis=-1, keepdims=True)
    m_ref[rows, :] = m_new
    return p.astype(BF16), alpha


def _accumulate(p, alpha, v, rows, acc_ref):
    acc_ref[rows, :] = alpha * acc_ref[rows, :] + _dot(p, v)


def _softmax_rows(u, shift, v, rows, m_ref, l_ref, acc_ref):
    p, alpha = _softmax_probs(u, shift, rows, m_ref, l_ref)
    _accumulate(p, alpha, v, rows, acc_ref)


def _softmax_init(m_ref, l_ref, acc_ref):
    m_ref[...] = jnp.full(m_ref.shape, NEG, F32)
    l_ref[...] = jnp.zeros_like(l_ref)
    acc_ref[...] = jnp.zeros_like(acc_ref)


def _fox_attn_body(q_ref, k_ref, v_ref, ccol_ref, crow_ref, o_ref, ct_ref, m_ref, l_ref, acc_ref, s_ref, *, tq):
    rs = min(ATTN_ROWS, tq)
    n_groups = tq // rs
    h = pl.program_id(1)
    qi = pl.program_id(2)
    ccol = ccol_ref[0]
    lane = lax.broadcasted_iota(I32, ccol.shape, 1)
    c_t = jnp.sum(jnp.where(lane == h, ccol, 0.0), axis=1, keepdims=True)
    ct_ref[...] = jnp.broadcast_to(c_t, ct_ref.shape)
    _softmax_init(m_ref, l_ref, acc_ref)

    def keys(kb):
        return pl.ds(pl.multiple_of(kb * tq, tq), tq)

    def logits(kb, r):
        rows = pl.ds(r * rs, rs)
        ks = keys(kb)
        s_ref[kb % 2, rows, :] = _dot_nt(q_ref[0, rows, :], k_ref[0, ks, :]) - crow_ref[0, pl.ds(h, 1), ks]

    def attend(kb, r, diagonal, prefetch):
        rows = pl.ds(r * rs, rs)
        u = s_ref[kb % 2, rows, :]
        if diagonal:
            row = lax.broadcasted_iota(I32, (rs, tq), 0) + r * rs
            col = lax.broadcasted_iota(I32, (rs, tq), 1)
            u = jnp.where(col <= row, u, NEG)
        if prefetch:
            logits(kb + 1, r)
        p, alpha = _softmax_probs(u, ct_ref[rows, :], rows, m_ref, l_ref)
        _accumulate(p, alpha, v_ref[0, keys(kb), :], rows, acc_ref)

    for r in range(n_groups):
        logits(0, r)

    def full_block(kb, carry):
        for r in range(n_groups):
            attend(kb, r, False, True)
        return carry

    def block_pair(i, carry):
        return full_block(2 * i + 1, full_block(2 * i, carry))

    lax.fori_loop(0, qi // 2, block_pair, 0)

    @pl.when(qi % 2 == 1)
    def _():
        full_block(qi - 1, 0)

    for r in range(n_groups):
        attend(qi, r, True, False)
    o_ref[0] = (acc_ref[...] / l_ref[...]).astype(o_ref.dtype)


def _fox_attention(qkv, c_col, c_row, tq=512):
    bsz, seq, _ = qkv.shape
    tq = min(tq, seq)
    nh = MIX_HEADS
    return pl.pallas_call(
        functools.partial(_fox_attn_body, tq=tq),
        grid=(bsz, nh, seq // tq),
        in_specs=[
            pl.BlockSpec((1, tq, HEAD_DIM), lambda b, h, i: (b, i, h)),
            pl.BlockSpec((1, seq, HEAD_DIM), lambda b, h, i: (b, 0, nh + h)),
            pl.BlockSpec((1, seq, HEAD_DIM), lambda b, h, i: (b, 0, 2 * nh + h)),
            pl.BlockSpec((1, tq, nh), lambda b, h, i: (b, i, 0)),
            pl.BlockSpec((1, nh, seq), lambda b, h, i: (b, 0, 0)),
        ],
        out_specs=pl.BlockSpec((1, tq, HEAD_DIM), lambda b, h, i: (b, i, h)),
        out_shape=jax.ShapeDtypeStruct((bsz, seq, nh * HEAD_DIM), BF16),
        scratch_shapes=[pltpu.VMEM((tq, HEAD_DIM), F32)] * 4 + [pltpu.VMEM((2, tq, tq), F32)],
        compiler_params=_cparams("parallel", "parallel", "arbitrary"),
        name="fox_attention",
    )(qkv, qkv, qkv, c_col, c_row)


def _t5_bucket_thresholds():
    n = np.arange(0, 4 * REL_MAX_DIST, dtype=np.int64)
    max_exact = REL_BUCKETS // 2
    ratio = np.log(np.maximum(n, 1).astype(np.float32) / np.float32(max_exact)) / np.float32(
        math.log(REL_MAX_DIST / max_exact))
    large = np.minimum(max_exact + (ratio * np.float32(REL_BUCKETS - max_exact)).astype(np.int32),
                       REL_BUCKETS - 1)
    bucket = np.where(n < max_exact, n, large)
    assert np.all(np.diff(bucket) >= 0) and bucket[-1] == REL_BUCKETS - 1
    return [int(np.argmax(bucket >= b)) for b in range(REL_BUCKETS)]


def _moba_body(rb_ref, q_ref, k_ref, v_ref, o_ref, kmean_ref, bias_ref, sel_ref, m_ref, l_ref, acc_ref,
               s_ref, *, n_blocks):
    blk = MOBA_BLOCK
    h = pl.program_id(1)
    qi = pl.program_id(2)
    far_bias = LOG2E * rb_ref[REL_BUCKETS - 1, h]

    @pl.when(qi == 0)
    def _():
        r = lax.broadcasted_iota(I32, (n_blocks, n_blocks * blk), 0)
        c = lax.broadcasted_iota(I32, (n_blocks, n_blocks * blk), 1)
        pool = jnp.where(c // blk == r, 1.0, 0.0).astype(BF16)
        kmean_ref[...] = _dot(pool, k_ref[0]) * (1.0 / blk)
        row = lax.broadcasted_iota(I32, (blk, blk), 0)
        col = lax.broadcasted_iota(I32, (blk, blk), 1)
        thresholds = _t5_bucket_thresholds()
        for slot, dist in ((0, row - col + blk), (1, row - col)):
            val = jnp.full((blk, blk), rb_ref[0, h], F32)
            for b in range(1, REL_BUCKETS):
                val = jnp.where(dist >= thresholds[b], rb_ref[b, h], val)
            val = LOG2E * val
            if slot == 1:
                val = jnp.where(dist >= 0, val, NEG)
            bias_ref[slot] = val

    q = q_ref[0]
    km_hi = kmean_ref[...].astype(BF16)
    km_lo = (kmean_ref[...] - km_hi.astype(F32)).astype(BF16)
    gate = _dot_nt(km_hi, q) + _dot_nt(km_lo, q)
    blk_id = lax.broadcasted_iota(I32, gate.shape, 0)
    gate = jnp.where(blk_id < qi, gate, NEG)
    sel = jnp.zeros(gate.shape, F32)
    for _ in range(MOBA_TOPK):
        best = jnp.max(gate, axis=0, keepdims=True)
        first = jnp.min(jnp.where(gate == best, blk_id, n_blocks), axis=0, keepdims=True)
        pick = jnp.logical_and(blk_id == first, best > 0.5 * NEG)
        sel = jnp.where(pick, 1.0, sel)
        gate = jnp.where(blk_id == first, NEG, gate)
    sel_ref[...] = jnp.concatenate([sel, jnp.zeros((HEAD_DIM - n_blocks, blk), F32)], axis=0).T

    _softmax_init(m_ref, l_ref, acc_ref)
    rs = blk
    n_groups = blk // rs

    def sel_col(j):
        lanes = lax.broadcasted_iota(I32, (blk, HEAD_DIM), 1)
        return jnp.sum(jnp.where(lanes == j, sel_ref[...], 0.0), axis=1, keepdims=True) > 0.5

    def finish():
        o_ref[0] = (acc_ref[...] / l_ref[...]).astype(o_ref.dtype)

    @pl.when(qi == 0)
    def _():
        for r in range(n_groups):
            rows = pl.ds(r * rs, rs)
            u = _dot_nt(q_ref[0, rows, :], k_ref[0, 0:blk, :]) + bias_ref[1, r * rs:(r + 1) * rs, :]
            _softmax_rows(u, None, v_ref[0, 0:blk, :], rows, m_ref, l_ref, acc_ref)
        finish()

    @pl.when(qi >= 1)
    def _():
        n_far = qi - 1
        odd = n_far % 2
        n_far_units = (n_far + odd) // 2

        def first_block(unit):
            return jnp.maximum(2 * unit - odd, 0)

        def keys(unit):
            return pl.ds(pl.multiple_of(first_block(unit) * blk, blk), 2 * blk)

        def logits(unit, r):
            rows = pl.ds(r * rs, rs)
            s_ref[unit % 2, rows, :] = _dot_nt(q_ref[0, rows, :], k_ref[0, keys(unit), :])

        def attend(unit, r, extra_a, extra_b, prefetch):
            rows = pl.ds(r * rs, rs)
            u = s_ref[unit % 2, rows, :]
            u = jnp.concatenate([u[:, :blk] + extra_a, u[:, blk:] + extra_b], axis=1)
            if prefetch:
                logits(unit + 1, r)
            p, alpha = _softmax_probs(u, None, rows, m_ref, l_ref)
            _accumulate(p, alpha, v_ref[0, keys(unit), :], rows, acc_ref)

        for r in range(n_groups):
            logits(0, r)

        def far_unit(unit, carry):
            j = first_block(unit)
            masked_b = jnp.logical_and(unit == 0, odd == 1)
            bias_a = jnp.where(sel_col(j), far_bias, NEG)
            bias_b = jnp.where(jnp.logical_and(sel_col(j + 1), jnp.logical_not(masked_b)), far_bias, NEG)
            for r in range(n_groups):
                attend(unit, r, bias_a[r * rs:(r + 1) * rs], bias_b[r * rs:(r + 1) * rs], True)
            return carry

        def unit_pair(i, carry):
            return far_unit(2 * i + 1, far_unit(2 * i, carry))

        lax.fori_loop(0, n_far_units // 2, unit_pair, 0)

        @pl.when(n_far_units % 2 == 1)
        def _():
            far_unit(n_far_units - 1, 0)

        prev_mask = jnp.where(sel_col(qi - 1), 0.0, NEG)
        for r in range(n_groups):
            rr = slice(r * rs, (r + 1) * rs)
            attend(n_far_units, r, bias_ref[0, rr, :] + prev_mask[rr], bias_ref[1, rr, :], False)
        finish()


def _moba_attention(qkv, rel_bias):
    bsz, seq, _ = qkv.shape
    nh = MIX_HEADS
    blk = MOBA_BLOCK
    assert seq % blk == 0 and seq // blk <= HEAD_DIM
    n_blocks = seq // blk
    grid_spec = pltpu.PrefetchScalarGridSpec(
        num_scalar_prefetch=0,
        grid=(bsz, nh, n_blocks),
        in_specs=[
            pl.BlockSpec(memory_space=pltpu.SMEM),
            pl.BlockSpec((1, blk, HEAD_DIM), lambda b, h, i: (b, i, h)),
            pl.BlockSpec((1, seq, HEAD_DIM), lambda b, h, i: (b, 0, nh + h)),
            pl.BlockSpec((1, seq, HEAD_DIM), lambda b, h, i: (b, 0, 2 * nh + h)),
        ],
        out_specs=pl.BlockSpec((1, blk, HEAD_DIM), lambda b, h, i: (b, i, h)),
        scratch_shapes=[
            pltpu.VMEM((n_blocks, HEAD_DIM), F32),
            pltpu.VMEM((2, blk, blk), F32),
            pltpu.VMEM((blk, HEAD_DIM), F32),
            pltpu.VMEM((blk, HEAD_DIM), F32), pltpu.VMEM((blk, HEAD_DIM), F32), pltpu.VMEM((blk, HEAD_DIM), F32),
            pltpu.VMEM((2, blk, 2 * blk), F32),
        ],
    )
    return pl.pallas_call(
        functools.partial(_moba_body, n_blocks=n_blocks),
        grid_spec=grid_spec,
        out_shape=jax.ShapeDtypeStruct((bsz, seq, nh * HEAD_DIM), BF16),
        compiler_params=_cparams("parallel", "parallel", "arbitrary"),
        name="moba_attention",
    )(rel_bias, qkv, qkv, qkv)


def _mem_attn_body(q_ref, kv_ref, o_ref):
    width = MEM_HEADS * HEAD_DIM
    for hh in range(MEM_HEADS):
        cs = slice(hh * HEAD_DIM, (hh + 1) * HEAD_DIM)
        q = q_ref[0, :, cs].astype(BF16)
        k = kv_ref[0, :, cs]
        v = kv_ref[0, :, width + hh * HEAD_DIM:width + (hh + 1) * HEAD_DIM]
        s = _dot_nt(q, k)
        p = jnp.exp(s - jnp.max(s, axis=-1, keepdims=True))
        o = _dot(p.astype(BF16), v) / jnp.sum(p, axis=-1, keepdims=True)
        o_ref[0, :, cs] = o.astype(o_ref.dtype)


def _memory_attention(proj, q_col_block, kv, tq=512):
    bsz, seq, _ = proj.shape
    tq = min(tq, seq)
    width = MEM_HEADS * HEAD_DIM
    n_mem = kv.shape[1]
    return pl.pallas_call(
        _mem_attn_body,
        grid=(bsz, seq // tq),
        in_specs=[
            pl.BlockSpec((1, tq, width), lambda b, i: (b, i, q_col_block)),
            pl.BlockSpec((1, n_mem, 2 * width), lambda b, i: (b, 0, 0)),
        ],
        out_specs=pl.BlockSpec((1, tq, width), lambda b, i: (b, i, 0)),
        out_shape=jax.ShapeDtypeStruct((bsz, seq, width), BF16),
        compiler_params=_cparams("parallel", "parallel"),
        name="memory_attention",
    )(proj, kv)


def _to_row_tiles(ref, y):
    rows, d = y.shape
    n_tiles = d // HEAD_DIM
    for c in range(n_tiles):
        ref[pl.ds(c, rows, stride=n_tiles), :] = y[:, c * HEAD_DIM:(c + 1) * HEAD_DIM]


def _from_row_tiles(ref, rows, n_tiles, *lead):
    return jnp.concatenate(
        [ref[(*lead, pl.ds(c, rows, stride=n_tiles), slice(None))] for c in range(n_tiles)], axis=1)


def _row_tile(ref, r, n_tiles, *lead):
    return ref.at[(*lead, pl.ds(pl.multiple_of(r * n_tiles, n_tiles), n_tiles))]


def _out_ln_body(mix_ref, mem_ref, x_ref, w1_ref, w2_ref, g_ref, b_ref, o_ref, ort_ref, *, alpha):
    hproj = _dot(mix_ref[...], w1_ref[...]) + _dot(mem_ref[...], w2_ref[...])
    y = _layer_norm(alpha * x_ref[...] + hproj, g_ref[...], b_ref[...])
    o_ref[...] = y
    _to_row_tiles(ort_ref, y)


def _out_proj_ln(mix, mem_out, x, w_out, g, b, alpha, tm=512):
    t, d = x.shape
    tm = min(tm, t)
    wm = mix.shape[1]
    we = mem_out.shape[1]
    return pl.pallas_call(
        functools.partial(_out_ln_body, alpha=alpha),
        grid=(t // tm,),
        in_specs=[
            pl.BlockSpec((tm, wm), lambda i: (i, 0)),
            pl.BlockSpec((tm, we), lambda i: (i, 0)),
            pl.BlockSpec((tm, d), lambda i: (i, 0)),
            pl.BlockSpec((wm, d), lambda i: (0, 0)),
            pl.BlockSpec((we, d), lambda i: (0, 0)),
            pl.BlockSpec((1, d), lambda i: (0, 0)),
            pl.BlockSpec((1, d), lambda i: (0, 0)),
        ],
        out_specs=[pl.BlockSpec((tm, d), lambda i: (i, 0)),
                   pl.BlockSpec((tm * (d // HEAD_DIM), HEAD_DIM), lambda i: (i, 0))],
        out_shape=[jax.ShapeDtypeStruct((t, d), F32),
                   jax.ShapeDtypeStruct((t * (d // HEAD_DIM), HEAD_DIM), F32)],
        compiler_params=_cparams("parallel"),
        name="out_proj_ln",
    )(mix, mem_out, x, w_out[:wm].astype(BF16), w_out[wm:].astype(BF16), g.reshape(1, d), b.reshape(1, d))


def _router_body(x_ref, wt_ref, b_ref, idx_ref, gate_ref, cnt_ref, run_ref, *, tb):
    @pl.when(pl.program_id(0) == 0)
    def _():
        run_ref[...] = jnp.zeros_like(run_ref)

    x = x_ref[...]
    xh = x.astype(BF16)
    xl = (x - xh.astype(F32)).astype(BF16)
    w = wt_ref[...]
    wh = w.astype(BF16)
    wl = (w - wh.astype(F32)).astype(BF16)
    logits = _dot_nt(wh, xh) + _dot_nt(wh, xl) + _dot_nt(wl, xh) + b_ref[...]

    n_e = logits.shape[0]
    eid = lax.broadcasted_iota(I32, logits.shape, 0)
    vals, idxs, hots = [], [], []
    multi = jnp.zeros(logits.shape, F32)
    for _ in range(TOP_K):
        best = jnp.max(logits, axis=0, keepdims=True)
        first = jnp.min(jnp.where(logits == best, eid, n_e), axis=0, keepdims=True)
        hot = eid == first
        vals.append(best)
        idxs.append(first)
        hots.append(hot)
        multi = jnp.where(hot, 1.0, multi)
        logits = jnp.where(hot, -jnp.inf, logits)

    exps = [jnp.exp(v - vals[0]) for v in vals]
    den = exps[0]
    for e in exps[1:]:
        den = den + e

    r = lax.broadcasted_iota(I32, (tb, tb), 0)
    c = lax.broadcasted_iota(I32, (tb, tb), 1)
    before = jnp.where(r < c, 1.0, 0.0).astype(BF16)
    pos = _dot(multi.astype(BF16), before) + run_ref[:, 0:1]
    for kk in range(TOP_K):
        idx_ref[kk:kk + 1, :] = idxs[kk]
        rank = jnp.sum(jnp.where(hots[kk], pos, 0.0), axis=0, keepdims=True)
        idx_ref[TOP_K + kk:TOP_K + kk + 1, :] = rank.astype(I32)
        gate_ref[kk:kk + 1, :] = exps[kk] / den
    gate_ref[TOP_K:2 * TOP_K, :] = jnp.zeros((TOP_K, tb), F32)
    run_ref[...] = run_ref[...] + jnp.sum(multi, axis=1, keepdims=True)
    cnt_ref[...] = run_ref[...]


def _router(x, router_w, router_b, tb=512):
    t, d = x.shape
    tb = min(tb, t)
    n_e = router_w.shape[1]
    return pl.pallas_call(
        functools.partial(_router_body, tb=tb),
        grid=(t // tb,),
        in_specs=[
            pl.BlockSpec((tb, d), lambda i: (i, 0)),
            pl.BlockSpec((n_e, d), lambda i: (0, 0)),
            pl.BlockSpec((n_e, 1), lambda i: (0, 0)),
        ],
        out_specs=[
            pl.BlockSpec((2 * TOP_K, tb), lambda i: (0, i)),
            pl.BlockSpec((2 * TOP_K, tb), lambda i: (0, i)),
            pl.BlockSpec((n_e, HEAD_DIM), lambda i: (0, 0)),
        ],
        out_shape=[
            jax.ShapeDtypeStruct((2 * TOP_K, t), I32),
            jax.ShapeDtypeStruct((2 * TOP_K, t), F32),
            jax.ShapeDtypeStruct((n_e, HEAD_DIM), F32),
        ],
        scratch_shapes=[pltpu.VMEM((n_e, HEAD_DIM), F32)],
        compiler_params=_cparams("arbitrary"),
        name="moe_router",
    )(x, router_w.T, router_b.reshape(n_e, 1))


def _dispatch_body(dest_ref, pad_lo_ref, pad_hi_ref, nu_ref, x_ref, xs_hbm, zero_ref, sem, zsem,
                   *, tb, n_tok, n_tiles, n_experts, n_blocks):
    base = pl.program_id(0) * tb

    def issue(t, carry):
        for kk in range(TOP_K):
            d = dest_ref[kk * n_tok + base + t]
            pltpu.make_async_copy(_row_tile(x_ref, t, n_tiles), _row_tile(xs_hbm, d, n_tiles), sem).start(
                priority=kk % 2)
        return carry

    lax.fori_loop(0, tb, issue, 0, unroll=DMA_UNROLL)

    @pl.when(pl.program_id(0) == 0)
    def _():
        zero_ref[...] = jnp.zeros_like(zero_ref)

        def pad_row(r):
            return pltpu.make_async_copy(zero_ref.at[pl.ds(0, n_tiles)], _row_tile(xs_hbm, r, n_tiles), zsem)

        def spare_block(blk):
            rows = pl.ds(pl.multiple_of(blk * MOE_BLOCK * n_tiles, MOE_BLOCK * n_tiles), MOE_BLOCK * n_tiles)
            return pltpu.make_async_copy(zero_ref, xs_hbm.at[rows], zsem)

        def for_all(action):
            for e in range(n_experts):
                lax.fori_loop(pad_lo_ref[e], pad_hi_ref[e], lambda r, c: (action(pad_row(r)), c)[1], 0)
            lax.fori_loop(nu_ref[0], n_blocks, lambda blk, c: (action(spare_block(blk)), c)[1], 0)

        for_all(lambda cp: cp.start())
        for_all(lambda cp: cp.wait())

    for kk in range(TOP_K):
        pltpu.make_async_copy(x_ref, xs_hbm.at[pl.ds(0, tb * n_tiles)], sem).wait()


def _dispatch(x, n_tiles, dest_flat, pad_lo, pad_hi, n_used, n_rows, tb=512):
    t = x.shape[0] // n_tiles
    tb = min(tb, t)
    grid_spec = pltpu.PrefetchScalarGridSpec(
        num_scalar_prefetch=4,
        grid=(t // tb,),
        in_specs=[pl.BlockSpec((tb * n_tiles, HEAD_DIM), lambda i, *_: (i, 0))],
        out_specs=pl.BlockSpec(memory_space=pl.ANY),
        scratch_shapes=[pltpu.VMEM((MOE_BLOCK * n_tiles, HEAD_DIM), x.dtype),
                        pltpu.SemaphoreType.DMA(()), pltpu.SemaphoreType.DMA(())],
    )
    return pl.pallas_call(
        functools.partial(_dispatch_body, tb=tb, n_tok=t, n_tiles=n_tiles, n_experts=pad_lo.shape[0],
                          n_blocks=n_rows // MOE_BLOCK),
        grid_spec=grid_spec,
        out_shape=jax.ShapeDtypeStruct((n_rows * n_tiles, HEAD_DIM), x.dtype),
        compiler_params=_cparams("arbitrary"),
        name="moe_dispatch",
    )(dest_flat, pad_lo, pad_hi, n_used, x)


UNZIP = 256


def _expert_body(be_ref, nu_ref, first_ref, slot_ref, next_ref, x_ref, wu_hbm, bg_ref, bl_ref, wd_hbm, bd_ref,
                 o_ref, wu_buf, wd_buf, wg_s, wl_s, wd_s, sem, *, layer, n_tiles):
    i = pl.program_id(0)
    n_used = nu_ref[0]

    def weight_copies(expert, slot):
        return (pltpu.make_async_copy(wu_hbm.at[layer, expert], wu_buf.at[slot], sem.at[0, slot]),
                pltpu.make_async_copy(wd_hbm.at[layer, expert], wd_buf.at[slot], sem.at[1, slot]))

    @pl.when(i == 0)
    def _():
        for cp in weight_copies(be_ref[0], 0):
            cp.start()

    @pl.when(jnp.logical_and(first_ref[i] == 1, i < n_used))
    def _():
        slot = slot_ref[i]
        for cp in weight_copies(be_ref[i], slot):
            cp.wait()

        @pl.when(next_ref[i] >= 0)
        def _():
            for cp in weight_copies(next_ref[i], 1 - slot):
                cp.start()

        half = UNZIP // 2
        r = lax.broadcasted_iota(I32, (UNZIP, UNZIP), 0)
        c = lax.broadcasted_iota(I32, (UNZIP, UNZIP), 1)
        src = jnp.where(c < half, 2 * c, 2 * (c - half) + 1)
        perm = jnp.where(r == src, 1.0, 0.0).astype(BF16)
        for g in range(wu_buf.shape[2] // UNZIP):
            w = wu_buf[slot, :, g * UNZIP:(g + 1) * UNZIP].astype(BF16)
            sep = _dot(w, perm).astype(BF16)
            wg_s[:, g * half:(g + 1) * half] = sep[:, :half]
            wl_s[:, g * half:(g + 1) * half] = sep[:, half:]
        wd_s[...] = wd_buf[slot].astype(BF16)

    @pl.when(i < n_used)
    def _():
        xb = _from_row_tiles(x_ref, MOE_BLOCK, n_tiles).astype(BF16)
        glu = jnp.minimum(_dot(xb, wg_s[...]) + bg_ref[0], SWIGLU_LIMIT)
        lin = jnp.clip(_dot(xb, wl_s[...]) + bl_ref[0], -SWIGLU_LIMIT, SWIGLU_LIMIT)
        act = glu * _sigmoid(SWIGLU_ALPHA * glu) * (lin + 1.0)
        _to_row_tiles(o_ref, _dot(act.astype(BF16), wd_s[...]) + bd_ref[0])

    @pl.when(i >= n_used)
    def _():
        o_ref[...] = jnp.zeros_like(o_ref)


def _expert_mlp(xs, n_tiles, blk_expert, n_used, counts, layer, w_up, bg, bl, w_down, bd):
    p_rows = xs.shape[0] // n_tiles
    d = n_tiles * HEAD_DIM
    f = w_down.shape[2]
    n_e = w_down.shape[1]
    n_blk = p_rows // MOE_BLOCK

    blocks = jnp.arange(n_blk, dtype=I32)
    valid = blocks < n_used[0]
    prev_e = jnp.concatenate([jnp.full((1,), -1, I32), blk_expert[:-1]])
    first = jnp.logical_and(valid, blk_expert != prev_e)
    slot = (jnp.cumsum(first.astype(I32)) - 1) % 2
    experts = jnp.arange(n_e, dtype=I32)
    later = jnp.logical_and(experts[None, :] > blk_expert[:, None], counts[None, :] > 0)
    nxt = jnp.min(jnp.where(later, experts[None, :], n_e), axis=1)
    nxt = jnp.where(nxt == n_e, -1, nxt)

    def blk(i, nu):
        return jnp.minimum(i, nu[0] - 1)

    def w_spec(shape):
        return pl.BlockSpec((1,) + shape, lambda i, be, nu, fi, sl, nx: (be[blk(i, nu)], 0, 0))

    grid_spec = pltpu.PrefetchScalarGridSpec(
        num_scalar_prefetch=5,
        grid=(n_blk,),
        in_specs=[
            pl.BlockSpec((MOE_BLOCK * n_tiles, HEAD_DIM), lambda i, be, nu, fi, sl, nx: (blk(i, nu), 0)),
            pl.BlockSpec(memory_space=pl.ANY), w_spec((1, f)), w_spec((1, f)),
            pl.BlockSpec(memory_space=pl.ANY), w_spec((1, d)),
        ],
        out_specs=pl.BlockSpec((MOE_BLOCK * n_tiles, HEAD_DIM), lambda i, be, nu, fi, sl, nx: (i, 0)),
        scratch_shapes=[
            pltpu.VMEM((2, d, 2 * f), F32), pltpu.VMEM((2, f, d), F32),
            pltpu.VMEM((d, f), BF16), pltpu.VMEM((d, f), BF16), pltpu.VMEM((f, d), BF16),
            pltpu.SemaphoreType.DMA((2, 2)),
        ],
    )
    return pl.pallas_call(
        functools.partial(_expert_body, layer=layer, n_tiles=n_tiles),
        grid_spec=grid_spec,
        out_shape=jax.ShapeDtypeStruct(xs.shape, F32),
        compiler_params=_cparams("arbitrary"),
        name="moe_experts",
    )(blk_expert, n_used, first.astype(I32), slot.astype(I32), nxt.astype(I32), xs, w_up, bg, bl, w_down, bd)


def _combine_body(dest_ref, ys_hbm, gate_ref, x_ref, g_ref, b_ref, o_ref, buf_ref, sem_ref,
                  *, tb, n_tok, n_tiles, alpha):
    i = pl.program_id(0)
    n_steps = pl.num_programs(0)

    def issue(step, slot):
        base = step * tb

        def one(t, carry):
            for kk in range(TOP_K):
                d = dest_ref[kk * n_tok + base + t]
                pltpu.make_async_copy(_row_tile(ys_hbm, d, n_tiles), _row_tile(buf_ref, t, n_tiles, slot, kk),
                                      sem_ref.at[slot]).start(priority=kk % 2)
            return carry

        lax.fori_loop(0, tb, one, 0, unroll=DMA_UNROLL)

    @pl.when(i == 0)
    def _():
        issue(0, 0)

    slot = lax.rem(i, 2)

    @pl.when(i + 1 < n_steps)
    def _():
        issue(i + 1, 1 - slot)

    for kk in range(TOP_K):
        pltpu.make_async_copy(ys_hbm.at[pl.ds(0, tb * n_tiles)], buf_ref.at[slot, kk], sem_ref.at[slot]).wait()

    g8 = gate_ref[...]
    gpad = jnp.concatenate([g8, jnp.zeros((HEAD_DIM - g8.shape[0], tb), F32)], axis=0)
    gt = gpad.T
    f = gt[:, 0:1] * _from_row_tiles(buf_ref, tb, n_tiles, slot, 0)
    for kk in range(1, TOP_K):
        f = f + gt[:, kk:kk + 1] * _from_row_tiles(buf_ref, tb, n_tiles, slot, kk)
    o_ref[...] = _layer_norm(alpha * x_ref[...] + f, g_ref[...], b_ref[...])


def _combine_ln(ys, dest_flat, gates, x, g, b, alpha, tb=256):
    t, d = x.shape
    tb = min(tb, t)
    n_tiles = d // HEAD_DIM
    grid_spec = pltpu.PrefetchScalarGridSpec(
        num_scalar_prefetch=1,
        grid=(t // tb,),
        in_specs=[
            pl.BlockSpec(memory_space=pl.ANY),
            pl.BlockSpec((2 * TOP_K, tb), lambda i, dest: (0, i)),
            pl.BlockSpec((tb, d), lambda i, dest: (i, 0)),
            pl.BlockSpec((1, d), lambda i, dest: (0, 0)),
            pl.BlockSpec((1, d), lambda i, dest: (0, 0)),
        ],
        out_specs=pl.BlockSpec((tb, d), lambda i, dest: (i, 0)),
        scratch_shapes=[pltpu.VMEM((2, TOP_K, tb * n_tiles, HEAD_DIM), F32), pltpu.SemaphoreType.DMA((2,))],
    )
    return pl.pallas_call(
        functools.partial(_combine_body, tb=tb, n_tok=t, n_tiles=n_tiles, alpha=alpha),
        grid_spec=grid_spec,
        out_shape=jax.ShapeDtypeStruct((t, d), F32),
        compiler_params=_cparams("arbitrary"),
        name="moe_combine_ln",
    )(dest_flat, ys, gates, x, g.reshape(1, d), b.reshape(1, d))


def _moe_ln(x, x_tiles, router_w, router_b, layer, w_up, b_up, w_down, b_down, g, b, alpha):
    t, d = x.shape
    n_tiles = d // HEAD_DIM
    n_e = router_w.shape[1]
    idx_rank, gates, cnt = _router(x, router_w, router_b)
    counts = cnt[:, 0].astype(I32)
    padded = ((counts + MOE_BLOCK - 1) // MOE_BLOCK) * MOE_BLOCK
    p_ends = jnp.cumsum(padded)
    p_starts = p_ends - padded
    n_blk = (t * TOP_K + MOE_BLOCK - 1) // MOE_BLOCK + n_e
    experts = jnp.arange(n_e, dtype=I32)
    start_of = jnp.sum(jnp.where(idx_rank[:TOP_K, :, None] == experts, p_starts, 0), axis=-1)
    dest_flat = (start_of + idx_rank[TOP_K:]).reshape(-1)
    blk_first_row = jnp.arange(n_blk, dtype=I32) * MOE_BLOCK
    blk_expert = jnp.clip(jnp.sum((p_ends[None, :] <= blk_first_row[:, None]).astype(I32), axis=1), 0, n_e - 1)
    n_used = (p_ends[-1:] // MOE_BLOCK).astype(I32)

    xs = _dispatch(x_tiles, n_tiles, dest_flat, p_starts + counts, p_ends, n_used, n_blk * MOE_BLOCK)
    bg = b_up[:, None, 0::2]
    bl = b_up[:, None, 1::2]
    ys = _expert_mlp(xs, n_tiles, blk_expert, n_used, counts, layer, w_up, bg, bl, w_down, b_down[:, None, :])
    return _combine_ln(ys, dest_flat, gates, x, g, b, alpha)


def kernel(x, mem, hgrn_w_in, hgrn_lb, hgrn_norm_g, fox_w_in, fox_b_f, moba_w_in, rel_bias, w_mem_kv, w_out,
           ln1_g, ln1_b, router_w, router_b, w_up, b_up, w_down, b_down, ln2_g, ln2_b):
    bsz, seq, d = x.shape
    depth = w_out.shape[0]
    t = bsz * seq
    mix_w = MIX_HEADS * HEAD_DIM
    mem_w = MEM_HEADS * HEAD_DIM
    alpha = (2 * depth) ** 0.25
    scale = HEAD_DIM ** -0.5
    n_mem = mem.shape[1]
    mem2 = mem.reshape(bsz * n_mem, d)

    def scale_vec(n_mixer_q, n_plain, n_mem_q):
        return jnp.concatenate([
            jnp.full((1, n_mixer_q), scale * LOG2E, F32), jnp.ones((1, n_plain), F32),
            jnp.full((1, n_mem_q), scale, F32)], axis=1)

    x2 = x.reshape(t, d)
    for i in range(depth):
        kind, j = i % 3, i // 3
        if kind == 0:
            w_in = hgrn_w_in[j].astype(BF16)
            proj = _matmul(x2, w_in, scale_vec(0, 4 * mix_w, mem_w), F32, 1024, w_in.shape[1] // 3)
            proj = proj.reshape(bsz, seq, -1)
            mix = _hgrn_mixer(proj, hgrn_lb, hgrn_norm_g[j], j)
            memq_block = 4 * mix_w // mem_w
        elif kind == 1:
            w = fox_w_in[j]
            w_in = jnp.concatenate([w[:, :3 * mix_w], w[:, 3 * mix_w + MIX_HEADS:]], axis=1).astype(BF16)
            proj = _matmul(x2, w_in, scale_vec(mix_w, 2 * mix_w, mem_w), BF16, 1024, w_in.shape[1] // 2)
            proj = proj.reshape(bsz, seq, -1)
            c_col, c_row = _fox_gate_cumsum(x2.reshape(bsz, seq, d), w[:, 3 * mix_w:3 * mix_w + MIX_HEADS],
                                            fox_b_f[j])
            mix = _fox_attention(proj, c_col, c_row)
            memq_block = 3 * mix_w // mem_w
        else:
            w_in = moba_w_in[j].astype(BF16)
            proj = _matmul(x2, w_in, scale_vec(mix_w, 2 * mix_w, mem_w), BF16, 1024, w_in.shape[1] // 2)
            proj = proj.reshape(bsz, seq, -1)
            mix = _moba_attention(proj, rel_bias)
            memq_block = 3 * mix_w // mem_w
        kv = _matmul(mem2, w_mem_kv[i].astype(BF16), jnp.ones((1, 2 * mem_w), F32), BF16, 512, 512)
        mem_out = _memory_attention(proj, memq_block, kv.reshape(bsz, n_mem, 2 * mem_w))
        x2, x2_tiles = _out_proj_ln(mix.reshape(t, mix_w), mem_out.reshape(t, mem_w), x2, w_out[i],
                                    ln1_g[i], ln1_b[i], alpha)
        x2 = _moe_ln(x2, x2_tiles, router_w[i], router_b[i], i, w_up, b_up[i], w_down, b_down[i],
                     ln2_g[i], ln2_b[i], alpha)
    return x2.reshape(bsz, seq, d)
```

```python
import functools
import math

import numpy as np
import jax
import jax.numpy as jnp
from jax import lax
from jax.experimental import pallas as pl
from jax.experimental.pallas import tpu as pltpu

F32 = jnp.float32
BF16 = jnp.bfloat16
I32 = jnp.int32

HEAD_DIM = 128
MIX_HEADS = 8
MEM_HEADS = 4
HGRN_CHUNK = 64
HGRN_SUB = 16
HGRN_EXP_CLAMP = 80.0
MOBA_BLOCK = 256
MOBA_TOPK = 3
REL_BUCKETS = 32
REL_MAX_DIST = 128
N_EXPERTS = 32
TOP_K = 4
MOE_BLOCK = 256
SWIGLU_LIMIT = 7.0
SWIGLU_ALPHA = 1.702
LN_EPS = 1e-5
RMS_EPS = 1e-6
NEG = -1e30
LOG2E = 1.4426950408889634
VMEM_LIMIT = 56 * 1024 * 1024
DMA_UNROLL = 8

NT_DIMS = (((1,), (1,)), ((), ()))
TN_DIMS = (((0,), (0,)), ((), ()))


def _cparams(*sem):
    return pltpu.CompilerParams(dimension_semantics=sem, vmem_limit_bytes=VMEM_LIMIT)


def _dot(a, b):
    return jnp.dot(a, b, preferred_element_type=F32)


def _dot_nt(a, b):
    return lax.dot_general(a, b, NT_DIMS, preferred_element_type=F32)


def _dot_tn(a, b):
    return lax.dot_general(a, b, TN_DIMS, preferred_element_type=F32)


def _split3(a):
    hi = a.astype(BF16)
    r = a - hi.astype(F32)
    mid = r.astype(BF16)
    lo = (r - mid.astype(F32)).astype(BF16)
    return hi, mid, lo


def _sigmoid(x):
    return 1.0 / (1.0 + jnp.exp(-x))


def _layer_norm(z, g, b):
    mu = jnp.mean(z, axis=-1, keepdims=True)
    zc = z - mu
    var = jnp.mean(zc * zc, axis=-1, keepdims=True)
    return zc * lax.rsqrt(var + LN_EPS) * g + b


def _mm_body(x_ref, w_ref, s_ref, o_ref, xb_ref):
    @pl.when(pl.program_id(1) == 0)
    def _():
        xb_ref[...] = x_ref[...].astype(BF16)

    acc = _dot(xb_ref[...], w_ref[...])
    o_ref[...] = (acc * s_ref[...]).astype(o_ref.dtype)


def _matmul(x, w, col_scale, out_dtype, tm, tn):
    m, k = x.shape
    n = w.shape[1]
    tm = min(tm, m)
    tn = min(tn, n)
    return pl.pallas_call(
        _mm_body,
        grid=(m // tm, n // tn),
        in_specs=[
            pl.BlockSpec((tm, k), lambda i, j: (i, 0)),
            pl.BlockSpec((k, tn), lambda i, j: (0, j)),
            pl.BlockSpec((1, tn), lambda i, j: (0, j)),
        ],
        out_specs=pl.BlockSpec((tm, tn), lambda i, j: (i, j)),
        out_shape=jax.ShapeDtypeStruct((m, n), out_dtype),
        scratch_shapes=[pltpu.VMEM((tm, k), BF16)],
        compiler_params=_cparams("parallel", "arbitrary"),
        name="proj_matmul",
    )(x, w, col_scale)


def _hgrn_body(q_ref, z_ref, i_ref, g_ref, lbp_ref, ng_ref, o_ref, st_ref, *, layer_j, tb):
    c_len, sb = HGRN_CHUNK, HGRN_SUB
    n_sub = c_len // sb

    @pl.when(pl.program_id(2) == 0)
    def _():
        st_ref[...] = jnp.zeros_like(st_ref)

    lbp = lbp_ref[...]
    ex = jnp.exp(lbp - jnp.max(lbp, axis=0, keepdims=True))
    p = ex / jnp.sum(ex, axis=0, keepdims=True)
    lb = jnp.zeros((1, HEAD_DIM), F32)
    for r in range(1, layer_j + 1):
        lb = lb + p[r:r + 1, :]
    log_lb = jnp.log(lb)
    log1m_lb = jnp.log1p(-lb)
    one_m_lb = 1.0 - lb
    ng = ng_ref[...]

    row = lax.broadcasted_iota(I32, (c_len, c_len), 0)
    col = lax.broadcasted_iota(I32, (c_len, c_len), 1)
    sub_start = (row // sb) * sb
    one = jnp.ones((c_len, c_len), F32)
    zero = jnp.zeros((c_len, c_len), F32)
    m_tri = jnp.where(col <= row, one, zero)
    m_in = jnp.where(col > sub_start, m_tri, zero)
    m_dec = jnp.where(col > row, one, zero)
    m_ke = jnp.where(col <= sub_start + (sb - 1), m_dec, zero)
    stack = jnp.concatenate([m_tri, m_in, m_ke, m_dec], axis=0).astype(BF16)
    same_sub = (row // sb) == (col // sb)
    diag_mask = jnp.logical_and(same_sub, col <= row)
    row_sub = row // sb
    krow_sub = lax.broadcasted_iota(I32, (c_len, HEAD_DIM), 0) // sb

    chunks = [pl.ds(c * c_len, c_len) for c in range(tb // c_len)]

    gates = []
    for sl in chunks:
        z = z_ref[0, sl, :]
        e = jnp.exp(-jnp.abs(z))
        inv = 1.0 / (1.0 + e)
        sig_neg = jnp.where(z >= 0, e, 1.0) * inv
        log_sig = jnp.minimum(z, 0.0) - jnp.log(1.0 + e)
        a2 = log1m_lb + log_sig
        log_f = jnp.maximum(log_lb, a2) + jnp.log(1.0 + jnp.exp(-jnp.abs(log_lb - a2)))
        qv = q_ref[0, sl, :]
        gates.append((log_f, one_m_lb * sig_neg, qv * _sigmoid(qv), i_ref[0, sl, :].astype(BF16)))

    prefix = []
    for log_f, _, _, _ in gates:
        pre = None
        for piece in _split3(log_f):
            t = _dot(stack, piece)
            pre = t if pre is None else pre + t
        prefix.append(pre)

    factors = []
    for (_, k, qf, _), pre in zip(gates, prefix):
        b = pre[0:c_len]
        d_in = pre[c_len:2 * c_len]
        d_ke = pre[2 * c_len:3 * c_len]
        d_dec = pre[3 * c_len:4 * c_len]
        q_in = qf * jnp.exp(d_in)
        k_diag = (k * jnp.exp(jnp.minimum(-d_in, HGRN_EXP_CLAMP))).astype(BF16)
        k_end = k * jnp.exp(d_ke)
        k_dec = (k * jnp.exp(d_dec)).astype(BF16)
        q_all = (qf * jnp.exp(b)).astype(BF16)
        b_ref = b - d_in
        lhs = [q_in.astype(BF16)]
        rhs = [k_diag]
        for j in range(n_sub - 1):
            b_end = b[(j + 1) * sb - 1:(j + 1) * sb, :]
            cross = jnp.exp(jnp.minimum(b_ref - b_end, 0.0))
            lhs.append((q_in * cross).astype(BF16))
            rhs.append(jnp.where(krow_sub == j, k_end, 0.0).astype(BF16))
        factors.append((lhs, rhs, k_dec, q_all, jnp.exp(b[c_len - 1:c_len, :])))

    scores = []
    for lhs, rhs, _, _, _ in factors:
        a = jnp.where(diag_mask, _dot_nt(lhs[0], rhs[0]), 0.0)
        for j in range(n_sub - 1):
            a = a + jnp.where(row_sub > j, _dot_nt(lhs[j + 1], rhs[j + 1]), 0.0)
        scores.append(a.astype(BF16))

    updates = [_dot_tn(vb, k_dec) for (_, _, _, vb), (_, _, k_dec, _, _) in zip(gates, factors)]
    states = [st_ref[...]]
    for (_, _, _, _, decay), kv in zip(factors, updates):
        states.append(states[-1] * decay + kv)
    st_ref[...] = states[-1]

    for sl, (_, _, _, vb), (_, _, _, q_all, _), a, st in zip(chunks, gates, factors, scores, states):
        o = _dot(a, vb) + _dot_nt(q_all, st.astype(BF16))
        o = o * lax.rsqrt(jnp.mean(o * o, axis=-1, keepdims=True) + RMS_EPS)
        gv = g_ref[0, sl, :]
        o_ref[0, sl, :] = (o * ng * (gv * _sigmoid(gv))).astype(o_ref.dtype)


def _hgrn_mixer(proj, lb_param, norm_g, layer_j, tb=512):
    bsz, seq, _ = proj.shape
    tb = min(tb, seq)
    n_layers = lb_param.shape[0]

    def col_block(off):
        return pl.BlockSpec((1, tb, HEAD_DIM), lambda b, h, t: (b, t, off + h))

    return pl.pallas_call(
        functools.partial(_hgrn_body, layer_j=layer_j, tb=tb),
        grid=(bsz, MIX_HEADS, seq // tb),
        in_specs=[
            col_block(0), col_block(MIX_HEADS), col_block(2 * MIX_HEADS), col_block(3 * MIX_HEADS),
            pl.BlockSpec((n_layers, HEAD_DIM), lambda b, h, t: (0, h)),
            pl.BlockSpec((1, HEAD_DIM), lambda b, h, t: (0, h)),
        ],
        out_specs=pl.BlockSpec((1, tb, HEAD_DIM), lambda b, h, t: (b, t, h)),
        out_shape=jax.ShapeDtypeStruct((bsz, seq, MIX_HEADS * HEAD_DIM), BF16),
        scratch_shapes=[pltpu.VMEM((HEAD_DIM, HEAD_DIM), F32)],
        compiler_params=_cparams("parallel", "parallel", "arbitrary"),
        name="hgrn2_mixer",
    )(proj, proj, proj, proj, lb_param, norm_g.reshape(1, -1))


def _log_sigmoid(z):
    return jnp.minimum(z, 0.0) - jnp.log1p(jnp.exp(-jnp.abs(z)))


def _fox_gate_body(x_ref, w_ref, wt_ref, bc_ref, br_ref, ccol_ref, crow_ref, carc_ref, carr_ref, *, tb):
    @pl.when(pl.program_id(1) == 0)
    def _():
        carc_ref[...] = jnp.zeros_like(carc_ref)
        carr_ref[...] = jnp.zeros_like(carr_ref)

    xb = x_ref[0].astype(BF16)
    ls_col = LOG2E * _log_sigmoid(_dot(xb, w_ref[...]) + bc_ref[...])
    ls_row = LOG2E * _log_sigmoid(_dot_nt(wt_ref[...], xb) + br_ref[...])
    row = lax.broadcasted_iota(I32, (tb, tb), 0)
    col = lax.broadcasted_iota(I32, (tb, tb), 1)
    lower = jnp.where(col <= row, 1.0, 0.0).astype(BF16)
    upper = jnp.where(row <= col, 1.0, 0.0).astype(BF16)
    c_col = carc_ref[...]
    for piece in _split3(ls_col):
        c_col = c_col + _dot(lower, piece)
    c_row = carr_ref[...]
    for piece in _split3(ls_row):
        c_row = c_row + _dot(piece, upper)
    ccol_ref[0] = c_col
    crow_ref[0] = c_row
    carc_ref[...] = c_col[tb - 1:tb, :]
    carr_ref[...] = c_row[:, tb - 1:tb]


def _fox_gate_cumsum(x, w_f, b_f, tb=512):
    bsz, seq, d = x.shape
    h = w_f.shape[1]
    tb = min(tb, seq)
    return pl.pallas_call(
        functools.partial(_fox_gate_body, tb=tb),
        grid=(bsz, seq // tb),
        in_specs=[
            pl.BlockSpec((1, tb, d), lambda b, t: (b, t, 0)),
            pl.BlockSpec((d, h), lambda b, t: (0, 0)),
            pl.BlockSpec((h, d), lambda b, t: (0, 0)),
            pl.BlockSpec((1, h), lambda b, t: (0, 0)),
            pl.BlockSpec((h, 1), lambda b, t: (0, 0)),
        ],
        out_specs=[
            pl.BlockSpec((1, tb, h), lambda b, t: (b, t, 0)),
            pl.BlockSpec((1, h, tb), lambda b, t: (b, 0, t)),
        ],
        out_shape=[
            jax.ShapeDtypeStruct((bsz, seq, h), F32),
            jax.ShapeDtypeStruct((bsz, h, seq), F32),
        ],
        scratch_shapes=[pltpu.VMEM((1, h), F32), pltpu.VMEM((h, 1), F32)],
        compiler_params=_cparams("parallel", "arbitrary"),
        name="fox_gate_cumsum",
    )(x, w_f.astype(BF16), w_f.T.astype(BF16), b_f.reshape(1, h), b_f.reshape(h, 1))


ATTN_ROWS = 128


def _softmax_probs(u, shift, rows, m_ref, l_ref):
    reps = u.shape[1] // HEAD_DIM
    m_old = m_ref[rows, :]
    if shift is None:
        m_new = jnp.maximum(m_old, jnp.max(u, axis=-1, keepdims=True))
        p = jnp.exp2(u - jnp.tile(m_new, (1, reps)))
    else:
        m_new = jnp.maximum(m_old, jnp.max(u, axis=-1, keepdims=True) + shift)
        p = jnp.exp2(u - jnp.tile(m_new - shift, (1, reps)))
    alpha = jnp.exp2(m_old - m_new)
    l_ref[rows, :] = alpha * l_ref[rows, :] + jnp.sum(p, axis=-1, keepdims=True)
    m_ref[rows, :] = m_new
    return p.astype(BF16), alpha


def _accumulate(p, alpha, v, rows, acc_ref):
    acc_ref[rows, :] = alpha * acc_ref[rows, :] + _dot(p, v)


def _softmax_rows(u, shift, v, rows, m_ref, l_ref, acc_ref):
    p, alpha = _softmax_probs(u, shift, rows, m_ref, l_ref)
    _accumulate(p, alpha, v, rows, acc_ref)


def _softmax_init(m_ref, l_ref, acc_ref):
    m_ref[...] = jnp.full(m_ref.shape, NEG, F32)
    l_ref[...] = jnp.zeros_like(l_ref)
    acc_ref[...] = jnp.zeros_like(acc_ref)


def _fox_attn_body(q_ref, k_ref, v_ref, ccol_ref, crow_ref, o_ref, ct_ref, m_ref, l_ref, acc_ref, s_ref, *, tq):
    rs = min(ATTN_ROWS, tq)
    n_groups = tq // rs
    h = pl.program_id(1)
    qi = pl.program_id(2)
    ccol = ccol_ref[0]
    lane = lax.broadcasted_iota(I32, ccol.shape, 1)
    c_t = jnp.sum(jnp.where(lane == h, ccol, 0.0), axis=1, keepdims=True)
    ct_ref[...] = jnp.broadcast_to(c_t, ct_ref.shape)
    _softmax_init(m_ref, l_ref, acc_ref)

    def keys(kb):
        return pl.ds(pl.multiple_of(kb * tq, tq), tq)

    def logits(kb, r):
        rows = pl.ds(r * rs, rs)
        ks = keys(kb)
        s_ref[kb % 2, rows, :] = _dot_nt(q_ref[0, rows, :], k_ref[0, ks, :]) - crow_ref[0, pl.ds(h, 1), ks]

    def attend(kb, r, diagonal, prefetch):
        rows = pl.ds(r * rs, rs)
        u = s_ref[kb % 2, rows, :]
        if diagonal:
            row = lax.broadcasted_iota(I32, (rs, tq), 0) + r * rs
            col = lax.broadcasted_iota(I32, (rs, tq), 1)
            u = jnp.where(col <= row, u, NEG)
        if prefetch:
            logits(kb + 1, r)
        p, alpha = _softmax_probs(u, ct_ref[rows, :], rows, m_ref, l_ref)
        _accumulate(p, alpha, v_ref[0, keys(kb), :], rows, acc_ref)

    for r in range(n_groups):
        logits(0, r)

    def full_block(kb, carry):
        for r in range(n_groups):
            attend(kb, r, False, True)
        return carry

    def block_pair(i, carry):
        return full_block(2 * i + 1, full_block(2 * i, carry))

    lax.fori_loop(0, qi // 2, block_pair, 0)

    @pl.when(qi % 2 == 1)
    def _():
        full_block(qi - 1, 0)

    for r in range(n_groups):
        attend(qi, r, True, False)
    o_ref[0] = (acc_ref[...] / l_ref[...]).astype(o_ref.dtype)


def _fox_attention(qkv, c_col, c_row, tq=512):
    bsz, seq, _ = qkv.shape
    tq = min(tq, seq)
    nh = MIX_HEADS
    return pl.pallas_call(
        functools.partial(_fox_attn_body, tq=tq),
        grid=(bsz, nh, seq // tq),
        in_specs=[
            pl.BlockSpec((1, tq, HEAD_DIM), lambda b, h, i: (b, i, h)),
            pl.BlockSpec((1, seq, HEAD_DIM), lambda b, h, i: (b, 0, nh + h)),
            pl.BlockSpec((1, seq, HEAD_DIM), lambda b, h, i: (b, 0, 2 * nh + h)),
            pl.BlockSpec((1, tq, nh), lambda b, h, i: (b, i, 0)),
            pl.BlockSpec((1, nh, seq), lambda b, h, i: (b, 0, 0)),
        ],
        out_specs=pl.BlockSpec((1, tq, HEAD_DIM), lambda b, h, i: (b, i, h)),
        out_shape=jax.ShapeDtypeStruct((bsz, seq, nh * HEAD_DIM), BF16),
        scratch_shapes=[pltpu.VMEM((tq, HEAD_DIM), F32)] * 4 + [pltpu.VMEM((2, tq, tq), F32)],
        compiler_params=_cparams("parallel", "parallel", "arbitrary"),
        name="fox_attention",
    )(qkv, qkv, qkv, c_col, c_row)


def _t5_bucket_thresholds():
    n = np.arange(0, 4 * REL_MAX_DIST, dtype=np.int64)
    max_exact = REL_BUCKETS // 2
    ratio = np.log(np.maximum(n, 1).astype(np.float32) / np.float32(max_exact)) / np.float32(
        math.log(REL_MAX_DIST / max_exact))
    large = np.minimum(max_exact + (ratio * np.float32(REL_BUCKETS - max_exact)).astype(np.int32),
                       REL_BUCKETS - 1)
    bucket = np.where(n < max_exact, n, large)
    assert np.all(np.diff(bucket) >= 0) and bucket[-1] == REL_BUCKETS - 1
    return [int(np.argmax(bucket >= b)) for b in range(REL_BUCKETS)]


def _moba_body(rb_ref, q_ref, k_ref, v_ref, o_ref, kmean_ref, bias_ref, sel_ref, m_ref, l_ref, acc_ref,
               s_ref, *, n_blocks):
    blk = MOBA_BLOCK
    h = pl.program_id(1)
    qi = pl.program_id(2)
    far_bias = LOG2E * rb_ref[REL_BUCKETS - 1, h]

    @pl.when(qi == 0)
    def _():
        r = lax.broadcasted_iota(I32, (n_blocks, n_blocks * blk), 0)
        c = lax.broadcasted_iota(I32, (n_blocks, n_blocks * blk), 1)
        pool = jnp.where(c // blk == r, 1.0, 0.0).astype(BF16)
        kmean_ref[...] = _dot(pool, k_ref[0]) * (1.0 / blk)
        row = lax.broadcasted_iota(I32, (blk, blk), 0)
        col = lax.broadcasted_iota(I32, (blk, blk), 1)
        thresholds = _t5_bucket_thresholds()
        for slot, dist in ((0, row - col + blk), (1, row - col)):
            val = jnp.full((blk, blk), rb_ref[0, h], F32)
            for b in range(1, REL_BUCKETS):
                val = jnp.where(dist >= thresholds[b], rb_ref[b, h], val)
            val = LOG2E * val
            if slot == 1:
                val = jnp.where(dist >= 0, val, NEG)
            bias_ref[slot] = val

    q = q_ref[0]
    km_hi = kmean_ref[...].astype(BF16)
    km_lo = (kmean_ref[...] - km_hi.astype(F32)).astype(BF16)
    gate = _dot_nt(km_hi, q) + _dot_nt(km_lo, q)
    blk_id = lax.broadcasted_iota(I32, gate.shape, 0)
    gate = jnp.where(blk_id < qi, gate, NEG)
    sel = jnp.zeros(gate.shape, F32)
    for _ in range(MOBA_TOPK):
        best = jnp.max(gate, axis=0, keepdims=True)
        first = jnp.min(jnp.where(gate == best, blk_id, n_blocks), axis=0, keepdims=True)
        pick = jnp.logical_and(blk_id == first, best > 0.5 * NEG)
        sel = jnp.where(pick, 1.0, sel)
        gate = jnp.where(blk_id == first, NEG, gate)
    sel_ref[...] = jnp.concatenate([sel, jnp.zeros((HEAD_DIM - n_blocks, blk), F32)], axis=0).T

    _softmax_init(m_ref, l_ref, acc_ref)
    rs = blk
    n_groups = blk // rs

    def sel_col(j):
        lanes = lax.broadcasted_iota(I32, (blk, HEAD_DIM), 1)
        return jnp.sum(jnp.where(lanes == j, sel_ref[...], 0.0), axis=1, keepdims=True) > 0.5

    def finish():
        o_ref[0] = (acc_ref[...] / l_ref[...]).astype(o_ref.dtype)

    @pl.when(qi == 0)
    def _():
        for r in range(n_groups):
            rows = pl.ds(r * rs, rs)
            u = _dot_nt(q_ref[0, rows, :], k_ref[0, 0:blk, :]) + bias_ref[1, r * rs:(r + 1) * rs, :]
            _softmax_rows(u, None, v_ref[0, 0:blk, :], rows, m_ref, l_ref, acc_ref)
        finish()

    @pl.when(qi >= 1)
    def _():
        n_far = qi - 1
        odd = n_far % 2
        n_far_units = (n_far + odd) // 2

        def first_block(unit):
            return jnp.maximum(2 * unit - odd, 0)

        def keys(unit):
            return pl.ds(pl.multiple_of(first_block(unit) * blk, blk), 2 * blk)

        def logits(unit, r):
            rows = pl.ds(r * rs, rs)
            s_ref[unit % 2, rows, :] = _dot_nt(q_ref[0, rows, :], k_ref[0, keys(unit), :])

        def attend(unit, r, extra_a, extra_b, prefetch):
            rows = pl.ds(r * rs, rs)
            u = s_ref[unit % 2, rows, :]
            u = jnp.concatenate([u[:, :blk] + extra_a, u[:, blk:] + extra_b], axis=1)
            if prefetch:
                logits(unit + 1, r)
            p, alpha = _softmax_probs(u, None, rows, m_ref, l_ref)
            _accumulate(p, alpha, v_ref[0, keys(unit), :], rows, acc_ref)

        for r in range(n_groups):
            logits(0, r)

        def far_unit(unit, carry):
            j = first_block(unit)
            masked_b = jnp.logical_and(unit == 0, odd == 1)
            bias_a = jnp.where(sel_col(j), far_bias, NEG)
            bias_b = jnp.where(jnp.logical_and(sel_col(j + 1), jnp.logical_not(masked_b)), far_bias, NEG)
            for r in range(n_groups):
                attend(unit, r, bias_a[r * rs:(r + 1) * rs], bias_b[r * rs:(r + 1) * rs], True)
            return carry

        def unit_pair(i, carry):
            return far_unit(2 * i + 1, far_unit(2 * i, carry))

        lax.fori_loop(0, n_far_units // 2, unit_pair, 0)

        @pl.when(n_far_units % 2 == 1)
        def _():
            far_unit(n_far_units - 1, 0)

        prev_mask = jnp.where(sel_col(qi - 1), 0.0, NEG)
        for r in range(n_groups):
            rr = slice(r * rs, (r + 1) * rs)
            attend(n_far_units, r, bias_ref[0, rr, :] + prev_mask[rr], bias_ref[1, rr, :], False)
        finish()


def _moba_attention(qkv, rel_bias):
    bsz, seq, _ = qkv.shape
    nh = MIX_HEADS
    blk = MOBA_BLOCK
    assert seq % blk == 0 and seq // blk <= HEAD_DIM
    n_blocks = seq // blk
    grid_spec = pltpu.PrefetchScalarGridSpec(
        num_scalar_prefetch=0,
        grid=(bsz, nh, n_blocks),
        in_specs=[
            pl.BlockSpec(memory_space=pltpu.SMEM),
            pl.BlockSpec((1, blk, HEAD_DIM), lambda b, h, i: (b, i, h)),
            pl.BlockSpec((1, seq, HEAD_DIM), lambda b, h, i: (b, 0, nh + h)),
            pl.BlockSpec((1, seq, HEAD_DIM), lambda b, h, i: (b, 0, 2 * nh + h)),
        ],
        out_specs=pl.BlockSpec((1, blk, HEAD_DIM), lambda b, h, i: (b, i, h)),
        scratch_shapes=[
            pltpu.VMEM((n_blocks, HEAD_DIM), F32),
            pltpu.VMEM((2, blk, blk), F32),
            pltpu.VMEM((blk, HEAD_DIM), F32),
            pltpu.VMEM((blk, HEAD_DIM), F32), pltpu.VMEM((blk, HEAD_DIM), F32), pltpu.VMEM((blk, HEAD_DIM), F32),
            pltpu.VMEM((2, blk, 2 * blk), F32),
        ],
    )
    return pl.pallas_call(
        functools.partial(_moba_body, n_blocks=n_blocks),
        grid_spec=grid_spec,
        out_shape=jax.ShapeDtypeStruct((bsz, seq, nh * HEAD_DIM), BF16),
        compiler_params=_cparams("parallel", "parallel", "arbitrary"),
        name="moba_attention",
    )(rel_bias, qkv, qkv, qkv)


def _mem_attn_body(q_ref, kv_ref, o_ref):
    width = MEM_HEADS * HEAD_DIM
    for hh in range(MEM_HEADS):
        cs = slice(hh * HEAD_DIM, (hh + 1) * HEAD_DIM)
        q = q_ref[0, :, cs].astype(BF16)
        k = kv_ref[0, :, cs]
        v = kv_ref[0, :, width + hh * HEAD_DIM:width + (hh + 1) * HEAD_DIM]
        s = _dot_nt(q, k)
        p = jnp.exp(s - jnp.max(s, axis=-1, keepdims=True))
        o = _dot(p.astype(BF16), v) / jnp.sum(p, axis=-1, keepdims=True)
        o_ref[0, :, cs] = o.astype(o_ref.dtype)


def _memory_attention(proj, q_col_block, kv, tq=512):
    bsz, seq, _ = proj.shape
    tq = min(tq, seq)
    width = MEM_HEADS * HEAD_DIM
    n_mem = kv.shape[1]
    return pl.pallas_call(
        _mem_attn_body,
        grid=(bsz, seq // tq),
        in_specs=[
            pl.BlockSpec((1, tq, width), lambda b, i: (b, i, q_col_block)),
            pl.BlockSpec((1, n_mem, 2 * width), lambda b, i: (b, 0, 0)),
        ],
        out_specs=pl.BlockSpec((1, tq, width), lambda b, i: (b, i, 0)),
        out_shape=jax.ShapeDtypeStruct((bsz, seq, width), BF16),
        compiler_params=_cparams("parallel", "parallel"),
        name="memory_attention",
    )(proj, kv)


def _to_row_tiles(ref, y):
    rows, d = y.shape
    n_tiles = d // HEAD_DIM
    for c in range(n_tiles):
        ref[pl.ds(c, rows, stride=n_tiles), :] = y[:, c * HEAD_DIM:(c + 1) * HEAD_DIM]


def _from_row_tiles(ref, rows, n_tiles, *lead):
    return jnp.concatenate(
        [ref[(*lead, pl.ds(c, rows, stride=n_tiles), slice(None))] for c in range(n_tiles)], axis=1)


def _row_tile(ref, r, n_tiles, *lead):
    return ref.at[(*lead, pl.ds(pl.multiple_of(r * n_tiles, n_tiles), n_tiles))]


def _route(x, wt_ref, b_ref, idx_ref, gate_ref, cnt_ref, run_ref):
    tb = x.shape[0]

    @pl.when(pl.program_id(0) == 0)
    def _():
        run_ref[...] = jnp.zeros_like(run_ref)

    xh = x.astype(BF16)
    xl = (x - xh.astype(F32)).astype(BF16)
    w = wt_ref[...]
    wh = w.astype(BF16)
    wl = (w - wh.astype(F32)).astype(BF16)
    logits = _dot_nt(wh, xh) + _dot_nt(wh, xl) + _dot_nt(wl, xh) + b_ref[...]

    n_e = logits.shape[0]
    eid = lax.broadcasted_iota(I32, logits.shape, 0)
    vals, idxs, hots = [], [], []
    multi = jnp.zeros(logits.shape, F32)
    for _ in range(TOP_K):
        best = jnp.max(logits, axis=0, keepdims=True)
        first = jnp.min(jnp.where(logits == best, eid, n_e), axis=0, keepdims=True)
        hot = eid == first
        vals.append(best)
        idxs.append(first)
        hots.append(hot)
        multi = jnp.where(hot, 1.0, multi)
        logits = jnp.where(hot, -jnp.inf, logits)

    exps = [jnp.exp(v - vals[0]) for v in vals]
    den = exps[0]
    for e in exps[1:]:
        den = den + e

    r = lax.broadcasted_iota(I32, (tb, tb), 0)
    c = lax.broadcasted_iota(I32, (tb, tb), 1)
    before = jnp.where(r < c, 1.0, 0.0).astype(BF16)
    pos = _dot(multi.astype(BF16), before) + run_ref[:, 0:1]
    for kk in range(TOP_K):
        idx_ref[kk:kk + 1, :] = idxs[kk]
        rank = jnp.sum(jnp.where(hots[kk], pos, 0.0), axis=0, keepdims=True)
        idx_ref[TOP_K + kk:TOP_K + kk + 1, :] = rank.astype(I32)
        gate_ref[kk:kk + 1, :] = exps[kk] / den
    gate_ref[TOP_K:2 * TOP_K, :] = jnp.zeros((TOP_K, tb), F32)
    run_ref[...] = run_ref[...] + jnp.sum(multi, axis=1, keepdims=True)
    cnt_ref[...] = run_ref[...]


def _out_ln_route_body(mix_ref, mem_ref, x_ref, w1_ref, w2_ref, g_ref, b_ref, wt_ref, rb_ref,
                       o_ref, ort_ref, idx_ref, gate_ref, cnt_ref, run_ref, *, alpha):
    hproj = _dot(mix_ref[...], w1_ref[...]) + _dot(mem_ref[...], w2_ref[...])
    y = _layer_norm(alpha * x_ref[...] + hproj, g_ref[...], b_ref[...])
    o_ref[...] = y
    _to_row_tiles(ort_ref, y)
    _route(y, wt_ref, rb_ref, idx_ref, gate_ref, cnt_ref, run_ref)


def _out_proj_ln_route(mix, mem_out, x, w_out, g, b, alpha, router_w, router_b, tm=512):
    t, d = x.shape
    tm = min(tm, t)
    wm = mix.shape[1]
    we = mem_out.shape[1]
    n_e = router_w.shape[1]
    return pl.pallas_call(
        functools.partial(_out_ln_route_body, alpha=alpha),
        grid=(t // tm,),
        in_specs=[
            pl.BlockSpec((tm, wm), lambda i: (i, 0)),
            pl.BlockSpec((tm, we), lambda i: (i, 0)),
            pl.BlockSpec((tm, d), lambda i: (i, 0)),
            pl.BlockSpec((wm, d), lambda i: (0, 0)),
            pl.BlockSpec((we, d), lambda i: (0, 0)),
            pl.BlockSpec((1, d), lambda i: (0, 0)),
            pl.BlockSpec((1, d), lambda i: (0, 0)),
            pl.BlockSpec((n_e, d), lambda i: (0, 0)),
            pl.BlockSpec((n_e, 1), lambda i: (0, 0)),
        ],
        out_specs=[
            pl.BlockSpec((tm, d), lambda i: (i, 0)),
            pl.BlockSpec((tm * (d // HEAD_DIM), HEAD_DIM), lambda i: (i, 0)),
            pl.BlockSpec((2 * TOP_K, tm), lambda i: (0, i)),
            pl.BlockSpec((2 * TOP_K, tm), lambda i: (0, i)),
            pl.BlockSpec((n_e, HEAD_DIM), lambda i: (0, 0)),
        ],
        out_shape=[
            jax.ShapeDtypeStruct((t, d), F32),
            jax.ShapeDtypeStruct((t * (d // HEAD_DIM), HEAD_DIM), F32),
            jax.ShapeDtypeStruct((2 * TOP_K, t), I32),
            jax.ShapeDtypeStruct((2 * TOP_K, t), F32),
            jax.ShapeDtypeStruct((n_e, HEAD_DIM), F32),
        ],
        scratch_shapes=[pltpu.VMEM((n_e, HEAD_DIM), F32)],
        compiler_params=_cparams("arbitrary"),
        name="out_proj_ln_route",
    )(mix, mem_out, x, w_out[:wm].astype(BF16), w_out[wm:].astype(BF16), g.reshape(1, d), b.reshape(1, d),
      router_w.T, router_b.reshape(n_e, 1))


def _dispatch_body(dest_ref, pad_lo_ref, pad_hi_ref, nu_ref, x_ref, xs_hbm, zero_ref, sem, zsem,
                   *, tb, n_tok, n_tiles, n_experts, n_blocks):
    base = pl.program_id(0) * tb

    def issue(t, carry):
        for kk in range(TOP_K):
            d = dest_ref[kk * n_tok + base + t]
            pltpu.make_async_copy(_row_tile(x_ref, t, n_tiles), _row_tile(xs_hbm, d, n_tiles), sem).start(
                priority=kk % 2)
        return carry

    lax.fori_loop(0, tb, issue, 0, unroll=DMA_UNROLL)

    @pl.when(pl.program_id(0) == 0)
    def _():
        zero_ref[...] = jnp.zeros_like(zero_ref)

        def pad_row(r):
            return pltpu.make_async_copy(zero_ref.at[pl.ds(0, n_tiles)], _row_tile(xs_hbm, r, n_tiles), zsem)

        def spare_block(blk):
            rows = pl.ds(pl.multiple_of(blk * MOE_BLOCK * n_tiles, MOE_BLOCK * n_tiles), MOE_BLOCK * n_tiles)
            return pltpu.make_async_copy(zero_ref, xs_hbm.at[rows], zsem)

        def for_all(action):
            for e in range(n_experts):
                lax.fori_loop(pad_lo_ref[e], pad_hi_ref[e], lambda r, c: (action(pad_row(r)), c)[1], 0)
            lax.fori_loop(nu_ref[0], n_blocks, lambda blk, c: (action(spare_block(blk)), c)[1], 0)

        for_all(lambda cp: cp.start())
        for_all(lambda cp: cp.wait())

    for kk in range(TOP_K):
        pltpu.make_async_copy(x_ref, xs_hbm.at[pl.ds(0, tb * n_tiles)], sem).wait()


def _dispatch(x, n_tiles, dest_flat, pad_lo, pad_hi, n_used, n_rows, tb=512):
    t = x.shape[0] // n_tiles
    tb = min(tb, t)
    grid_spec = pltpu.PrefetchScalarGridSpec(
        num_scalar_prefetch=4,
        grid=(t // tb,),
        in_specs=[pl.BlockSpec((tb * n_tiles, HEAD_DIM), lambda i, *_: (i, 0))],
        out_specs=pl.BlockSpec(memory_space=pl.ANY),
        scratch_shapes=[pltpu.VMEM((MOE_BLOCK * n_tiles, HEAD_DIM), x.dtype),
                        pltpu.SemaphoreType.DMA(()), pltpu.SemaphoreType.DMA(())],
    )
    return pl.pallas_call(
        functools.partial(_dispatch_body, tb=tb, n_tok=t, n_tiles=n_tiles, n_experts=pad_lo.shape[0],
                          n_blocks=n_rows // MOE_BLOCK),
        grid_spec=grid_spec,
        out_shape=jax.ShapeDtypeStruct((n_rows * n_tiles, HEAD_DIM), x.dtype),
        compiler_params=_cparams("arbitrary"),
        name="moe_dispatch",
    )(dest_flat, pad_lo, pad_hi, n_used, x)


UNZIP = 256


def _expert_body(be_ref, nu_ref, first_ref, slot_ref, next_ref, x_ref, wu_hbm, bg_ref, bl_ref, wd_hbm, bd_ref,
                 o_ref, wu_buf, wd_buf, wg_s, wl_s, wd_s, sem, *, layer, n_tiles):
    i = pl.program_id(0)
    n_used = nu_ref[0]

    def weight_copies(expert, slot):
        return (pltpu.make_async_copy(wu_hbm.at[layer, expert], wu_buf.at[slot], sem.at[0, slot]),
                pltpu.make_async_copy(wd_hbm.at[layer, expert], wd_buf.at[slot], sem.at[1, slot]))

    @pl.when(i == 0)
    def _():
        for cp in weight_copies(be_ref[0], 0):
            cp.start()

    @pl.when(jnp.logical_and(first_ref[i] == 1, i < n_used))
    def _():
        slot = slot_ref[i]
        for cp in weight_copies(be_ref[i], slot):
            cp.wait()

        @pl.when(next_ref[i] >= 0)
        def _():
            for cp in weight_copies(next_ref[i], 1 - slot):
                cp.start()

        half = UNZIP // 2
        r = lax.broadcasted_iota(I32, (UNZIP, UNZIP), 0)
        c = lax.broadcasted_iota(I32, (UNZIP, UNZIP), 1)
        src = jnp.where(c < half, 2 * c, 2 * (c - half) + 1)
        perm = jnp.where(r == src, 1.0, 0.0).astype(BF16)
        for g in range(wu_buf.shape[2] // UNZIP):
            w = wu_buf[slot, :, g * UNZIP:(g + 1) * UNZIP].astype(BF16)
            sep = _dot(w, perm).astype(BF16)
            wg_s[:, g * half:(g + 1) * half] = sep[:, :half]
            wl_s[:, g * half:(g + 1) * half] = sep[:, half:]
        wd_s[...] = wd_buf[slot].astype(BF16)

    @pl.when(i < n_used)
    def _():
        xb = _from_row_tiles(x_ref, MOE_BLOCK, n_tiles).astype(BF16)
        glu = jnp.minimum(_dot(xb, wg_s[...]) + bg_ref[0], SWIGLU_LIMIT)
        lin = jnp.clip(_dot(xb, wl_s[...]) + bl_ref[0], -SWIGLU_LIMIT, SWIGLU_LIMIT)
        act = glu * _sigmoid(SWIGLU_ALPHA * glu) * (lin + 1.0)
        _to_row_tiles(o_ref, _dot(act.astype(BF16), wd_s[...]) + bd_ref[0])

    @pl.when(i >= n_used)
    def _():
        o_ref[...] = jnp.zeros_like(o_ref)


def _expert_mlp(xs, n_tiles, blk_expert, n_used, counts, layer, w_up, bg, bl, w_down, bd):
    p_rows = xs.shape[0] // n_tiles
    d = n_tiles * HEAD_DIM
    f = w_down.shape[2]
    n_e = w_down.shape[1]
    n_blk = p_rows // MOE_BLOCK

    blocks = jnp.arange(n_blk, dtype=I32)
    valid = blocks < n_used[0]
    prev_e = jnp.concatenate([jnp.full((1,), -1, I32), blk_expert[:-1]])
    first = jnp.logical_and(valid, blk_expert != prev_e)
    slot = (jnp.cumsum(first.astype(I32)) - 1) % 2
    experts = jnp.arange(n_e, dtype=I32)
    later = jnp.logical_and(experts[None, :] > blk_expert[:, None], counts[None, :] > 0)
    nxt = jnp.min(jnp.where(later, experts[None, :], n_e), axis=1)
    nxt = jnp.where(nxt == n_e, -1, nxt)

    def blk(i, nu):
        return jnp.minimum(i, nu[0] - 1)

    def w_spec(shape):
        return pl.BlockSpec((1,) + shape, lambda i, be, nu, fi, sl, nx: (be[blk(i, nu)], 0, 0))

    grid_spec = pltpu.PrefetchScalarGridSpec(
        num_scalar_prefetch=5,
        grid=(n_blk,),
        in_specs=[
            pl.BlockSpec((MOE_BLOCK * n_tiles, HEAD_DIM), lambda i, be, nu, fi, sl, nx: (blk(i, nu), 0)),
            pl.BlockSpec(memory_space=pl.ANY), w_spec((1, f)), w_spec((1, f)),
            pl.BlockSpec(memory_space=pl.ANY), w_spec((1, d)),
        ],
        out_specs=pl.BlockSpec((MOE_BLOCK * n_tiles, HEAD_DIM), lambda i, be, nu, fi, sl, nx: (i, 0)),
        scratch_shapes=[
            pltpu.VMEM((2, d, 2 * f), F32), pltpu.VMEM((2, f, d), F32),
            pltpu.VMEM((d, f), BF16), pltpu.VMEM((d, f), BF16), pltpu.VMEM((f, d), BF16),
            pltpu.SemaphoreType.DMA((2, 2)),
        ],
    )
    return pl.pallas_call(
        functools.partial(_expert_body, layer=layer, n_tiles=n_tiles),
        grid_spec=grid_spec,
        out_shape=jax.ShapeDtypeStruct(xs.shape, F32),
        compiler_params=_cparams("arbitrary"),
        name="moe_experts",
    )(blk_expert, n_used, first.astype(I32), slot.astype(I32), nxt.astype(I32), xs, w_up, bg, bl, w_down, bd)


def _combine_body(dest_ref, ys_hbm, gate_ref, x_ref, g_ref, b_ref, o_ref, buf_ref, sem_ref,
                  *, tb, n_tok, n_tiles, alpha):
    i = pl.program_id(0)
    n_steps = pl.num_programs(0)

    def issue(step, slot):
        base = step * tb

        def one(t, carry):
            for kk in range(TOP_K):
                d = dest_ref[kk * n_tok + base + t]
                pltpu.make_async_copy(_row_tile(ys_hbm, d, n_tiles), _row_tile(buf_ref, t, n_tiles, slot, kk),
                                      sem_ref.at[slot]).start(priority=kk % 2)
            return carry

        lax.fori_loop(0, tb, one, 0, unroll=DMA_UNROLL)

    @pl.when(i == 0)
    def _():
        issue(0, 0)

    slot = lax.rem(i, 2)

    @pl.when(i + 1 < n_steps)
    def _():
        issue(i + 1, 1 - slot)

    for kk in range(TOP_K):
        pltpu.make_async_copy(ys_hbm.at[pl.ds(0, tb * n_tiles)], buf_ref.at[slot, kk], sem_ref.at[slot]).wait()

    g8 = gate_ref[...]
    gpad = jnp.concatenate([g8, jnp.zeros((HEAD_DIM - g8.shape[0], tb), F32)], axis=0)
    gt = gpad.T
    f = gt[:, 0:1] * _from_row_tiles(buf_ref, tb, n_tiles, slot, 0)
    for kk in range(1, TOP_K):
        f = f + gt[:, kk:kk + 1] * _from_row_tiles(buf_ref, tb, n_tiles, slot, kk)
    o_ref[...] = _layer_norm(alpha * x_ref[...] + f, g_ref[...], b_ref[...])


def _combine_ln(ys, dest_flat, gates, x, g, b, alpha, tb=256):
    t, d = x.shape
    tb = min(tb, t)
    n_tiles = d // HEAD_DIM
    grid_spec = pltpu.PrefetchScalarGridSpec(
        num_scalar_prefetch=1,
        grid=(t // tb,),
        in_specs=[
            pl.BlockSpec(memory_space=pl.ANY),
            pl.BlockSpec((2 * TOP_K, tb), lambda i, dest: (0, i)),
            pl.BlockSpec((tb, d), lambda i, dest: (i, 0)),
            pl.BlockSpec((1, d), lambda i, dest: (0, 0)),
            pl.BlockSpec((1, d), lambda i, dest: (0, 0)),
        ],
        out_specs=pl.BlockSpec((tb, d), lambda i, dest: (i, 0)),
        scratch_shapes=[pltpu.VMEM((2, TOP_K, tb * n_tiles, HEAD_DIM), F32), pltpu.SemaphoreType.DMA((2,))],
    )
    return pl.pallas_call(
        functools.partial(_combine_body, tb=tb, n_tok=t, n_tiles=n_tiles, alpha=alpha),
        grid_spec=grid_spec,
        out_shape=jax.ShapeDtypeStruct((t, d), F32),
        compiler_params=_cparams("arbitrary"),
        name="moe_combine_ln",
    )(dest_flat, ys, gates, x, g.reshape(1, d), b.reshape(1, d))


def _moe_ln(x, x_tiles, idx_rank, gates, cnt, layer, w_up, b_up, w_down, b_down, g, b, alpha):
    t, d = x.shape
    n_tiles = d // HEAD_DIM
    n_e = cnt.shape[0]
    counts = cnt[:, 0].astype(I32)
    padded = ((counts + MOE_BLOCK - 1) // MOE_BLOCK) * MOE_BLOCK
    p_ends = jnp.cumsum(padded)
    p_starts = p_ends - padded
    n_blk = (t * TOP_K + MOE_BLOCK - 1) // MOE_BLOCK + n_e
    experts = jnp.arange(n_e, dtype=I32)
    start_of = jnp.sum(jnp.where(idx_rank[:TOP_K, :, None] == experts, p_starts, 0), axis=-1)
    dest_flat = (start_of + idx_rank[TOP_K:]).reshape(-1)
    blk_first_row = jnp.arange(n_blk, dtype=I32) * MOE_BLOCK
    blk_expert = jnp.clip(jnp.sum((p_ends[None, :] <= blk_first_row[:, None]).astype(I32), axis=1), 0, n_e - 1)
    n_used = (p_ends[-1:] // MOE_BLOCK).astype(I32)

    xs = _dispatch(x_tiles, n_tiles, dest_flat, p_starts + counts, p_ends, n_used, n_blk * MOE_BLOCK)
    bg = b_up[:, None, 0::2]
    bl = b_up[:, None, 1::2]
    ys = _expert_mlp(xs, n_tiles, blk_expert, n_used, counts, layer, w_up, bg, bl, w_down, b_down[:, None, :])
    return _combine_ln(ys, dest_flat, gates, x, g, b, alpha)


def kernel(x, mem, hgrn_w_in, hgrn_lb, hgrn_norm_g, fox_w_in, fox_b_f, moba_w_in, rel_bias, w_mem_kv, w_out,
           ln1_g, ln1_b, router_w, router_b, w_up, b_up, w_down, b_down, ln2_g, ln2_b):
    bsz, seq, d = x.shape
    depth = w_out.shape[0]
    t = bsz * seq
    mix_w = MIX_HEADS * HEAD_DIM
    mem_w = MEM_HEADS * HEAD_DIM
    alpha = (2 * depth) ** 0.25
    scale = HEAD_DIM ** -0.5
    n_mem = mem.shape[1]
    mem2 = mem.reshape(bsz * n_mem, d)

    def scale_vec(n_mixer_q, n_plain, n_mem_q):
        return jnp.concatenate([
            jnp.full((1, n_mixer_q), scale * LOG2E, F32), jnp.ones((1, n_plain), F32),
            jnp.full((1, n_mem_q), scale, F32)], axis=1)

    x2 = x.reshape(t, d)
    for i in range(depth):
        kind, j = i % 3, i // 3
        if kind == 0:
            w_in = hgrn_w_in[j].astype(BF16)
            proj = _matmul(x2, w_in, scale_vec(0, 4 * mix_w, mem_w), F32, 1024, w_in.shape[1] // 3)
            proj = proj.reshape(bsz, seq, -1)
            mix = _hgrn_mixer(proj, hgrn_lb, hgrn_norm_g[j], j)
            memq_block = 4 * mix_w // mem_w
        elif kind == 1:
            w = fox_w_in[j]
            w_in = jnp.concatenate([w[:, :3 * mix_w], w[:, 3 * mix_w + MIX_HEADS:]], axis=1).astype(BF16)
            proj = _matmul(x2, w_in, scale_vec(mix_w, 2 * mix_w, mem_w), BF16, 1024, w_in.shape[1] // 2)
            proj = proj.reshape(bsz, seq, -1)
            c_col, c_row = _fox_gate_cumsum(x2.reshape(bsz, seq, d), w[:, 3 * mix_w:3 * mix_w + MIX_HEADS],
                                            fox_b_f[j])
            mix = _fox_attention(proj, c_col, c_row)
            memq_block = 3 * mix_w // mem_w
        else:
            w_in = moba_w_in[j].astype(BF16)
            proj = _matmul(x2, w_in, scale_vec(mix_w, 2 * mix_w, mem_w), BF16, 1024, w_in.shape[1] // 2)
            proj = proj.reshape(bsz, seq, -1)
            mix = _moba_attention(proj, rel_bias)
            memq_block = 3 * mix_w // mem_w
        kv = _matmul(mem2, w_mem_kv[i].astype(BF16), jnp.ones((1, 2 * mem_w), F32), BF16, 512, 512)
        mem_out = _memory_attention(proj, memq_block, kv.reshape(bsz, n_mem, 2 * mem_w))
        x2, x2_tiles, idx_rank, gates, cnt = _out_proj_ln_route(
            mix.reshape(t, mix_w), mem_out.reshape(t, mem_w), x2, w_out[i], ln1_g[i], ln1_b[i], alpha,
            router_w[i], router_b[i])
        x2 = _moe_ln(x2, x2_tiles, idx_rank, gates, cnt, i, w_up, b_up[i], w_down, b_down[i],
                     ln2_g[i], ln2_b[i], alpha)
    return x2.reshape(bsz, seq, d)
```

```python
import functools
import math

import numpy as np
import jax
import jax.numpy as jnp
from jax import lax
from jax.experimental import pallas as pl
from jax.experimental.pallas import tpu as pltpu

F32 = jnp.float32
BF16 = jnp.bfloat16
I32 = jnp.int32

HEAD_DIM = 128
MIX_HEADS = 8
MEM_HEADS = 4
HGRN_CHUNK = 64
HGRN_SUB = 16
HGRN_EXP_CLAMP = 80.0
MOBA_BLOCK = 256
MOBA_TOPK = 3
REL_BUCKETS = 32
REL_MAX_DIST = 128
N_EXPERTS = 32
TOP_K = 4
MOE_BLOCK = 256
SWIGLU_LIMIT = 7.0
SWIGLU_ALPHA = 1.702
LN_EPS = 1e-5
RMS_EPS = 1e-6
NEG = -1e30
LOG2E = 1.4426950408889634
VMEM_LIMIT = 56 * 1024 * 1024
DMA_UNROLL = 8

NT_DIMS = (((1,), (1,)), ((), ()))
TN_DIMS = (((0,), (0,)), ((), ()))


def _cparams(*sem):
    return pltpu.CompilerParams(dimension_semantics=sem, vmem_limit_bytes=VMEM_LIMIT)


def _dot(a, b):
    return jnp.dot(a, b, preferred_element_type=F32)


def _dot_nt(a, b):
    return lax.dot_general(a, b, NT_DIMS, preferred_element_type=F32)


def _dot_tn(a, b):
    return lax.dot_general(a, b, TN_DIMS, preferred_element_type=F32)


def _split3(a):
    hi = a.astype(BF16)
    r = a - hi.astype(F32)
    mid = r.astype(BF16)
    lo = (r - mid.astype(F32)).astype(BF16)
    return hi, mid, lo


def _sigmoid(x):
    return 1.0 / (1.0 + jnp.exp(-x))


def _layer_norm(z, g, b):
    mu = jnp.mean(z, axis=-1, keepdims=True)
    zc = z - mu
    var = jnp.mean(zc * zc, axis=-1, keepdims=True)
    return zc * lax.rsqrt(var + LN_EPS) * g + b


def _mm_body(x_ref, w_ref, s_ref, o_ref, xb_ref):
    @pl.when(pl.program_id(1) == 0)
    def _():
        xb_ref[...] = x_ref[...].astype(BF16)

    acc = _dot(xb_ref[...], w_ref[...])
    o_ref[...] = (acc * s_ref[...]).astype(o_ref.dtype)


def _matmul(x, w, col_scale, out_dtype, tm, tn):
    m, k = x.shape
    n = w.shape[1]
    tm = min(tm, m)
    tn = min(tn, n)
    return pl.pallas_call(
        _mm_body,
        grid=(m // tm, n // tn),
        in_specs=[
            pl.BlockSpec((tm, k), lambda i, j: (i, 0)),
            pl.BlockSpec((k, tn), lambda i, j: (0, j)),
            pl.BlockSpec((1, tn), lambda i, j: (0, j)),
        ],
        out_specs=pl.BlockSpec((tm, tn), lambda i, j: (i, j)),
        out_shape=jax.ShapeDtypeStruct((m, n), out_dtype),
        scratch_shapes=[pltpu.VMEM((tm, k), BF16)],
        compiler_params=_cparams("parallel", "arbitrary"),
        name="proj_matmul",
    )(x, w, col_scale)


def _hgrn_body(q_ref, z_ref, i_ref, g_ref, lbp_ref, ng_ref, o_ref, st_ref, *, layer_j, tb):
    c_len, sb = HGRN_CHUNK, HGRN_SUB
    n_sub = c_len // sb

    @pl.when(pl.program_id(2) == 0)
    def _():
        st_ref[...] = jnp.zeros_like(st_ref)

    lbp = lbp_ref[...]
    ex = jnp.exp(lbp - jnp.max(lbp, axis=0, keepdims=True))
    p = ex / jnp.sum(ex, axis=0, keepdims=True)
    lb = jnp.zeros((1, HEAD_DIM), F32)
    for r in range(1, layer_j + 1):
        lb = lb + p[r:r + 1, :]
    log_lb = jnp.log(lb)
    log1m_lb = jnp.log1p(-lb)
    one_m_lb = 1.0 - lb
    ng = ng_ref[...]

    row = lax.broadcasted_iota(I32, (c_len, c_len), 0)
    col = lax.broadcasted_iota(I32, (c_len, c_len), 1)
    sub_start = (row // sb) * sb
    one = jnp.ones((c_len, c_len), F32)
    zero = jnp.zeros((c_len, c_len), F32)
    m_tri = jnp.where(col <= row, one, zero)
    m_in = jnp.where(col > sub_start, m_tri, zero)
    m_dec = jnp.where(col > row, one, zero)
    m_ke = jnp.where(col <= sub_start + (sb - 1), m_dec, zero)
    stack = jnp.concatenate([m_tri, m_in, m_ke, m_dec], axis=0).astype(BF16)
    same_sub = (row // sb) == (col // sb)
    diag_mask = jnp.logical_and(same_sub, col <= row)
    row_sub = row // sb
    krow_sub = lax.broadcasted_iota(I32, (c_len, HEAD_DIM), 0) // sb

    chunks = [pl.ds(c * c_len, c_len) for c in range(tb // c_len)]

    gates = []
    for sl in chunks:
        z = z_ref[0, sl, :]
        e = jnp.exp(-jnp.abs(z))
        inv = 1.0 / (1.0 + e)
        sig_neg = jnp.where(z >= 0, e, 1.0) * inv
        log_sig = jnp.minimum(z, 0.0) - jnp.log(1.0 + e)
        a2 = log1m_lb + log_sig
        log_f = jnp.maximum(log_lb, a2) + jnp.log(1.0 + jnp.exp(-jnp.abs(log_lb - a2)))
        qv = q_ref[0, sl, :]
        gates.append((log_f, one_m_lb * sig_neg, qv * _sigmoid(qv), i_ref[0, sl, :].astype(BF16)))

    prefix = []
    for log_f, _, _, _ in gates:
        pre = None
        for piece in _split3(log_f):
            t = _dot(stack, piece)
            pre = t if pre is None else pre + t
        prefix.append(pre)

    factors = []
    for (_, k, qf, _), pre in zip(gates, prefix):
        b = pre[0:c_len]
        d_in = pre[c_len:2 * c_len]
        d_ke = pre[2 * c_len:3 * c_len]
        d_dec = pre[3 * c_len:4 * c_len]
        q_in = qf * jnp.exp(d_in)
        k_diag = (k * jnp.exp(jnp.minimum(-d_in, HGRN_EXP_CLAMP))).astype(BF16)
        k_end = k * jnp.exp(d_ke)
        k_dec = (k * jnp.exp(d_dec)).astype(BF16)
        q_all = (qf * jnp.exp(b)).astype(BF16)
        b_ref = b - d_in
        lhs = [q_in.astype(BF16)]
        rhs = [k_diag]
        for j in range(n_sub - 1):
            b_end = b[(j + 1) * sb - 1:(j + 1) * sb, :]
            cross = jnp.exp(jnp.minimum(b_ref - b_end, 0.0))
            lhs.append((q_in * cross).astype(BF16))
            rhs.append(jnp.where(krow_sub == j, k_end, 0.0).astype(BF16))
        factors.append((lhs, rhs, k_dec, q_all, jnp.exp(b[c_len - 1:c_len, :])))

    scores = []
    for lhs, rhs, _, _, _ in factors:
        a = jnp.where(diag_mask, _dot_nt(lhs[0], rhs[0]), 0.0)
        for j in range(n_sub - 1):
            a = a + jnp.where(row_sub > j, _dot_nt(lhs[j + 1], rhs[j + 1]), 0.0)
        scores.append(a.astype(BF16))

    updates = [_dot_tn(vb, k_dec) for (_, _, _, vb), (_, _, k_dec, _, _) in zip(gates, factors)]
    states = [st_ref[...]]
    for (_, _, _, _, decay), kv in zip(factors, updates):
        states.append(states[-1] * decay + kv)
    st_ref[...] = states[-1]

    for sl, (_, _, _, vb), (_, _, _, q_all, _), a, st in zip(chunks, gates, factors, scores, states):
        o = _dot(a, vb) + _dot_nt(q_all, st.astype(BF16))
        o = o * lax.rsqrt(jnp.mean(o * o, axis=-1, keepdims=True) + RMS_EPS)
        gv = g_ref[0, sl, :]
        o_ref[0, sl, :] = (o * ng * (gv * _sigmoid(gv))).astype(o_ref.dtype)


def _hgrn_mixer(proj, lb_param, norm_g, layer_j, tb=512):
    bsz, seq, _ = proj.shape
    tb = min(tb, seq)
    n_layers = lb_param.shape[0]

    def col_block(off):
        return pl.BlockSpec((1, tb, HEAD_DIM), lambda b, h, t: (b, t, off + h))

    return pl.pallas_call(
        functools.partial(_hgrn_body, layer_j=layer_j, tb=tb),
        grid=(bsz, MIX_HEADS, seq // tb),
        in_specs=[
            col_block(0), col_block(MIX_HEADS), col_block(2 * MIX_HEADS), col_block(3 * MIX_HEADS),
            pl.BlockSpec((n_layers, HEAD_DIM), lambda b, h, t: (0, h)),
            pl.BlockSpec((1, HEAD_DIM), lambda b, h, t: (0, h)),
        ],
        out_specs=pl.BlockSpec((1, tb, HEAD_DIM), lambda b, h, t: (b, t, h)),
        out_shape=jax.ShapeDtypeStruct((bsz, seq, MIX_HEADS * HEAD_DIM), BF16),
        scratch_shapes=[pltpu.VMEM((HEAD_DIM, HEAD_DIM), F32)],
        compiler_params=_cparams("parallel", "parallel", "arbitrary"),
        name="hgrn2_mixer",
    )(proj, proj, proj, proj, lb_param, norm_g.reshape(1, -1))


def _log_sigmoid(z):
    return jnp.minimum(z, 0.0) - jnp.log1p(jnp.exp(-jnp.abs(z)))


def _fox_gate_body(x_ref, w_ref, wt_ref, bc_ref, br_ref, ccol_ref, crow_ref, carc_ref, carr_ref, *, tb):
    @pl.when(pl.program_id(1) == 0)
    def _():
        carc_ref[...] = jnp.zeros_like(carc_ref)
        carr_ref[...] = jnp.zeros_like(carr_ref)

    xb = x_ref[0].astype(BF16)
    ls_col = LOG2E * _log_sigmoid(_dot(xb, w_ref[...]) + bc_ref[...])
    ls_row = LOG2E * _log_sigmoid(_dot_nt(wt_ref[...], xb) + br_ref[...])
    row = lax.broadcasted_iota(I32, (tb, tb), 0)
    col = lax.broadcasted_iota(I32, (tb, tb), 1)
    lower = jnp.where(col <= row, 1.0, 0.0).astype(BF16)
    upper = jnp.where(row <= col, 1.0, 0.0).astype(BF16)
    c_col = carc_ref[...]
    for piece in _split3(ls_col):
        c_col = c_col + _dot(lower, piece)
    c_row = carr_ref[...]
    for piece in _split3(ls_row):
        c_row = c_row + _dot(piece, upper)
    ccol_ref[0] = c_col
    crow_ref[0] = c_row
    carc_ref[...] = c_col[tb - 1:tb, :]
    carr_ref[...] = c_row[:, tb - 1:tb]


def _fox_gate_cumsum(x, w_f, b_f, tb=512):
    bsz, seq, d = x.shape
    h = w_f.shape[1]
    tb = min(tb, seq)
    return pl.pallas_call(
        functools.partial(_fox_gate_body, tb=tb),
        grid=(bsz, seq // tb),
        in_specs=[
            pl.BlockSpec((1, tb, d), lambda b, t: (b, t, 0)),
            pl.BlockSpec((d, h), lambda b, t: (0, 0)),
            pl.BlockSpec((h, d), lambda b, t: (0, 0)),
            pl.BlockSpec((1, h), lambda b, t: (0, 0)),
            pl.BlockSpec((h, 1), lambda b, t: (0, 0)),
        ],
        out_specs=[
            pl.BlockSpec((1, tb, h), lambda b, t: (b, t, 0)),
            pl.BlockSpec((1, h, tb), lambda b, t: (b, 0, t)),
        ],
        out_shape=[
            jax.ShapeDtypeStruct((bsz, seq, h), F32),
            jax.ShapeDtypeStruct((bsz, h, seq), F32),
        ],
        scratch_shapes=[pltpu.VMEM((1, h), F32), pltpu.VMEM((h, 1), F32)],
        compiler_params=_cparams("parallel", "arbitrary"),
        name="fox_gate_cumsum",
    )(x, w_f.astype(BF16), w_f.T.astype(BF16), b_f.reshape(1, h), b_f.reshape(h, 1))


ATTN_ROWS = 128


def _softmax_probs(u, shift, rows, m_ref, l_ref):
    reps = u.shape[1] // HEAD_DIM
    m_old = m_ref[rows, :]
    if shift is None:
        m_new = jnp.maximum(m_old, jnp.max(u, axis=-1, keepdims=True))
        p = jnp.exp2(u - jnp.tile(m_new, (1, reps)))
    else:
        m_new = jnp.maximum(m_old, jnp.max(u, axis=-1, keepdims=True) + shift)
        p = jnp.exp2(u - jnp.tile(m_new - shift, (1, reps)))
    alpha = jnp.exp2(m_old - m_new)
    l_ref[rows, :] = alpha * l_ref[rows, :] + jnp.sum(p, axis=-1, keepdims=True)
    m_ref[rows, :] = m_new
    return p.astype(BF16), alpha


def _accumulate(p, alpha, v, rows, acc_ref):
    acc_ref[rows, :] = alpha * acc_ref[rows, :] + _dot(p, v)


def _softmax_init(m_ref, l_ref, acc_ref):
    m_ref[...] = jnp.full(m_ref.shape, NEG, F32)
    l_ref[...] = jnp.zeros_like(l_ref)
    acc_ref[...] = jnp.zeros_like(acc_ref)


def _fox_attn_body(q_ref, k_ref, v_ref, ccol_ref, crow_ref, o_ref, ct_ref, m_ref, l_ref, acc_ref, s_ref, *, tq):
    rs = min(ATTN_ROWS, tq)
    n_groups = tq // rs
    h = pl.program_id(1)
    qi = pl.program_id(2)
    ccol = ccol_ref[0]
    lane = lax.broadcasted_iota(I32, ccol.shape, 1)
    c_t = jnp.sum(jnp.where(lane == h, ccol, 0.0), axis=1, keepdims=True)
    ct_ref[...] = jnp.broadcast_to(c_t, ct_ref.shape)
    _softmax_init(m_ref, l_ref, acc_ref)

    def keys(kb):
        return pl.ds(pl.multiple_of(kb * tq, tq), tq)

    def logits(kb, r):
        rows = pl.ds(r * rs, rs)
        ks = keys(kb)
        s_ref[kb % 2, rows, :] = _dot_nt(q_ref[0, rows, :], k_ref[0, ks, :]) - crow_ref[0, pl.ds(h, 1), ks]

    def attend(kb, r, diagonal, prefetch):
        rows = pl.ds(r * rs, rs)
        u = s_ref[kb % 2, rows, :]
        if diagonal:
            row = lax.broadcasted_iota(I32, (rs, tq), 0) + r * rs
            col = lax.broadcasted_iota(I32, (rs, tq), 1)
            u = jnp.where(col <= row, u, NEG)
        if prefetch:
            logits(kb + 1, r)
        p, alpha = _softmax_probs(u, ct_ref[rows, :], rows, m_ref, l_ref)
        _accumulate(p, alpha, v_ref[0, keys(kb), :], rows, acc_ref)

    for r in range(n_groups):
        logits(0, r)

    def full_block(kb, carry):
        for r in range(n_groups):
            attend(kb, r, False, True)
        return carry

    def block_pair(i, carry):
        return full_block(2 * i + 1, full_block(2 * i, carry))

    lax.fori_loop(0, qi // 2, block_pair, 0)

    @pl.when(qi % 2 == 1)
    def _():
        full_block(qi - 1, 0)

    for r in range(n_groups):
        attend(qi, r, True, False)
    o_ref[0] = (acc_ref[...] / l_ref[...]).astype(o_ref.dtype)


def _fox_attention(qkv, c_col, c_row, tq=512):
    bsz, seq, _ = qkv.shape
    tq = min(tq, seq)
    nh = MIX_HEADS
    return pl.pallas_call(
        functools.partial(_fox_attn_body, tq=tq),
        grid=(bsz, nh, seq // tq),
        in_specs=[
            pl.BlockSpec((1, tq, HEAD_DIM), lambda b, h, i: (b, i, h)),
            pl.BlockSpec((1, seq, HEAD_DIM), lambda b, h, i: (b, 0, nh + h)),
            pl.BlockSpec((1, seq, HEAD_DIM), lambda b, h, i: (b, 0, 2 * nh + h)),
            pl.BlockSpec((1, tq, nh), lambda b, h, i: (b, i, 0)),
            pl.BlockSpec((1, nh, seq), lambda b, h, i: (b, 0, 0)),
        ],
        out_specs=pl.BlockSpec((1, tq, HEAD_DIM), lambda b, h, i: (b, i, h)),
        out_shape=jax.ShapeDtypeStruct((bsz, seq, nh * HEAD_DIM), BF16),
        scratch_shapes=[pltpu.VMEM((tq, HEAD_DIM), F32)] * 4 + [pltpu.VMEM((2, tq, tq), F32)],
        compiler_params=_cparams("parallel", "parallel", "arbitrary"),
        name="fox_attention",
    )(qkv, qkv, qkv, c_col, c_row)


def _t5_bucket_thresholds():
    n = np.arange(0, 4 * REL_MAX_DIST, dtype=np.int64)
    max_exact = REL_BUCKETS // 2
    ratio = np.log(np.maximum(n, 1).astype(np.float32) / np.float32(max_exact)) / np.float32(
        math.log(REL_MAX_DIST / max_exact))
    large = np.minimum(max_exact + (ratio * np.float32(REL_BUCKETS - max_exact)).astype(np.int32),
                       REL_BUCKETS - 1)
    bucket = np.where(n < max_exact, n, large)
    assert np.all(np.diff(bucket) >= 0) and bucket[-1] == REL_BUCKETS - 1
    return [int(np.argmax(bucket >= b)) for b in range(REL_BUCKETS)]


def _moba_body(rb_ref, q_ref, k_ref, v_ref, o_ref, kmean_ref, bias_ref, sel_ref, m_ref, l_ref, acc_ref,
               s_ref, *, n_blocks):
    blk = MOBA_BLOCK
    tq = 2 * blk
    h = pl.program_id(1)
    ti = pl.program_id(2)
    far_bias = LOG2E * rb_ref[REL_BUCKETS - 1, h]

    @pl.when(ti == 0)
    def _():
        r = lax.broadcasted_iota(I32, (n_blocks, n_blocks * blk), 0)
        c = lax.broadcasted_iota(I32, (n_blocks, n_blocks * blk), 1)
        pool = jnp.where(c // blk == r, 1.0, 0.0).astype(BF16)
        kmean_ref[...] = _dot(pool, k_ref[0]) * (1.0 / blk)
        row = lax.broadcasted_iota(I32, (blk, blk), 0)
        col = lax.broadcasted_iota(I32, (blk, blk), 1)
        thresholds = _t5_bucket_thresholds()
        for slot, dist in ((0, row - col + blk), (1, row - col)):
            val = jnp.full((blk, blk), rb_ref[0, h], F32)
            for b in range(1, REL_BUCKETS):
                val = jnp.where(dist >= thresholds[b], rb_ref[b, h], val)
            val = LOG2E * val
            if slot == 1:
                val = jnp.where(dist >= 0, val, NEG)
            bias_ref[slot] = val

    q = q_ref[0]
    km_hi = kmean_ref[...].astype(BF16)
    km_lo = (kmean_ref[...] - km_hi.astype(F32)).astype(BF16)
    gate = _dot_nt(km_hi, q) + _dot_nt(km_lo, q)
    blk_id = lax.broadcasted_iota(I32, gate.shape, 0)
    own = 2 * ti + lax.broadcasted_iota(I32, gate.shape, 1) // blk
    gate = jnp.where(blk_id < own, gate, NEG)
    sel = jnp.zeros(gate.shape, F32)
    for _ in range(MOBA_TOPK):
        best = jnp.max(gate, axis=0, keepdims=True)
        first = jnp.min(jnp.where(gate == best, blk_id, n_blocks), axis=0, keepdims=True)
        pick = jnp.logical_and(blk_id == first, best > 0.5 * NEG)
        sel = jnp.where(pick, 1.0, sel)
        gate = jnp.where(blk_id == first, NEG, gate)
    sel_ref[...] = jnp.concatenate([sel, jnp.zeros((HEAD_DIM - n_blocks, tq), F32)], axis=0).T

    _softmax_init(m_ref, l_ref, acc_ref)
    groups = (pl.ds(0, blk), pl.ds(blk, blk))

    def selected(j, rows):
        lanes = lax.broadcasted_iota(I32, (blk, HEAD_DIM), 1)
        return jnp.sum(jnp.where(lanes == j, sel_ref[rows, :], 0.0), axis=1, keepdims=True) > 0.5

    def far_term(j, rows):
        return jnp.where(selected(j, rows), far_bias, NEG)

    def prev_term(j, rows):
        return bias_ref[0] + jnp.where(selected(j, rows), 0.0, NEG)

    def keys(unit, n_keys=tq):
        return pl.ds(pl.multiple_of(unit * tq, tq), n_keys)

    def logits(unit, rows):
        s_ref[unit % 2, rows, :] = _dot_nt(q_ref[0, rows, :], k_ref[0, keys(unit), :])

    def attend(unit, rows, term_a, term_b, prefetch):
        u = s_ref[unit % 2, rows, :]
        if term_b is None:
            u = u[:, :blk] + term_a
            ks = keys(unit, blk)
        else:
            u = jnp.concatenate([u[:, :blk] + term_a, u[:, blk:] + term_b], axis=1)
            ks = keys(unit)
        if prefetch:
            logits(unit + 1, rows)
        p, alpha = _softmax_probs(u, None, rows, m_ref, l_ref)
        _accumulate(p, alpha, v_ref[0, ks, :], rows, acc_ref)

    for rows in groups:
        logits(0, rows)

    def far_unit(unit, carry):
        for rows in groups:
            attend(unit, rows, far_term(2 * unit, rows), far_term(2 * unit + 1, rows), True)
        return carry

    def unit_pair(i, carry):
        return far_unit(2 * i + 1, far_unit(2 * i, carry))

    n_far_units = jnp.maximum(ti - 1, 0)
    lax.fori_loop(0, n_far_units // 2, unit_pair, 0)

    @pl.when(n_far_units % 2 == 1)
    def _():
        far_unit(n_far_units - 1, 0)

    @pl.when(ti >= 1)
    def _():
        j = 2 * ti - 2
        attend(ti - 1, groups[0], far_term(j, groups[0]), prev_term(j + 1, groups[0]), True)
        attend(ti - 1, groups[1], far_term(j, groups[1]), far_term(j + 1, groups[1]), True)

    attend(ti, groups[0], bias_ref[1], None, False)
    attend(ti, groups[1], prev_term(2 * ti, groups[1]), bias_ref[1], False)
    o_ref[0] = (acc_ref[...] / l_ref[...]).astype(o_ref.dtype)


def _moba_attention(qkv, rel_bias):
    bsz, seq, _ = qkv.shape
    nh = MIX_HEADS
    blk = MOBA_BLOCK
    tq = 2 * blk
    assert seq % tq == 0 and seq // blk <= HEAD_DIM
    n_blocks = seq // blk
    grid_spec = pltpu.PrefetchScalarGridSpec(
        num_scalar_prefetch=0,
        grid=(bsz, nh, seq // tq),
        in_specs=[
            pl.BlockSpec(memory_space=pltpu.SMEM),
            pl.BlockSpec((1, tq, HEAD_DIM), lambda b, h, i: (b, i, h)),
            pl.BlockSpec((1, seq, HEAD_DIM), lambda b, h, i: (b, 0, nh + h)),
            pl.BlockSpec((1, seq, HEAD_DIM), lambda b, h, i: (b, 0, 2 * nh + h)),
        ],
        out_specs=pl.BlockSpec((1, tq, HEAD_DIM), lambda b, h, i: (b, i, h)),
        scratch_shapes=[
            pltpu.VMEM((n_blocks, HEAD_DIM), F32),
            pltpu.VMEM((2, blk, blk), F32),
            pltpu.VMEM((tq, HEAD_DIM), F32),
            pltpu.VMEM((tq, HEAD_DIM), F32), pltpu.VMEM((tq, HEAD_DIM), F32), pltpu.VMEM((tq, HEAD_DIM), F32),
            pltpu.VMEM((2, tq, tq), F32),
        ],
    )
    return pl.pallas_call(
        functools.partial(_moba_body, n_blocks=n_blocks),
        grid_spec=grid_spec,
        out_shape=jax.ShapeDtypeStruct((bsz, seq, nh * HEAD_DIM), BF16),
        compiler_params=_cparams("parallel", "parallel", "arbitrary"),
        name="moba_attention",
    )(rel_bias, qkv, qkv, qkv)


def _mem_attn_body(q_ref, kv_ref, o_ref):
    width = MEM_HEADS * HEAD_DIM
    for hh in range(MEM_HEADS):
        cs = slice(hh * HEAD_DIM, (hh + 1) * HEAD_DIM)
        q = q_ref[0, :, cs].astype(BF16)
        k = kv_ref[0, :, cs]
        v = kv_ref[0, :, width + hh * HEAD_DIM:width + (hh + 1) * HEAD_DIM]
        s = _dot_nt(q, k)
        p = jnp.exp(s - jnp.max(s, axis=-1, keepdims=True))
        o = _dot(p.astype(BF16), v) / jnp.sum(p, axis=-1, keepdims=True)
        o_ref[0, :, cs] = o.astype(o_ref.dtype)


def _memory_attention(proj, q_col_block, kv, tq=512):
    bsz, seq, _ = proj.shape
    tq = min(tq, seq)
    width = MEM_HEADS * HEAD_DIM
    n_mem = kv.shape[1]
    return pl.pallas_call(
        _mem_attn_body,
        grid=(bsz, seq // tq),
        in_specs=[
            pl.BlockSpec((1, tq, width), lambda b, i: (b, i, q_col_block)),
            pl.BlockSpec((1, n_mem, 2 * width), lambda b, i: (b, 0, 0)),
        ],
        out_specs=pl.BlockSpec((1, tq, width), lambda b, i: (b, i, 0)),
        out_shape=jax.ShapeDtypeStruct((bsz, seq, width), BF16),
        compiler_params=_cparams("parallel", "parallel"),
        name="memory_attention",
    )(proj, kv)


def _to_row_tiles(ref, y):
    rows, d = y.shape
    n_tiles = d // HEAD_DIM
    for c in range(n_tiles):
        ref[pl.ds(c, rows, stride=n_tiles), :] = y[:, c * HEAD_DIM:(c + 1) * HEAD_DIM]


def _from_row_tiles(ref, rows, n_tiles, *lead):
    return jnp.concatenate(
        [ref[(*lead, pl.ds(c, rows, stride=n_tiles), slice(None))] for c in range(n_tiles)], axis=1)


def _row_tile(ref, r, n_tiles, *lead):
    return ref.at[(*lead, pl.ds(pl.multiple_of(r * n_tiles, n_tiles), n_tiles))]


def _route(x, wt_ref, b_ref, idx_ref, gate_ref, cnt_ref, run_ref):
    tb = x.shape[0]

    @pl.when(pl.program_id(0) == 0)
    def _():
        run_ref[...] = jnp.zeros_like(run_ref)

    xh = x.astype(BF16)
    xl = (x - xh.astype(F32)).astype(BF16)
    w = wt_ref[...]
    wh = w.astype(BF16)
    wl = (w - wh.astype(F32)).astype(BF16)
    logits = _dot_nt(wh, xh) + _dot_nt(wh, xl) + _dot_nt(wl, xh) + b_ref[...]

    n_e = logits.shape[0]
    eid = lax.broadcasted_iota(I32, logits.shape, 0)
    vals, idxs, hots = [], [], []
    multi = jnp.zeros(logits.shape, F32)
    for _ in range(TOP_K):
        best = jnp.max(logits, axis=0, keepdims=True)
        first = jnp.min(jnp.where(logits == best, eid, n_e), axis=0, keepdims=True)
        hot = eid == first
        vals.append(best)
        idxs.append(first)
        hots.append(hot)
        multi = jnp.where(hot, 1.0, multi)
        logits = jnp.where(hot, -jnp.inf, logits)

    exps = [jnp.exp(v - vals[0]) for v in vals]
    den = exps[0]
    for e in exps[1:]:
        den = den + e

    r = lax.broadcasted_iota(I32, (tb, tb), 0)
    c = lax.broadcasted_iota(I32, (tb, tb), 1)
    before = jnp.where(r < c, 1.0, 0.0).astype(BF16)
    pos = _dot(multi.astype(BF16), before) + run_ref[:, 0:1]
    for kk in range(TOP_K):
        idx_ref[kk:kk + 1, :] = idxs[kk]
        rank = jnp.sum(jnp.where(hots[kk], pos, 0.0), axis=0, keepdims=True)
        idx_ref[TOP_K + kk:TOP_K + kk + 1, :] = rank.astype(I32)
        gate_ref[kk:kk + 1, :] = exps[kk] / den
    gate_ref[TOP_K:2 * TOP_K, :] = jnp.zeros((TOP_K, tb), F32)
    run_ref[...] = run_ref[...] + jnp.sum(multi, axis=1, keepdims=True)
    cnt_ref[...] = run_ref[...]


def _out_ln_route_body(mix_ref, mem_ref, x_ref, w1_ref, w2_ref, g_ref, b_ref, wt_ref, rb_ref,
                       o_ref, ort_ref, idx_ref, gate_ref, cnt_ref, run_ref, *, alpha):
    hproj = _dot(mix_ref[...], w1_ref[...]) + _dot(mem_ref[...], w2_ref[...])
    y = _layer_norm(alpha * x_ref[...] + hproj, g_ref[...], b_ref[...])
    o_ref[...] = y
    _to_row_tiles(ort_ref, y)
    _route(y, wt_ref, rb_ref, idx_ref, gate_ref, cnt_ref, run_ref)


def _out_proj_ln_route(mix, mem_out, x, w_out, g, b, alpha, router_w, router_b, tm=512):
    t, d = x.shape
    tm = min(tm, t)
    wm = mix.shape[1]
    we = mem_out.shape[1]
    n_e = router_w.shape[1]
    return pl.pallas_call(
        functools.partial(_out_ln_route_body, alpha=alpha),
        grid=(t // tm,),
        in_specs=[
            pl.BlockSpec((tm, wm), lambda i: (i, 0)),
            pl.BlockSpec((tm, we), lambda i: (i, 0)),
            pl.BlockSpec((tm, d), lambda i: (i, 0)),
            pl.BlockSpec((wm, d), lambda i: (0, 0)),
            pl.BlockSpec((we, d), lambda i: (0, 0)),
            pl.BlockSpec((1, d), lambda i: (0, 0)),
            pl.BlockSpec((1, d), lambda i: (0, 0)),
            pl.BlockSpec((n_e, d), lambda i: (0, 0)),
            pl.BlockSpec((n_e, 1), lambda i: (0, 0)),
        ],
        out_specs=[
            pl.BlockSpec((tm, d), lambda i: (i, 0)),
            pl.BlockSpec((tm * (d // HEAD_DIM), HEAD_DIM), lambda i: (i, 0)),
            pl.BlockSpec((2 * TOP_K, tm), lambda i: (0, i)),
            pl.BlockSpec((2 * TOP_K, tm), lambda i: (0, i)),
            pl.BlockSpec((n_e, HEAD_DIM), lambda i: (0, 0)),
        ],
        out_shape=[
            jax.ShapeDtypeStruct((t, d), F32),
            jax.ShapeDtypeStruct((t * (d // HEAD_DIM), HEAD_DIM), F32),
            jax.ShapeDtypeStruct((2 * TOP_K, t), I32),
            jax.ShapeDtypeStruct((2 * TOP_K, t), F32),
            jax.ShapeDtypeStruct((n_e, HEAD_DIM), F32),
        ],
        scratch_shapes=[pltpu.VMEM((n_e, HEAD_DIM), F32)],
        compiler_params=_cparams("arbitrary"),
        name="out_proj_ln_route",
    )(mix, mem_out, x, w_out[:wm].astype(BF16), w_out[wm:].astype(BF16), g.reshape(1, d), b.reshape(1, d),
      router_w.T, router_b.reshape(n_e, 1))


def _dispatch_body(dest_ref, pad_lo_ref, pad_hi_ref, nu_ref, x_ref, xs_hbm, zero_ref, sem, zsem,
                   *, tb, n_tok, n_tiles, n_experts, n_blocks):
    base = pl.program_id(0) * tb

    def issue(t, carry):
        for kk in range(TOP_K):
            d = dest_ref[kk * n_tok + base + t]
            pltpu.make_async_copy(_row_tile(x_ref, t, n_tiles), _row_tile(xs_hbm, d, n_tiles), sem).start(
                priority=kk % 2)
        return carry

    lax.fori_loop(0, tb, issue, 0, unroll=DMA_UNROLL)

    @pl.when(pl.program_id(0) == 0)
    def _():
        zero_ref[...] = jnp.zeros_like(zero_ref)

        def pad_row(r):
            return pltpu.make_async_copy(zero_ref.at[pl.ds(0, n_tiles)], _row_tile(xs_hbm, r, n_tiles), zsem)

        def spare_block(blk):
            rows = pl.ds(pl.multiple_of(blk * MOE_BLOCK * n_tiles, MOE_BLOCK * n_tiles), MOE_BLOCK * n_tiles)
            return pltpu.make_async_copy(zero_ref, xs_hbm.at[rows], zsem)

        def for_all(action):
            for e in range(n_experts):
                lax.fori_loop(pad_lo_ref[e], pad_hi_ref[e], lambda r, c: (action(pad_row(r)), c)[1], 0)
            lax.fori_loop(nu_ref[0], n_blocks, lambda blk, c: (action(spare_block(blk)), c)[1], 0)

        for_all(lambda cp: cp.start())
        for_all(lambda cp: cp.wait())

    for kk in range(TOP_K):
        pltpu.make_async_copy(x_ref, xs_hbm.at[pl.ds(0, tb * n_tiles)], sem).wait()


def _dispatch(x, n_tiles, dest_flat, pad_lo, pad_hi, n_used, n_rows, tb=512):
    t = x.shape[0] // n_tiles
    tb = min(tb, t)
    grid_spec = pltpu.PrefetchScalarGridSpec(
        num_scalar_prefetch=4,
        grid=(t // tb,),
        in_specs=[pl.BlockSpec((tb * n_tiles, HEAD_DIM), lambda i, *_: (i, 0))],
        out_specs=pl.BlockSpec(memory_space=pl.ANY),
        scratch_shapes=[pltpu.VMEM((MOE_BLOCK * n_tiles, HEAD_DIM), x.dtype),
                        pltpu.SemaphoreType.DMA(()), pltpu.SemaphoreType.DMA(())],
    )
    return pl.pallas_call(
        functools.partial(_dispatch_body, tb=tb, n_tok=t, n_tiles=n_tiles, n_experts=pad_lo.shape[0],
                          n_blocks=n_rows // MOE_BLOCK),
        grid_spec=grid_spec,
        out_shape=jax.ShapeDtypeStruct((n_rows * n_tiles, HEAD_DIM), x.dtype),
        compiler_params=_cparams("arbitrary"),
        name="moe_dispatch",
    )(dest_flat, pad_lo, pad_hi, n_used, x)


UNZIP = 256


def _expert_body(be_ref, nu_ref, first_ref, slot_ref, next_ref, x_ref, wu_hbm, bg_ref, bl_ref, wd_hbm, bd_ref,
                 o_ref, wu_buf, wd_buf, wg_s, wl_s, wd_s, sem, *, layer, n_tiles):
    i = pl.program_id(0)
    n_used = nu_ref[0]

    def weight_copies(expert, slot):
        return (pltpu.make_async_copy(wu_hbm.at[layer, expert], wu_buf.at[slot], sem.at[0, slot]),
                pltpu.make_async_copy(wd_hbm.at[layer, expert], wd_buf.at[slot], sem.at[1, slot]))

    @pl.when(i == 0)
    def _():
        for cp in weight_copies(be_ref[0], 0):
            cp.start()

    @pl.when(jnp.logical_and(first_ref[i] == 1, i < n_used))
    def _():
        slot = slot_ref[i]
        for cp in weight_copies(be_ref[i], slot):
            cp.wait()

        @pl.when(next_ref[i] >= 0)
        def _():
            for cp in weight_copies(next_ref[i], 1 - slot):
                cp.start()

        half = UNZIP // 2
        r = lax.broadcasted_iota(I32, (UNZIP, UNZIP), 0)
        c = lax.broadcasted_iota(I32, (UNZIP, UNZIP), 1)
        src = jnp.where(c < half, 2 * c, 2 * (c - half) + 1)
        perm = jnp.where(r == src, 1.0, 0.0).astype(BF16)
        for g in range(wu_buf.shape[2] // UNZIP):
            w = wu_buf[slot, :, g * UNZIP:(g + 1) * UNZIP].astype(BF16)
            sep = _dot(w, perm).astype(BF16)
            wg_s[:, g * half:(g + 1) * half] = sep[:, :half]
            wl_s[:, g * half:(g + 1) * half] = sep[:, half:]
        wd_s[...] = wd_buf[slot].astype(BF16)

    @pl.when(i < n_used)
    def _():
        xb = _from_row_tiles(x_ref, MOE_BLOCK, n_tiles).astype(BF16)
        glu = jnp.minimum(_dot(xb, wg_s[...]) + bg_ref[0], SWIGLU_LIMIT)
        lin = jnp.clip(_dot(xb, wl_s[...]) + bl_ref[0], -SWIGLU_LIMIT, SWIGLU_LIMIT)
        act = glu * _sigmoid(SWIGLU_ALPHA * glu) * (lin + 1.0)
        _to_row_tiles(o_ref, _dot(act.astype(BF16), wd_s[...]) + bd_ref[0])

    @pl.when(i >= n_used)
    def _():
        o_ref[...] = jnp.zeros_like(o_ref)


def _expert_mlp(xs, n_tiles, blk_expert, n_used, counts, layer, w_up, bg, bl, w_down, bd):
    p_rows = xs.shape[0] // n_tiles
    d = n_tiles * HEAD_DIM
    f = w_down.shape[2]
    n_e = w_down.shape[1]
    n_blk = p_rows // MOE_BLOCK

    blocks = jnp.arange(n_blk, dtype=I32)
    valid = blocks < n_used[0]
    prev_e = jnp.concatenate([jnp.full((1,), -1, I32), blk_expert[:-1]])
    first = jnp.logical_and(valid, blk_expert != prev_e)
    slot = (jnp.cumsum(first.astype(I32)) - 1) % 2
    experts = jnp.arange(n_e, dtype=I32)
    later = jnp.logical_and(experts[None, :] > blk_expert[:, None], counts[None, :] > 0)
    nxt = jnp.min(jnp.where(later, experts[None, :], n_e), axis=1)
    nxt = jnp.where(nxt == n_e, -1, nxt)

    def blk(i, nu):
        return jnp.minimum(i, nu[0] - 1)

    def w_spec(shape):
        return pl.BlockSpec((1,) + shape, lambda i, be, nu, fi, sl, nx: (be[blk(i, nu)], 0, 0))

    grid_spec = pltpu.PrefetchScalarGridSpec(
        num_scalar_prefetch=5,
        grid=(n_blk,),
        in_specs=[
            pl.BlockSpec((MOE_BLOCK * n_tiles, HEAD_DIM), lambda i, be, nu, fi, sl, nx: (blk(i, nu), 0)),
            pl.BlockSpec(memory_space=pl.ANY), w_spec((1, f)), w_spec((1, f)),
            pl.BlockSpec(memory_space=pl.ANY), w_spec((1, d)),
        ],
        out_specs=pl.BlockSpec((MOE_BLOCK * n_tiles, HEAD_DIM), lambda i, be, nu, fi, sl, nx: (i, 0)),
        scratch_shapes=[
            pltpu.VMEM((2, d, 2 * f), F32), pltpu.VMEM((2, f, d), F32),
            pltpu.VMEM((d, f), BF16), pltpu.VMEM((d, f), BF16), pltpu.VMEM((f, d), BF16),
            pltpu.SemaphoreType.DMA((2, 2)),
        ],
    )
    return pl.pallas_call(
        functools.partial(_expert_body, layer=layer, n_tiles=n_tiles),
        grid_spec=grid_spec,
        out_shape=jax.ShapeDtypeStruct(xs.shape, F32),
        compiler_params=_cparams("arbitrary"),
        name="moe_experts",
    )(blk_expert, n_used, first.astype(I32), slot.astype(I32), nxt.astype(I32), xs, w_up, bg, bl, w_down, bd)


def _combine_body(dest_ref, ys_hbm, gate_ref, x_ref, g_ref, b_ref, o_ref, buf_ref, sem_ref,
                  *, tb, n_tok, n_tiles, alpha):
    i = pl.program_id(0)
    n_steps = pl.num_programs(0)

    def issue(step, slot):
        base = step * tb

        def one(t, carry):
            for kk in range(TOP_K):
                d = dest_ref[kk * n_tok + base + t]
                pltpu.make_async_copy(_row_tile(ys_hbm, d, n_tiles), _row_tile(buf_ref, t, n_tiles, slot, kk),
                                      sem_ref.at[slot]).start(priority=kk % 2)
            return carry

        lax.fori_loop(0, tb, one, 0, unroll=DMA_UNROLL)

    @pl.when(i == 0)
    def _():
        issue(0, 0)

    slot = lax.rem(i, 2)

    @pl.when(i + 1 < n_steps)
    def _():
        issue(i + 1, 1 - slot)

    for kk in range(TOP_K):
        pltpu.make_async_copy(ys_hbm.at[pl.ds(0, tb * n_tiles)], buf_ref.at[slot, kk], sem_ref.at[slot]).wait()

    g8 = gate_ref[...]
    gpad = jnp.concatenate([g8, jnp.zeros((HEAD_DIM - g8.shape[0], tb), F32)], axis=0)
    gt = gpad.T
    f = gt[:, 0:1] * _from_row_tiles(buf_ref, tb, n_tiles, slot, 0)
    for kk in range(1, TOP_K):
        f = f + gt[:, kk:kk + 1] * _from_row_tiles(buf_ref, tb, n_tiles, slot, kk)
    o_ref[...] = _layer_norm(alpha * x_ref[...] + f, g_ref[...], b_ref[...])


def _combine_ln(ys, dest_flat, gates, x, g, b, alpha, tb=256):
    t, d = x.shape
    tb = min(tb, t)
    n_tiles = d // HEAD_DIM
    grid_spec = pltpu.PrefetchScalarGridSpec(
        num_scalar_prefetch=1,
        grid=(t // tb,),
        in_specs=[
            pl.BlockSpec(memory_space=pl.ANY),
            pl.BlockSpec((2 * TOP_K, tb), lambda i, dest: (0, i)),
            pl.BlockSpec((tb, d), lambda i, dest: (i, 0)),
            pl.BlockSpec((1, d), lambda i, dest: (0, 0)),
            pl.BlockSpec((1, d), lambda i, dest: (0, 0)),
        ],
        out_specs=pl.BlockSpec((tb, d), lambda i, dest: (i, 0)),
        scratch_shapes=[pltpu.VMEM((2, TOP_K, tb * n_tiles, HEAD_DIM), F32), pltpu.SemaphoreType.DMA((2,))],
    )
    return pl.pallas_call(
        functools.partial(_combine_body, tb=tb, n_tok=t, n_tiles=n_tiles, alpha=alpha),
        grid_spec=grid_spec,
        out_shape=jax.ShapeDtypeStruct((t, d), F32),
        compiler_params=_cparams("arbitrary"),
        name="moe_combine_ln",
    )(dest_flat, ys, gates, x, g.reshape(1, d), b.reshape(1, d))


def _moe_ln(x, x_tiles, idx_rank, gates, cnt, layer, w_up, b_up, w_down, b_down, g, b, alpha):
    t, d = x.shape
    n_tiles = d // HEAD_DIM
    n_e = cnt.shape[0]
    counts = cnt[:, 0].astype(I32)
    padded = ((counts + MOE_BLOCK - 1) // MOE_BLOCK) * MOE_BLOCK
    p_ends = jnp.cumsum(padded)
    p_starts = p_ends - padded
    n_blk = (t * TOP_K + MOE_BLOCK - 1) // MOE_BLOCK + n_e
    experts = jnp.arange(n_e, dtype=I32)
    start_of = jnp.sum(jnp.where(idx_rank[:TOP_K, :, None] == experts, p_starts, 0), axis=-1)
    dest_flat = (start_of + idx_rank[TOP_K:]).reshape(-1)
    blk_first_row = jnp.arange(n_blk, dtype=I32) * MOE_BLOCK
    blk_expert = jnp.clip(jnp.sum((p_ends[None, :] <= blk_first_row[:, None]).astype(I32), axis=1), 0, n_e - 1)
    n_used = (p_ends[-1:] // MOE_BLOCK).astype(I32)

    xs = _dispatch(x_tiles, n_tiles, dest_flat, p_starts + counts, p_ends, n_used, n_blk * MOE_BLOCK)
    bg = b_up[:, None, 0::2]
    bl = b_up[:, None, 1::2]
    ys = _expert_mlp(xs, n_tiles, blk_expert, n_used, counts, layer, w_up, bg, bl, w_down, b_down[:, None, :])
    return _combine_ln(ys, dest_flat, gates, x, g, b, alpha)


def kernel(x, mem, hgrn_w_in, hgrn_lb, hgrn_norm_g, fox_w_in, fox_b_f, moba_w_in, rel_bias, w_mem_kv, w_out,
           ln1_g, ln1_b, router_w, router_b, w_up, b_up, w_down, b_down, ln2_g, ln2_b):
    bsz, seq, d = x.shape
    depth = w_out.shape[0]
    t = bsz * seq
    mix_w = MIX_HEADS * HEAD_DIM
    mem_w = MEM_HEADS * HEAD_DIM
    alpha = (2 * depth) ** 0.25
    scale = HEAD_DIM ** -0.5
    n_mem = mem.shape[1]
    mem2 = mem.reshape(bsz * n_mem, d)

    def scale_vec(n_mixer_q, n_plain, n_mem_q):
        return jnp.concatenate([
            jnp.full((1, n_mixer_q), scale * LOG2E, F32), jnp.ones((1, n_plain), F32),
            jnp.full((1, n_mem_q), scale, F32)], axis=1)

    x2 = x.reshape(t, d)
    for i in range(depth):
        kind, j = i % 3, i // 3
        if kind == 0:
            w_in = hgrn_w_in[j].astype(BF16)
            proj = _matmul(x2, w_in, scale_vec(0, 4 * mix_w, mem_w), F32, 1024, w_in.shape[1] // 3)
            proj = proj.reshape(bsz, seq, -1)
            mix = _hgrn_mixer(proj, hgrn_lb, hgrn_norm_g[j], j)
            memq_block = 4 * mix_w // mem_w
        elif kind == 1:
            w = fox_w_in[j]
            w_in = jnp.concatenate([w[:, :3 * mix_w], w[:, 3 * mix_w + MIX_HEADS:]], axis=1).astype(BF16)
            proj = _matmul(x2, w_in, scale_vec(mix_w, 2 * mix_w, mem_w), BF16, 1024, w_in.shape[1] // 2)
            proj = proj.reshape(bsz, seq, -1)
            c_col, c_row = _fox_gate_cumsum(x2.reshape(bsz, seq, d), w[:, 3 * mix_w:3 * mix_w + MIX_HEADS],
                                            fox_b_f[j])
            mix = _fox_attention(proj, c_col, c_row)
            memq_block = 3 * mix_w // mem_w
        else:
            w_in = moba_w_in[j].astype(BF16)
            proj = _matmul(x2, w_in, scale_vec(mix_w, 2 * mix_w, mem_w), BF16, 1024, w_in.shape[1] // 2)
            proj = proj.reshape(bsz, seq, -1)
            mix = _moba_attention(proj, rel_bias)
            memq_block = 3 * mix_w // mem_w
        kv = _matmul(mem2, w_mem_kv[i].astype(BF16), jnp.ones((1, 2 * mem_w), F32), BF16, 512, 512)
        mem_out = _memory_attention(proj, memq_block, kv.reshape(bsz, n_mem, 2 * mem_w))
        x2, x2_tiles, idx_rank, gates, cnt = _out_proj_ln_route(
            mix.reshape(t, mix_w), mem_out.reshape(t, mem_w), x2, w_out[i], ln1_g[i], ln1_b[i], alpha,
            router_w[i], router_b[i])
        x2 = _moe_ln(x2, x2_tiles, idx_rank, gates, cnt, i, w_up, b_up[i], w_down, b_down[i],
                     ln2_g[i], ln2_b[i], alpha)
    return x2.reshape(bsz, seq, d)
```

```python
import functools
import math

import numpy as np
import jax
import jax.numpy as jnp
from jax import lax
from jax.experimental import pallas as pl
from jax.experimental.pallas import tpu as pltpu

F32 = jnp.float32
BF16 = jnp.bfloat16
I32 = jnp.int32

HEAD_DIM = 128
MIX_HEADS = 8
MEM_HEADS = 4
HGRN_CHUNK = 64
HGRN_SUB = 16
HGRN_EXP_CLAMP = 80.0
MOBA_BLOCK = 256
MOBA_TOPK = 3
REL_BUCKETS = 32
REL_MAX_DIST = 128
N_EXPERTS = 32
TOP_K = 4
MOE_BLOCK = 256
SWIGLU_LIMIT = 7.0
SWIGLU_ALPHA = 1.702
LN_EPS = 1e-5
RMS_EPS = 1e-6
NEG = -1e30
LOG2E = 1.4426950408889634
VMEM_LIMIT = 56 * 1024 * 1024
DMA_UNROLL = 8

NT_DIMS = (((1,), (1,)), ((), ()))
TN_DIMS = (((0,), (0,)), ((), ()))


def _cparams(*sem):
    return pltpu.CompilerParams(dimension_semantics=sem, vmem_limit_bytes=VMEM_LIMIT)


def _dot(a, b):
    return jnp.dot(a, b, preferred_element_type=F32)


def _dot_nt(a, b):
    return lax.dot_general(a, b, NT_DIMS, preferred_element_type=F32)


def _dot_tn(a, b):
    return lax.dot_general(a, b, TN_DIMS, preferred_element_type=F32)


def _split3(a):
    hi = a.astype(BF16)
    r = a - hi.astype(F32)
    mid = r.astype(BF16)
    lo = (r - mid.astype(F32)).astype(BF16)
    return hi, mid, lo


def _sigmoid(x):
    return 1.0 / (1.0 + jnp.exp(-x))


def _layer_norm(z, g, b):
    mu = jnp.mean(z, axis=-1, keepdims=True)
    zc = z - mu
    var = jnp.mean(zc * zc, axis=-1, keepdims=True)
    return zc * lax.rsqrt(var + LN_EPS) * g + b


def _mm_body(x_ref, w_ref, s_ref, o_ref, xb_ref):
    @pl.when(pl.program_id(1) == 0)
    def _():
        xb_ref[...] = x_ref[...].astype(BF16)

    acc = _dot(xb_ref[...], w_ref[...])
    o_ref[...] = (acc * s_ref[...]).astype(o_ref.dtype)


def _matmul(x, w, col_scale, out_dtype, tm, tn):
    m, k = x.shape
    n = w.shape[1]
    tm = min(tm, m)
    tn = min(tn, n)
    return pl.pallas_call(
        _mm_body,
        grid=(m // tm, n // tn),
        in_specs=[
            pl.BlockSpec((tm, k), lambda i, j: (i, 0)),
            pl.BlockSpec((k, tn), lambda i, j: (0, j)),
            pl.BlockSpec((1, tn), lambda i, j: (0, j)),
        ],
        out_specs=pl.BlockSpec((tm, tn), lambda i, j: (i, j)),
        out_shape=jax.ShapeDtypeStruct((m, n), out_dtype),
        scratch_shapes=[pltpu.VMEM((tm, k), BF16)],
        compiler_params=_cparams("parallel", "arbitrary"),
        name="proj_matmul",
    )(x, w, col_scale)


def _hgrn_body(q_ref, z_ref, i_ref, g_ref, lbp_ref, ng_ref, o_ref, st_ref, *, layer_j, tb):
    c_len, sb = HGRN_CHUNK, HGRN_SUB
    n_sub = c_len // sb

    @pl.when(pl.program_id(2) == 0)
    def _():
        st_ref[...] = jnp.zeros_like(st_ref)

    lbp = lbp_ref[...]
    ex = jnp.exp(lbp - jnp.max(lbp, axis=0, keepdims=True))
    p = ex / jnp.sum(ex, axis=0, keepdims=True)
    lb = jnp.zeros((1, HEAD_DIM), F32)
    for r in range(1, layer_j + 1):
        lb = lb + p[r:r + 1, :]
    log_lb = jnp.log(lb)
    log1m_lb = jnp.log1p(-lb)
    one_m_lb = 1.0 - lb
    ng = ng_ref[...]

    row = lax.broadcasted_iota(I32, (c_len, c_len), 0)
    col = lax.broadcasted_iota(I32, (c_len, c_len), 1)
    sub_start = (row // sb) * sb
    one = jnp.ones((c_len, c_len), F32)
    zero = jnp.zeros((c_len, c_len), F32)
    m_tri = jnp.where(col <= row, one, zero)
    m_in = jnp.where(col > sub_start, m_tri, zero)
    m_dec = jnp.where(col > row, one, zero)
    m_ke = jnp.where(col <= sub_start + (sb - 1), m_dec, zero)
    stack = jnp.concatenate([m_tri, m_in, m_ke, m_dec], axis=0).astype(BF16)
    same_sub = (row // sb) == (col // sb)
    diag_mask = jnp.logical_and(same_sub, col <= row)
    row_sub = row // sb
    krow_sub = lax.broadcasted_iota(I32, (c_len, HEAD_DIM), 0) // sb

    chunks = [pl.ds(c * c_len, c_len) for c in range(tb // c_len)]

    gates = []
    for sl in chunks:
        z = z_ref[0, sl, :]
        e = jnp.exp(-jnp.abs(z))
        inv = 1.0 / (1.0 + e)
        sig_neg = jnp.where(z >= 0, e, 1.0) * inv
        log_sig = jnp.minimum(z, 0.0) - jnp.log(1.0 + e)
        a2 = log1m_lb + log_sig
        log_f = jnp.maximum(log_lb, a2) + jnp.log(1.0 + jnp.exp(-jnp.abs(log_lb - a2)))
        qv = q_ref[0, sl, :]
        gates.append((log_f, one_m_lb * sig_neg, qv * _sigmoid(qv), i_ref[0, sl, :].astype(BF16)))

    prefix = []
    for log_f, _, _, _ in gates:
        pre = None
        for piece in _split3(log_f):
            t = _dot(stack, piece)
            pre = t if pre is None else pre + t
        prefix.append(pre)

    factors = []
    for (_, k, qf, _), pre in zip(gates, prefix):
        b = pre[0:c_len]
        d_in = pre[c_len:2 * c_len]
        d_ke = pre[2 * c_len:3 * c_len]
        d_dec = pre[3 * c_len:4 * c_len]
        q_in = qf * jnp.exp(d_in)
        k_diag = (k * jnp.exp(jnp.minimum(-d_in, HGRN_EXP_CLAMP))).astype(BF16)
        k_end = k * jnp.exp(d_ke)
        k_dec = (k * jnp.exp(d_dec)).astype(BF16)
        q_all = (qf * jnp.exp(b)).astype(BF16)
        b_ref = b - d_in
        lhs = [q_in.astype(BF16)]
        rhs = [k_diag]
        for j in range(n_sub - 1):
            b_end = b[(j + 1) * sb - 1:(j + 1) * sb, :]
            cross = jnp.exp(jnp.minimum(b_ref - b_end, 0.0))
            lhs.append((q_in * cross).astype(BF16))
            rhs.append(jnp.where(krow_sub == j, k_end, 0.0).astype(BF16))
        factors.append((lhs, rhs, k_dec, q_all, jnp.exp(b[c_len - 1:c_len, :])))

    scores = []
    for lhs, rhs, _, _, _ in factors:
        a = jnp.where(diag_mask, _dot_nt(lhs[0], rhs[0]), 0.0)
        for j in range(n_sub - 1):
            a = a + jnp.where(row_sub > j, _dot_nt(lhs[j + 1], rhs[j + 1]), 0.0)
        scores.append(a.astype(BF16))

    updates = [_dot_tn(vb, k_dec) for (_, _, _, vb), (_, _, k_dec, _, _) in zip(gates, factors)]
    states = [st_ref[...]]
    for (_, _, _, _, decay), kv in zip(factors, updates):
        states.append(states[-1] * decay + kv)
    st_ref[...] = states[-1]

    for sl, (_, _, _, vb), (_, _, _, q_all, _), a, st in zip(chunks, gates, factors, scores, states):
        o = _dot(a, vb) + _dot_nt(q_all, st.astype(BF16))
        o = o * lax.rsqrt(jnp.mean(o * o, axis=-1, keepdims=True) + RMS_EPS)
        gv = g_ref[0, sl, :]
        o_ref[0, sl, :] = (o * ng * (gv * _sigmoid(gv))).astype(o_ref.dtype)


def _hgrn_mixer(proj, lb_param, norm_g, layer_j, tb=2048):
    bsz, seq, _ = proj.shape
    tb = min(tb, seq)
    n_layers = lb_param.shape[0]

    def col_block(off):
        return pl.BlockSpec((1, tb, HEAD_DIM), lambda b, h, t: (b, t, off + h))

    return pl.pallas_call(
        functools.partial(_hgrn_body, layer_j=layer_j, tb=tb),
        grid=(bsz, MIX_HEADS, seq // tb),
        in_specs=[
            col_block(0), col_block(MIX_HEADS), col_block(2 * MIX_HEADS), col_block(3 * MIX_HEADS),
            pl.BlockSpec((n_layers, HEAD_DIM), lambda b, h, t: (0, h)),
            pl.BlockSpec((1, HEAD_DIM), lambda b, h, t: (0, h)),
        ],
        out_specs=pl.BlockSpec((1, tb, HEAD_DIM), lambda b, h, t: (b, t, h)),
        out_shape=jax.ShapeDtypeStruct((bsz, seq, MIX_HEADS * HEAD_DIM), BF16),
        scratch_shapes=[pltpu.VMEM((HEAD_DIM, HEAD_DIM), F32)],
        compiler_params=_cparams("parallel", "parallel", "arbitrary"),
        name="hgrn2_mixer",
    )(proj, proj, proj, proj, lb_param, norm_g.reshape(1, -1))


def _log_sigmoid(z):
    return jnp.minimum(z, 0.0) - jnp.log1p(jnp.exp(-jnp.abs(z)))


def _fox_gate_body(x_ref, w_ref, wt_ref, bc_ref, br_ref, ccol_ref, crow_ref, carc_ref, carr_ref, *, tb):
    @pl.when(pl.program_id(1) == 0)
    def _():
        carc_ref[...] = jnp.zeros_like(carc_ref)
        carr_ref[...] = jnp.zeros_like(carr_ref)

    xb = x_ref[0].astype(BF16)
    ls_col = LOG2E * _log_sigmoid(_dot(xb, w_ref[...]) + bc_ref[...])
    ls_row = LOG2E * _log_sigmoid(_dot_nt(wt_ref[...], xb) + br_ref[...])
    row = lax.broadcasted_iota(I32, (tb, tb), 0)
    col = lax.broadcasted_iota(I32, (tb, tb), 1)
    lower = jnp.where(col <= row, 1.0, 0.0).astype(BF16)
    upper = jnp.where(row <= col, 1.0, 0.0).astype(BF16)
    c_col = carc_ref[...]
    for piece in _split3(ls_col):
        c_col = c_col + _dot(lower, piece)
    c_row = carr_ref[...]
    for piece in _split3(ls_row):
        c_row = c_row + _dot(piece, upper)
    ccol_ref[0] = c_col
    crow_ref[0] = c_row
    carc_ref[...] = c_col[tb - 1:tb, :]
    carr_ref[...] = c_row[:, tb - 1:tb]


def _fox_gate_cumsum(x, w_f, b_f, tb=512):
    bsz, seq, d = x.shape
    h = w_f.shape[1]
    tb = min(tb, seq)
    return pl.pallas_call(
        functools.partial(_fox_gate_body, tb=tb),
        grid=(bsz, seq // tb),
        in_specs=[
            pl.BlockSpec((1, tb, d), lambda b, t: (b, t, 0)),
            pl.BlockSpec((d, h), lambda b, t: (0, 0)),
            pl.BlockSpec((h, d), lambda b, t: (0, 0)),
            pl.BlockSpec((1, h), lambda b, t: (0, 0)),
            pl.BlockSpec((h, 1), lambda b, t: (0, 0)),
        ],
        out_specs=[
            pl.BlockSpec((1, tb, h), lambda b, t: (b, t, 0)),
            pl.BlockSpec((1, h, tb), lambda b, t: (b, 0, t)),
        ],
        out_shape=[
            jax.ShapeDtypeStruct((bsz, seq, h), F32),
            jax.ShapeDtypeStruct((bsz, h, seq), F32),
        ],
        scratch_shapes=[pltpu.VMEM((1, h), F32), pltpu.VMEM((h, 1), F32)],
        compiler_params=_cparams("parallel", "arbitrary"),
        name="fox_gate_cumsum",
    )(x, w_f.astype(BF16), w_f.T.astype(BF16), b_f.reshape(1, h), b_f.reshape(h, 1))


ATTN_ROWS = 256


def _softmax_probs(u, shift, rows, m_ref, l_ref):
    reps = u.shape[1] // HEAD_DIM
    m_old = m_ref[rows, :]
    if shift is None:
        m_new = jnp.maximum(m_old, jnp.max(u, axis=-1, keepdims=True))
        p = jnp.exp2(u - jnp.tile(m_new, (1, reps)))
    else:
        m_new = jnp.maximum(m_old, jnp.max(u, axis=-1, keepdims=True) + shift)
        p = jnp.exp2(u - jnp.tile(m_new - shift, (1, reps)))
    alpha = jnp.exp2(m_old - m_new)
    l_ref[rows, :] = alpha * l_ref[rows, :] + jnp.sum(p, axis=-1, keepdims=True)
    m_ref[rows, :] = m_new
    return p.astype(BF16), alpha


def _accumulate(p, alpha, v, rows, acc_ref):
    acc_ref[rows, :] = alpha * acc_ref[rows, :] + _dot(p, v)


def _softmax_init(m_ref, l_ref, acc_ref):
    m_ref[...] = jnp.full(m_ref.shape, NEG, F32)
    l_ref[...] = jnp.zeros_like(l_ref)
    acc_ref[...] = jnp.zeros_like(acc_ref)


def _fox_attn_body(q_ref, k_ref, v_ref, ccol_ref, crow_ref, o_ref, ct_ref, m_ref, l_ref, acc_ref, s_ref, *, tq):
    rs = min(ATTN_ROWS, tq)
    n_groups = tq // rs
    h = pl.program_id(1)
    qi = pl.program_id(2)
    ccol = ccol_ref[0]
    lane = lax.broadcasted_iota(I32, ccol.shape, 1)
    c_t = jnp.sum(jnp.where(lane == h, ccol, 0.0), axis=1, keepdims=True)
    ct_ref[...] = jnp.broadcast_to(c_t, ct_ref.shape)
    _softmax_init(m_ref, l_ref, acc_ref)

    def keys(kb):
        return pl.ds(pl.multiple_of(kb * tq, tq), tq)

    def logits(kb, r):
        rows = pl.ds(r * rs, rs)
        ks = keys(kb)
        s_ref[kb % 2, rows, :] = _dot_nt(q_ref[0, rows, :], k_ref[0, ks, :]) - crow_ref[0, pl.ds(h, 1), ks]

    def attend(kb, r, diagonal, prefetch):
        rows = pl.ds(r * rs, rs)
        u = s_ref[kb % 2, rows, :]
        if diagonal:
            row = lax.broadcasted_iota(I32, (rs, tq), 0) + r * rs
            col = lax.broadcasted_iota(I32, (rs, tq), 1)
            u = jnp.where(col <= row, u, NEG)
        if prefetch:
            logits(kb + 1, r)
        p, alpha = _softmax_probs(u, ct_ref[rows, :], rows, m_ref, l_ref)
        _accumulate(p, alpha, v_ref[0, keys(kb), :], rows, acc_ref)

    for r in range(n_groups):
        logits(0, r)

    def full_block(kb, carry):
        for r in range(n_groups):
            attend(kb, r, False, True)
        return carry

    def block_pair(i, carry):
        return full_block(2 * i + 1, full_block(2 * i, carry))

    lax.fori_loop(0, qi // 2, block_pair, 0)

    @pl.when(qi % 2 == 1)
    def _():
        full_block(qi - 1, 0)

    for r in range(n_groups):
        attend(qi, r, True, False)
    o_ref[0] = (acc_ref[...] / l_ref[...]).astype(o_ref.dtype)


def _fox_attention(qkv, c_col, c_row, tq=512):
    bsz, seq, _ = qkv.shape
    tq = min(tq, seq)
    nh = MIX_HEADS
    return pl.pallas_call(
        functools.partial(_fox_attn_body, tq=tq),
        grid=(bsz, nh, seq // tq),
        in_specs=[
            pl.BlockSpec((1, tq, HEAD_DIM), lambda b, h, i: (b, i, h)),
            pl.BlockSpec((1, seq, HEAD_DIM), lambda b, h, i: (b, 0, nh + h)),
            pl.BlockSpec((1, seq, HEAD_DIM), lambda b, h, i: (b, 0, 2 * nh + h)),
            pl.BlockSpec((1, tq, nh), lambda b, h, i: (b, i, 0)),
            pl.BlockSpec((1, nh, seq), lambda b, h, i: (b, 0, 0)),
        ],
        out_specs=pl.BlockSpec((1, tq, HEAD_DIM), lambda b, h, i: (b, i, h)),
        out_shape=jax.ShapeDtypeStruct((bsz, seq, nh * HEAD_DIM), BF16),
        scratch_shapes=[pltpu.VMEM((tq, HEAD_DIM), F32)] * 4 + [pltpu.VMEM((2, tq, tq), F32)],
        compiler_params=_cparams("parallel", "parallel", "arbitrary"),
        name="fox_attention",
    )(qkv, qkv, qkv, c_col, c_row)


def _t5_bucket_thresholds():
    n = np.arange(0, 4 * REL_MAX_DIST, dtype=np.int64)
    max_exact = REL_BUCKETS // 2
    ratio = np.log(np.maximum(n, 1).astype(np.float32) / np.float32(max_exact)) / np.float32(
        math.log(REL_MAX_DIST / max_exact))
    large = np.minimum(max_exact + (ratio * np.float32(REL_BUCKETS - max_exact)).astype(np.int32),
                       REL_BUCKETS - 1)
    bucket = np.where(n < max_exact, n, large)
    assert np.all(np.diff(bucket) >= 0) and bucket[-1] == REL_BUCKETS - 1
    return [int(np.argmax(bucket >= b)) for b in range(REL_BUCKETS)]


def _moba_body(rb_ref, q_ref, k_ref, v_ref, o_ref, kmean_ref, bias_ref, sel_ref, m_ref, l_ref, acc_ref,
               s_ref, *, n_blocks):
    blk = MOBA_BLOCK
    tq = 2 * blk
    h = pl.program_id(1)
    ti = pl.program_id(2)
    far_bias = LOG2E * rb_ref[REL_BUCKETS - 1, h]

    @pl.when(ti == 0)
    def _():
        r = lax.broadcasted_iota(I32, (n_blocks, n_blocks * blk), 0)
        c = lax.broadcasted_iota(I32, (n_blocks, n_blocks * blk), 1)
        pool = jnp.where(c // blk == r, 1.0, 0.0).astype(BF16)
        kmean_ref[...] = _dot(pool, k_ref[0]) * (1.0 / blk)
        row = lax.broadcasted_iota(I32, (blk, blk), 0)
        col = lax.broadcasted_iota(I32, (blk, blk), 1)
        thresholds = _t5_bucket_thresholds()
        for slot, dist in ((0, row - col + blk), (1, row - col)):
            val = jnp.full((blk, blk), rb_ref[0, h], F32)
            for b in range(1, REL_BUCKETS):
                val = jnp.where(dist >= thresholds[b], rb_ref[b, h], val)
            val = LOG2E * val
            if slot == 1:
                val = jnp.where(dist >= 0, val, NEG)
            bias_ref[slot] = val

    q = q_ref[0]
    km_hi = kmean_ref[...].astype(BF16)
    km_lo = (kmean_ref[...] - km_hi.astype(F32)).astype(BF16)
    gate = _dot_nt(km_hi, q) + _dot_nt(km_lo, q)
    blk_id = lax.broadcasted_iota(I32, gate.shape, 0)
    own = 2 * ti + lax.broadcasted_iota(I32, gate.shape, 1) // blk
    gate = jnp.where(blk_id < own, gate, NEG)
    sel = jnp.zeros(gate.shape, F32)
    for _ in range(MOBA_TOPK):
        best = jnp.max(gate, axis=0, keepdims=True)
        first = jnp.min(jnp.where(gate == best, blk_id, n_blocks), axis=0, keepdims=True)
        pick = jnp.logical_and(blk_id == first, best > 0.5 * NEG)
        sel = jnp.where(pick, 1.0, sel)
        gate = jnp.where(blk_id == first, NEG, gate)
    sel_ref[...] = jnp.concatenate([sel, jnp.zeros((HEAD_DIM - n_blocks, tq), F32)], axis=0).T

    _softmax_init(m_ref, l_ref, acc_ref)
    groups = (pl.ds(0, blk), pl.ds(blk, blk))

    def selected(j, rows):
        lanes = lax.broadcasted_iota(I32, (blk, HEAD_DIM), 1)
        return jnp.sum(jnp.where(lanes == j, sel_ref[rows, :], 0.0), axis=1, keepdims=True) > 0.5

    def far_term(j, rows):
        return jnp.where(selected(j, rows), far_bias, NEG)

    def prev_term(j, rows):
        return bias_ref[0] + jnp.where(selected(j, rows), 0.0, NEG)

    def keys(unit, n_keys=tq):
        return pl.ds(pl.multiple_of(unit * tq, tq), n_keys)

    def logits(unit, rows):
        s_ref[unit % 2, rows, :] = _dot_nt(q_ref[0, rows, :], k_ref[0, keys(unit), :])

    def attend(unit, rows, term_a, term_b, prefetch):
        u = s_ref[unit % 2, rows, :]
        if term_b is None:
            u = u[:, :blk] + term_a
            ks = keys(unit, blk)
        else:
            u = jnp.concatenate([u[:, :blk] + term_a, u[:, blk:] + term_b], axis=1)
            ks = keys(unit)
        if prefetch:
            logits(unit + 1, rows)
        p, alpha = _softmax_probs(u, None, rows, m_ref, l_ref)
        _accumulate(p, alpha, v_ref[0, ks, :], rows, acc_ref)

    for rows in groups:
        logits(0, rows)

    def far_unit(unit, carry):
        for rows in groups:
            attend(unit, rows, far_term(2 * unit, rows), far_term(2 * unit + 1, rows), True)
        return carry

    def unit_pair(i, carry):
        return far_unit(2 * i + 1, far_unit(2 * i, carry))

    n_far_units = jnp.maximum(ti - 1, 0)
    lax.fori_loop(0, n_far_units // 2, unit_pair, 0)

    @pl.when(n_far_units % 2 == 1)
    def _():
        far_unit(n_far_units - 1, 0)

    @pl.when(ti >= 1)
    def _():
        j = 2 * ti - 2
        attend(ti - 1, groups[0], far_term(j, groups[0]), prev_term(j + 1, groups[0]), True)
        attend(ti - 1, groups[1], far_term(j, groups[1]), far_term(j + 1, groups[1]), True)

    attend(ti, groups[0], bias_ref[1], None, False)
    attend(ti, groups[1], prev_term(2 * ti, groups[1]), bias_ref[1], False)
    o_ref[0] = (acc_ref[...] / l_ref[...]).astype(o_ref.dtype)


def _moba_attention(qkv, rel_bias):
    bsz, seq, _ = qkv.shape
    nh = MIX_HEADS
    blk = MOBA_BLOCK
    tq = 2 * blk
    assert seq % tq == 0 and seq // blk <= HEAD_DIM
    n_blocks = seq // blk
    grid_spec = pltpu.PrefetchScalarGridSpec(
        num_scalar_prefetch=0,
        grid=(bsz, nh, seq // tq),
        in_specs=[
            pl.BlockSpec(memory_space=pltpu.SMEM),
            pl.BlockSpec((1, tq, HEAD_DIM), lambda b, h, i: (b, i, h)),
            pl.BlockSpec((1, seq, HEAD_DIM), lambda b, h, i: (b, 0, nh + h)),
            pl.BlockSpec((1, seq, HEAD_DIM), lambda b, h, i: (b, 0, 2 * nh + h)),
        ],
        out_specs=pl.BlockSpec((1, tq, HEAD_DIM), lambda b, h, i: (b, i, h)),
        scratch_shapes=[
            pltpu.VMEM((n_blocks, HEAD_DIM), F32),
            pltpu.VMEM((2, blk, blk), F32),
            pltpu.VMEM((tq, HEAD_DIM), F32),
            pltpu.VMEM((tq, HEAD_DIM), F32), pltpu.VMEM((tq, HEAD_DIM), F32), pltpu.VMEM((tq, HEAD_DIM), F32),
            pltpu.VMEM((2, tq, tq), F32),
        ],
    )
    return pl.pallas_call(
        functools.partial(_moba_body, n_blocks=n_blocks),
        grid_spec=grid_spec,
        out_shape=jax.ShapeDtypeStruct((bsz, seq, nh * HEAD_DIM), BF16),
        compiler_params=_cparams("parallel", "parallel", "arbitrary"),
        name="moba_attention",
    )(rel_bias, qkv, qkv, qkv)


def _mem_attn_body(q_ref, kv_ref, o_ref):
    width = MEM_HEADS * HEAD_DIM
    for hh in range(MEM_HEADS):
        cs = slice(hh * HEAD_DIM, (hh + 1) * HEAD_DIM)
        q = q_ref[0, :, cs].astype(BF16)
        k = kv_ref[0, :, cs]
        v = kv_ref[0, :, width + hh * HEAD_DIM:width + (hh + 1) * HEAD_DIM]
        s = _dot_nt(q, k)
        p = jnp.exp(s - jnp.max(s, axis=-1, keepdims=True))
        o = _dot(p.astype(BF16), v) / jnp.sum(p, axis=-1, keepdims=True)
        o_ref[0, :, cs] = o.astype(o_ref.dtype)


def _memory_attention(proj, q_col_block, kv, tq=512):
    bsz, seq, _ = proj.shape
    tq = min(tq, seq)
    width = MEM_HEADS * HEAD_DIM
    n_mem = kv.shape[1]
    return pl.pallas_call(
        _mem_attn_body,
        grid=(bsz, seq // tq),
        in_specs=[
            pl.BlockSpec((1, tq, width), lambda b, i: (b, i, q_col_block)),
            pl.BlockSpec((1, n_mem, 2 * width), lambda b, i: (b, 0, 0)),
        ],
        out_specs=pl.BlockSpec((1, tq, width), lambda b, i: (b, i, 0)),
        out_shape=jax.ShapeDtypeStruct((bsz, seq, width), BF16),
        compiler_params=_cparams("parallel", "parallel"),
        name="memory_attention",
    )(proj, kv)


def _to_row_tiles(ref, y):
    rows, d = y.shape
    n_tiles = d // HEAD_DIM
    for c in range(n_tiles):
        ref[pl.ds(c, rows, stride=n_tiles), :] = y[:, c * HEAD_DIM:(c + 1) * HEAD_DIM]


def _from_row_tiles(ref, rows, n_tiles, *lead):
    return jnp.concatenate(
        [ref[(*lead, pl.ds(c, rows, stride=n_tiles), slice(None))] for c in range(n_tiles)], axis=1)


def _row_tile(ref, r, n_tiles, *lead):
    return ref.at[(*lead, pl.ds(pl.multiple_of(r * n_tiles, n_tiles), n_tiles))]


def _route(x, wt_ref, b_ref, idx_ref, gate_ref, cnt_ref, run_ref):
    tb = x.shape[0]

    @pl.when(pl.program_id(0) == 0)
    def _():
        run_ref[...] = jnp.zeros_like(run_ref)

    xh = x.astype(BF16)
    xl = (x - xh.astype(F32)).astype(BF16)
    w = wt_ref[...]
    wh = w.astype(BF16)
    wl = (w - wh.astype(F32)).astype(BF16)
    logits = _dot_nt(wh, xh) + _dot_nt(wh, xl) + _dot_nt(wl, xh) + b_ref[...]

    n_e = logits.shape[0]
    eid = lax.broadcasted_iota(I32, logits.shape, 0)
    vals, idxs, hots = [], [], []
    multi = jnp.zeros(logits.shape, F32)
    for _ in range(TOP_K):
        best = jnp.max(logits, axis=0, keepdims=True)
        first = jnp.min(jnp.where(logits == best, eid, n_e), axis=0, keepdims=True)
        hot = eid == first
        vals.append(best)
        idxs.append(first)
        hots.append(hot)
        multi = jnp.where(hot, 1.0, multi)
        logits = jnp.where(hot, -jnp.inf, logits)

    exps = [jnp.exp(v - vals[0]) for v in vals]
    den = exps[0]
    for e in exps[1:]:
        den = den + e

    r = lax.broadcasted_iota(I32, (tb, tb), 0)
    c = lax.broadcasted_iota(I32, (tb, tb), 1)
    before = jnp.where(r < c, 1.0, 0.0).astype(BF16)
    pos = _dot(multi.astype(BF16), before) + run_ref[:, 0:1]
    for kk in range(TOP_K):
        idx_ref[kk:kk + 1, :] = idxs[kk]
        rank = jnp.sum(jnp.where(hots[kk], pos, 0.0), axis=0, keepdims=True)
        idx_ref[TOP_K + kk:TOP_K + kk + 1, :] = rank.astype(I32)
        gate_ref[kk:kk + 1, :] = exps[kk] / den
    gate_ref[TOP_K:2 * TOP_K, :] = jnp.zeros((TOP_K, tb), F32)
    run_ref[...] = run_ref[...] + jnp.sum(multi, axis=1, keepdims=True)
    cnt_ref[...] = run_ref[...]


def _out_ln_route_body(mix_ref, mem_ref, x_ref, w1_ref, w2_ref, g_ref, b_ref, wt_ref, rb_ref,
                       o_ref, ort_ref, idx_ref, gate_ref, cnt_ref, run_ref, *, alpha):
    hproj = _dot(mix_ref[...], w1_ref[...]) + _dot(mem_ref[...], w2_ref[...])
    y = _layer_norm(alpha * x_ref[...] + hproj, g_ref[...], b_ref[...])
    o_ref[...] = y
    _to_row_tiles(ort_ref, y)
    _route(y, wt_ref, rb_ref, idx_ref, gate_ref, cnt_ref, run_ref)


def _out_proj_ln_route(mix, mem_out, x, w_out, g, b, alpha, router_w, router_b, tm=512):
    t, d = x.shape
    tm = min(tm, t)
    wm = mix.shape[1]
    we = mem_out.shape[1]
    n_e = router_w.shape[1]
    return pl.pallas_call(
        functools.partial(_out_ln_route_body, alpha=alpha),
        grid=(t // tm,),
        in_specs=[
            pl.BlockSpec((tm, wm), lambda i: (i, 0)),
            pl.BlockSpec((tm, we), lambda i: (i, 0)),
            pl.BlockSpec((tm, d), lambda i: (i, 0)),
            pl.BlockSpec((wm, d), lambda i: (0, 0)),
            pl.BlockSpec((we, d), lambda i: (0, 0)),
            pl.BlockSpec((1, d), lambda i: (0, 0)),
            pl.BlockSpec((1, d), lambda i: (0, 0)),
            pl.BlockSpec((n_e, d), lambda i: (0, 0)),
            pl.BlockSpec((n_e, 1), lambda i: (0, 0)),
        ],
        out_specs=[
            pl.BlockSpec((tm, d), lambda i: (i, 0)),
            pl.BlockSpec((tm * (d // HEAD_DIM), HEAD_DIM), lambda i: (i, 0)),
            pl.BlockSpec((2 * TOP_K, tm), lambda i: (0, i)),
            pl.BlockSpec((2 * TOP_K, tm), lambda i: (0, i)),
            pl.BlockSpec((n_e, HEAD_DIM), lambda i: (0, 0)),
        ],
        out_shape=[
            jax.ShapeDtypeStruct((t, d), F32),
            jax.ShapeDtypeStruct((t * (d // HEAD_DIM), HEAD_DIM), F32),
            jax.ShapeDtypeStruct((2 * TOP_K, t), I32),
            jax.ShapeDtypeStruct((2 * TOP_K, t), F32),
            jax.ShapeDtypeStruct((n_e, HEAD_DIM), F32),
        ],
        scratch_shapes=[pltpu.VMEM((n_e, HEAD_DIM), F32)],
        compiler_params=_cparams("arbitrary"),
        name="out_proj_ln_route",
    )(mix, mem_out, x, w_out[:wm].astype(BF16), w_out[wm:].astype(BF16), g.reshape(1, d), b.reshape(1, d),
      router_w.T, router_b.reshape(n_e, 1))


def _dispatch_body(dest_ref, pad_lo_ref, pad_hi_ref, nu_ref, x_ref, xs_hbm, zero_ref, sem, zsem,
                   *, tb, n_tok, n_tiles, n_experts, n_blocks):
    base = pl.program_id(0) * tb

    def issue(t, carry):
        for kk in range(TOP_K):
            d = dest_ref[kk * n_tok + base + t]
            pltpu.make_async_copy(_row_tile(x_ref, t, n_tiles), _row_tile(xs_hbm, d, n_tiles), sem).start(
                priority=kk % 2)
        return carry

    lax.fori_loop(0, tb, issue, 0, unroll=DMA_UNROLL)

    @pl.when(pl.program_id(0) == 0)
    def _():
        zero_ref[...] = jnp.zeros_like(zero_ref)

        def pad_row(r):
            return pltpu.make_async_copy(zero_ref.at[pl.ds(0, n_tiles)], _row_tile(xs_hbm, r, n_tiles), zsem)

        def spare_block(blk):
            rows = pl.ds(pl.multiple_of(blk * MOE_BLOCK * n_tiles, MOE_BLOCK * n_tiles), MOE_BLOCK * n_tiles)
            return pltpu.make_async_copy(zero_ref, xs_hbm.at[rows], zsem)

        def for_all(action):
            for e in range(n_experts):
                lax.fori_loop(pad_lo_ref[e], pad_hi_ref[e], lambda r, c: (action(pad_row(r)), c)[1], 0)
            lax.fori_loop(nu_ref[0], n_blocks, lambda blk, c: (action(spare_block(blk)), c)[1], 0)

        for_all(lambda cp: cp.start())
        for_all(lambda cp: cp.wait())

    for kk in range(TOP_K):
        pltpu.make_async_copy(x_ref, xs_hbm.at[pl.ds(0, tb * n_tiles)], sem).wait()


def _dispatch(x, n_tiles, dest_flat, pad_lo, pad_hi, n_used, n_rows, tb=512):
    t = x.shape[0] // n_tiles
    tb = min(tb, t)
    grid_spec = pltpu.PrefetchScalarGridSpec(
        num_scalar_prefetch=4,
        grid=(t // tb,),
        in_specs=[pl.BlockSpec((tb * n_tiles, HEAD_DIM), lambda i, *_: (i, 0))],
        out_specs=pl.BlockSpec(memory_space=pl.ANY),
        scratch_shapes=[pltpu.VMEM((MOE_BLOCK * n_tiles, HEAD_DIM), x.dtype),
                        pltpu.SemaphoreType.DMA(()), pltpu.SemaphoreType.DMA(())],
    )
    return pl.pallas_call(
        functools.partial(_dispatch_body, tb=tb, n_tok=t, n_tiles=n_tiles, n_experts=pad_lo.shape[0],
                          n_blocks=n_rows // MOE_BLOCK),
        grid_spec=grid_spec,
        out_shape=jax.ShapeDtypeStruct((n_rows * n_tiles, HEAD_DIM), x.dtype),
        compiler_params=_cparams("arbitrary"),
        name="moe_dispatch",
    )(dest_flat, pad_lo, pad_hi, n_used, x)


UNZIP = 256


def _expert_body(be_ref, nu_ref, first_ref, slot_ref, next_ref, x_ref, wu_hbm, bg_ref, bl_ref, wd_hbm, bd_ref,
                 o_ref, wu_buf, wd_buf, wg_s, wl_s, wd_s, sem, *, layer, n_tiles):
    i = pl.program_id(0)
    n_used = nu_ref[0]

    def weight_copies(expert, slot):
        return (pltpu.make_async_copy(wu_hbm.at[layer, expert], wu_buf.at[slot], sem.at[0, slot]),
                pltpu.make_async_copy(wd_hbm.at[layer, expert], wd_buf.at[slot], sem.at[1, slot]))

    @pl.when(i == 0)
    def _():
        for cp in weight_copies(be_ref[0], 0):
            cp.start()

    @pl.when(jnp.logical_and(first_ref[i] == 1, i < n_used))
    def _():
        slot = slot_ref[i]
        for cp in weight_copies(be_ref[i], slot):
            cp.wait()

        @pl.when(next_ref[i] >= 0)
        def _():
            for cp in weight_copies(next_ref[i], 1 - slot):
                cp.start()

        half = UNZIP // 2
        r = lax.broadcasted_iota(I32, (UNZIP, UNZIP), 0)
        c = lax.broadcasted_iota(I32, (UNZIP, UNZIP), 1)
        src = jnp.where(c < half, 2 * c, 2 * (c - half) + 1)
        perm = jnp.where(r == src, 1.0, 0.0).astype(BF16)
        for g in range(wu_buf.shape[2] // UNZIP):
            w = wu_buf[slot, :, g * UNZIP:(g + 1) * UNZIP].astype(BF16)
            sep = _dot(w, perm).astype(BF16)
            wg_s[:, g * half:(g + 1) * half] = sep[:, :half]
            wl_s[:, g * half:(g + 1) * half] = sep[:, half:]
        wd_s[...] = wd_buf[slot].astype(BF16)

    @pl.when(i < n_used)
    def _():
        xb = _from_row_tiles(x_ref, MOE_BLOCK, n_tiles).astype(BF16)
        glu = jnp.minimum(_dot(xb, wg_s[...]) + bg_ref[0], SWIGLU_LIMIT)
        lin = jnp.clip(_dot(xb, wl_s[...]) + bl_ref[0], -SWIGLU_LIMIT, SWIGLU_LIMIT)
        act = glu * _sigmoid(SWIGLU_ALPHA * glu) * (lin + 1.0)
        _to_row_tiles(o_ref, _dot(act.astype(BF16), wd_s[...]) + bd_ref[0])

    @pl.when(i >= n_used)
    def _():
        o_ref[...] = jnp.zeros_like(o_ref)


def _expert_mlp(xs, n_tiles, blk_expert, n_used, counts, layer, w_up, bg, bl, w_down, bd):
    p_rows = xs.shape[0] // n_tiles
    d = n_tiles * HEAD_DIM
    f = w_down.shape[2]
    n_e = w_down.shape[1]
    n_blk = p_rows // MOE_BLOCK

    blocks = jnp.arange(n_blk, dtype=I32)
    valid = blocks < n_used[0]
    prev_e = jnp.concatenate([jnp.full((1,), -1, I32), blk_expert[:-1]])
    first = jnp.logical_and(valid, blk_expert != prev_e)
    slot = (jnp.cumsum(first.astype(I32)) - 1) % 2
    experts = jnp.arange(n_e, dtype=I32)
    later = jnp.logical_and(experts[None, :] > blk_expert[:, None], counts[None, :] > 0)
    nxt = jnp.min(jnp.where(later, experts[None, :], n_e), axis=1)
    nxt = jnp.where(nxt == n_e, -1, nxt)

    def blk(i, nu):
        return jnp.minimum(i, nu[0] - 1)

    def w_spec(shape):
        return pl.BlockSpec((1,) + shape, lambda i, be, nu, fi, sl, nx: (be[blk(i, nu)], 0, 0))

    grid_spec = pltpu.PrefetchScalarGridSpec(
        num_scalar_prefetch=5,
        grid=(n_blk,),
        in_specs=[
            pl.BlockSpec((MOE_BLOCK * n_tiles, HEAD_DIM), lambda i, be, nu, fi, sl, nx: (blk(i, nu), 0)),
            pl.BlockSpec(memory_space=pl.ANY), w_spec((1, f)), w_spec((1, f)),
            pl.BlockSpec(memory_space=pl.ANY), w_spec((1, d)),
        ],
        out_specs=pl.BlockSpec((MOE_BLOCK * n_tiles, HEAD_DIM), lambda i, be, nu, fi, sl, nx: (i, 0)),
        scratch_shapes=[
            pltpu.VMEM((2, d, 2 * f), F32), pltpu.VMEM((2, f, d), F32),
            pltpu.VMEM((d, f), BF16), pltpu.VMEM((d, f), BF16), pltpu.VMEM((f, d), BF16),
            pltpu.SemaphoreType.DMA((2, 2)),
        ],
    )
    return pl.pallas_call(
        functools.partial(_expert_body, layer=layer, n_tiles=n_tiles),
        grid_spec=grid_spec,
        out_shape=jax.ShapeDtypeStruct(xs.shape, F32),
        compiler_params=_cparams("arbitrary"),
        name="moe_experts",
    )(blk_expert, n_used, first.astype(I32), slot.astype(I32), nxt.astype(I32), xs, w_up, bg, bl, w_down, bd)


def _combine_body(dest_ref, ys_hbm, gate_ref, x_ref, g_ref, b_ref, o_ref, buf_ref, sem_ref,
                  *, tb, n_tok, n_tiles, alpha):
    i = pl.program_id(0)
    n_steps = pl.num_programs(0)

    def issue(step, slot):
        base = step * tb

        def one(t, carry):
            for kk in range(TOP_K):
                d = dest_ref[kk * n_tok + base + t]
                pltpu.make_async_copy(_row_tile(ys_hbm, d, n_tiles), _row_tile(buf_ref, t, n_tiles, slot, kk),
                                      sem_ref.at[slot]).start(priority=kk % 2)
            return carry

        lax.fori_loop(0, tb, one, 0, unroll=DMA_UNROLL)

    @pl.when(i == 0)
    def _():
        issue(0, 0)

    slot = lax.rem(i, 2)

    @pl.when(i + 1 < n_steps)
    def _():
        issue(i + 1, 1 - slot)

    for kk in range(TOP_K):
        pltpu.make_async_copy(ys_hbm.at[pl.ds(0, tb * n_tiles)], buf_ref.at[slot, kk], sem_ref.at[slot]).wait()

    g8 = gate_ref[...]
    gpad = jnp.concatenate([g8, jnp.zeros((HEAD_DIM - g8.shape[0], tb), F32)], axis=0)
    gt = gpad.T
    f = gt[:, 0:1] * _from_row_tiles(buf_ref, tb, n_tiles, slot, 0)
    for kk in range(1, TOP_K):
        f = f + gt[:, kk:kk + 1] * _from_row_tiles(buf_ref, tb, n_tiles, slot, kk)
    o_ref[...] = _layer_norm(alpha * x_ref[...] + f, g_ref[...], b_ref[...])


def _combine_ln(ys, dest_flat, gates, x, g, b, alpha, tb=256):
    t, d = x.shape
    tb = min(tb, t)
    n_tiles = d // HEAD_DIM
    grid_spec = pltpu.PrefetchScalarGridSpec(
        num_scalar_prefetch=1,
        grid=(t // tb,),
        in_specs=[
            pl.BlockSpec(memory_space=pl.ANY),
            pl.BlockSpec((2 * TOP_K, tb), lambda i, dest: (0, i)),
            pl.BlockSpec((tb, d), lambda i, dest: (i, 0)),
            pl.BlockSpec((1, d), lambda i, dest: (0, 0)),
            pl.BlockSpec((1, d), lambda i, dest: (0, 0)),
        ],
        out_specs=pl.BlockSpec((tb, d), lambda i, dest: (i, 0)),
        scratch_shapes=[pltpu.VMEM((2, TOP_K, tb * n_tiles, HEAD_DIM), F32), pltpu.SemaphoreType.DMA((2,))],
    )
    return pl.pallas_call(
        functools.partial(_combine_body, tb=tb, n_tok=t, n_tiles=n_tiles, alpha=alpha),
        grid_spec=grid_spec,
        out_shape=jax.ShapeDtypeStruct((t, d), F32),
        compiler_params=_cparams("arbitrary"),
        name="moe_combine_ln",
    )(dest_flat, ys, gates, x, g.reshape(1, d), b.reshape(1, d))


def _moe_ln(x, x_tiles, idx_rank, gates, cnt, layer, w_up, b_up, w_down, b_down, g, b, alpha):
    t, d = x.shape
    n_tiles = d // HEAD_DIM
    n_e = cnt.shape[0]
    counts = cnt[:, 0].astype(I32)
    padded = ((counts + MOE_BLOCK - 1) // MOE_BLOCK) * MOE_BLOCK
    p_ends = jnp.cumsum(padded)
    p_starts = p_ends - padded
    n_blk = (t * TOP_K + MOE_BLOCK - 1) // MOE_BLOCK + n_e
    experts = jnp.arange(n_e, dtype=I32)
    start_of = jnp.sum(jnp.where(idx_rank[:TOP_K, :, None] == experts, p_starts, 0), axis=-1)
    dest_flat = (start_of + idx_rank[TOP_K:]).reshape(-1)
    blk_first_row = jnp.arange(n_blk, dtype=I32) * MOE_BLOCK
    blk_expert = jnp.clip(jnp.sum((p_ends[None, :] <= blk_first_row[:, None]).astype(I32), axis=1), 0, n_e - 1)
    n_used = (p_ends[-1:] // MOE_BLOCK).astype(I32)

    xs = _dispatch(x_tiles, n_tiles, dest_flat, p_starts + counts, p_ends, n_used, n_blk * MOE_BLOCK)
    bg = b_up[:, None, 0::2]
    bl = b_up[:, None, 1::2]
    ys = _expert_mlp(xs, n_tiles, blk_expert, n_used, counts, layer, w_up, bg, bl, w_down, b_down[:, None, :])
    return _combine_ln(ys, dest_flat, gates, x, g, b, alpha)


def kernel(x, mem, hgrn_w_in, hgrn_lb, hgrn_norm_g, fox_w_in, fox_b_f, moba_w_in, rel_bias, w_mem_kv, w_out,
           ln1_g, ln1_b, router_w, router_b, w_up, b_up, w_down, b_down, ln2_g, ln2_b):
    bsz, seq, d = x.shape
    depth = w_out.shape[0]
    t = bsz * seq
    mix_w = MIX_HEADS * HEAD_DIM
    mem_w = MEM_HEADS * HEAD_DIM
    alpha = (2 * depth) ** 0.25
    scale = HEAD_DIM ** -0.5
    n_mem = mem.shape[1]
    mem2 = mem.reshape(bsz * n_mem, d)

    def scale_vec(n_mixer_q, n_plain, n_mem_q):
        return jnp.concatenate([
            jnp.full((1, n_mixer_q), scale * LOG2E, F32), jnp.ones((1, n_plain), F32),
            jnp.full((1, n_mem_q), scale, F32)], axis=1)

    x2 = x.reshape(t, d)
    for i in range(depth):
        kind, j = i % 3, i // 3
        if kind == 0:
            w_in = hgrn_w_in[j].astype(BF16)
            proj = _matmul(x2, w_in, scale_vec(0, 4 * mix_w, mem_w), F32, 1024, w_in.shape[1] // 3)
            proj = proj.reshape(bsz, seq, -1)
            mix = _hgrn_mixer(proj, hgrn_lb, hgrn_norm_g[j], j)
            memq_block = 4 * mix_w // mem_w
        elif kind == 1:
            w = fox_w_in[j]
            w_in = jnp.concatenate([w[:, :3 * mix_w], w[:, 3 * mix_w + MIX_HEADS:]], axis=1).astype(BF16)
            proj = _matmul(x2, w_in, scale_vec(mix_w, 2 * mix_w, mem_w), BF16, 1024, w_in.shape[1] // 2)
            proj = proj.reshape(bsz, seq, -1)
            c_col, c_row = _fox_gate_cumsum(x2.reshape(bsz, seq, d), w[:, 3 * mix_w:3 * mix_w + MIX_HEADS],
                                            fox_b_f[j])
            mix = _fox_attention(proj, c_col, c_row)
            memq_block = 3 * mix_w // mem_w
        else:
            w_in = moba_w_in[j].astype(BF16)
            proj = _matmul(x2, w_in, scale_vec(mix_w, 2 * mix_w, mem_w), BF16, 1024, w_in.shape[1] // 2)
            proj = proj.reshape(bsz, seq, -1)
            mix = _moba_attention(proj, rel_bias)
            memq_block = 3 * mix_w // mem_w
        kv = _matmul(mem2, w_mem_kv[i].astype(BF16), jnp.ones((1, 2 * mem_w), F32), BF16, 512, 512)
        mem_out = _memory_attention(proj, memq_block, kv.reshape(bsz, n_mem, 2 * mem_w))
        x2, x2_tiles, idx_rank, gates, cnt = _out_proj_ln_route(
            mix.reshape(t, mix_w), mem_out.reshape(t, mem_w), x2, w_out[i], ln1_g[i], ln1_b[i], alpha,
            router_w[i], router_b[i])
        x2 = _moe_ln(x2, x2_tiles, idx_rank, gates, cnt, i, w_up, b_up[i], w_down, b_down[i],
                     ln2_g[i], ln2_b[i], alpha)
    return x2.reshape(bsz, seq, d)
```

```python
import functools
import math

import numpy as np
import jax
import jax.numpy as jnp
from jax import lax
from jax.experimental import pallas as pl
from jax.experimental.pallas import tpu as pltpu

F32 = jnp.float32
BF16 = jnp.bfloat16
I32 = jnp.int32

HEAD_DIM = 128
MIX_HEADS = 8
MEM_HEADS = 4
HGRN_CHUNK = 64
HGRN_SUB = 16
HGRN_EXP_CLAMP = 80.0
MOBA_BLOCK = 256
MOBA_TOPK = 3
REL_BUCKETS = 32
REL_MAX_DIST = 128
N_EXPERTS = 32
TOP_K = 4
MOE_BLOCK = 256
SWIGLU_LIMIT = 7.0
SWIGLU_ALPHA = 1.702
LN_EPS = 1e-5
RMS_EPS = 1e-6
NEG = -1e30
LOG2E = 1.4426950408889634
VMEM_LIMIT = 56 * 1024 * 1024
DMA_UNROLL = 8

NT_DIMS = (((1,), (1,)), ((), ()))
TN_DIMS = (((0,), (0,)), ((), ()))


def _cparams(*sem):
    return pltpu.CompilerParams(dimension_semantics=sem, vmem_limit_bytes=VMEM_LIMIT)


def _dot(a, b):
    return jnp.dot(a, b, preferred_element_type=F32)


def _dot_nt(a, b):
    return lax.dot_general(a, b, NT_DIMS, preferred_element_type=F32)


def _dot_tn(a, b):
    return lax.dot_general(a, b, TN_DIMS, preferred_element_type=F32)


def _split3(a):
    hi = a.astype(BF16)
    r = a - hi.astype(F32)
    mid = r.astype(BF16)
    lo = (r - mid.astype(F32)).astype(BF16)
    return hi, mid, lo


def _sigmoid(x):
    return 1.0 / (1.0 + jnp.exp(-x))


def _layer_norm(z, g, b):
    mu = jnp.mean(z, axis=-1, keepdims=True)
    zc = z - mu
    var = jnp.mean(zc * zc, axis=-1, keepdims=True)
    return zc * lax.rsqrt(var + LN_EPS) * g + b


def _mm_body(x_ref, w_ref, s_ref, o_ref, xb_ref):
    @pl.when(pl.program_id(1) == 0)
    def _():
        xb_ref[...] = x_ref[...].astype(BF16)

    acc = _dot(xb_ref[...], w_ref[...])
    o_ref[...] = (acc * s_ref[...]).astype(o_ref.dtype)


def _matmul(x, w, col_scale, out_dtype, tm, tn):
    m, k = x.shape
    n = w.shape[1]
    tm = min(tm, m)
    tn = min(tn, n)
    return pl.pallas_call(
        _mm_body,
        grid=(m // tm, n // tn),
        in_specs=[
            pl.BlockSpec((tm, k), lambda i, j: (i, 0)),
            pl.BlockSpec((k, tn), lambda i, j: (0, j)),
            pl.BlockSpec((1, tn), lambda i, j: (0, j)),
        ],
        out_specs=pl.BlockSpec((tm, tn), lambda i, j: (i, j)),
        out_shape=jax.ShapeDtypeStruct((m, n), out_dtype),
        scratch_shapes=[pltpu.VMEM((tm, k), BF16)],
        compiler_params=_cparams("parallel", "arbitrary"),
        name="proj_matmul",
    )(x, w, col_scale)


def _hgrn_body(q_ref, z_ref, i_ref, g_ref, lbp_ref, ng_ref, o_ref, st_ref, *, layer_j, tb):
    c_len, sb = HGRN_CHUNK, HGRN_SUB
    n_sub = c_len // sb

    @pl.when(pl.program_id(2) == 0)
    def _():
        st_ref[...] = jnp.zeros_like(st_ref)

    lbp = lbp_ref[...]
    ex = jnp.exp(lbp - jnp.max(lbp, axis=0, keepdims=True))
    p = ex / jnp.sum(ex, axis=0, keepdims=True)
    lb = jnp.zeros((1, HEAD_DIM), F32)
    for r in range(1, layer_j + 1):
        lb = lb + p[r:r + 1, :]
    log_lb = jnp.log(lb)
    log1m_lb = jnp.log1p(-lb)
    one_m_lb = 1.0 - lb
    ng = ng_ref[...]

    row = lax.broadcasted_iota(I32, (c_len, c_len), 0)
    col = lax.broadcasted_iota(I32, (c_len, c_len), 1)
    sub_start = (row // sb) * sb
    one = jnp.ones((c_len, c_len), F32)
    zero = jnp.zeros((c_len, c_len), F32)
    m_tri = jnp.where(col <= row, one, zero)
    m_in = jnp.where(col > sub_start, m_tri, zero)
    m_dec = jnp.where(col > row, one, zero)
    m_ke = jnp.where(col <= sub_start + (sb - 1), m_dec, zero)
    stack = jnp.concatenate([m_tri, m_in, m_ke, m_dec], axis=0).astype(BF16)
    same_sub = (row // sb) == (col // sb)
    diag_mask = jnp.logical_and(same_sub, col <= row)
    row_sub = row // sb
    krow_sub = lax.broadcasted_iota(I32, (c_len, HEAD_DIM), 0) // sb

    chunks = [pl.ds(c * c_len, c_len) for c in range(tb // c_len)]

    gates = []
    for sl in chunks:
        z = z_ref[0, sl, :]
        e = jnp.exp(-jnp.abs(z))
        inv = 1.0 / (1.0 + e)
        sig_neg = jnp.where(z >= 0, e, 1.0) * inv
        log_sig = jnp.minimum(z, 0.0) - jnp.log(1.0 + e)
        a2 = log1m_lb + log_sig
        log_f = jnp.maximum(log_lb, a2) + jnp.log(1.0 + jnp.exp(-jnp.abs(log_lb - a2)))
        qv = q_ref[0, sl, :]
        gates.append((log_f, one_m_lb * sig_neg, qv * _sigmoid(qv), i_ref[0, sl, :].astype(BF16)))

    prefix = []
    for log_f, _, _, _ in gates:
        pre = None
        for piece in _split3(log_f):
            t = _dot(stack, piece)
            pre = t if pre is None else pre + t
        prefix.append(pre)

    factors = []
    for (_, k, qf, _), pre in zip(gates, prefix):
        b = pre[0:c_len]
        d_in = pre[c_len:2 * c_len]
        d_ke = pre[2 * c_len:3 * c_len]
        d_dec = pre[3 * c_len:4 * c_len]
        q_in = qf * jnp.exp(d_in)
        k_diag = (k * jnp.exp(jnp.minimum(-d_in, HGRN_EXP_CLAMP))).astype(BF16)
        k_end = k * jnp.exp(d_ke)
        k_dec = (k * jnp.exp(d_dec)).astype(BF16)
        q_all = (qf * jnp.exp(b)).astype(BF16)
        b_ref = b - d_in
        lhs = [q_in.astype(BF16)]
        rhs = [k_diag]
        for j in range(n_sub - 1):
            b_end = b[(j + 1) * sb - 1:(j + 1) * sb, :]
            cross = jnp.exp(jnp.minimum(b_ref - b_end, 0.0))
            lhs.append((q_in * cross).astype(BF16))
            rhs.append(jnp.where(krow_sub == j, k_end, 0.0).astype(BF16))
        factors.append((lhs, rhs, k_dec, q_all, jnp.exp(b[c_len - 1:c_len, :])))

    scores = []
    for lhs, rhs, _, _, _ in factors:
        a = jnp.where(diag_mask, _dot_nt(lhs[0], rhs[0]), 0.0)
        for j in range(n_sub - 1):
            a = a + jnp.where(row_sub > j, _dot_nt(lhs[j + 1], rhs[j + 1]), 0.0)
        scores.append(a.astype(BF16))

    updates = [_dot_tn(vb, k_dec) for (_, _, _, vb), (_, _, k_dec, _, _) in zip(gates, factors)]
    states = [st_ref[...]]
    for (_, _, _, _, decay), kv in zip(factors, updates):
        states.append(states[-1] * decay + kv)
    st_ref[...] = states[-1]

    for sl, (_, _, _, vb), (_, _, _, q_all, _), a, st in zip(chunks, gates, factors, scores, states):
        o = _dot(a, vb) + _dot_nt(q_all, st.astype(BF16))
        o = o * lax.rsqrt(jnp.mean(o * o, axis=-1, keepdims=True) + RMS_EPS)
        gv = g_ref[0, sl, :]
        o_ref[0, sl, :] = (o * ng * (gv * _sigmoid(gv))).astype(o_ref.dtype)


def _hgrn_mixer(proj, lb_param, norm_g, layer_j, tb=2048):
    bsz, seq, _ = proj.shape
    tb = min(tb, seq)
    n_layers = lb_param.shape[0]

    def col_block(off):
        return pl.BlockSpec((1, tb, HEAD_DIM), lambda b, h, t: (b, t, off + h))

    return pl.pallas_call(
        functools.partial(_hgrn_body, layer_j=layer_j, tb=tb),
        grid=(bsz, MIX_HEADS, seq // tb),
        in_specs=[
            col_block(0), col_block(MIX_HEADS), col_block(2 * MIX_HEADS), col_block(3 * MIX_HEADS),
            pl.BlockSpec((n_layers, HEAD_DIM), lambda b, h, t: (0, h)),
            pl.BlockSpec((1, HEAD_DIM), lambda b, h, t: (0, h)),
        ],
        out_specs=pl.BlockSpec((1, tb, HEAD_DIM), lambda b, h, t: (b, t, h)),
        out_shape=jax.ShapeDtypeStruct((bsz, seq, MIX_HEADS * HEAD_DIM), BF16),
        scratch_shapes=[pltpu.VMEM((HEAD_DIM, HEAD_DIM), F32)],
        compiler_params=_cparams("parallel", "parallel", "arbitrary"),
        name="hgrn2_mixer",
    )(proj, proj, proj, proj, lb_param, norm_g.reshape(1, -1))


def _log_sigmoid(z):
    return jnp.minimum(z, 0.0) - jnp.log1p(jnp.exp(-jnp.abs(z)))


def _fox_gate_body(x_ref, w_ref, wt_ref, bc_ref, br_ref, ccol_ref, crow_ref, carc_ref, carr_ref, *, tb):
    @pl.when(pl.program_id(1) == 0)
    def _():
        carc_ref[...] = jnp.zeros_like(carc_ref)
        carr_ref[...] = jnp.zeros_like(carr_ref)

    xb = x_ref[0].astype(BF16)
    ls_col = LOG2E * _log_sigmoid(_dot(xb, w_ref[...]) + bc_ref[...])
    ls_row = LOG2E * _log_sigmoid(_dot_nt(wt_ref[...], xb) + br_ref[...])
    row = lax.broadcasted_iota(I32, (tb, tb), 0)
    col = lax.broadcasted_iota(I32, (tb, tb), 1)
    lower = jnp.where(col <= row, 1.0, 0.0).astype(BF16)
    upper = jnp.where(row <= col, 1.0, 0.0).astype(BF16)
    c_col = carc_ref[...]
    for piece in _split3(ls_col):
        c_col = c_col + _dot(lower, piece)
    c_row = carr_ref[...]
    for piece in _split3(ls_row):
        c_row = c_row + _dot(piece, upper)
    ccol_ref[0] = c_col
    crow_ref[0] = c_row
    carc_ref[...] = c_col[tb - 1:tb, :]
    carr_ref[...] = c_row[:, tb - 1:tb]


def _fox_gate_cumsum(x, w_f, b_f, tb=512):
    bsz, seq, d = x.shape
    h = w_f.shape[1]
    tb = min(tb, seq)
    return pl.pallas_call(
        functools.partial(_fox_gate_body, tb=tb),
        grid=(bsz, seq // tb),
        in_specs=[
            pl.BlockSpec((1, tb, d), lambda b, t: (b, t, 0)),
            pl.BlockSpec((d, h), lambda b, t: (0, 0)),
            pl.BlockSpec((h, d), lambda b, t: (0, 0)),
            pl.BlockSpec((1, h), lambda b, t: (0, 0)),
            pl.BlockSpec((h, 1), lambda b, t: (0, 0)),
        ],
        out_specs=[
            pl.BlockSpec((1, tb, h), lambda b, t: (b, t, 0)),
            pl.BlockSpec((1, h, tb), lambda b, t: (b, 0, t)),
        ],
        out_shape=[
            jax.ShapeDtypeStruct((bsz, seq, h), F32),
            jax.ShapeDtypeStruct((bsz, h, seq), F32),
        ],
        scratch_shapes=[pltpu.VMEM((1, h), F32), pltpu.VMEM((h, 1), F32)],
        compiler_params=_cparams("parallel", "arbitrary"),
        name="fox_gate_cumsum",
    )(x, w_f.astype(BF16), w_f.T.astype(BF16), b_f.reshape(1, h), b_f.reshape(h, 1))


ATTN_ROWS = 128


def _softmax_probs(u, shift, rows, m_ref, l_ref):
    reps = u.shape[1] // HEAD_DIM
    m_old = m_ref[rows, :]
    if shift is None:
        m_new = jnp.maximum(m_old, jnp.max(u, axis=-1, keepdims=True))
        p = jnp.exp2(u - jnp.tile(m_new, (1, reps)))
    else:
        m_new = jnp.maximum(m_old, jnp.max(u, axis=-1, keepdims=True) + shift)
        p = jnp.exp2(u - jnp.tile(m_new - shift, (1, reps)))
    alpha = jnp.exp2(m_old - m_new)
    l_ref[rows, :] = alpha * l_ref[rows, :] + jnp.sum(p, axis=-1, keepdims=True)
    m_ref[rows, :] = m_new
    return p.astype(BF16), alpha


def _accumulate(p, alpha, v, rows, acc_ref):
    acc_ref[rows, :] = alpha * acc_ref[rows, :] + _dot(p, v)


def _softmax_init(m_ref, l_ref, acc_ref):
    m_ref[...] = jnp.full(m_ref.shape, NEG, F32)
    l_ref[...] = jnp.zeros_like(l_ref)
    acc_ref[...] = jnp.zeros_like(acc_ref)


def _fox_attn_body(q_ref, k_ref, v_ref, ccol_ref, crow_ref, o_ref, ct_ref, m_ref, l_ref, acc_ref, s_ref, *, tq):
    rs = min(ATTN_ROWS, tq)
    n_groups = tq // rs
    h = pl.program_id(1)
    qi = pl.program_id(2)
    ccol = ccol_ref[0]
    lane = lax.broadcasted_iota(I32, ccol.shape, 1)
    c_t = jnp.sum(jnp.where(lane == h, ccol, 0.0), axis=1, keepdims=True)
    ct_ref[...] = jnp.broadcast_to(c_t, ct_ref.shape)
    _softmax_init(m_ref, l_ref, acc_ref)

    def keys(kb):
        return pl.ds(pl.multiple_of(kb * tq, tq), tq)

    def logits(kb, r):
        rows = pl.ds(r * rs, rs)
        ks = keys(kb)
        s_ref[kb % 2, rows, :] = _dot_nt(q_ref[0, rows, :], k_ref[0, ks, :]) - crow_ref[0, pl.ds(h, 1), ks]

    def attend(kb, r, diagonal, prefetch):
        rows = pl.ds(r * rs, rs)
        u = s_ref[kb % 2, rows, :]
        if diagonal:
            row = lax.broadcasted_iota(I32, (rs, tq), 0) + r * rs
            col = lax.broadcasted_iota(I32, (rs, tq), 1)
            u = jnp.where(col <= row, u, NEG)
        if prefetch:
            logits(kb + 1, r)
        p, alpha = _softmax_probs(u, ct_ref[rows, :], rows, m_ref, l_ref)
        _accumulate(p, alpha, v_ref[0, keys(kb), :], rows, acc_ref)

    for r in range(n_groups):
        logits(0, r)

    def full_block(kb, carry):
        for r in range(n_groups):
            attend(kb, r, False, True)
        return carry

    def block_pair(i, carry):
        return full_block(2 * i + 1, full_block(2 * i, carry))

    lax.fori_loop(0, qi // 2, block_pair, 0)

    @pl.when(qi % 2 == 1)
    def _():
        full_block(qi - 1, 0)

    for r in range(n_groups):
        attend(qi, r, True, False)
    o_ref[0] = (acc_ref[...] / l_ref[...]).astype(o_ref.dtype)


def _fox_attention(qkv, c_col, c_row, tq=512):
    bsz, seq, _ = qkv.shape
    tq = min(tq, seq)
    nh = MIX_HEADS
    return pl.pallas_call(
        functools.partial(_fox_attn_body, tq=tq),
        grid=(bsz, nh, seq // tq),
        in_specs=[
            pl.BlockSpec((1, tq, HEAD_DIM), lambda b, h, i: (b, i, h)),
            pl.BlockSpec((1, seq, HEAD_DIM), lambda b, h, i: (b, 0, nh + h)),
            pl.BlockSpec((1, seq, HEAD_DIM), lambda b, h, i: (b, 0, 2 * nh + h)),
            pl.BlockSpec((1, tq, nh), lambda b, h, i: (b, i, 0)),
            pl.BlockSpec((1, nh, seq), lambda b, h, i: (b, 0, 0)),
        ],
        out_specs=pl.BlockSpec((1, tq, HEAD_DIM), lambda b, h, i: (b, i, h)),
        out_shape=jax.ShapeDtypeStruct((bsz, seq, nh * HEAD_DIM), BF16),
        scratch_shapes=[pltpu.VMEM((tq, HEAD_DIM), F32)] * 4 + [pltpu.VMEM((2, tq, tq), F32)],
        compiler_params=_cparams("parallel", "parallel", "arbitrary"),
        name="fox_attention",
    )(qkv, qkv, qkv, c_col, c_row)


def _t5_bucket_thresholds():
    n = np.arange(0, 4 * REL_MAX_DIST, dtype=np.int64)
    max_exact = REL_BUCKETS // 2
    ratio = np.log(np.maximum(n, 1).astype(np.float32) / np.float32(max_exact)) / np.float32(
        math.log(REL_MAX_DIST / max_exact))
    large = np.minimum(max_exact + (ratio * np.float32(REL_BUCKETS - max_exact)).astype(np.int32),
                       REL_BUCKETS - 1)
    bucket = np.where(n < max_exact, n, large)
    assert np.all(np.diff(bucket) >= 0) and bucket[-1] == REL_BUCKETS - 1
    return [int(np.argmax(bucket >= b)) for b in range(REL_BUCKETS)]


def _moba_body(rb_ref, q_ref, k_ref, v_ref, o_ref, kmean_ref, bias_ref, sel_ref, m_ref, l_ref, acc_ref,
               s_ref, *, n_blocks):
    blk = MOBA_BLOCK
    tq = 2 * blk
    h = pl.program_id(1)
    ti = pl.program_id(2)
    far_bias = LOG2E * rb_ref[REL_BUCKETS - 1, h]

    @pl.when(ti == 0)
    def _():
        r = lax.broadcasted_iota(I32, (n_blocks, n_blocks * blk), 0)
        c = lax.broadcasted_iota(I32, (n_blocks, n_blocks * blk), 1)
        pool = jnp.where(c // blk == r, 1.0, 0.0).astype(BF16)
        kmean_ref[...] = _dot(pool, k_ref[0]) * (1.0 / blk)
        row = lax.broadcasted_iota(I32, (blk, blk), 0)
        col = lax.broadcasted_iota(I32, (blk, blk), 1)
        thresholds = _t5_bucket_thresholds()
        for slot, dist in ((0, row - col + blk), (1, row - col)):
            val = jnp.full((blk, blk), rb_ref[0, h], F32)
            for b in range(1, REL_BUCKETS):
                val = jnp.where(dist >= thresholds[b], rb_ref[b, h], val)
            val = LOG2E * val
            if slot == 1:
                val = jnp.where(dist >= 0, val, NEG)
            bias_ref[slot] = val

    q = q_ref[0]
    km_hi = kmean_ref[...].astype(BF16)
    km_lo = (kmean_ref[...] - km_hi.astype(F32)).astype(BF16)
    gate = _dot_nt(km_hi, q) + _dot_nt(km_lo, q)
    blk_id = lax.broadcasted_iota(I32, gate.shape, 0)
    own = 2 * ti + lax.broadcasted_iota(I32, gate.shape, 1) // blk
    gate = jnp.where(blk_id < own, gate, NEG)
    sel = jnp.zeros(gate.shape, F32)
    for _ in range(MOBA_TOPK):
        best = jnp.max(gate, axis=0, keepdims=True)
        first = jnp.min(jnp.where(gate == best, blk_id, n_blocks), axis=0, keepdims=True)
        pick = jnp.logical_and(blk_id == first, best > 0.5 * NEG)
        sel = jnp.where(pick, 1.0, sel)
        gate = jnp.where(blk_id == first, NEG, gate)
    sel_ref[...] = jnp.concatenate([sel, jnp.zeros((HEAD_DIM - n_blocks, tq), F32)], axis=0).T

    _softmax_init(m_ref, l_ref, acc_ref)
    groups = (pl.ds(0, blk), pl.ds(blk, blk))

    def selected(j, rows):
        lanes = lax.broadcasted_iota(I32, (blk, HEAD_DIM), 1)
        return jnp.sum(jnp.where(lanes == j, sel_ref[rows, :], 0.0), axis=1, keepdims=True) > 0.5

    def far_term(j, rows):
        return jnp.where(selected(j, rows), far_bias, NEG)

    def prev_term(j, rows):
        return bias_ref[0] + jnp.where(selected(j, rows), 0.0, NEG)

    def keys(unit, n_keys=tq):
        return pl.ds(pl.multiple_of(unit * tq, tq), n_keys)

    def logits(unit, rows):
        s_ref[unit % 2, rows, :] = _dot_nt(q_ref[0, rows, :], k_ref[0, keys(unit), :])

    def attend(unit, rows, term_a, term_b, prefetch):
        u = s_ref[unit % 2, rows, :]
        if term_b is None:
            u = u[:, :blk] + term_a
            ks = keys(unit, blk)
        else:
            u = jnp.concatenate([u[:, :blk] + term_a, u[:, blk:] + term_b], axis=1)
            ks = keys(unit)
        if prefetch:
            logits(unit + 1, rows)
        p, alpha = _softmax_probs(u, None, rows, m_ref, l_ref)
        _accumulate(p, alpha, v_ref[0, ks, :], rows, acc_ref)

    for rows in groups:
        logits(0, rows)

    def far_unit(unit, carry):
        for rows in groups:
            attend(unit, rows, far_term(2 * unit, rows), far_term(2 * unit + 1, rows), True)
        return carry

    def unit_pair(i, carry):
        return far_unit(2 * i + 1, far_unit(2 * i, carry))

    n_far_units = jnp.maximum(ti - 1, 0)
    lax.fori_loop(0, n_far_units // 2, unit_pair, 0)

    @pl.when(n_far_units % 2 == 1)
    def _():
        far_unit(n_far_units - 1, 0)

    @pl.when(ti >= 1)
    def _():
        j = 2 * ti - 2
        attend(ti - 1, groups[0], far_term(j, groups[0]), prev_term(j + 1, groups[0]), True)
        attend(ti - 1, groups[1], far_term(j, groups[1]), far_term(j + 1, groups[1]), True)

    attend(ti, groups[0], bias_ref[1], None, False)
    attend(ti, groups[1], prev_term(2 * ti, groups[1]), bias_ref[1], False)
    o_ref[0] = (acc_ref[...] / l_ref[...]).astype(o_ref.dtype)


def _moba_attention(qkv, rel_bias):
    bsz, seq, _ = qkv.shape
    nh = MIX_HEADS
    blk = MOBA_BLOCK
    tq = 2 * blk
    assert seq % tq == 0 and seq // blk <= HEAD_DIM
    n_blocks = seq // blk
    grid_spec = pltpu.PrefetchScalarGridSpec(
        num_scalar_prefetch=0,
        grid=(bsz, nh, seq // tq),
        in_specs=[
            pl.BlockSpec(memory_space=pltpu.SMEM),
            pl.BlockSpec((1, tq, HEAD_DIM), lambda b, h, i: (b, i, h)),
            pl.BlockSpec((1, seq, HEAD_DIM), lambda b, h, i: (b, 0, nh + h)),
            pl.BlockSpec((1, seq, HEAD_DIM), lambda b, h, i: (b, 0, 2 * nh + h)),
        ],
        out_specs=pl.BlockSpec((1, tq, HEAD_DIM), lambda b, h, i: (b, i, h)),
        scratch_shapes=[
            pltpu.VMEM((n_blocks, HEAD_DIM), F32),
            pltpu.VMEM((2, blk, blk), F32),
            pltpu.VMEM((tq, HEAD_DIM), F32),
            pltpu.VMEM((tq, HEAD_DIM), F32), pltpu.VMEM((tq, HEAD_DIM), F32), pltpu.VMEM((tq, HEAD_DIM), F32),
            pltpu.VMEM((2, tq, tq), F32),
        ],
    )
    return pl.pallas_call(
        functools.partial(_moba_body, n_blocks=n_blocks),
        grid_spec=grid_spec,
        out_shape=jax.ShapeDtypeStruct((bsz, seq, nh * HEAD_DIM), BF16),
        compiler_params=_cparams("parallel", "parallel", "arbitrary"),
        name="moba_attention",
    )(rel_bias, qkv, qkv, qkv)


def _mem_attn_body(q_ref, kv_ref, o_ref):
    width = MEM_HEADS * HEAD_DIM
    for hh in range(MEM_HEADS):
        cs = slice(hh * HEAD_DIM, (hh + 1) * HEAD_DIM)
        q = q_ref[0, :, cs].astype(BF16)
        k = kv_ref[0, :, cs]
        v = kv_ref[0, :, width + hh * HEAD_DIM:width + (hh + 1) * HEAD_DIM]
        s = _dot_nt(q, k)
        p = jnp.exp(s - jnp.max(s, axis=-1, keepdims=True))
        o = _dot(p.astype(BF16), v) / jnp.sum(p, axis=-1, keepdims=True)
        o_ref[0, :, cs] = o.astype(o_ref.dtype)


def _memory_attention(proj, q_col_block, kv, tq=512):
    bsz, seq, _ = proj.shape
    tq = min(tq, seq)
    width = MEM_HEADS * HEAD_DIM
    n_mem = kv.shape[1]
    return pl.pallas_call(
        _mem_attn_body,
        grid=(bsz, seq // tq),
        in_specs=[
            pl.BlockSpec((1, tq, width), lambda b, i: (b, i, q_col_block)),
            pl.BlockSpec((1, n_mem, 2 * width), lambda b, i: (b, 0, 0)),
        ],
        out_specs=pl.BlockSpec((1, tq, width), lambda b, i: (b, i, 0)),
        out_shape=jax.ShapeDtypeStruct((bsz, seq, width), BF16),
        compiler_params=_cparams("parallel", "parallel"),
        name="memory_attention",
    )(proj, kv)


def _to_row_tiles(ref, y):
    rows, d = y.shape
    n_tiles = d // HEAD_DIM
    for c in range(n_tiles):
        ref[pl.ds(c, rows, stride=n_tiles), :] = y[:, c * HEAD_DIM:(c + 1) * HEAD_DIM]


def _from_row_tiles(ref, rows, n_tiles, *lead):
    return jnp.concatenate(
        [ref[(*lead, pl.ds(c, rows, stride=n_tiles), slice(None))] for c in range(n_tiles)], axis=1)


def _row_tile(ref, r, n_tiles, *lead):
    return ref.at[(*lead, pl.ds(pl.multiple_of(r * n_tiles, n_tiles), n_tiles))]


def _route(x, wt_ref, b_ref, idx_ref, gate_ref, cnt_ref, run_ref):
    tb = x.shape[0]

    @pl.when(pl.program_id(0) == 0)
    def _():
        run_ref[...] = jnp.zeros_like(run_ref)

    xh = x.astype(BF16)
    xl = (x - xh.astype(F32)).astype(BF16)
    w = wt_ref[...]
    wh = w.astype(BF16)
    wl = (w - wh.astype(F32)).astype(BF16)
    logits = _dot_nt(wh, xh) + _dot_nt(wh, xl) + _dot_nt(wl, xh) + b_ref[...]

    n_e = logits.shape[0]
    eid = lax.broadcasted_iota(I32, logits.shape, 0)
    vals, idxs, hots = [], [], []
    multi = jnp.zeros(logits.shape, F32)
    for _ in range(TOP_K):
        best = jnp.max(logits, axis=0, keepdims=True)
        first = jnp.min(jnp.where(logits == best, eid, n_e), axis=0, keepdims=True)
        hot = eid == first
        vals.append(best)
        idxs.append(first)
        hots.append(hot)
        multi = jnp.where(hot, 1.0, multi)
        logits = jnp.where(hot, -jnp.inf, logits)

    exps = [jnp.exp(v - vals[0]) for v in vals]
    den = exps[0]
    for e in exps[1:]:
        den = den + e

    r = lax.broadcasted_iota(I32, (tb, tb), 0)
    c = lax.broadcasted_iota(I32, (tb, tb), 1)
    before = jnp.where(r < c, 1.0, 0.0).astype(BF16)
    pos = _dot(multi.astype(BF16), before) + run_ref[:, 0:1]
    for kk in range(TOP_K):
        idx_ref[kk:kk + 1, :] = idxs[kk]
        rank = jnp.sum(jnp.where(hots[kk], pos, 0.0), axis=0, keepdims=True)
        idx_ref[TOP_K + kk:TOP_K + kk + 1, :] = rank.astype(I32)
        gate_ref[kk:kk + 1, :] = exps[kk] / den
    gate_ref[TOP_K:2 * TOP_K, :] = jnp.zeros((TOP_K, tb), F32)
    run_ref[...] = run_ref[...] + jnp.sum(multi, axis=1, keepdims=True)
    cnt_ref[...] = run_ref[...]


def _out_ln_route_body(mix_ref, mem_ref, x_ref, w1_ref, w2_ref, g_ref, b_ref, wt_ref, rb_ref,
                       o_ref, ort_ref, idx_ref, gate_ref, cnt_ref, run_ref, *, alpha):
    hproj = _dot(mix_ref[...], w1_ref[...]) + _dot(mem_ref[...], w2_ref[...])
    y = _layer_norm(alpha * x_ref[...] + hproj, g_ref[...], b_ref[...])
    o_ref[...] = y
    _to_row_tiles(ort_ref, y)
    _route(y, wt_ref, rb_ref, idx_ref, gate_ref, cnt_ref, run_ref)


def _out_proj_ln_route(mix, mem_out, x, w_out, g, b, alpha, router_w, router_b, tm=512):
    t, d = x.shape
    tm = min(tm, t)
    wm = mix.shape[1]
    we = mem_out.shape[1]
    n_e = router_w.shape[1]
    return pl.pallas_call(
        functools.partial(_out_ln_route_body, alpha=alpha),
        grid=(t // tm,),
        in_specs=[
            pl.BlockSpec((tm, wm), lambda i: (i, 0)),
            pl.BlockSpec((tm, we), lambda i: (i, 0)),
            pl.BlockSpec((tm, d), lambda i: (i, 0)),
            pl.BlockSpec((wm, d), lambda i: (0, 0)),
            pl.BlockSpec((we, d), lambda i: (0, 0)),
            pl.BlockSpec((1, d), lambda i: (0, 0)),
            pl.BlockSpec((1, d), lambda i: (0, 0)),
            pl.BlockSpec((n_e, d), lambda i: (0, 0)),
            pl.BlockSpec((n_e, 1), lambda i: (0, 0)),
        ],
        out_specs=[
            pl.BlockSpec((tm, d), lambda i: (i, 0)),
            pl.BlockSpec((tm * (d // HEAD_DIM), HEAD_DIM), lambda i: (i, 0)),
            pl.BlockSpec((2 * TOP_K, tm), lambda i: (0, i)),
            pl.BlockSpec((2 * TOP_K, tm), lambda i: (0, i)),
            pl.BlockSpec((n_e, HEAD_DIM), lambda i: (0, 0)),
        ],
        out_shape=[
            jax.ShapeDtypeStruct((t, d), F32),
            jax.ShapeDtypeStruct((t * (d // HEAD_DIM), HEAD_DIM), F32),
            jax.ShapeDtypeStruct((2 * TOP_K, t), I32),
            jax.ShapeDtypeStruct((2 * TOP_K, t), F32),
            jax.ShapeDtypeStruct((n_e, HEAD_DIM), F32),
        ],
        scratch_shapes=[pltpu.VMEM((n_e, HEAD_DIM), F32)],
        compiler_params=_cparams("arbitrary"),
        name="out_proj_ln_route",
    )(mix, mem_out, x, w_out[:wm].astype(BF16), w_out[wm:].astype(BF16), g.reshape(1, d), b.reshape(1, d),
      router_w.T, router_b.reshape(n_e, 1))


def _dispatch_body(dest_ref, pad_lo_ref, pad_hi_ref, nu_ref, x_ref, xs_hbm, zero_ref, sem, zsem,
                   *, tb, n_tok, n_tiles, n_experts, n_blocks):
    base = pl.program_id(0) * tb

    def issue(t, carry):
        for kk in range(TOP_K):
            d = dest_ref[kk * n_tok + base + t]
            pltpu.make_async_copy(_row_tile(x_ref, t, n_tiles), _row_tile(xs_hbm, d, n_tiles), sem.at[kk]).start(
                priority=kk % 2)
        return carry

    lax.fori_loop(0, tb, issue, 0, unroll=DMA_UNROLL)

    @pl.when(pl.program_id(0) == 0)
    def _():
        zero_ref[...] = jnp.zeros_like(zero_ref)

        def pad_row(r):
            return pltpu.make_async_copy(zero_ref.at[pl.ds(0, n_tiles)], _row_tile(xs_hbm, r, n_tiles), zsem)

        def spare_block(blk):
            rows = pl.ds(pl.multiple_of(blk * MOE_BLOCK * n_tiles, MOE_BLOCK * n_tiles), MOE_BLOCK * n_tiles)
            return pltpu.make_async_copy(zero_ref, xs_hbm.at[rows], zsem)

        def for_all(action):
            for e in range(n_experts):
                lax.fori_loop(pad_lo_ref[e], pad_hi_ref[e], lambda r, c: (action(pad_row(r)), c)[1], 0)
            lax.fori_loop(nu_ref[0], n_blocks, lambda blk, c: (action(spare_block(blk)), c)[1], 0)

        for_all(lambda cp: cp.start())
        for_all(lambda cp: cp.wait())

    for kk in range(TOP_K):
        pltpu.make_async_copy(x_ref, xs_hbm.at[pl.ds(0, tb * n_tiles)], sem.at[kk]).wait()


def _dispatch(x, n_tiles, dest_flat, pad_lo, pad_hi, n_used, n_rows, tb=1024):
    t = x.shape[0] // n_tiles
    tb = min(tb, t)
    grid_spec = pltpu.PrefetchScalarGridSpec(
        num_scalar_prefetch=4,
        grid=(t // tb,),
        in_specs=[pl.BlockSpec((tb * n_tiles, HEAD_DIM), lambda i, *_: (i, 0))],
        out_specs=pl.BlockSpec(memory_space=pl.ANY),
        scratch_shapes=[pltpu.VMEM((MOE_BLOCK * n_tiles, HEAD_DIM), x.dtype),
                        pltpu.SemaphoreType.DMA((TOP_K,)), pltpu.SemaphoreType.DMA(())],
    )
    return pl.pallas_call(
        functools.partial(_dispatch_body, tb=tb, n_tok=t, n_tiles=n_tiles, n_experts=pad_lo.shape[0],
                          n_blocks=n_rows // MOE_BLOCK),
        grid_spec=grid_spec,
        out_shape=jax.ShapeDtypeStruct((n_rows * n_tiles, HEAD_DIM), x.dtype),
        compiler_params=_cparams("arbitrary"),
        name="moe_dispatch",
    )(dest_flat, pad_lo, pad_hi, n_used, x)


UNZIP = 256


def _expert_body(be_ref, nu_ref, first_ref, slot_ref, next_ref, x_ref, wu_hbm, bg_ref, bl_ref, wd_hbm, bd_ref,
                 o_ref, wu_buf, wd_buf, wg_s, wl_s, wd_s, sem, *, layer, n_tiles):
    i = pl.program_id(0)
    n_used = nu_ref[0]

    def weight_copies(expert, slot):
        return (pltpu.make_async_copy(wu_hbm.at[layer, expert], wu_buf.at[slot], sem.at[0, slot]),
                pltpu.make_async_copy(wd_hbm.at[layer, expert], wd_buf.at[slot], sem.at[1, slot]))

    @pl.when(i == 0)
    def _():
        for cp in weight_copies(be_ref[0], 0):
            cp.start()

    @pl.when(jnp.logical_and(first_ref[i] == 1, i < n_used))
    def _():
        slot = slot_ref[i]
        for cp in weight_copies(be_ref[i], slot):
            cp.wait()

        @pl.when(next_ref[i] >= 0)
        def _():
            for cp in weight_copies(next_ref[i], 1 - slot):
                cp.start()

        half = UNZIP // 2
        r = lax.broadcasted_iota(I32, (UNZIP, UNZIP), 0)
        c = lax.broadcasted_iota(I32, (UNZIP, UNZIP), 1)
        src = jnp.where(c < half, 2 * c, 2 * (c - half) + 1)
        perm = jnp.where(r == src, 1.0, 0.0).astype(BF16)
        for g in range(wu_buf.shape[2] // UNZIP):
            w = wu_buf[slot, :, g * UNZIP:(g + 1) * UNZIP].astype(BF16)
            sep = _dot(w, perm).astype(BF16)
            wg_s[:, g * half:(g + 1) * half] = sep[:, :half]
            wl_s[:, g * half:(g + 1) * half] = sep[:, half:]
        wd_s[...] = wd_buf[slot].astype(BF16)

    @pl.when(i < n_used)
    def _():
        xb = _from_row_tiles(x_ref, MOE_BLOCK, n_tiles).astype(BF16)
        glu = jnp.minimum(_dot(xb, wg_s[...]) + bg_ref[0], SWIGLU_LIMIT)
        lin = jnp.clip(_dot(xb, wl_s[...]) + bl_ref[0], -SWIGLU_LIMIT, SWIGLU_LIMIT)
        act = glu * _sigmoid(SWIGLU_ALPHA * glu) * (lin + 1.0)
        _to_row_tiles(o_ref, _dot(act.astype(BF16), wd_s[...]) + bd_ref[0])

    @pl.when(i >= n_used)
    def _():
        o_ref[...] = jnp.zeros_like(o_ref)


def _expert_mlp(xs, n_tiles, blk_expert, n_used, counts, layer, w_up, bg, bl, w_down, bd):
    p_rows = xs.shape[0] // n_tiles
    d = n_tiles * HEAD_DIM
    f = w_down.shape[2]
    n_e = w_down.shape[1]
    n_blk = p_rows // MOE_BLOCK

    blocks = jnp.arange(n_blk, dtype=I32)
    valid = blocks < n_used[0]
    prev_e = jnp.concatenate([jnp.full((1,), -1, I32), blk_expert[:-1]])
    first = jnp.logical_and(valid, blk_expert != prev_e)
    slot = (jnp.cumsum(first.astype(I32)) - 1) % 2
    experts = jnp.arange(n_e, dtype=I32)
    later = jnp.logical_and(experts[None, :] > blk_expert[:, None], counts[None, :] > 0)
    nxt = jnp.min(jnp.where(later, experts[None, :], n_e), axis=1)
    nxt = jnp.where(nxt == n_e, -1, nxt)

    def blk(i, nu):
        return jnp.minimum(i, nu[0] - 1)

    def w_spec(shape):
        return pl.BlockSpec((1,) + shape, lambda i, be, nu, fi, sl, nx: (be[blk(i, nu)], 0, 0))

    grid_spec = pltpu.PrefetchScalarGridSpec(
        num_scalar_prefetch=5,
        grid=(n_blk,),
        in_specs=[
            pl.BlockSpec((MOE_BLOCK * n_tiles, HEAD_DIM), lambda i, be, nu, fi, sl, nx: (blk(i, nu), 0)),
            pl.BlockSpec(memory_space=pl.ANY), w_spec((1, f)), w_spec((1, f)),
            pl.BlockSpec(memory_space=pl.ANY), w_spec((1, d)),
        ],
        out_specs=pl.BlockSpec((MOE_BLOCK * n_tiles, HEAD_DIM), lambda i, be, nu, fi, sl, nx: (i, 0)),
        scratch_shapes=[
            pltpu.VMEM((2, d, 2 * f), F32), pltpu.VMEM((2, f, d), F32),
            pltpu.VMEM((d, f), BF16), pltpu.VMEM((d, f), BF16), pltpu.VMEM((f, d), BF16),
            pltpu.SemaphoreType.DMA((2, 2)),
        ],
    )
    return pl.pallas_call(
        functools.partial(_expert_body, layer=layer, n_tiles=n_tiles),
        grid_spec=grid_spec,
        out_shape=jax.ShapeDtypeStruct(xs.shape, F32),
        compiler_params=_cparams("arbitrary"),
        name="moe_experts",
    )(blk_expert, n_used, first.astype(I32), slot.astype(I32), nxt.astype(I32), xs, w_up, bg, bl, w_down, bd)


def _combine_body(dest_ref, ys_hbm, gate_ref, x_ref, g_ref, b_ref, o_ref, buf_ref, sem_ref,
                  *, tb, n_tok, n_tiles, alpha):
    i = pl.program_id(0)
    n_steps = pl.num_programs(0)

    def issue(step, slot):
        base = step * tb

        def one(t, carry):
            for kk in range(TOP_K):
                d = dest_ref[kk * n_tok + base + t]
                pltpu.make_async_copy(_row_tile(ys_hbm, d, n_tiles), _row_tile(buf_ref, t, n_tiles, slot, kk),
                                      sem_ref.at[slot]).start(priority=kk % 2)
            return carry

        lax.fori_loop(0, tb, one, 0, unroll=DMA_UNROLL)

    @pl.when(i == 0)
    def _():
        issue(0, 0)

    slot = lax.rem(i, 2)

    @pl.when(i + 1 < n_steps)
    def _():
        issue(i + 1, 1 - slot)

    for kk in range(TOP_K):
        pltpu.make_async_copy(ys_hbm.at[pl.ds(0, tb * n_tiles)], buf_ref.at[slot, kk], sem_ref.at[slot]).wait()

    g8 = gate_ref[...]
    gpad = jnp.concatenate([g8, jnp.zeros((HEAD_DIM - g8.shape[0], tb), F32)], axis=0)
    gt = gpad.T
    f = gt[:, 0:1] * _from_row_tiles(buf_ref, tb, n_tiles, slot, 0)
    for kk in range(1, TOP_K):
        f = f + gt[:, kk:kk + 1] * _from_row_tiles(buf_ref, tb, n_tiles, slot, kk)
    o_ref[...] = _layer_norm(alpha * x_ref[...] + f, g_ref[...], b_ref[...])


def _combine_ln(ys, dest_flat, gates, x, g, b, alpha, tb=256):
    t, d = x.shape
    tb = min(tb, t)
    n_tiles = d // HEAD_DIM
    grid_spec = pltpu.PrefetchScalarGridSpec(
        num_scalar_prefetch=1,
        grid=(t // tb,),
        in_specs=[
            pl.BlockSpec(memory_space=pl.ANY),
            pl.BlockSpec((2 * TOP_K, tb), lambda i, dest: (0, i)),
            pl.BlockSpec((tb, d), lambda i, dest: (i, 0)),
            pl.BlockSpec((1, d), lambda i, dest: (0, 0)),
            pl.BlockSpec((1, d), lambda i, dest: (0, 0)),
        ],
        out_specs=pl.BlockSpec((tb, d), lambda i, dest: (i, 0)),
        scratch_shapes=[pltpu.VMEM((2, TOP_K, tb * n_tiles, HEAD_DIM), F32), pltpu.SemaphoreType.DMA((2,))],
    )
    return pl.pallas_call(
        functools.partial(_combine_body, tb=tb, n_tok=t, n_tiles=n_tiles, alpha=alpha),
        grid_spec=grid_spec,
        out_shape=jax.ShapeDtypeStruct((t, d), F32),
        compiler_params=_cparams("arbitrary"),
        name="moe_combine_ln",
    )(dest_flat, ys, gates, x, g.reshape(1, d), b.reshape(1, d))


def _moe_ln(x, x_tiles, idx_rank, gates, cnt, layer, w_up, b_up, w_down, b_down, g, b, alpha):
    t, d = x.shape
    n_tiles = d // HEAD_DIM
    n_e = cnt.shape[0]
    counts = cnt[:, 0].astype(I32)
    padded = ((counts + MOE_BLOCK - 1) // MOE_BLOCK) * MOE_BLOCK
    p_ends = jnp.cumsum(padded)
    p_starts = p_ends - padded
    n_blk = (t * TOP_K + MOE_BLOCK - 1) // MOE_BLOCK + n_e
    experts = jnp.arange(n_e, dtype=I32)
    start_of = jnp.sum(jnp.where(idx_rank[:TOP_K, :, None] == experts, p_starts, 0), axis=-1)
    dest_flat = (start_of + idx_rank[TOP_K:]).reshape(-1)
    blk_first_row = jnp.arange(n_blk, dtype=I32) * MOE_BLOCK
    blk_expert = jnp.clip(jnp.sum((p_ends[None, :] <= blk_first_row[:, None]).astype(I32), axis=1), 0, n_e - 1)
    n_used = (p_ends[-1:] // MOE_BLOCK).astype(I32)

    xs = _dispatch(x_tiles, n_tiles, dest_flat, p_starts + counts, p_ends, n_used, n_blk * MOE_BLOCK)
    bg = b_up[:, None, 0::2]
    bl = b_up[:, None, 1::2]
    ys = _expert_mlp(xs, n_tiles, blk_expert, n_used, counts, layer, w_up, bg, bl, w_down, b_down[:, None, :])
    return _combine_ln(ys, dest_flat, gates, x, g, b, alpha)


def kernel(x, mem, hgrn_w_in, hgrn_lb, hgrn_norm_g, fox_w_in, fox_b_f, moba_w_in, rel_bias, w_mem_kv, w_out,
           ln1_g, ln1_b, router_w, router_b, w_up, b_up, w_down, b_down, ln2_g, ln2_b):
    bsz, seq, d = x.shape
    depth = w_out.shape[0]
    t = bsz * seq
    mix_w = MIX_HEADS * HEAD_DIM
    mem_w = MEM_HEADS * HEAD_DIM
    alpha = (2 * depth) ** 0.25
    scale = HEAD_DIM ** -0.5
    n_mem = mem.shape[1]
    mem2 = mem.reshape(bsz * n_mem, d)

    def scale_vec(n_mixer_q, n_plain, n_mem_q):
        return jnp.concatenate([
            jnp.full((1, n_mixer_q), scale * LOG2E, F32), jnp.ones((1, n_plain), F32),
            jnp.full((1, n_mem_q), scale, F32)], axis=1)

    x2 = x.reshape(t, d)
    for i in range(depth):
        kind, j = i % 3, i // 3
        if kind == 0:
            w_in = hgrn_w_in[j].astype(BF16)
            proj = _matmul(x2, w_in, scale_vec(0, 4 * mix_w, mem_w), F32, 1024, w_in.shape[1] // 3)
            proj = proj.reshape(bsz, seq, -1)
            mix = _hgrn_mixer(proj, hgrn_lb, hgrn_norm_g[j], j)
            memq_block = 4 * mix_w // mem_w
        elif kind == 1:
            w = fox_w_in[j]
            w_in = jnp.concatenate([w[:, :3 * mix_w], w[:, 3 * mix_w + MIX_HEADS:]], axis=1).astype(BF16)
            proj = _matmul(x2, w_in, scale_vec(mix_w, 2 * mix_w, mem_w), BF16, 1024, w_in.shape[1] // 2)
            proj = proj.reshape(bsz, seq, -1)
            c_col, c_row = _fox_gate_cumsum(x2.reshape(bsz, seq, d), w[:, 3 * mix_w:3 * mix_w + MIX_HEADS],
                                            fox_b_f[j])
            mix = _fox_attention(proj, c_col, c_row)
            memq_block = 3 * mix_w // mem_w
        else:
            w_in = moba_w_in[j].astype(BF16)
            proj = _matmul(x2, w_in, scale_vec(mix_w, 2 * mix_w, mem_w), BF16, 1024, w_in.shape[1] // 2)
            proj = proj.reshape(bsz, seq, -1)
            mix = _moba_attention(proj, rel_bias)
            memq_block = 3 * mix_w // mem_w
        kv = _matmul(mem2, w_mem_kv[i].astype(BF16), jnp.ones((1, 2 * mem_w), F32), BF16, 512, 512)
        mem_out = _memory_attention(proj, memq_block, kv.reshape(bsz, n_mem, 2 * mem_w))
        x2, x2_tiles, idx_rank, gates, cnt = _out_proj_ln_route(
            mix.reshape(t, mix_w), mem_out.reshape(t, mem_w), x2, w_out[i], ln1_g[i], ln1_b[i], alpha,
            router_w[i], router_b[i])
        x2 = _moe_ln(x2, x2_tiles, idx_rank, gates, cnt, i, w_up, b_up[i], w_down, b_down[i],
                     ln2_g[i], ln2_b[i], alpha)
    return x2.reshape(bsz, seq, d)
```

```python
import functools
import math

import numpy as np
import jax
import jax.numpy as jnp
from jax import lax
from jax.experimental import pallas as pl
from jax.experimental.pallas import tpu as pltpu

F32 = jnp.float32
BF16 = jnp.bfloat16
I32 = jnp.int32

HEAD_DIM = 128
MIX_HEADS = 8
MEM_HEADS = 4
HGRN_CHUNK = 64
HGRN_SUB = 16
HGRN_EXP_CLAMP = 80.0
MOBA_BLOCK = 256
MOBA_TOPK = 3
REL_BUCKETS = 32
REL_MAX_DIST = 128
N_EXPERTS = 32
TOP_K = 4
MOE_BLOCK = 256
SWIGLU_LIMIT = 7.0
SWIGLU_ALPHA = 1.702
LN_EPS = 1e-5
RMS_EPS = 1e-6
NEG = -1e30
LOG2E = 1.4426950408889634
VMEM_LIMIT = 56 * 1024 * 1024
DMA_UNROLL = 8

NT_DIMS = (((1,), (1,)), ((), ()))
TN_DIMS = (((0,), (0,)), ((), ()))


def _cparams(*sem):
    return pltpu.CompilerParams(dimension_semantics=sem, vmem_limit_bytes=VMEM_LIMIT)


def _dot(a, b):
    return jnp.dot(a, b, preferred_element_type=F32)


def _dot_nt(a, b):
    return lax.dot_general(a, b, NT_DIMS, preferred_element_type=F32)


def _dot_tn(a, b):
    return lax.dot_general(a, b, TN_DIMS, preferred_element_type=F32)


def _split3(a):
    hi = a.astype(BF16)
    r = a - hi.astype(F32)
    mid = r.astype(BF16)
    lo = (r - mid.astype(F32)).astype(BF16)
    return hi, mid, lo


def _sigmoid(x):
    return 1.0 / (1.0 + jnp.exp(-x))


def _layer_norm(z, g, b):
    mu = jnp.mean(z, axis=-1, keepdims=True)
    zc = z - mu
    var = jnp.mean(zc * zc, axis=-1, keepdims=True)
    return zc * lax.rsqrt(var + LN_EPS) * g + b


def _mm_body(x_ref, w_ref, s_ref, o_ref, xb_ref):
    @pl.when(pl.program_id(1) == 0)
    def _():
        xb_ref[...] = x_ref[...].astype(BF16)

    acc = _dot(xb_ref[...], w_ref[...])
    o_ref[...] = (acc * s_ref[...]).astype(o_ref.dtype)


def _matmul(x, w, col_scale, out_dtype, tm, tn):
    m, k = x.shape
    n = w.shape[1]
    tm = min(tm, m)
    tn = min(tn, n)
    return pl.pallas_call(
        _mm_body,
        grid=(m // tm, n // tn),
        in_specs=[
            pl.BlockSpec((tm, k), lambda i, j: (i, 0)),
            pl.BlockSpec((k, tn), lambda i, j: (0, j)),
            pl.BlockSpec((1, tn), lambda i, j: (0, j)),
        ],
        out_specs=pl.BlockSpec((tm, tn), lambda i, j: (i, j)),
        out_shape=jax.ShapeDtypeStruct((m, n), out_dtype),
        scratch_shapes=[pltpu.VMEM((tm, k), BF16)],
        compiler_params=_cparams("parallel", "arbitrary"),
        name="proj_matmul",
    )(x, w, col_scale)


def _hgrn_body(q_ref, z_ref, i_ref, g_ref, lbp_ref, ng_ref, o_ref, st_ref, *, layer_j, tb):
    c_len, sb = HGRN_CHUNK, HGRN_SUB
    n_sub = c_len // sb

    @pl.when(pl.program_id(2) == 0)
    def _():
        st_ref[...] = jnp.zeros_like(st_ref)

    lbp = lbp_ref[...]
    ex = jnp.exp(lbp - jnp.max(lbp, axis=0, keepdims=True))
    p = ex / jnp.sum(ex, axis=0, keepdims=True)
    lb = jnp.zeros((1, HEAD_DIM), F32)
    for r in range(1, layer_j + 1):
        lb = lb + p[r:r + 1, :]
    log_lb = jnp.log(lb)
    log1m_lb = jnp.log1p(-lb)
    one_m_lb = 1.0 - lb
    ng = ng_ref[...]

    row = lax.broadcasted_iota(I32, (c_len, c_len), 0)
    col = lax.broadcasted_iota(I32, (c_len, c_len), 1)
    sub_start = (row // sb) * sb
    one = jnp.ones((c_len, c_len), F32)
    zero = jnp.zeros((c_len, c_len), F32)
    m_tri = jnp.where(col <= row, one, zero)
    m_in = jnp.where(col > sub_start, m_tri, zero)
    m_dec = jnp.where(col > row, one, zero)
    m_ke = jnp.where(col <= sub_start + (sb - 1), m_dec, zero)
    stack = jnp.concatenate([m_tri, m_in, m_ke, m_dec], axis=0).astype(BF16)
    same_sub = (row // sb) == (col // sb)
    diag_mask = jnp.logical_and(same_sub, col <= row)
    row_sub = row // sb
    krow_sub = lax.broadcasted_iota(I32, (c_len, HEAD_DIM), 0) // sb

    chunks = [pl.ds(c * c_len, c_len) for c in range(tb // c_len)]

    gates = []
    for sl in chunks:
        z = z_ref[0, sl, :]
        e = jnp.exp(-jnp.abs(z))
        inv = 1.0 / (1.0 + e)
        sig_neg = jnp.where(z >= 0, e, 1.0) * inv
        log_sig = jnp.minimum(z, 0.0) - jnp.log(1.0 + e)
        a2 = log1m_lb + log_sig
        log_f = jnp.maximum(log_lb, a2) + jnp.log(1.0 + jnp.exp(-jnp.abs(log_lb - a2)))
        qv = q_ref[0, sl, :]
        gates.append((log_f, one_m_lb * sig_neg, qv * _sigmoid(qv), i_ref[0, sl, :].astype(BF16)))

    prefix = []
    for log_f, _, _, _ in gates:
        pre = None
        for piece in _split3(log_f):
            t = _dot(stack, piece)
            pre = t if pre is None else pre + t
        prefix.append(pre)

    factors = []
    for (_, k, qf, _), pre in zip(gates, prefix):
        b = pre[0:c_len]
        d_in = pre[c_len:2 * c_len]
        d_ke = pre[2 * c_len:3 * c_len]
        d_dec = pre[3 * c_len:4 * c_len]
        q_in = qf * jnp.exp(d_in)
        k_diag = (k * jnp.exp(jnp.minimum(-d_in, HGRN_EXP_CLAMP))).astype(BF16)
        k_end = k * jnp.exp(d_ke)
        k_dec = (k * jnp.exp(d_dec)).astype(BF16)
        q_all = (qf * jnp.exp(b)).astype(BF16)
        b_ref = b - d_in
        lhs = [q_in.astype(BF16)]
        rhs = [k_diag]
        for j in range(n_sub - 1):
            b_end = b[(j + 1) * sb - 1:(j + 1) * sb, :]
            cross = jnp.exp(jnp.minimum(b_ref - b_end, 0.0))
            lhs.append((q_in * cross).astype(BF16))
            rhs.append(jnp.where(krow_sub == j, k_end, 0.0).astype(BF16))
        factors.append((lhs, rhs, k_dec, q_all, jnp.exp(b[c_len - 1:c_len, :])))

    scores = []
    for lhs, rhs, _, _, _ in factors:
        a = jnp.where(diag_mask, _dot_nt(lhs[0], rhs[0]), 0.0)
        for j in range(n_sub - 1):
            a = a + jnp.where(row_sub > j, _dot_nt(lhs[j + 1], rhs[j + 1]), 0.0)
        scores.append(a.astype(BF16))

    updates = [_dot_tn(vb, k_dec) for (_, _, _, vb), (_, _, k_dec, _, _) in zip(gates, factors)]
    states = [st_ref[...]]
    for (_, _, _, _, decay), kv in zip(factors, updates):
        states.append(states[-1] * decay + kv)
    st_ref[...] = states[-1]

    for sl, (_, _, _, vb), (_, _, _, q_all, _), a, st in zip(chunks, gates, factors, scores, states):
        o = _dot(a, vb) + _dot_nt(q_all, st.astype(BF16))
        o = o * lax.rsqrt(jnp.mean(o * o, axis=-1, keepdims=True) + RMS_EPS)
        gv = g_ref[0, sl, :]
        o_ref[0, sl, :] = (o * ng * (gv * _sigmoid(gv))).astype(o_ref.dtype)


def _hgrn_mixer(proj, lb_param, norm_g, layer_j, tb=4096):
    bsz, seq, _ = proj.shape
    tb = min(tb, seq)
    n_layers = lb_param.shape[0]

    def col_block(off):
        return pl.BlockSpec((1, tb, HEAD_DIM), lambda b, h, t: (b, t, off + h))

    return pl.pallas_call(
        functools.partial(_hgrn_body, layer_j=layer_j, tb=tb),
        grid=(bsz, MIX_HEADS, seq // tb),
        in_specs=[
            col_block(0), col_block(MIX_HEADS), col_block(2 * MIX_HEADS), col_block(3 * MIX_HEADS),
            pl.BlockSpec((n_layers, HEAD_DIM), lambda b, h, t: (0, h)),
            pl.BlockSpec((1, HEAD_DIM), lambda b, h, t: (0, h)),
        ],
        out_specs=pl.BlockSpec((1, tb, HEAD_DIM), lambda b, h, t: (b, t, h)),
        out_shape=jax.ShapeDtypeStruct((bsz, seq, MIX_HEADS * HEAD_DIM), BF16),
        scratch_shapes=[pltpu.VMEM((HEAD_DIM, HEAD_DIM), F32)],
        compiler_params=_cparams("parallel", "parallel", "arbitrary"),
        name="hgrn2_mixer",
    )(proj, proj, proj, proj, lb_param, norm_g.reshape(1, -1))


def _log_sigmoid(z):
    return jnp.minimum(z, 0.0) - jnp.log1p(jnp.exp(-jnp.abs(z)))


def _fox_gate_body(x_ref, w_ref, wt_ref, bc_ref, br_ref, ccol_ref, crow_ref, carc_ref, carr_ref, *, tb):
    @pl.when(pl.program_id(1) == 0)
    def _():
        carc_ref[...] = jnp.zeros_like(carc_ref)
        carr_ref[...] = jnp.zeros_like(carr_ref)

    xb = x_ref[0].astype(BF16)
    ls_col = LOG2E * _log_sigmoid(_dot(xb, w_ref[...]) + bc_ref[...])
    ls_row = LOG2E * _log_sigmoid(_dot_nt(wt_ref[...], xb) + br_ref[...])
    row = lax.broadcasted_iota(I32, (tb, tb), 0)
    col = lax.broadcasted_iota(I32, (tb, tb), 1)
    lower = jnp.where(col <= row, 1.0, 0.0).astype(BF16)
    upper = jnp.where(row <= col, 1.0, 0.0).astype(BF16)
    c_col = carc_ref[...]
    for piece in _split3(ls_col):
        c_col = c_col + _dot(lower, piece)
    c_row = carr_ref[...]
    for piece in _split3(ls_row):
        c_row = c_row + _dot(piece, upper)
    ccol_ref[0] = c_col
    crow_ref[0] = c_row
    carc_ref[...] = c_col[tb - 1:tb, :]
    carr_ref[...] = c_row[:, tb - 1:tb]


def _fox_gate_cumsum(x, w_f, b_f, tb=512):
    bsz, seq, d = x.shape
    h = w_f.shape[1]
    tb = min(tb, seq)
    return pl.pallas_call(
        functools.partial(_fox_gate_body, tb=tb),
        grid=(bsz, seq // tb),
        in_specs=[
            pl.BlockSpec((1, tb, d), lambda b, t: (b, t, 0)),
            pl.BlockSpec((d, h), lambda b, t: (0, 0)),
            pl.BlockSpec((h, d), lambda b, t: (0, 0)),
            pl.BlockSpec((1, h), lambda b, t: (0, 0)),
            pl.BlockSpec((h, 1), lambda b, t: (0, 0)),
        ],
        out_specs=[
            pl.BlockSpec((1, tb, h), lambda b, t: (b, t, 0)),
            pl.BlockSpec((1, h, tb), lambda b, t: (b, 0, t)),
        ],
        out_shape=[
            jax.ShapeDtypeStruct((bsz, seq, h), F32),
            jax.ShapeDtypeStruct((bsz, h, seq), F32),
        ],
        scratch_shapes=[pltpu.VMEM((1, h), F32), pltpu.VMEM((h, 1), F32)],
        compiler_params=_cparams("parallel", "arbitrary"),
        name="fox_gate_cumsum",
    )(x, w_f.astype(BF16), w_f.T.astype(BF16), b_f.reshape(1, h), b_f.reshape(h, 1))


ATTN_ROWS = 128


def _softmax_probs(u, shift, rows, m_ref, l_ref):
    reps = u.shape[1] // HEAD_DIM
    m_old = m_ref[rows, :]
    if shift is None:
        m_new = jnp.maximum(m_old, jnp.max(u, axis=-1, keepdims=True))
        p = jnp.exp2(u - jnp.tile(m_new, (1, reps)))
    else:
        m_new = jnp.maximum(m_old, jnp.max(u, axis=-1, keepdims=True) + shift)
        p = jnp.exp2(u - jnp.tile(m_new - shift, (1, reps)))
    alpha = jnp.exp2(m_old - m_new)
    l_ref[rows, :] = alpha * l_ref[rows, :] + jnp.sum(p, axis=-1, keepdims=True)
    m_ref[rows, :] = m_new
    return p.astype(BF16), alpha


def _accumulate(p, alpha, v, rows, acc_ref):
    acc_ref[rows, :] = alpha * acc_ref[rows, :] + _dot(p, v)


def _softmax_init(m_ref, l_ref, acc_ref):
    m_ref[...] = jnp.full(m_ref.shape, NEG, F32)
    l_ref[...] = jnp.zeros_like(l_ref)
    acc_ref[...] = jnp.zeros_like(acc_ref)


def _fox_attn_body(q_ref, k_ref, v_ref, ccol_ref, crow_ref, o_ref, ct_ref, m_ref, l_ref, acc_ref, s_ref, *, tq):
    rs = min(ATTN_ROWS, tq)
    n_groups = tq // rs
    h = pl.program_id(1)
    qi = pl.program_id(2)
    ccol = ccol_ref[0]
    lane = lax.broadcasted_iota(I32, ccol.shape, 1)
    c_t = jnp.sum(jnp.where(lane == h, ccol, 0.0), axis=1, keepdims=True)
    ct_ref[...] = jnp.broadcast_to(c_t, ct_ref.shape)
    _softmax_init(m_ref, l_ref, acc_ref)

    def keys(kb):
        return pl.ds(pl.multiple_of(kb * tq, tq), tq)

    def logits(kb, r):
        rows = pl.ds(r * rs, rs)
        ks = keys(kb)
        s_ref[kb % 2, rows, :] = _dot_nt(q_ref[0, rows, :], k_ref[0, ks, :]) - crow_ref[0, pl.ds(h, 1), ks]

    def attend(kb, r, diagonal, prefetch):
        rows = pl.ds(r * rs, rs)
        u = s_ref[kb % 2, rows, :]
        if diagonal:
            row = lax.broadcasted_iota(I32, (rs, tq), 0) + r * rs
            col = lax.broadcasted_iota(I32, (rs, tq), 1)
            u = jnp.where(col <= row, u, NEG)
        if prefetch:
            logits(kb + 1, r)
        p, alpha = _softmax_probs(u, ct_ref[rows, :], rows, m_ref, l_ref)
        _accumulate(p, alpha, v_ref[0, keys(kb), :], rows, acc_ref)

    for r in range(n_groups):
        logits(0, r)

    def full_block(kb, carry):
        for r in range(n_groups):
            attend(kb, r, False, True)
        return carry

    def block_pair(i, carry):
        return full_block(2 * i + 1, full_block(2 * i, carry))

    lax.fori_loop(0, qi // 2, block_pair, 0)

    @pl.when(qi % 2 == 1)
    def _():
        full_block(qi - 1, 0)

    for r in range(n_groups):
        attend(qi, r, True, False)
    o_ref[0] = (acc_ref[...] / l_ref[...]).astype(o_ref.dtype)


def _fox_attention(qkv, c_col, c_row, tq=512):
    bsz, seq, _ = qkv.shape
    tq = min(tq, seq)
    nh = MIX_HEADS
    return pl.pallas_call(
        functools.partial(_fox_attn_body, tq=tq),
        grid=(bsz, nh, seq // tq),
        in_specs=[
            pl.BlockSpec((1, tq, HEAD_DIM), lambda b, h, i: (b, i, h)),
            pl.BlockSpec((1, seq, HEAD_DIM), lambda b, h, i: (b, 0, nh + h)),
            pl.BlockSpec((1, seq, HEAD_DIM), lambda b, h, i: (b, 0, 2 * nh + h)),
            pl.BlockSpec((1, tq, nh), lambda b, h, i: (b, i, 0)),
            pl.BlockSpec((1, nh, seq), lambda b, h, i: (b, 0, 0)),
        ],
        out_specs=pl.BlockSpec((1, tq, HEAD_DIM), lambda b, h, i: (b, i, h)),
        out_shape=jax.ShapeDtypeStruct((bsz, seq, nh * HEAD_DIM), BF16),
        scratch_shapes=[pltpu.VMEM((tq, HEAD_DIM), F32)] * 4 + [pltpu.VMEM((2, tq, tq), F32)],
        compiler_params=_cparams("parallel", "parallel", "arbitrary"),
        name="fox_attention",
    )(qkv, qkv, qkv, c_col, c_row)


def _t5_bucket_thresholds():
    n = np.arange(0, 4 * REL_MAX_DIST, dtype=np.int64)
    max_exact = REL_BUCKETS // 2
    ratio = np.log(np.maximum(n, 1).astype(np.float32) / np.float32(max_exact)) / np.float32(
        math.log(REL_MAX_DIST / max_exact))
    large = np.minimum(max_exact + (ratio * np.float32(REL_BUCKETS - max_exact)).astype(np.int32),
                       REL_BUCKETS - 1)
    bucket = np.where(n < max_exact, n, large)
    assert np.all(np.diff(bucket) >= 0) and bucket[-1] == REL_BUCKETS - 1
    return [int(np.argmax(bucket >= b)) for b in range(REL_BUCKETS)]


def _moba_body(rb_ref, q_ref, k_ref, v_ref, o_ref, kmean_ref, bias_ref, sel_ref, m_ref, l_ref, acc_ref,
               s_ref, *, n_blocks):
    blk = MOBA_BLOCK
    tq = 2 * blk
    h = pl.program_id(1)
    ti = pl.program_id(2)
    far_bias = LOG2E * rb_ref[REL_BUCKETS - 1, h]

    @pl.when(ti == 0)
    def _():
        r = lax.broadcasted_iota(I32, (n_blocks, n_blocks * blk), 0)
        c = lax.broadcasted_iota(I32, (n_blocks, n_blocks * blk), 1)
        pool = jnp.where(c // blk == r, 1.0, 0.0).astype(BF16)
        kmean_ref[...] = _dot(pool, k_ref[0]) * (1.0 / blk)
        row = lax.broadcasted_iota(I32, (blk, blk), 0)
        col = lax.broadcasted_iota(I32, (blk, blk), 1)
        thresholds = _t5_bucket_thresholds()
        for slot, dist in ((0, row - col + blk), (1, row - col)):
            val = jnp.full((blk, blk), rb_ref[0, h], F32)
            for b in range(1, REL_BUCKETS):
                val = jnp.where(dist >= thresholds[b], rb_ref[b, h], val)
            val = LOG2E * val
            if slot == 1:
                val = jnp.where(dist >= 0, val, NEG)
            bias_ref[slot] = val

    q = q_ref[0]
    km_hi = kmean_ref[...].astype(BF16)
    km_lo = (kmean_ref[...] - km_hi.astype(F32)).astype(BF16)
    gate = _dot_nt(km_hi, q) + _dot_nt(km_lo, q)
    blk_id = lax.broadcasted_iota(I32, gate.shape, 0)
    own = 2 * ti + lax.broadcasted_iota(I32, gate.shape, 1) // blk
    gate = jnp.where(blk_id < own, gate, NEG)
    sel = jnp.zeros(gate.shape, F32)
    for _ in range(MOBA_TOPK):
        best = jnp.max(gate, axis=0, keepdims=True)
        first = jnp.min(jnp.where(gate == best, blk_id, n_blocks), axis=0, keepdims=True)
        pick = jnp.logical_and(blk_id == first, best > 0.5 * NEG)
        sel = jnp.where(pick, 1.0, sel)
        gate = jnp.where(blk_id == first, NEG, gate)
    sel_ref[...] = jnp.concatenate([sel, jnp.zeros((HEAD_DIM - n_blocks, tq), F32)], axis=0).T

    _softmax_init(m_ref, l_ref, acc_ref)
    groups = (pl.ds(0, blk), pl.ds(blk, blk))

    def selected(j, rows):
        lanes = lax.broadcasted_iota(I32, (blk, HEAD_DIM), 1)
        return jnp.sum(jnp.where(lanes == j, sel_ref[rows, :], 0.0), axis=1, keepdims=True) > 0.5

    def far_term(j, rows):
        return jnp.where(selected(j, rows), far_bias, NEG)

    def prev_term(j, rows):
        return bias_ref[0] + jnp.where(selected(j, rows), 0.0, NEG)

    def keys(unit, n_keys=tq):
        return pl.ds(pl.multiple_of(unit * tq, tq), n_keys)

    def logits(unit, rows):
        s_ref[unit % 2, rows, :] = _dot_nt(q_ref[0, rows, :], k_ref[0, keys(unit), :])

    def attend(unit, rows, term_a, term_b, prefetch):
        u = s_ref[unit % 2, rows, :]
        if term_b is None:
            u = u[:, :blk] + term_a
            ks = keys(unit, blk)
        else:
            u = jnp.concatenate([u[:, :blk] + term_a, u[:, blk:] + term_b], axis=1)
            ks = keys(unit)
        if prefetch:
            logits(unit + 1, rows)
        p, alpha = _softmax_probs(u, None, rows, m_ref, l_ref)
        _accumulate(p, alpha, v_ref[0, ks, :], rows, acc_ref)

    for rows in groups:
        logits(0, rows)

    def far_unit(unit, carry):
        for rows in groups:
            attend(unit, rows, far_term(2 * unit, rows), far_term(2 * unit + 1, rows), True)
        return carry

    def unit_pair(i, carry):
        return far_unit(2 * i + 1, far_unit(2 * i, carry))

    n_far_units = jnp.maximum(ti - 1, 0)
    lax.fori_loop(0, n_far_units // 2, unit_pair, 0)

    @pl.when(n_far_units % 2 == 1)
    def _():
        far_unit(n_far_units - 1, 0)

    @pl.when(ti >= 1)
    def _():
        j = 2 * ti - 2
        attend(ti - 1, groups[0], far_term(j, groups[0]), prev_term(j + 1, groups[0]), True)
        attend(ti - 1, groups[1], far_term(j, groups[1]), far_term(j + 1, groups[1]), True)

    attend(ti, groups[0], bias_ref[1], None, False)
    attend(ti, groups[1], prev_term(2 * ti, groups[1]), bias_ref[1], False)
    o_ref[0] = (acc_ref[...] / l_ref[...]).astype(o_ref.dtype)


def _moba_attention(qkv, rel_bias):
    bsz, seq, _ = qkv.shape
    nh = MIX_HEADS
    blk = MOBA_BLOCK
    tq = 2 * blk
    assert seq % tq == 0 and seq // blk <= HEAD_DIM
    n_blocks = seq // blk
    grid_spec = pltpu.PrefetchScalarGridSpec(
        num_scalar_prefetch=0,
        grid=(bsz, nh, seq // tq),
        in_specs=[
            pl.BlockSpec(memory_space=pltpu.SMEM),
            pl.BlockSpec((1, tq, HEAD_DIM), lambda b, h, i: (b, i, h)),
            pl.BlockSpec((1, seq, HEAD_DIM), lambda b, h, i: (b, 0, nh + h)),
            pl.BlockSpec((1, seq, HEAD_DIM), lambda b, h, i: (b, 0, 2 * nh + h)),
        ],
        out_specs=pl.BlockSpec((1, tq, HEAD_DIM), lambda b, h, i: (b, i, h)),
        scratch_shapes=[
            pltpu.VMEM((n_blocks, HEAD_DIM), F32),
            pltpu.VMEM((2, blk, blk), F32),
            pltpu.VMEM((tq, HEAD_DIM), F32),
            pltpu.VMEM((tq, HEAD_DIM), F32), pltpu.VMEM((tq, HEAD_DIM), F32), pltpu.VMEM((tq, HEAD_DIM), F32),
            pltpu.VMEM((2, tq, tq), F32),
        ],
    )
    return pl.pallas_call(
        functools.partial(_moba_body, n_blocks=n_blocks),
        grid_spec=grid_spec,
        out_shape=jax.ShapeDtypeStruct((bsz, seq, nh * HEAD_DIM), BF16),
        compiler_params=_cparams("parallel", "parallel", "arbitrary"),
        name="moba_attention",
    )(rel_bias, qkv, qkv, qkv)


def _mem_attn_body(q_ref, kv_ref, o_ref):
    width = MEM_HEADS * HEAD_DIM
    for hh in range(MEM_HEADS):
        cs = slice(hh * HEAD_DIM, (hh + 1) * HEAD_DIM)
        q = q_ref[0, :, cs].astype(BF16)
        k = kv_ref[0, :, cs]
        v = kv_ref[0, :, width + hh * HEAD_DIM:width + (hh + 1) * HEAD_DIM]
        s = _dot_nt(q, k)
        p = jnp.exp(s - jnp.max(s, axis=-1, keepdims=True))
        o = _dot(p.astype(BF16), v) / jnp.sum(p, axis=-1, keepdims=True)
        o_ref[0, :, cs] = o.astype(o_ref.dtype)


def _memory_attention(proj, q_col_block, kv, tq=512):
    bsz, seq, _ = proj.shape
    tq = min(tq, seq)
    width = MEM_HEADS * HEAD_DIM
    n_mem = kv.shape[1]
    return pl.pallas_call(
        _mem_attn_body,
        grid=(bsz, seq // tq),
        in_specs=[
            pl.BlockSpec((1, tq, width), lambda b, i: (b, i, q_col_block)),
            pl.BlockSpec((1, n_mem, 2 * width), lambda b, i: (b, 0, 0)),
        ],
        out_specs=pl.BlockSpec((1, tq, width), lambda b, i: (b, i, 0)),
        out_shape=jax.ShapeDtypeStruct((bsz, seq, width), BF16),
        compiler_params=_cparams("parallel", "parallel"),
        name="memory_attention",
    )(proj, kv)


def _to_row_tiles(ref, y):
    rows, d = y.shape
    n_tiles = d // HEAD_DIM
    for c in range(n_tiles):
        ref[pl.ds(c, rows, stride=n_tiles), :] = y[:, c * HEAD_DIM:(c + 1) * HEAD_DIM]


def _from_row_tiles(ref, rows, n_tiles, *lead):
    return jnp.concatenate(
        [ref[(*lead, pl.ds(c, rows, stride=n_tiles), slice(None))] for c in range(n_tiles)], axis=1)


def _row_tile(ref, r, n_tiles, *lead):
    return ref.at[(*lead, pl.ds(pl.multiple_of(r * n_tiles, n_tiles), n_tiles))]


def _route(x, wt_ref, b_ref, idx_ref, gate_ref, cnt_ref, run_ref):
    tb = x.shape[0]

    @pl.when(pl.program_id(0) == 0)
    def _():
        run_ref[...] = jnp.zeros_like(run_ref)

    xh = x.astype(BF16)
    xl = (x - xh.astype(F32)).astype(BF16)
    w = wt_ref[...]
    wh = w.astype(BF16)
    wl = (w - wh.astype(F32)).astype(BF16)
    logits = _dot_nt(wh, xh) + _dot_nt(wh, xl) + _dot_nt(wl, xh) + b_ref[...]

    n_e = logits.shape[0]
    eid = lax.broadcasted_iota(I32, logits.shape, 0)
    vals, idxs, hots = [], [], []
    multi = jnp.zeros(logits.shape, F32)
    for _ in range(TOP_K):
        best = jnp.max(logits, axis=0, keepdims=True)
        first = jnp.min(jnp.where(logits == best, eid, n_e), axis=0, keepdims=True)
        hot = eid == first
        vals.append(best)
        idxs.append(first)
        hots.append(hot)
        multi = jnp.where(hot, 1.0, multi)
        logits = jnp.where(hot, -jnp.inf, logits)

    exps = [jnp.exp(v - vals[0]) for v in vals]
    den = exps[0]
    for e in exps[1:]:
        den = den + e

    r = lax.broadcasted_iota(I32, (tb, tb), 0)
    c = lax.broadcasted_iota(I32, (tb, tb), 1)
    before = jnp.where(r < c, 1.0, 0.0).astype(BF16)
    pos = _dot(multi.astype(BF16), before) + run_ref[:, 0:1]
    for kk in range(TOP_K):
        idx_ref[kk:kk + 1, :] = idxs[kk]
        rank = jnp.sum(jnp.where(hots[kk], pos, 0.0), axis=0, keepdims=True)
        idx_ref[TOP_K + kk:TOP_K + kk + 1, :] = rank.astype(I32)
        gate_ref[kk:kk + 1, :] = exps[kk] / den
    gate_ref[TOP_K:2 * TOP_K, :] = jnp.zeros((TOP_K, tb), F32)
    run_ref[...] = run_ref[...] + jnp.sum(multi, axis=1, keepdims=True)
    cnt_ref[...] = run_ref[...]


def _out_ln_route_body(mix_ref, mem_ref, x_ref, w1_ref, w2_ref, g_ref, b_ref, wt_ref, rb_ref,
                       o_ref, ort_ref, idx_ref, gate_ref, cnt_ref, run_ref, *, alpha):
    hproj = _dot(mix_ref[...], w1_ref[...]) + _dot(mem_ref[...], w2_ref[...])
    y = _layer_norm(alpha * x_ref[...] + hproj, g_ref[...], b_ref[...])
    o_ref[...] = y
    _to_row_tiles(ort_ref, y)
    _route(y, wt_ref, rb_ref, idx_ref, gate_ref, cnt_ref, run_ref)


def _out_proj_ln_route(mix, mem_out, x, w_out, g, b, alpha, router_w, router_b, tm=512):
    t, d = x.shape
    tm = min(tm, t)
    wm = mix.shape[1]
    we = mem_out.shape[1]
    n_e = router_w.shape[1]
    return pl.pallas_call(
        functools.partial(_out_ln_route_body, alpha=alpha),
        grid=(t // tm,),
        in_specs=[
            pl.BlockSpec((tm, wm), lambda i: (i, 0)),
            pl.BlockSpec((tm, we), lambda i: (i, 0)),
            pl.BlockSpec((tm, d), lambda i: (i, 0)),
            pl.BlockSpec((wm, d), lambda i: (0, 0)),
            pl.BlockSpec((we, d), lambda i: (0, 0)),
            pl.BlockSpec((1, d), lambda i: (0, 0)),
            pl.BlockSpec((1, d), lambda i: (0, 0)),
            pl.BlockSpec((n_e, d), lambda i: (0, 0)),
            pl.BlockSpec((n_e, 1), lambda i: (0, 0)),
        ],
        out_specs=[
            pl.BlockSpec((tm, d), lambda i: (i, 0)),
            pl.BlockSpec((tm * (d // HEAD_DIM), HEAD_DIM), lambda i: (i, 0)),
            pl.BlockSpec((2 * TOP_K, tm), lambda i: (0, i)),
            pl.BlockSpec((2 * TOP_K, tm), lambda i: (0, i)),
            pl.BlockSpec((n_e, HEAD_DIM), lambda i: (0, 0)),
        ],
        out_shape=[
            jax.ShapeDtypeStruct((t, d), F32),
            jax.ShapeDtypeStruct((t * (d // HEAD_DIM), HEAD_DIM), F32),
            jax.ShapeDtypeStruct((2 * TOP_K, t), I32),
            jax.ShapeDtypeStruct((2 * TOP_K, t), F32),
            jax.ShapeDtypeStruct((n_e, HEAD_DIM), F32),
        ],
        scratch_shapes=[pltpu.VMEM((n_e, HEAD_DIM), F32)],
        compiler_params=_cparams("arbitrary"),
        name="out_proj_ln_route",
    )(mix, mem_out, x, w_out[:wm].astype(BF16), w_out[wm:].astype(BF16), g.reshape(1, d), b.reshape(1, d),
      router_w.T, router_b.reshape(n_e, 1))


def _dispatch_body(dest_ref, pad_lo_ref, pad_hi_ref, nu_ref, x_ref, xs_hbm, zero_ref, sem, zsem,
                   *, tb, n_tok, n_tiles, n_experts, n_blocks):
    base = pl.program_id(0) * tb

    def issue(t, carry):
        for kk in range(TOP_K):
            d = dest_ref[kk * n_tok + base + t]
            pltpu.make_async_copy(_row_tile(x_ref, t, n_tiles), _row_tile(xs_hbm, d, n_tiles), sem.at[kk]).start(
                priority=kk % 2)
        return carry

    lax.fori_loop(0, tb, issue, 0, unroll=DMA_UNROLL)

    @pl.when(pl.program_id(0) == 0)
    def _():
        zero_ref[...] = jnp.zeros_like(zero_ref)

        def pad_row(r):
            return pltpu.make_async_copy(zero_ref.at[pl.ds(0, n_tiles)], _row_tile(xs_hbm, r, n_tiles), zsem)

        def spare_block(blk):
            rows = pl.ds(pl.multiple_of(blk * MOE_BLOCK * n_tiles, MOE_BLOCK * n_tiles), MOE_BLOCK * n_tiles)
            return pltpu.make_async_copy(zero_ref, xs_hbm.at[rows], zsem)

        def for_all(action):
            for e in range(n_experts):
                lax.fori_loop(pad_lo_ref[e], pad_hi_ref[e], lambda r, c: (action(pad_row(r)), c)[1], 0)
            lax.fori_loop(nu_ref[0], n_blocks, lambda blk, c: (action(spare_block(blk)), c)[1], 0)

        for_all(lambda cp: cp.start())
        for_all(lambda cp: cp.wait())

    for kk in range(TOP_K):
        pltpu.make_async_copy(x_ref, xs_hbm.at[pl.ds(0, tb * n_tiles)], sem.at[kk]).wait()


def _dispatch(x, n_tiles, dest_flat, pad_lo, pad_hi, n_used, n_rows, tb=2048):
    t = x.shape[0] // n_tiles
    tb = min(tb, t)
    grid_spec = pltpu.PrefetchScalarGridSpec(
        num_scalar_prefetch=4,
        grid=(t // tb,),
        in_specs=[pl.BlockSpec((tb * n_tiles, HEAD_DIM), lambda i, *_: (i, 0))],
        out_specs=pl.BlockSpec(memory_space=pl.ANY),
        scratch_shapes=[pltpu.VMEM((MOE_BLOCK * n_tiles, HEAD_DIM), x.dtype),
                        pltpu.SemaphoreType.DMA((TOP_K,)), pltpu.SemaphoreType.DMA(())],
    )
    return pl.pallas_call(
        functools.partial(_dispatch_body, tb=tb, n_tok=t, n_tiles=n_tiles, n_experts=pad_lo.shape[0],
                          n_blocks=n_rows // MOE_BLOCK),
        grid_spec=grid_spec,
        out_shape=jax.ShapeDtypeStruct((n_rows * n_tiles, HEAD_DIM), x.dtype),
        compiler_params=_cparams("arbitrary"),
        name="moe_dispatch",
    )(dest_flat, pad_lo, pad_hi, n_used, x)


UNZIP = 256


def _expert_body(be_ref, nu_ref, first_ref, slot_ref, next_ref, x_ref, wu_hbm, bg_ref, bl_ref, wd_hbm, bd_ref,
                 o_ref, wu_buf, wd_buf, wg_s, wl_s, wd_s, sem, *, layer, n_tiles):
    i = pl.program_id(0)
    n_used = nu_ref[0]

    def weight_copies(expert, slot):
        return (pltpu.make_async_copy(wu_hbm.at[layer, expert], wu_buf.at[slot], sem.at[0, slot]),
                pltpu.make_async_copy(wd_hbm.at[layer, expert], wd_buf.at[slot], sem.at[1, slot]))

    @pl.when(i == 0)
    def _():
        for cp in weight_copies(be_ref[0], 0):
            cp.start()

    @pl.when(jnp.logical_and(first_ref[i] == 1, i < n_used))
    def _():
        slot = slot_ref[i]
        for cp in weight_copies(be_ref[i], slot):
            cp.wait()

        @pl.when(next_ref[i] >= 0)
        def _():
            for cp in weight_copies(next_ref[i], 1 - slot):
                cp.start()

        half = UNZIP // 2
        r = lax.broadcasted_iota(I32, (UNZIP, UNZIP), 0)
        c = lax.broadcasted_iota(I32, (UNZIP, UNZIP), 1)
        src = jnp.where(c < half, 2 * c, 2 * (c - half) + 1)
        perm = jnp.where(r == src, 1.0, 0.0).astype(BF16)
        for g in range(wu_buf.shape[2] // UNZIP):
            w = wu_buf[slot, :, g * UNZIP:(g + 1) * UNZIP].astype(BF16)
            sep = _dot(w, perm).astype(BF16)
            wg_s[:, g * half:(g + 1) * half] = sep[:, :half]
            wl_s[:, g * half:(g + 1) * half] = sep[:, half:]
        wd_s[...] = wd_buf[slot].astype(BF16)

    @pl.when(i < n_used)
    def _():
        xb = _from_row_tiles(x_ref, MOE_BLOCK, n_tiles).astype(BF16)
        glu = jnp.minimum(_dot(xb, wg_s[...]) + bg_ref[0], SWIGLU_LIMIT)
        lin = jnp.clip(_dot(xb, wl_s[...]) + bl_ref[0], -SWIGLU_LIMIT, SWIGLU_LIMIT)
        act = glu * _sigmoid(SWIGLU_ALPHA * glu) * (lin + 1.0)
        _to_row_tiles(o_ref, _dot(act.astype(BF16), wd_s[...]) + bd_ref[0])

    @pl.when(i >= n_used)
    def _():
        o_ref[...] = jnp.zeros_like(o_ref)


def _expert_mlp(xs, n_tiles, blk_expert, n_used, counts, layer, w_up, bg, bl, w_down, bd):
    p_rows = xs.shape[0] // n_tiles
    d = n_tiles * HEAD_DIM
    f = w_down.shape[2]
    n_e = w_down.shape[1]
    n_blk = p_rows // MOE_BLOCK

    blocks = jnp.arange(n_blk, dtype=I32)
    valid = blocks < n_used[0]
    prev_e = jnp.concatenate([jnp.full((1,), -1, I32), blk_expert[:-1]])
    first = jnp.logical_and(valid, blk_expert != prev_e)
    slot = (jnp.cumsum(first.astype(I32)) - 1) % 2
    experts = jnp.arange(n_e, dtype=I32)
    later = jnp.logical_and(experts[None, :] > blk_expert[:, None], counts[None, :] > 0)
    nxt = jnp.min(jnp.where(later, experts[None, :], n_e), axis=1)
    nxt = jnp.where(nxt == n_e, -1, nxt)

    def blk(i, nu):
        return jnp.minimum(i, nu[0] - 1)

    def w_spec(shape):
        return pl.BlockSpec((1,) + shape, lambda i, be, nu, fi, sl, nx: (be[blk(i, nu)], 0, 0))

    grid_spec = pltpu.PrefetchScalarGridSpec(
        num_scalar_prefetch=5,
        grid=(n_blk,),
        in_specs=[
            pl.BlockSpec((MOE_BLOCK * n_tiles, HEAD_DIM), lambda i, be, nu, fi, sl, nx: (blk(i, nu), 0)),
            pl.BlockSpec(memory_space=pl.ANY), w_spec((1, f)), w_spec((1, f)),
            pl.BlockSpec(memory_space=pl.ANY), w_spec((1, d)),
        ],
        out_specs=pl.BlockSpec((MOE_BLOCK * n_tiles, HEAD_DIM), lambda i, be, nu, fi, sl, nx: (i, 0)),
        scratch_shapes=[
            pltpu.VMEM((2, d, 2 * f), F32), pltpu.VMEM((2, f, d), F32),
            pltpu.VMEM((d, f), BF16), pltpu.VMEM((d, f), BF16), pltpu.VMEM((f, d), BF16),
            pltpu.SemaphoreType.DMA((2, 2)),
        ],
    )
    return pl.pallas_call(
        functools.partial(_expert_body, layer=layer, n_tiles=n_tiles),
        grid_spec=grid_spec,
        out_shape=jax.ShapeDtypeStruct(xs.shape, F32),
        compiler_params=_cparams("arbitrary"),
        name="moe_experts",
    )(blk_expert, n_used, first.astype(I32), slot.astype(I32), nxt.astype(I32), xs, w_up, bg, bl, w_down, bd)


def _combine_body(dest_ref, ys_hbm, gate_ref, x_ref, g_ref, b_ref, o_ref, buf_ref, sem_ref,
                  *, tb, n_tok, n_tiles, alpha):
    i = pl.program_id(0)
    n_steps = pl.num_programs(0)

    def issue(step, slot):
        base = step * tb

        def one(t, carry):
            for kk in range(TOP_K):
                d = dest_ref[kk * n_tok + base + t]
                pltpu.make_async_copy(_row_tile(ys_hbm, d, n_tiles), _row_tile(buf_ref, t, n_tiles, slot, kk),
                                      sem_ref.at[slot]).start(priority=kk % 2)
            return carry

        lax.fori_loop(0, tb, one, 0, unroll=DMA_UNROLL)

    @pl.when(i == 0)
    def _():
        issue(0, 0)

    slot = lax.rem(i, 2)

    @pl.when(i + 1 < n_steps)
    def _():
        issue(i + 1, 1 - slot)

    for kk in range(TOP_K):
        pltpu.make_async_copy(ys_hbm.at[pl.ds(0, tb * n_tiles)], buf_ref.at[slot, kk], sem_ref.at[slot]).wait()

    g8 = gate_ref[...]
    gpad = jnp.concatenate([g8, jnp.zeros((HEAD_DIM - g8.shape[0], tb), F32)], axis=0)
    gt = gpad.T
    f = gt[:, 0:1] * _from_row_tiles(buf_ref, tb, n_tiles, slot, 0)
    for kk in range(1, TOP_K):
        f = f + gt[:, kk:kk + 1] * _from_row_tiles(buf_ref, tb, n_tiles, slot, kk)
    o_ref[...] = _layer_norm(alpha * x_ref[...] + f, g_ref[...], b_ref[...])


def _combine_ln(ys, dest_flat, gates, x, g, b, alpha, tb=512):
    t, d = x.shape
    tb = min(tb, t)
    n_tiles = d // HEAD_DIM
    grid_spec = pltpu.PrefetchScalarGridSpec(
        num_scalar_prefetch=1,
        grid=(t // tb,),
        in_specs=[
            pl.BlockSpec(memory_space=pl.ANY),
            pl.BlockSpec((2 * TOP_K, tb), lambda i, dest: (0, i)),
            pl.BlockSpec((tb, d), lambda i, dest: (i, 0)),
            pl.BlockSpec((1, d), lambda i, dest: (0, 0)),
            pl.BlockSpec((1, d), lambda i, dest: (0, 0)),
        ],
        out_specs=pl.BlockSpec((tb, d), lambda i, dest: (i, 0)),
        scratch_shapes=[pltpu.VMEM((2, TOP_K, tb * n_tiles, HEAD_DIM), F32), pltpu.SemaphoreType.DMA((2,))],
    )
    return pl.pallas_call(
        functools.partial(_combine_body, tb=tb, n_tok=t, n_tiles=n_tiles, alpha=alpha),
        grid_spec=grid_spec,
        out_shape=jax.ShapeDtypeStruct((t, d), F32),
        compiler_params=_cparams("arbitrary"),
        name="moe_combine_ln",
    )(dest_flat, ys, gates, x, g.reshape(1, d), b.reshape(1, d))


def _moe_ln(x, x_tiles, idx_rank, gates, cnt, layer, w_up, b_up, w_down, b_down, g, b, alpha):
    t, d = x.shape
    n_tiles = d // HEAD_DIM
    n_e = cnt.shape[0]
    counts = cnt[:, 0].astype(I32)
    padded = ((counts + MOE_BLOCK - 1) // MOE_BLOCK) * MOE_BLOCK
    p_ends = jnp.cumsum(padded)
    p_starts = p_ends - padded
    n_blk = (t * TOP_K + MOE_BLOCK - 1) // MOE_BLOCK + n_e
    experts = jnp.arange(n_e, dtype=I32)
    start_of = jnp.sum(jnp.where(idx_rank[:TOP_K, :, None] == experts, p_starts, 0), axis=-1)
    dest_flat = (start_of + idx_rank[TOP_K:]).reshape(-1)
    blk_first_row = jnp.arange(n_blk, dtype=I32) * MOE_BLOCK
    blk_expert = jnp.clip(jnp.sum((p_ends[None, :] <= blk_first_row[:, None]).astype(I32), axis=1), 0, n_e - 1)
    n_used = (p_ends[-1:] // MOE_BLOCK).astype(I32)

    xs = _dispatch(x_tiles, n_tiles, dest_flat, p_starts + counts, p_ends, n_used, n_blk * MOE_BLOCK)
    bg = b_up[:, None, 0::2]
    bl = b_up[:, None, 1::2]
    ys = _expert_mlp(xs, n_tiles, blk_expert, n_used, counts, layer, w_up, bg, bl, w_down, b_down[:, None, :])
    return _combine_ln(ys, dest_flat, gates, x, g, b, alpha)


def kernel(x, mem, hgrn_w_in, hgrn_lb, hgrn_norm_g, fox_w_in, fox_b_f, moba_w_in, rel_bias, w_mem_kv, w_out,
           ln1_g, ln1_b, router_w, router_b, w_up, b_up, w_down, b_down, ln2_g, ln2_b):
    bsz, seq, d = x.shape
    depth = w_out.shape[0]
    t = bsz * seq
    mix_w = MIX_HEADS * HEAD_DIM
    mem_w = MEM_HEADS * HEAD_DIM
    alpha = (2 * depth) ** 0.25
    scale = HEAD_DIM ** -0.5
    n_mem = mem.shape[1]
    mem2 = mem.reshape(bsz * n_mem, d)

    def scale_vec(n_mixer_q, n_plain, n_mem_q):
        return jnp.concatenate([
            jnp.full((1, n_mixer_q), scale * LOG2E, F32), jnp.ones((1, n_plain), F32),
            jnp.full((1, n_mem_q), scale, F32)], axis=1)

    x2 = x.reshape(t, d)
    for i in range(depth):
        kind, j = i % 3, i // 3
        if kind == 0:
            w_in = hgrn_w_in[j].astype(BF16)
            proj = _matmul(x2, w_in, scale_vec(0, 4 * mix_w, mem_w), F32, 1024, w_in.shape[1] // 3)
            proj = proj.reshape(bsz, seq, -1)
            mix = _hgrn_mixer(proj, hgrn_lb, hgrn_norm_g[j], j)
            memq_block = 4 * mix_w // mem_w
        elif kind == 1:
            w = fox_w_in[j]
            w_in = jnp.concatenate([w[:, :3 * mix_w], w[:, 3 * mix_w + MIX_HEADS:]], axis=1).astype(BF16)
            proj = _matmul(x2, w_in, scale_vec(mix_w, 2 * mix_w, mem_w), BF16, 1024, w_in.shape[1] // 2)
            proj = proj.reshape(bsz, seq, -1)
            c_col, c_row = _fox_gate_cumsum(x2.reshape(bsz, seq, d), w[:, 3 * mix_w:3 * mix_w + MIX_HEADS],
                                            fox_b_f[j])
            mix = _fox_attention(proj, c_col, c_row)
            memq_block = 3 * mix_w // mem_w
        else:
            w_in = moba_w_in[j].astype(BF16)
            proj = _matmul(x2, w_in, scale_vec(mix_w, 2 * mix_w, mem_w), BF16, 1024, w_in.shape[1] // 2)
            proj = proj.reshape(bsz, seq, -1)
            mix = _moba_attention(proj, rel_bias)
            memq_block = 3 * mix_w // mem_w
        kv = _matmul(mem2, w_mem_kv[i].astype(BF16), jnp.ones((1, 2 * mem_w), F32), BF16, 512, 512)
        mem_out = _memory_attention(proj, memq_block, kv.reshape(bsz, n_mem, 2 * mem_w))
        x2, x2_tiles, idx_rank, gates, cnt = _out_proj_ln_route(
            mix.reshape(t, mix_w), mem_out.reshape(t, mem_w), x2, w_out[i], ln1_g[i], ln1_b[i], alpha,
            router_w[i], router_b[i])
        x2 = _moe_ln(x2, x2_tiles, idx_rank, gates, cnt, i, w_up, b_up[i], w_down, b_down[i],
                     ln2_g[i], ln2_b[i], alpha)
    return x2.reshape(bsz, seq, d)
```

```python
import functools
import math

import numpy as np
import jax
import jax.numpy as jnp
from jax import lax
from jax.experimental import pallas as pl
from jax.experimental.pallas import tpu as pltpu

F32 = jnp.float32
BF16 = jnp.bfloat16
I32 = jnp.int32

HEAD_DIM = 128
MIX_HEADS = 8
MEM_HEADS = 4
HGRN_CHUNK = 64
HGRN_SUB = 16
HGRN_EXP_CLAMP = 80.0
MOBA_BLOCK = 256
MOBA_TOPK = 3
REL_BUCKETS = 32
REL_MAX_DIST = 128
N_EXPERTS = 32
TOP_K = 4
MOE_BLOCK = 256
SWIGLU_LIMIT = 7.0
SWIGLU_ALPHA = 1.702
LN_EPS = 1e-5
RMS_EPS = 1e-6
NEG = -1e30
LOG2E = 1.4426950408889634
VMEM_LIMIT = 56 * 1024 * 1024
DMA_UNROLL = 8

NT_DIMS = (((1,), (1,)), ((), ()))
TN_DIMS = (((0,), (0,)), ((), ()))


def _cparams(*sem):
    return pltpu.CompilerParams(dimension_semantics=sem, vmem_limit_bytes=VMEM_LIMIT)


def _dot(a, b):
    return jnp.dot(a, b, preferred_element_type=F32)


def _dot_nt(a, b):
    return lax.dot_general(a, b, NT_DIMS, preferred_element_type=F32)


def _dot_tn(a, b):
    return lax.dot_general(a, b, TN_DIMS, preferred_element_type=F32)


def _split3(a):
    hi = a.astype(BF16)
    r = a - hi.astype(F32)
    mid = r.astype(BF16)
    lo = (r - mid.astype(F32)).astype(BF16)
    return hi, mid, lo


def _sigmoid(x):
    return 1.0 / (1.0 + jnp.exp(-x))


def _layer_norm(z, g, b):
    mu = jnp.mean(z, axis=-1, keepdims=True)
    zc = z - mu
    var = jnp.mean(zc * zc, axis=-1, keepdims=True)
    return zc * lax.rsqrt(var + LN_EPS) * g + b


def _mm_body(x_ref, w_ref, s_ref, o_ref, xb_ref):
    @pl.when(pl.program_id(1) == 0)
    def _():
        xb_ref[...] = x_ref[...].astype(BF16)

    acc = _dot(xb_ref[...], w_ref[...])
    o_ref[...] = (acc * s_ref[...]).astype(o_ref.dtype)


def _matmul(x, w, col_scale, out_dtype, tm, tn):
    m, k = x.shape
    n = w.shape[1]
    tm = min(tm, m)
    tn = min(tn, n)
    return pl.pallas_call(
        _mm_body,
        grid=(m // tm, n // tn),
        in_specs=[
            pl.BlockSpec((tm, k), lambda i, j: (i, 0)),
            pl.BlockSpec((k, tn), lambda i, j: (0, j)),
            pl.BlockSpec((1, tn), lambda i, j: (0, j)),
        ],
        out_specs=pl.BlockSpec((tm, tn), lambda i, j: (i, j)),
        out_shape=jax.ShapeDtypeStruct((m, n), out_dtype),
        scratch_shapes=[pltpu.VMEM((tm, k), BF16)],
        compiler_params=_cparams("parallel", "arbitrary"),
        name="proj_matmul",
    )(x, w, col_scale)


def _hgrn_body(q_ref, z_ref, i_ref, g_ref, lbp_ref, ng_ref, o_ref, st_ref, *, layer_j, tb):
    c_len, sb = HGRN_CHUNK, HGRN_SUB
    n_sub = c_len // sb

    @pl.when(pl.program_id(2) == 0)
    def _():
        st_ref[...] = jnp.zeros_like(st_ref)

    lbp = lbp_ref[...]
    ex = jnp.exp(lbp - jnp.max(lbp, axis=0, keepdims=True))
    p = ex / jnp.sum(ex, axis=0, keepdims=True)
    lb = jnp.zeros((1, HEAD_DIM), F32)
    for r in range(1, layer_j + 1):
        lb = lb + p[r:r + 1, :]
    log_lb = jnp.log(lb)
    log1m_lb = jnp.log1p(-lb)
    one_m_lb = 1.0 - lb
    ng = ng_ref[...]

    row = lax.broadcasted_iota(I32, (c_len, c_len), 0)
    col = lax.broadcasted_iota(I32, (c_len, c_len), 1)
    sub_start = (row // sb) * sb
    one = jnp.ones((c_len, c_len), F32)
    zero = jnp.zeros((c_len, c_len), F32)
    m_tri = jnp.where(col <= row, one, zero)
    m_in = jnp.where(col > sub_start, m_tri, zero)
    m_dec = jnp.where(col > row, one, zero)
    m_ke = jnp.where(col <= sub_start + (sb - 1), m_dec, zero)
    stack = jnp.concatenate([m_tri, m_in, m_ke, m_dec], axis=0).astype(BF16)
    same_sub = (row // sb) == (col // sb)
    diag_mask = jnp.logical_and(same_sub, col <= row)
    row_sub = row // sb
    krow_sub = lax.broadcasted_iota(I32, (c_len, HEAD_DIM), 0) // sb

    chunks = [pl.ds(c * c_len, c_len) for c in range(tb // c_len)]

    gates = []
    for sl in chunks:
        z = z_ref[0, sl, :]
        e = jnp.exp(-jnp.abs(z))
        inv = 1.0 / (1.0 + e)
        sig_neg = jnp.where(z >= 0, e, 1.0) * inv
        log_sig = jnp.minimum(z, 0.0) - jnp.log(1.0 + e)
        a2 = log1m_lb + log_sig
        log_f = jnp.maximum(log_lb, a2) + jnp.log(1.0 + jnp.exp(-jnp.abs(log_lb - a2)))
        qv = q_ref[0, sl, :]
        gates.append((log_f, one_m_lb * sig_neg, qv * _sigmoid(qv), i_ref[0, sl, :].astype(BF16)))

    prefix = []
    for log_f, _, _, _ in gates:
        pre = None
        for piece in _split3(log_f):
            t = _dot(stack, piece)
            pre = t if pre is None else pre + t
        prefix.append(pre)

    factors = []
    for (_, k, qf, _), pre in zip(gates, prefix):
        b = pre[0:c_len]
        d_in = pre[c_len:2 * c_len]
        d_ke = pre[2 * c_len:3 * c_len]
        d_dec = pre[3 * c_len:4 * c_len]
        q_in = qf * jnp.exp(d_in)
        k_diag = (k * jnp.exp(jnp.minimum(-d_in, HGRN_EXP_CLAMP))).astype(BF16)
        k_end = k * jnp.exp(d_ke)
        k_dec = (k * jnp.exp(d_dec)).astype(BF16)
        q_all = (qf * jnp.exp(b)).astype(BF16)
        b_ref = b - d_in
        lhs = [q_in.astype(BF16)]
        rhs = [k_diag]
        for j in range(n_sub - 1):
            b_end = b[(j + 1) * sb - 1:(j + 1) * sb, :]
            cross = jnp.exp(jnp.minimum(b_ref - b_end, 0.0))
            lhs.append((q_in * cross).astype(BF16))
            rhs.append(jnp.where(krow_sub == j, k_end, 0.0).astype(BF16))
        factors.append((lhs, rhs, k_dec, q_all, jnp.exp(b[c_len - 1:c_len, :])))

    scores = []
    for lhs, rhs, _, _, _ in factors:
        a = jnp.where(diag_mask, _dot_nt(lhs[0], rhs[0]), 0.0)
        for j in range(n_sub - 1):
            a = a + jnp.where(row_sub > j, _dot_nt(lhs[j + 1], rhs[j + 1]), 0.0)
        scores.append(a.astype(BF16))

    updates = [_dot_tn(vb, k_dec) for (_, _, _, vb), (_, _, k_dec, _, _) in zip(gates, factors)]
    states = [st_ref[...]]
    for (_, _, _, _, decay), kv in zip(factors, updates):
        states.append(states[-1] * decay + kv)
    st_ref[...] = states[-1]

    for sl, (_, _, _, vb), (_, _, _, q_all, _), a, st in zip(chunks, gates, factors, scores, states):
        o = _dot(a, vb) + _dot_nt(q_all, st.astype(BF16))
        o = o * lax.rsqrt(jnp.mean(o * o, axis=-1, keepdims=True) + RMS_EPS)
        gv = g_ref[0, sl, :]
        o_ref[0, sl, :] = (o * ng * (gv * _sigmoid(gv))).astype(o_ref.dtype)


def _hgrn_mixer(proj, lb_param, norm_g, layer_j, tb=4096):
    bsz, seq, _ = proj.shape
    tb = min(tb, seq)
    n_layers = lb_param.shape[0]

    def col_block(off):
        return pl.BlockSpec((1, tb, HEAD_DIM), lambda b, h, t: (b, t, off + h))

    return pl.pallas_call(
        functools.partial(_hgrn_body, layer_j=layer_j, tb=tb),
        grid=(bsz, MIX_HEADS, seq // tb),
        in_specs=[
            col_block(0), col_block(MIX_HEADS), col_block(2 * MIX_HEADS), col_block(3 * MIX_HEADS),
            pl.BlockSpec((n_layers, HEAD_DIM), lambda b, h, t: (0, h)),
            pl.BlockSpec((1, HEAD_DIM), lambda b, h, t: (0, h)),
        ],
        out_specs=pl.BlockSpec((1, tb, HEAD_DIM), lambda b, h, t: (b, t, h)),
        out_shape=jax.ShapeDtypeStruct((bsz, seq, MIX_HEADS * HEAD_DIM), BF16),
        scratch_shapes=[pltpu.VMEM((HEAD_DIM, HEAD_DIM), F32)],
        compiler_params=_cparams("parallel", "parallel", "arbitrary"),
        name="hgrn2_mixer",
    )(proj, proj, proj, proj, lb_param, norm_g.reshape(1, -1))


def _log_sigmoid(z):
    return jnp.minimum(z, 0.0) - jnp.log1p(jnp.exp(-jnp.abs(z)))


def _fox_gate_body(x_ref, w_ref, wt_ref, bc_ref, br_ref, ccol_ref, crow_ref, carc_ref, carr_ref, *, tb):
    @pl.when(pl.program_id(1) == 0)
    def _():
        carc_ref[...] = jnp.zeros_like(carc_ref)
        carr_ref[...] = jnp.zeros_like(carr_ref)

    xb = x_ref[0].astype(BF16)
    ls_col = LOG2E * _log_sigmoid(_dot(xb, w_ref[...]) + bc_ref[...])
    ls_row = LOG2E * _log_sigmoid(_dot_nt(wt_ref[...], xb) + br_ref[...])
    row = lax.broadcasted_iota(I32, (tb, tb), 0)
    col = lax.broadcasted_iota(I32, (tb, tb), 1)
    lower = jnp.where(col <= row, 1.0, 0.0).astype(BF16)
    upper = jnp.where(row <= col, 1.0, 0.0).astype(BF16)
    c_col = carc_ref[...]
    for piece in _split3(ls_col):
        c_col = c_col + _dot(lower, piece)
    c_row = carr_ref[...]
    for piece in _split3(ls_row):
        c_row = c_row + _dot(piece, upper)
    ccol_ref[0] = c_col
    crow_ref[0] = c_row
    carc_ref[...] = c_col[tb - 1:tb, :]
    carr_ref[...] = c_row[:, tb - 1:tb]


def _fox_gate_cumsum(x, w_f, b_f, tb=512):
    bsz, seq, d = x.shape
    h = w_f.shape[1]
    tb = min(tb, seq)
    return pl.pallas_call(
        functools.partial(_fox_gate_body, tb=tb),
        grid=(bsz, seq // tb),
        in_specs=[
            pl.BlockSpec((1, tb, d), lambda b, t: (b, t, 0)),
            pl.BlockSpec((d, h), lambda b, t: (0, 0)),
            pl.BlockSpec((h, d), lambda b, t: (0, 0)),
            pl.BlockSpec((1, h), lambda b, t: (0, 0)),
            pl.BlockSpec((h, 1), lambda b, t: (0, 0)),
        ],
        out_specs=[
            pl.BlockSpec((1, tb, h), lambda b, t: (b, t, 0)),
            pl.BlockSpec((1, h, tb), lambda b, t: (b, 0, t)),
        ],
        out_shape=[
            jax.ShapeDtypeStruct((bsz, seq, h), F32),
            jax.ShapeDtypeStruct((bsz, h, seq), F32),
        ],
        scratch_shapes=[pltpu.VMEM((1, h), F32), pltpu.VMEM((h, 1), F32)],
        compiler_params=_cparams("parallel", "arbitrary"),
        name="fox_gate_cumsum",
    )(x, w_f.astype(BF16), w_f.T.astype(BF16), b_f.reshape(1, h), b_f.reshape(h, 1))


ATTN_ROWS = 128


def _softmax_probs(u, shift, rows, m_ref, l_ref):
    reps = u.shape[1] // HEAD_DIM
    m_old = m_ref[rows, :]
    if shift is None:
        m_new = jnp.maximum(m_old, jnp.max(u, axis=-1, keepdims=True))
        p = jnp.exp2(u - jnp.tile(m_new, (1, reps)))
    else:
        m_new = jnp.maximum(m_old, jnp.max(u, axis=-1, keepdims=True) + shift)
        p = jnp.exp2(u - jnp.tile(m_new - shift, (1, reps)))
    alpha = jnp.exp2(m_old - m_new)
    l_ref[rows, :] = alpha * l_ref[rows, :] + jnp.sum(p, axis=-1, keepdims=True)
    m_ref[rows, :] = m_new
    return p.astype(BF16), alpha


def _accumulate(p, alpha, v, rows, acc_ref):
    acc_ref[rows, :] = alpha * acc_ref[rows, :] + _dot(p, v)


def _softmax_init(m_ref, l_ref, acc_ref):
    m_ref[...] = jnp.full(m_ref.shape, NEG, F32)
    l_ref[...] = jnp.zeros_like(l_ref)
    acc_ref[...] = jnp.zeros_like(acc_ref)


def _fox_attn_body(q_ref, k_ref, v_ref, ccol_ref, crow_ref, o_ref, ct_ref, m_ref, l_ref, acc_ref, s_ref, *, tq):
    rs = min(ATTN_ROWS, tq)
    n_groups = tq // rs
    h = pl.program_id(1)
    qi = pl.program_id(2)
    ccol = ccol_ref[0]
    lane = lax.broadcasted_iota(I32, ccol.shape, 1)
    c_t = jnp.sum(jnp.where(lane == h, ccol, 0.0), axis=1, keepdims=True)
    ct_ref[...] = jnp.broadcast_to(c_t, ct_ref.shape)
    _softmax_init(m_ref, l_ref, acc_ref)

    def keys(kb):
        return pl.ds(pl.multiple_of(kb * tq, tq), tq)

    def logits(kb, r):
        rows = pl.ds(r * rs, rs)
        ks = keys(kb)
        s_ref[kb % 2, rows, :] = _dot_nt(q_ref[0, rows, :], k_ref[0, ks, :]) - crow_ref[0, pl.ds(h, 1), ks]

    def attend(kb, r, diagonal, prefetch):
        rows = pl.ds(r * rs, rs)
        u = s_ref[kb % 2, rows, :]
        if diagonal:
            row = lax.broadcasted_iota(I32, (rs, tq), 0) + r * rs
            col = lax.broadcasted_iota(I32, (rs, tq), 1)
            u = jnp.where(col <= row, u, NEG)
        if prefetch:
            logits(kb + 1, r)
        p, alpha = _softmax_probs(u, ct_ref[rows, :], rows, m_ref, l_ref)
        _accumulate(p, alpha, v_ref[0, keys(kb), :], rows, acc_ref)

    for r in range(n_groups):
        logits(0, r)

    def full_block(kb, carry):
        for r in range(n_groups):
            attend(kb, r, False, True)
        return carry

    def block_pair(i, carry):
        return full_block(2 * i + 1, full_block(2 * i, carry))

    lax.fori_loop(0, qi // 2, block_pair, 0)

    @pl.when(qi % 2 == 1)
    def _():
        full_block(qi - 1, 0)

    for r in range(n_groups):
        attend(qi, r, True, False)
    o_ref[0] = (acc_ref[...] / l_ref[...]).astype(o_ref.dtype)


def _fox_attention(qkv, c_col, c_row, tq=512):
    bsz, seq, _ = qkv.shape
    tq = min(tq, seq)
    nh = MIX_HEADS
    return pl.pallas_call(
        functools.partial(_fox_attn_body, tq=tq),
        grid=(bsz, nh, seq // tq),
        in_specs=[
            pl.BlockSpec((1, tq, HEAD_DIM), lambda b, h, i: (b, i, h)),
            pl.BlockSpec((1, seq, HEAD_DIM), lambda b, h, i: (b, 0, nh + h)),
            pl.BlockSpec((1, seq, HEAD_DIM), lambda b, h, i: (b, 0, 2 * nh + h)),
            pl.BlockSpec((1, tq, nh), lambda b, h, i: (b, i, 0)),
            pl.BlockSpec((1, nh, seq), lambda b, h, i: (b, 0, 0)),
        ],
        out_specs=pl.BlockSpec((1, tq, HEAD_DIM), lambda b, h, i: (b, i, h)),
        out_shape=jax.ShapeDtypeStruct((bsz, seq, nh * HEAD_DIM), BF16),
        scratch_shapes=[pltpu.VMEM((tq, HEAD_DIM), F32)] * 4 + [pltpu.VMEM((2, tq, tq), F32)],
        compiler_params=_cparams("parallel", "parallel", "arbitrary"),
        name="fox_attention",
    )(qkv, qkv, qkv, c_col, c_row)


def _t5_bucket_thresholds():
    n = np.arange(0, 4 * REL_MAX_DIST, dtype=np.int64)
    max_exact = REL_BUCKETS // 2
    ratio = np.log(np.maximum(n, 1).astype(np.float32) / np.float32(max_exact)) / np.float32(
        math.log(REL_MAX_DIST / max_exact))
    large = np.minimum(max_exact + (ratio * np.float32(REL_BUCKETS - max_exact)).astype(np.int32),
                       REL_BUCKETS - 1)
    bucket = np.where(n < max_exact, n, large)
    assert np.all(np.diff(bucket) >= 0) and bucket[-1] == REL_BUCKETS - 1
    return [int(np.argmax(bucket >= b)) for b in range(REL_BUCKETS)]


def _moba_body(rb_ref, q_ref, k_ref, v_ref, o_ref, kmean_ref, bias_ref, sel_ref, m_ref, l_ref, acc_ref,
               s_ref, *, n_blocks):
    blk = MOBA_BLOCK
    tq = 2 * blk
    h = pl.program_id(1)
    ti = pl.program_id(2)
    far_bias = LOG2E * rb_ref[REL_BUCKETS - 1, h]

    @pl.when(ti == 0)
    def _():
        r = lax.broadcasted_iota(I32, (n_blocks, n_blocks * blk), 0)
        c = lax.broadcasted_iota(I32, (n_blocks, n_blocks * blk), 1)
        pool = jnp.where(c // blk == r, 1.0, 0.0).astype(BF16)
        kmean_ref[...] = _dot(pool, k_ref[0]) * (1.0 / blk)
        row = lax.broadcasted_iota(I32, (blk, blk), 0)
        col = lax.broadcasted_iota(I32, (blk, blk), 1)
        thresholds = _t5_bucket_thresholds()
        for slot, dist in ((0, row - col + blk), (1, row - col)):
            val = jnp.full((blk, blk), rb_ref[0, h], F32)
            for b in range(1, REL_BUCKETS):
                val = jnp.where(dist >= thresholds[b], rb_ref[b, h], val)
            val = LOG2E * val
            if slot == 1:
                val = jnp.where(dist >= 0, val, NEG)
            bias_ref[slot] = val

    q = q_ref[0]
    km_hi = kmean_ref[...].astype(BF16)
    km_lo = (kmean_ref[...] - km_hi.astype(F32)).astype(BF16)
    gate = _dot_nt(km_hi, q) + _dot_nt(km_lo, q)
    blk_id = lax.broadcasted_iota(I32, gate.shape, 0)
    own = 2 * ti + lax.broadcasted_iota(I32, gate.shape, 1) // blk
    gate = jnp.where(blk_id < own, gate, NEG)
    sel = jnp.zeros(gate.shape, F32)
    for _ in range(MOBA_TOPK):
        best = jnp.max(gate, axis=0, keepdims=True)
        first = jnp.min(jnp.where(gate == best, blk_id, n_blocks), axis=0, keepdims=True)
        pick = jnp.logical_and(blk_id == first, best > 0.5 * NEG)
        sel = jnp.where(pick, 1.0, sel)
        gate = jnp.where(blk_id == first, NEG, gate)
    sel_ref[...] = jnp.concatenate([sel, jnp.zeros((HEAD_DIM - n_blocks, tq), F32)], axis=0).T

    _softmax_init(m_ref, l_ref, acc_ref)
    groups = (pl.ds(0, blk), pl.ds(blk, blk))

    def selected(j, rows):
        lanes = lax.broadcasted_iota(I32, (blk, HEAD_DIM), 1)
        return jnp.sum(jnp.where(lanes == j, sel_ref[rows, :], 0.0), axis=1, keepdims=True) > 0.5

    def far_term(j, rows):
        return jnp.where(selected(j, rows), far_bias, NEG)

    def prev_term(j, rows):
        return bias_ref[0] + jnp.where(selected(j, rows), 0.0, NEG)

    def keys(unit, n_keys=tq):
        return pl.ds(pl.multiple_of(unit * tq, tq), n_keys)

    def logits(unit, rows):
        s_ref[unit % 2, rows, :] = _dot_nt(q_ref[0, rows, :], k_ref[0, keys(unit), :])

    def attend(unit, rows, term_a, term_b, prefetch):
        u = s_ref[unit % 2, rows, :]
        if term_b is None:
            u = u[:, :blk] + term_a
            ks = keys(unit, blk)
        else:
            u = jnp.concatenate([u[:, :blk] + term_a, u[:, blk:] + term_b], axis=1)
            ks = keys(unit)
        if prefetch:
            logits(unit + 1, rows)
        p, alpha = _softmax_probs(u, None, rows, m_ref, l_ref)
        _accumulate(p, alpha, v_ref[0, ks, :], rows, acc_ref)

    for rows in groups:
        logits(0, rows)

    def far_unit(unit, carry):
        for rows in groups:
            attend(unit, rows, far_term(2 * unit, rows), far_term(2 * unit + 1, rows), True)
        return carry

    def unit_pair(i, carry):
        return far_unit(2 * i + 1, far_unit(2 * i, carry))

    n_far_units = jnp.maximum(ti - 1, 0)
    lax.fori_loop(0, n_far_units // 2, unit_pair, 0)

    @pl.when(n_far_units % 2 == 1)
    def _():
        far_unit(n_far_units - 1, 0)

    @pl.when(ti >= 1)
    def _():
        j = 2 * ti - 2
        attend(ti - 1, groups[0], far_term(j, groups[0]), prev_term(j + 1, groups[0]), True)
        attend(ti - 1, groups[1], far_term(j, groups[1]), far_term(j + 1, groups[1]), True)

    attend(ti, groups[0], bias_ref[1], None, False)
    attend(ti, groups[1], prev_term(2 * ti, groups[1]), bias_ref[1], False)
    o_ref[0] = (acc_ref[...] / l_ref[...]).astype(o_ref.dtype)


def _moba_attention(qkv, rel_bias):
    bsz, seq, _ = qkv.shape
    nh = MIX_HEADS
    blk = MOBA_BLOCK
    tq = 2 * blk
    assert seq % tq == 0 and seq // blk <= HEAD_DIM
    n_blocks = seq // blk
    grid_spec = pltpu.PrefetchScalarGridSpec(
        num_scalar_prefetch=0,
        grid=(bsz, nh, seq // tq),
        in_specs=[
            pl.BlockSpec(memory_space=pltpu.SMEM),
            pl.BlockSpec((1, tq, HEAD_DIM), lambda b, h, i: (b, i, h)),
            pl.BlockSpec((1, seq, HEAD_DIM), lambda b, h, i: (b, 0, nh + h)),
            pl.BlockSpec((1, seq, HEAD_DIM), lambda b, h, i: (b, 0, 2 * nh + h)),
        ],
        out_specs=pl.BlockSpec((1, tq, HEAD_DIM), lambda b, h, i: (b, i, h)),
        scratch_shapes=[
            pltpu.VMEM((n_blocks, HEAD_DIM), F32),
            pltpu.VMEM((2, blk, blk), F32),
            pltpu.VMEM((tq, HEAD_DIM), F32),
            pltpu.VMEM((tq, HEAD_DIM), F32), pltpu.VMEM((tq, HEAD_DIM), F32), pltpu.VMEM((tq, HEAD_DIM), F32),
            pltpu.VMEM((2, tq, tq), F32),
        ],
    )
    return pl.pallas_call(
        functools.partial(_moba_body, n_blocks=n_blocks),
        grid_spec=grid_spec,
        out_shape=jax.ShapeDtypeStruct((bsz, seq, nh * HEAD_DIM), BF16),
        compiler_params=_cparams("parallel", "parallel", "arbitrary"),
        name="moba_attention",
    )(rel_bias, qkv, qkv, qkv)


def _mem_attn_body(q_ref, kv_ref, o_ref):
    width = MEM_HEADS * HEAD_DIM
    for hh in range(MEM_HEADS):
        cs = slice(hh * HEAD_DIM, (hh + 1) * HEAD_DIM)
        q = q_ref[0, :, cs].astype(BF16)
        k = kv_ref[0, :, cs]
        v = kv_ref[0, :, width + hh * HEAD_DIM:width + (hh + 1) * HEAD_DIM]
        s = _dot_nt(q, k)
        p = jnp.exp(s - jnp.max(s, axis=-1, keepdims=True))
        o = _dot(p.astype(BF16), v) / jnp.sum(p, axis=-1, keepdims=True)
        o_ref[0, :, cs] = o.astype(o_ref.dtype)


def _memory_attention(proj, q_col_block, kv, tq=512):
    bsz, seq, _ = proj.shape
    tq = min(tq, seq)
    width = MEM_HEADS * HEAD_DIM
    n_mem = kv.shape[1]
    return pl.pallas_call(
        _mem_attn_body,
        grid=(bsz, seq // tq),
        in_specs=[
            pl.BlockSpec((1, tq, width), lambda b, i: (b, i, q_col_block)),
            pl.BlockSpec((1, n_mem, 2 * width), lambda b, i: (b, 0, 0)),
        ],
        out_specs=pl.BlockSpec((1, tq, width), lambda b, i: (b, i, 0)),
        out_shape=jax.ShapeDtypeStruct((bsz, seq, width), BF16),
        compiler_params=_cparams("parallel", "parallel"),
        name="memory_attention",
    )(proj, kv)


def _to_row_tiles(ref, y):
    rows, d = y.shape
    n_tiles = d // HEAD_DIM
    for c in range(n_tiles):
        ref[pl.ds(c, rows, stride=n_tiles), :] = y[:, c * HEAD_DIM:(c + 1) * HEAD_DIM]


def _from_row_tiles(ref, rows, n_tiles, *lead):
    return jnp.concatenate(
        [ref[(*lead, pl.ds(c, rows, stride=n_tiles), slice(None))] for c in range(n_tiles)], axis=1)


def _row_tile(ref, r, n_tiles, *lead):
    return ref.at[(*lead, pl.ds(pl.multiple_of(r * n_tiles, n_tiles), n_tiles))]


def _route(x, wt_ref, b_ref, idx_ref, gate_ref, cnt_ref, run_ref):
    tb = x.shape[0]

    @pl.when(pl.program_id(0) == 0)
    def _():
        run_ref[...] = jnp.zeros_like(run_ref)

    xh = x.astype(BF16)
    xl = (x - xh.astype(F32)).astype(BF16)
    w = wt_ref[...]
    wh = w.astype(BF16)
    wl = (w - wh.astype(F32)).astype(BF16)
    logits = _dot_nt(wh, xh) + _dot_nt(wh, xl) + _dot_nt(wl, xh) + b_ref[...]

    n_e = logits.shape[0]
    eid = lax.broadcasted_iota(I32, logits.shape, 0)
    vals, idxs, hots = [], [], []
    multi = jnp.zeros(logits.shape, F32)
    for _ in range(TOP_K):
        best = jnp.max(logits, axis=0, keepdims=True)
        first = jnp.min(jnp.where(logits == best, eid, n_e), axis=0, keepdims=True)
        hot = eid == first
        vals.append(best)
        idxs.append(first)
        hots.append(hot)
        multi = jnp.where(hot, 1.0, multi)
        logits = jnp.where(hot, -jnp.inf, logits)

    exps = [jnp.exp(v - vals[0]) for v in vals]
    den = exps[0]
    for e in exps[1:]:
        den = den + e

    r = lax.broadcasted_iota(I32, (tb, tb), 0)
    c = lax.broadcasted_iota(I32, (tb, tb), 1)
    before = jnp.where(r < c, 1.0, 0.0).astype(BF16)
    pos = _dot(multi.astype(BF16), before) + run_ref[:, 0:1]
    for kk in range(TOP_K):
        idx_ref[kk:kk + 1, :] = idxs[kk]
        rank = jnp.sum(jnp.where(hots[kk], pos, 0.0), axis=0, keepdims=True)
        idx_ref[TOP_K + kk:TOP_K + kk + 1, :] = rank.astype(I32)
        gate_ref[kk:kk + 1, :] = exps[kk] / den
    gate_ref[TOP_K:2 * TOP_K, :] = jnp.zeros((TOP_K, tb), F32)
    run_ref[...] = run_ref[...] + jnp.sum(multi, axis=1, keepdims=True)
    cnt_ref[...] = run_ref[...]


def _out_ln_route_body(mix_ref, mem_ref, x_ref, w1_ref, w2_ref, g_ref, b_ref, wt_ref, rb_ref,
                       o_ref, ort_ref, idx_ref, gate_ref, cnt_ref, run_ref, *, alpha):
    hproj = _dot(mix_ref[...], w1_ref[...]) + _dot(mem_ref[...], w2_ref[...])
    y = _layer_norm(alpha * x_ref[...] + hproj, g_ref[...], b_ref[...])
    o_ref[...] = y
    _to_row_tiles(ort_ref, y)
    _route(y, wt_ref, rb_ref, idx_ref, gate_ref, cnt_ref, run_ref)


def _out_proj_ln_route(mix, mem_out, x, w_out, g, b, alpha, router_w, router_b, tm=512):
    t, d = x.shape
    tm = min(tm, t)
    wm = mix.shape[1]
    we = mem_out.shape[1]
    n_e = router_w.shape[1]
    return pl.pallas_call(
        functools.partial(_out_ln_route_body, alpha=alpha),
        grid=(t // tm,),
        in_specs=[
            pl.BlockSpec((tm, wm), lambda i: (i, 0)),
            pl.BlockSpec((tm, we), lambda i: (i, 0)),
            pl.BlockSpec((tm, d), lambda i: (i, 0)),
            pl.BlockSpec((wm, d), lambda i: (0, 0)),
            pl.BlockSpec((we, d), lambda i: (0, 0)),
            pl.BlockSpec((1, d), lambda i: (0, 0)),
            pl.BlockSpec((1, d), lambda i: (0, 0)),
            pl.BlockSpec((n_e, d), lambda i: (0, 0)),
            pl.BlockSpec((n_e, 1), lambda i: (0, 0)),
        ],
        out_specs=[
            pl.BlockSpec((tm, d), lambda i: (i, 0)),
            pl.BlockSpec((tm * (d // HEAD_DIM), HEAD_DIM), lambda i: (i, 0)),
            pl.BlockSpec((2 * TOP_K, tm), lambda i: (0, i)),
            pl.BlockSpec((2 * TOP_K, tm), lambda i: (0, i)),
            pl.BlockSpec((n_e, HEAD_DIM), lambda i: (0, 0)),
        ],
        out_shape=[
            jax.ShapeDtypeStruct((t, d), F32),
            jax.ShapeDtypeStruct((t * (d // HEAD_DIM), HEAD_DIM), F32),
            jax.ShapeDtypeStruct((2 * TOP_K, t), I32),
            jax.ShapeDtypeStruct((2 * TOP_K, t), F32),
            jax.ShapeDtypeStruct((n_e, HEAD_DIM), F32),
        ],
        scratch_shapes=[pltpu.VMEM((n_e, HEAD_DIM), F32)],
        compiler_params=_cparams("arbitrary"),
        name="out_proj_ln_route",
    )(mix, mem_out, x, w_out[:wm].astype(BF16), w_out[wm:].astype(BF16), g.reshape(1, d), b.reshape(1, d),
      router_w.T, router_b.reshape(n_e, 1))


def _dispatch_body(dest_ref, pad_lo_ref, pad_hi_ref, nu_ref, x_ref, xs_hbm, zero_ref, sem, zsem,
                   *, tb, n_tok, n_tiles, n_experts, n_blocks):
    base = pl.program_id(0) * tb

    def issue(t, carry):
        for kk in range(TOP_K):
            d = dest_ref[kk * n_tok + base + t]
            pltpu.make_async_copy(_row_tile(x_ref, t, n_tiles), _row_tile(xs_hbm, d, n_tiles), sem.at[kk]).start(
                priority=kk % 2)
        return carry

    lax.fori_loop(0, tb, issue, 0, unroll=DMA_UNROLL)

    @pl.when(pl.program_id(0) == 0)
    def _():
        zero_ref[...] = jnp.zeros_like(zero_ref)

        def pad_row(r):
            return pltpu.make_async_copy(zero_ref.at[pl.ds(0, n_tiles)], _row_tile(xs_hbm, r, n_tiles), zsem)

        def spare_block(blk):
            rows = pl.ds(pl.multiple_of(blk * MOE_BLOCK * n_tiles, MOE_BLOCK * n_tiles), MOE_BLOCK * n_tiles)
            return pltpu.make_async_copy(zero_ref, xs_hbm.at[rows], zsem)

        def for_all(action):
            for e in range(n_experts):
                lax.fori_loop(pad_lo_ref[e], pad_hi_ref[e], lambda r, c: (action(pad_row(r)), c)[1], 0)
            lax.fori_loop(nu_ref[0], n_blocks, lambda blk, c: (action(spare_block(blk)), c)[1], 0)

        for_all(lambda cp: cp.start())
        for_all(lambda cp: cp.wait())

    for kk in range(TOP_K):
        pltpu.make_async_copy(x_ref, xs_hbm.at[pl.ds(0, tb * n_tiles)], sem.at[kk]).wait()


def _dispatch(x, n_tiles, dest_flat, pad_lo, pad_hi, n_used, n_rows, tb=2048):
    t = x.shape[0] // n_tiles
    tb = min(tb, t)
    grid_spec = pltpu.PrefetchScalarGridSpec(
        num_scalar_prefetch=4,
        grid=(t // tb,),
        in_specs=[pl.BlockSpec((tb * n_tiles, HEAD_DIM), lambda i, *_: (i, 0))],
        out_specs=pl.BlockSpec(memory_space=pl.ANY),
        scratch_shapes=[pltpu.VMEM((MOE_BLOCK * n_tiles, HEAD_DIM), x.dtype),
                        pltpu.SemaphoreType.DMA((TOP_K,)), pltpu.SemaphoreType.DMA(())],
    )
    return pl.pallas_call(
        functools.partial(_dispatch_body, tb=tb, n_tok=t, n_tiles=n_tiles, n_experts=pad_lo.shape[0],
                          n_blocks=n_rows // MOE_BLOCK),
        grid_spec=grid_spec,
        out_shape=jax.ShapeDtypeStruct((n_rows * n_tiles, HEAD_DIM), x.dtype),
        compiler_params=_cparams("arbitrary"),
        name="moe_dispatch",
    )(dest_flat, pad_lo, pad_hi, n_used, x)


UNZIP = 256


def _expert_body(be_ref, nu_ref, first_ref, slot_ref, next_ref, x_ref, wu_hbm, bg_ref, bl_ref, wd_hbm, bd_ref,
                 o_ref, wu_buf, wd_buf, wg_s, wl_s, wd_s, sem, *, layer, n_tiles):
    i = pl.program_id(0)
    n_used = nu_ref[0]

    def weight_copies(expert, slot):
        return (pltpu.make_async_copy(wu_hbm.at[layer, expert], wu_buf.at[slot], sem.at[0, slot]),
                pltpu.make_async_copy(wd_hbm.at[layer, expert], wd_buf.at[slot], sem.at[1, slot]))

    @pl.when(i == 0)
    def _():
        for cp in weight_copies(be_ref[0], 0):
            cp.start(priority=1)

    @pl.when(jnp.logical_and(first_ref[i] == 1, i < n_used))
    def _():
        slot = slot_ref[i]
        for cp in weight_copies(be_ref[i], slot):
            cp.wait()

        @pl.when(next_ref[i] >= 0)
        def _():
            for cp in weight_copies(next_ref[i], 1 - slot):
                cp.start(priority=1)

        half = UNZIP // 2
        r = lax.broadcasted_iota(I32, (UNZIP, UNZIP), 0)
        c = lax.broadcasted_iota(I32, (UNZIP, UNZIP), 1)
        src = jnp.where(c < half, 2 * c, 2 * (c - half) + 1)
        perm = jnp.where(r == src, 1.0, 0.0).astype(BF16)
        for g in range(wu_buf.shape[2] // UNZIP):
            w = wu_buf[slot, :, g * UNZIP:(g + 1) * UNZIP].astype(BF16)
            sep = _dot(w, perm).astype(BF16)
            wg_s[:, g * half:(g + 1) * half] = sep[:, :half]
            wl_s[:, g * half:(g + 1) * half] = sep[:, half:]
        wd_s[...] = wd_buf[slot].astype(BF16)

    @pl.when(i < n_used)
    def _():
        xb = _from_row_tiles(x_ref, MOE_BLOCK, n_tiles).astype(BF16)
        glu = jnp.minimum(_dot(xb, wg_s[...]) + bg_ref[0], SWIGLU_LIMIT)
        lin = jnp.clip(_dot(xb, wl_s[...]) + bl_ref[0], -SWIGLU_LIMIT, SWIGLU_LIMIT)
        act = glu * _sigmoid(SWIGLU_ALPHA * glu) * (lin + 1.0)
        _to_row_tiles(o_ref, _dot(act.astype(BF16), wd_s[...]) + bd_ref[0])

    @pl.when(i >= n_used)
    def _():
        o_ref[...] = jnp.zeros_like(o_ref)


def _expert_mlp(xs, n_tiles, blk_expert, n_used, counts, layer, w_up, bg, bl, w_down, bd):
    p_rows = xs.shape[0] // n_tiles
    d = n_tiles * HEAD_DIM
    f = w_down.shape[2]
    n_e = w_down.shape[1]
    n_blk = p_rows // MOE_BLOCK

    blocks = jnp.arange(n_blk, dtype=I32)
    valid = blocks < n_used[0]
    prev_e = jnp.concatenate([jnp.full((1,), -1, I32), blk_expert[:-1]])
    first = jnp.logical_and(valid, blk_expert != prev_e)
    slot = (jnp.cumsum(first.astype(I32)) - 1) % 2
    experts = jnp.arange(n_e, dtype=I32)
    later = jnp.logical_and(experts[None, :] > blk_expert[:, None], counts[None, :] > 0)
    nxt = jnp.min(jnp.where(later, experts[None, :], n_e), axis=1)
    nxt = jnp.where(nxt == n_e, -1, nxt)

    def blk(i, nu):
        return jnp.minimum(i, nu[0] - 1)

    def w_spec(shape):
        return pl.BlockSpec((1,) + shape, lambda i, be, nu, fi, sl, nx: (be[blk(i, nu)], 0, 0))

    grid_spec = pltpu.PrefetchScalarGridSpec(
        num_scalar_prefetch=5,
        grid=(n_blk,),
        in_specs=[
            pl.BlockSpec((MOE_BLOCK * n_tiles, HEAD_DIM), lambda i, be, nu, fi, sl, nx: (blk(i, nu), 0)),
            pl.BlockSpec(memory_space=pl.ANY), w_spec((1, f)), w_spec((1, f)),
            pl.BlockSpec(memory_space=pl.ANY), w_spec((1, d)),
        ],
        out_specs=pl.BlockSpec((MOE_BLOCK * n_tiles, HEAD_DIM), lambda i, be, nu, fi, sl, nx: (i, 0)),
        scratch_shapes=[
            pltpu.VMEM((2, d, 2 * f), F32), pltpu.VMEM((2, f, d), F32),
            pltpu.VMEM((d, f), BF16), pltpu.VMEM((d, f), BF16), pltpu.VMEM((f, d), BF16),
            pltpu.SemaphoreType.DMA((2, 2)),
        ],
    )
    return pl.pallas_call(
        functools.partial(_expert_body, layer=layer, n_tiles=n_tiles),
        grid_spec=grid_spec,
        out_shape=jax.ShapeDtypeStruct(xs.shape, F32),
        compiler_params=_cparams("arbitrary"),
        name="moe_experts",
    )(blk_expert, n_used, first.astype(I32), slot.astype(I32), nxt.astype(I32), xs, w_up, bg, bl, w_down, bd)


def _combine_body(dest_ref, ys_hbm, gate_ref, x_ref, g_ref, b_ref, o_ref, buf_ref, sem_ref,
                  *, tb, n_tok, n_tiles, alpha):
    i = pl.program_id(0)
    n_steps = pl.num_programs(0)

    def issue(step, slot):
        base = step * tb

        def one(t, carry):
            for kk in range(TOP_K):
                d = dest_ref[kk * n_tok + base + t]
                pltpu.make_async_copy(_row_tile(ys_hbm, d, n_tiles), _row_tile(buf_ref, t, n_tiles, slot, kk),
                                      sem_ref.at[slot]).start(priority=kk % 2)
            return carry

        lax.fori_loop(0, tb, one, 0, unroll=DMA_UNROLL)

    @pl.when(i == 0)
    def _():
        issue(0, 0)

    slot = lax.rem(i, 2)

    @pl.when(i + 1 < n_steps)
    def _():
        issue(i + 1, 1 - slot)

    for kk in range(TOP_K):
        pltpu.make_async_copy(ys_hbm.at[pl.ds(0, tb * n_tiles)], buf_ref.at[slot, kk], sem_ref.at[slot]).wait()

    g8 = gate_ref[...]
    gpad = jnp.concatenate([g8, jnp.zeros((HEAD_DIM - g8.shape[0], tb), F32)], axis=0)
    gt = gpad.T
    f = gt[:, 0:1] * _from_row_tiles(buf_ref, tb, n_tiles, slot, 0)
    for kk in range(1, TOP_K):
        f = f + gt[:, kk:kk + 1] * _from_row_tiles(buf_ref, tb, n_tiles, slot, kk)
    o_ref[...] = _layer_norm(alpha * x_ref[...] + f, g_ref[...], b_ref[...])


def _combine_ln(ys, dest_flat, gates, x, g, b, alpha, tb=512):
    t, d = x.shape
    tb = min(tb, t)
    n_tiles = d // HEAD_DIM
    grid_spec = pltpu.PrefetchScalarGridSpec(
        num_scalar_prefetch=1,
        grid=(t // tb,),
        in_specs=[
            pl.BlockSpec(memory_space=pl.ANY),
            pl.BlockSpec((2 * TOP_K, tb), lambda i, dest: (0, i)),
            pl.BlockSpec((tb, d), lambda i, dest: (i, 0)),
            pl.BlockSpec((1, d), lambda i, dest: (0, 0)),
            pl.BlockSpec((1, d), lambda i, dest: (0, 0)),
        ],
        out_specs=pl.BlockSpec((tb, d), lambda i, dest: (i, 0)),
        scratch_shapes=[pltpu.VMEM((2, TOP_K, tb * n_tiles, HEAD_DIM), F32), pltpu.SemaphoreType.DMA((2,))],
    )
    return pl.pallas_call(
        functools.partial(_combine_body, tb=tb, n_tok=t, n_tiles=n_tiles, alpha=alpha),
        grid_spec=grid_spec,
        out_shape=jax.ShapeDtypeStruct((t, d), F32),
        compiler_params=_cparams("arbitrary"),
        name="moe_combine_ln",
    )(dest_flat, ys, gates, x, g.reshape(1, d), b.reshape(1, d))


def _moe_ln(x, x_tiles, idx_rank, gates, cnt, layer, w_up, b_up, w_down, b_down, g, b, alpha):
    t, d = x.shape
    n_tiles = d // HEAD_DIM
    n_e = cnt.shape[0]
    counts = cnt[:, 0].astype(I32)
    padded = ((counts + MOE_BLOCK - 1) // MOE_BLOCK) * MOE_BLOCK
    p_ends = jnp.cumsum(padded)
    p_starts = p_ends - padded
    n_blk = (t * TOP_K + MOE_BLOCK - 1) // MOE_BLOCK + n_e
    experts = jnp.arange(n_e, dtype=I32)
    start_of = jnp.sum(jnp.where(idx_rank[:TOP_K, :, None] == experts, p_starts, 0), axis=-1)
    dest_flat = (start_of + idx_rank[TOP_K:]).reshape(-1)
    blk_first_row = jnp.arange(n_blk, dtype=I32) * MOE_BLOCK
    blk_expert = jnp.clip(jnp.sum((p_ends[None, :] <= blk_first_row[:, None]).astype(I32), axis=1), 0, n_e - 1)
    n_used = (p_ends[-1:] // MOE_BLOCK).astype(I32)

    xs = _dispatch(x_tiles, n_tiles, dest_flat, p_starts + counts, p_ends, n_used, n_blk * MOE_BLOCK)
    bg = b_up[:, None, 0::2]
    bl = b_up[:, None, 1::2]
    ys = _expert_mlp(xs, n_tiles, blk_expert, n_used, counts, layer, w_up, bg, bl, w_down, b_down[:, None, :])
    return _combine_ln(ys, dest_flat, gates, x, g, b, alpha)


def kernel(x, mem, hgrn_w_in, hgrn_lb, hgrn_norm_g, fox_w_in, fox_b_f, moba_w_in, rel_bias, w_mem_kv, w_out,
           ln1_g, ln1_b, router_w, router_b, w_up, b_up, w_down, b_down, ln2_g, ln2_b):
    bsz, seq, d = x.shape
    depth = w_out.shape[0]
    t = bsz * seq
    mix_w = MIX_HEADS * HEAD_DIM
    mem_w = MEM_HEADS * HEAD_DIM
    alpha = (2 * depth) ** 0.25
    scale = HEAD_DIM ** -0.5
    n_mem = mem.shape[1]
    mem2 = mem.reshape(bsz * n_mem, d)

    def scale_vec(n_mixer_q, n_plain, n_mem_q):
        return jnp.concatenate([
            jnp.full((1, n_mixer_q), scale * LOG2E, F32), jnp.ones((1, n_plain), F32),
            jnp.full((1, n_mem_q), scale, F32)], axis=1)

    x2 = x.reshape(t, d)
    for i in range(depth):
        kind, j = i % 3, i // 3
        if kind == 0:
            w_in = hgrn_w_in[j].astype(BF16)
            proj = _matmul(x2, w_in, scale_vec(0, 4 * mix_w, mem_w), F32, 1024, w_in.shape[1] // 3)
            proj = proj.reshape(bsz, seq, -1)
            mix = _hgrn_mixer(proj, hgrn_lb, hgrn_norm_g[j], j)
            memq_block = 4 * mix_w // mem_w
        elif kind == 1:
            w = fox_w_in[j]
            w_in = jnp.concatenate([w[:, :3 * mix_w], w[:, 3 * mix_w + MIX_HEADS:]], axis=1).astype(BF16)
            proj = _matmul(x2, w_in, scale_vec(mix_w, 2 * mix_w, mem_w), BF16, 1024, w_in.shape[1] // 2)
            proj = proj.reshape(bsz, seq, -1)
            c_col, c_row = _fox_gate_cumsum(x2.reshape(bsz, seq, d), w[:, 3 * mix_w:3 * mix_w + MIX_HEADS],
                                            fox_b_f[j])
            mix = _fox_attention(proj, c_col, c_row)
            memq_block = 3 * mix_w // mem_w
        else:
            w_in = moba_w_in[j].astype(BF16)
            proj = _matmul(x2, w_in, scale_vec(mix_w, 2 * mix_w, mem_w), BF16, 1024, w_in.shape[1] // 2)
            proj = proj.reshape(bsz, seq, -1)
            mix = _moba_attention(proj, rel_bias)
            memq_block = 3 * mix_w // mem_w
        kv = _matmul(mem2, w_mem_kv[i].astype(BF16), jnp.ones((1, 2 * mem_w), F32), BF16, 512, 512)
        mem_out = _memory_attention(proj, memq_block, kv.reshape(bsz, n_mem, 2 * mem_w))
        x2, x2_tiles, idx_rank, gates, cnt = _out_proj_ln_route(
            mix.reshape(t, mix_w), mem_out.reshape(t, mem_w), x2, w_out[i], ln1_g[i], ln1_b[i], alpha,
            router_w[i], router_b[i])
        x2 = _moe_ln(x2, x2_tiles, idx_rank, gates, cnt, i, w_up, b_up[i], w_down, b_down[i],
                     ln2_g[i], ln2_b[i], alpha)
    return x2.reshape(bsz, seq, d)
```
